```python
import math
import jax, jax.numpy as jnp
from jax import lax
import numpy as np

D_MODEL = 1024
BATCH = 8
SEQ = 4096
DEPTH = 2

HEAD_DIM = 64
ROPE_THETA = 10000.0
NORM_EPS = 1e-6
D_FF = 2816
BAND_BLOCK = 128
GATHER_CHUNK = 32
NEG_INF = -1e30
FORCE_SCORE = 1e4

A_Q_HEADS = 8
A_KV_HEADS = 2
A_WINDOW = 128
B_Q_HEADS = 8
B_KV_HEADS = 2
NSA_CMP_LEN = 32
NSA_CMP_STRIDE = 16
NSA_CMP_HIDDEN = 256
NSA_SEL_LEN = 64
NSA_SEL_COUNT = 8
NSA_WINDOW = 512
C_GROUPS = ((128, 1), (512, 4), (2048, 16))
C_HEADS_PER_GROUP = 4
C_HEADS = len(C_GROUPS) * C_HEADS_PER_GROUP
D_HEADS = 4
MOBA_BLOCK = 256
MOBA_TOPK = 3

EVEN_SIZES = ([A_Q_HEADS * HEAD_DIM] + [A_KV_HEADS * HEAD_DIM] * 2 + [B_Q_HEADS * HEAD_DIM]
              + [B_KV_HEADS * HEAD_DIM] * 6 + [3 * B_Q_HEADS])
EVEN_IN = sum(EVEN_SIZES)
EVEN_OUT = (A_Q_HEADS + B_Q_HEADS) * HEAD_DIM
ODD_SIZES = [C_HEADS * HEAD_DIM] * 3 + [D_HEADS * HEAD_DIM] * 3
ODD_IN = sum(ODD_SIZES)
ODD_OUT = (C_HEADS_PER_GROUP + D_HEADS) * HEAD_DIM
N_EVEN = (DEPTH + 1) // 2
N_ODD = DEPTH // 2

kernel_name = "hybrid_swa_nsa_dilated_moba_macaron"


def rms_norm(x, g):
    xf = x.astype(jnp.float32)
    y = xf * lax.rsqrt(jnp.mean(xf * xf, axis=-1, keepdims=True) + NORM_EPS)
    return (y * g.astype(jnp.float32)).astype(x.dtype)


def swiglu(x, wi, wo):
    gate, up = jnp.split(x @ wi, 2, axis=-1)
    return (jax.nn.silu(gate) * up) @ wo


def rope(x):
    T, dh = x.shape[-2], x.shape[-1]
    inv = 1.0 / (ROPE_THETA ** (jnp.arange(0, dh, 2, dtype=jnp.float32) / dh))
    ang = jnp.arange(T, dtype=jnp.float32)[:, None] * inv[None, :]
    cos = jnp.cos(ang).astype(x.dtype)
    sin = jnp.sin(ang).astype(x.dtype)
    x1, x2 = jnp.split(x, 2, axis=-1)
    return jnp.concatenate([x1 * cos - x2 * sin, x2 * cos + x1 * sin], axis=-1)


def to_heads(z, n):
    B, T, _ = z.shape
    return z.reshape(B, T, n, HEAD_DIM).transpose(0, 2, 1, 3)


def from_heads(o):
    B, H, T, dh = o.shape
    return o.transpose(0, 2, 1, 3).reshape(B, T, H * dh)


def split_cols(z, sizes):
    offs = []
    acc = 0
    for s in sizes[:-1]:
        acc += s
        offs.append(acc)
    return jnp.split(z, offs, axis=-1)


def masked_softmax_stats(s, mask, sink=None):
    s = jnp.where(mask, s, NEG_INF)
    m = jnp.max(s, axis=-1)
    if sink is not None:
        m = jnp.maximum(m, sink)
    p = jnp.where(mask, jnp.exp(s - m[..., None]), 0.0)
    denom = jnp.sum(p, axis=-1)
    if sink is not None:
        denom = denom + jnp.exp(sink - m)
    denom = jnp.where(denom > 0, denom, 1.0)
    return p, m, denom


def banded_attention(q, k, v, max_dist, sink=None):
    B, Hk, G, L, dh = q.shape
    bq = math.gcd(L, BAND_BLOCK)
    nb = L // bq
    nw = -(-max_dist // bq)
    pad = nw * bq
    kp = jnp.pad(k, ((0, 0), (0, 0), (pad, 0), (0, 0))).reshape(B, Hk, nb + nw, bq, dh)
    vp = jnp.pad(v, ((0, 0), (0, 0), (pad, 0), (0, 0))).reshape(B, Hk, nb + nw, bq, dh)
    kwin = jnp.concatenate([kp[:, :, j:j + nb] for j in range(nw + 1)], axis=3)
    vwin = jnp.concatenate([vp[:, :, j:j + nb] for j in range(nw + 1)], axis=3)
    qb = q.reshape(B, Hk, G, nb, bq, dh)
    s = jnp.einsum('bhgnqd,bhnkd->bhgnqk', qb, kwin).astype(jnp.float32) * (dh ** -0.5)
    qpos = jnp.arange(nb)[:, None] * bq + jnp.arange(bq)[None, :]
    kpos = jnp.arange(nb)[:, None] * bq - pad + jnp.arange((nw + 1) * bq)[None, :]
    dist = qpos[:, :, None] - kpos[:, None, :]
    mask = (dist >= 0) & (dist <= max_dist) & (kpos[:, None, :] >= 0)
    sink_b = None if sink is None else sink.astype(jnp.float32)[None, :, :, None, None]
    p, m, denom = masked_softmax_stats(s, mask, sink_b)
    o = jnp.einsum('bhgnqk,bhnkd->bhgnqd', p.astype(v.dtype), vwin).astype(jnp.float32)
    o = (o / denom[..., None]).astype(q.dtype)
    lse = m + jnp.log(denom)
    return o.reshape(B, Hk, G, L, dh), lse.reshape(B, Hk, G, L)


def gathered_block_attention(q, k_blocks, v_blocks, idx, valid):
    B, Hk, G, T, dh = q.shape
    bs = k_blocks.shape[3]
    K = idx.shape[-1]
    c = math.gcd(T, GATHER_CHUNK)
    nc = T // c
    qs = jnp.moveaxis(q.reshape(B, Hk, G, nc, c, dh), 3, 0)
    ids = jnp.moveaxis(idx.reshape(B, Hk, nc, c, K), 2, 0)
    vals = jnp.moveaxis(valid.reshape(B, Hk, nc, c, K), 2, 0)
    poss = jnp.arange(T).reshape(nc, c)
    take = jax.vmap(jax.vmap(lambda blocks, ix: blocks[ix]))
    scale = dh ** -0.5

    def chunk(args):
        qc, ic, vc, pc = args
        kg = take(k_blocks, ic).reshape(B, Hk, c, K * bs, dh)
        vg = take(v_blocks, ic).reshape(B, Hk, c, K * bs, dh)
        kpos = (ic[..., None] * bs + jnp.arange(bs)).reshape(B, Hk, c, K * bs)
        mask = jnp.repeat(vc, bs, axis=-1) & (kpos <= pc[:, None])
        s = jnp.einsum('bhgcd,bhckd->bhgck', qc, kg).astype(jnp.float32) * scale
        p, _, denom = masked_softmax_stats(s, mask[:, :, None])
        o = jnp.einsum('bhgck,bhckd->bhgcd', p.astype(vg.dtype), vg).astype(jnp.float32)
        return (o / denom[..., None]).astype(q.dtype)

    out = lax.map(chunk, (qs, ids, vals, poss))
    return jnp.moveaxis(out, 0, 3).reshape(B, Hk, G, T, dh)


def nsa_attention(q, k_cmp, v_cmp, k_sel, v_sel, k_win, v_win, gate_logits, cmp_pe, cmp_w1, cmp_w2):
    B, Hq, T, dh = q.shape
    Hk = k_cmp.shape[1]
    G = Hq // Hk
    scale = dh ** -0.5
    qg = q.reshape(B, Hk, G, T, dh)
    qr = rope(q).reshape(B, Hk, G, T, dh)
    t = jnp.arange(T)
    nc = (T - NSA_CMP_LEN) // NSA_CMP_STRIDE + 1
    starts = jnp.arange(nc) * NSA_CMP_STRIDE
    win = starts[:, None] + jnp.arange(NSA_CMP_LEN)[None, :]

    def compress(z, pe, w1, w2):
        zb = z[:, :, win] + pe
        return jax.nn.gelu(zb.reshape(B, Hk, nc, NSA_CMP_LEN * dh) @ w1) @ w2

    kc = compress(k_cmp, cmp_pe[0], cmp_w1[0], cmp_w2[0])
    vc = compress(v_cmp, cmp_pe[1], cmp_w1[1], cmp_w2[1])
    cmask = (starts + NSA_CMP_LEN - 1)[None, :] <= t[:, None]
    s = jnp.einsum('bhgtd,bhnd->bhgtn', qg, kc).astype(jnp.float32) * scale
    p, _, denom = masked_softmax_stats(s, cmask)
    p_cmp = p / denom[..., None]
    o_cmp = jnp.einsum('bhgtn,bhnd->bhgtd', p_cmp.astype(vc.dtype), vc)
    nsb = T // NSA_SEL_LEN
    sel_start = jnp.arange(nsb) * NSA_SEL_LEN
    overlap = jnp.clip(jnp.minimum((starts + NSA_CMP_LEN)[:, None], (sel_start + NSA_SEL_LEN)[None, :])
                       - jnp.maximum(starts[:, None], sel_start[None, :]), 0, None)
    overlap = overlap.astype(jnp.float32) / NSA_CMP_LEN
    imp = jnp.einsum('bhgtn,nj->bhtj', p_cmp, overlap)
    j = jnp.arange(nsb)[None, :]
    cb = (t // NSA_SEL_LEN)[:, None]
    forced = (j == 0) | (j == cb) | (j == cb - 1)
    score = jnp.where(j <= cb, jnp.where(forced, FORCE_SCORE, imp), NEG_INF)
    top, sel_idx = lax.top_k(score, min(NSA_SEL_COUNT, nsb))
    sel_valid = top > 0.5 * NEG_INF
    ksb = rope(k_sel).reshape(B, Hk, nsb, NSA_SEL_LEN, dh)
    vsb = v_sel.reshape(B, Hk, nsb, NSA_SEL_LEN, dh)
    o_sel = gathered_block_attention(qr, ksb, vsb, sel_idx.astype(jnp.int32), sel_valid)
    o_win, _ = banded_attention(qr, rope(k_win), v_win, NSA_WINDOW - 1)
    g = jax.nn.sigmoid(gate_logits.astype(jnp.float32)).astype(q.dtype)
    g = g.transpose(0, 2, 1, 3).reshape(B, Hk, G, T, 3)
    o = g[..., 0:1] * o_cmp + g[..., 1:2] * o_sel + g[..., 2:3] * o_win
    return o.reshape(B, Hq, T, dh)


def dilated_attention(q, k, v):
    B, H, T, dh = q.shape
    hg = C_HEADS_PER_GROUP
    outs, lses = [], []
    for gi, (w, r) in enumerate(C_GROUPS):
        L = T // r

        def strided(z):
            return z[:, gi * hg:(gi + 1) * hg].reshape(B, hg, L, r, dh).transpose(0, 1, 3, 2, 4).reshape(B, hg * r, L, dh)

        o, lse = banded_attention(strided(q)[:, :, None], strided(k), strided(v), w // r)
        outs.append(o[:, :, 0].reshape(B, hg, r, L, dh).transpose(0, 1, 3, 2, 4).reshape(B, hg, T, dh))
        lses.append(lse[:, :, 0].reshape(B, hg, r, L).transpose(0, 1, 3, 2).reshape(B, hg, T))
    wts = jax.nn.softmax(jnp.stack(lses), axis=0)
    return jnp.einsum('gbht,gbhtd->bhtd', wts.astype(q.dtype), jnp.stack(outs))


def moba_attention(q, k, v):
    B, H, T, dh = q.shape
    nb = -(-T // MOBA_BLOCK)
    padT = nb * MOBA_BLOCK - T
    kb = jnp.pad(k, ((0, 0), (0, 0), (0, padT), (0, 0))).reshape(B, H, nb, MOBA_BLOCK, dh)
    vb = jnp.pad(v, ((0, 0), (0, 0), (0, padT), (0, 0))).reshape(B, H, nb, MOBA_BLOCK, dh)
    cb = (jnp.arange(T) // MOBA_BLOCK).astype(jnp.int32)
    own = jnp.broadcast_to(cb, (B, H, T))[..., None]
    n_sel = min(MOBA_TOPK, nb - 1)
    if n_sel > 0:
        kmean = jnp.mean(kb.astype(jnp.float32), axis=3)
        gate = jnp.einsum('bhtd,bhnd->bhtn', q.astype(jnp.float32), kmean)
        past = jnp.arange(nb)[None, :] < cb[:, None]
        top, sel = lax.top_k(jnp.where(past, gate, NEG_INF), n_sel)
        idx = jnp.concatenate([sel.astype(jnp.int32), own], axis=-1)
        valid = jnp.concatenate([top > 0.5 * NEG_INF, jnp.ones_like(own, dtype=bool)], axis=-1)
    else:
        idx = own
        valid = jnp.ones_like(own, dtype=bool)
    o = gathered_block_attention(q[:, :, None], kb, vb, idx, valid)
    return o[:, :, 0]


def even_mixer(h, w_in, w_out, sinks, cmp_pe, cmp_w1, cmp_w2):
    B, T, _ = h.shape
    aq, ak, av, bq, bkc, bvc, bks, bvs, bkw, bvw, bg = split_cols(h @ w_in, EVEN_SIZES)
    ga = A_Q_HEADS // A_KV_HEADS
    qa = rope(to_heads(aq, A_Q_HEADS)).reshape(B, A_KV_HEADS, ga, T, HEAD_DIM)
    oa, _ = banded_attention(qa, rope(to_heads(ak, A_KV_HEADS)), to_heads(av, A_KV_HEADS),
                             A_WINDOW - 1, sinks.reshape(A_KV_HEADS, ga))
    oa = oa.reshape(B, A_Q_HEADS, T, HEAD_DIM)
    ob = nsa_attention(to_heads(bq, B_Q_HEADS), to_heads(bkc, B_KV_HEADS), to_heads(bvc, B_KV_HEADS),
                       to_heads(bks, B_KV_HEADS), to_heads(bvs, B_KV_HEADS),
                       to_heads(bkw, B_KV_HEADS), to_heads(bvw, B_KV_HEADS),
                       bg.reshape(B, T, B_Q_HEADS, 3), cmp_pe, cmp_w1, cmp_w2)
    return from_heads(jnp.concatenate([oa, ob], axis=1)) @ w_out


def odd_mixer(h, w_in, w_out):
    cq, ck, cv, dq, dk, dv = split_cols(h @ w_in, ODD_SIZES)
    oc = dilated_attention(rope(to_heads(cq, C_HEADS)), rope(to_heads(ck, C_HEADS)), to_heads(cv, C_HEADS))
    od = moba_attention(rope(to_heads(dq, D_HEADS)), rope(to_heads(dk, D_HEADS)), to_heads(dv, D_HEADS))
    return from_heads(jnp.concatenate([oc, od], axis=1)) @ w_out


def setup_inputs(seed: int = 0) -> dict:
    key = jax.random.key(seed)
    ks = jax.random.split(key, 16)
    f32 = jnp.float32

    def normal(k, shape, fan_in):
        return jax.random.normal(k, shape, f32) * fan_in ** -0.5

    def gain(k, shape):
        return 1.0 + 0.05 * jax.random.normal(k, shape, f32)

    return {
        "x": jax.random.normal(ks[0], (BATCH, SEQ, D_MODEL), f32),
        "ffn_norm_pre": gain(ks[1], (DEPTH, D_MODEL)),
        "mix_norm": gain(ks[2], (DEPTH, D_MODEL)),
        "ffn_norm_post": gain(ks[3], (DEPTH, D_MODEL)),
        "ffn_wi": normal(ks[4], (DEPTH, 2, D_MODEL, 2 * D_FF), D_MODEL),
        "ffn_wo": normal(ks[5], (DEPTH, 2, D_FF, D_MODEL), D_FF),
        "even_w_in": normal(ks[6], (N_EVEN, D_MODEL, EVEN_IN), D_MODEL),
        "even_w_out": normal(ks[7], (N_EVEN, EVEN_OUT, D_MODEL), EVEN_OUT),
        "even_sinks": 0.5 * jax.random.normal(ks[8], (N_EVEN, A_Q_HEADS), f32),
        "nsa_cmp_pe": 0.1 * jax.random.normal(ks[9], (N_EVEN, 2, NSA_CMP_LEN, HEAD_DIM), f32),
        "nsa_cmp_w1": normal(ks[10], (N_EVEN, 2, NSA_CMP_LEN * HEAD_DIM, NSA_CMP_HIDDEN), NSA_CMP_LEN * HEAD_DIM),
        "nsa_cmp_w2": normal(ks[11], (N_EVEN, 2, NSA_CMP_HIDDEN, HEAD_DIM), NSA_CMP_HIDDEN),
        "odd_w_in": normal(ks[12], (N_ODD, D_MODEL, ODD_IN), D_MODEL),
        "odd_w_out": normal(ks[13], (N_ODD, ODD_OUT, D_MODEL), ODD_OUT),
        "final_norm": gain(ks[14], (D_MODEL,)),
    }


def reference(x, ffn_norm_pre, mix_norm, ffn_norm_post, ffn_wi, ffn_wo, even_w_in, even_w_out,
              even_sinks, nsa_cmp_pe, nsa_cmp_w1, nsa_cmp_w2, odd_w_in, odd_w_out, final_norm):
    h = x
    for layer in range(DEPTH):
        i = layer // 2
        h = h + 0.5 * swiglu(rms_norm(h, ffn_norm_pre[layer]), ffn_wi[layer, 0], ffn_wo[layer, 0])
        hn = rms_norm(h, mix_norm[layer])
        if layer % 2 == 0:
            h = h + even_mixer(hn, even_w_in[i], even_w_out[i], even_sinks[i],
                               nsa_cmp_pe[i], nsa_cmp_w1[i], nsa_cmp_w2[i])
        else:
            h = h + odd_mixer(hn, odd_w_in[i], odd_w_out[i])
        h = h + 0.5 * swiglu(rms_norm(h, ffn_norm_post[layer]), ffn_wi[layer, 1], ffn_wo[layer, 1])
    return rms_norm(h, final_norm)
```

```python
import functools
import math

import jax
import jax.numpy as jnp
from jax import lax
from jax.experimental import pallas as pl
from jax.experimental.pallas import tpu as pltpu

D_MODEL = 1024
HEAD_DIM = 64
ROPE_THETA = 10000.0
NORM_EPS = 1e-6
D_FF = 2816
NEG_INF = -1e30
FORCE_SCORE = 1e4

A_Q_HEADS = 8
A_KV_HEADS = 2
A_WINDOW = 128
B_Q_HEADS = 8
B_KV_HEADS = 2
NSA_CMP_LEN = 32
NSA_CMP_STRIDE = 16
NSA_CMP_HIDDEN = 256
NSA_SEL_LEN = 64
NSA_SEL_COUNT = 8
NSA_WINDOW = 512
C_GROUPS = ((128, 1), (512, 4), (2048, 16))
C_HEADS_PER_GROUP = 4
C_HEADS = len(C_GROUPS) * C_HEADS_PER_GROUP
D_HEADS = 4
MOBA_BLOCK = 256
MOBA_TOPK = 3

LANES = 128
SCALE = HEAD_DIM ** -0.5
MASK_BIAS = -32768.0
REMOVED = -3e38
VMEM_LIMIT = 52 * 1024 * 1024

FFN_TM = 512
FFN_TF = 1408
PROJ_TM = 512
BAND_TL = 512
BAND_SB = 128
FLASH_T = 256
MOBA_T = 512
CMP_TQ = 256

BF16 = jnp.bfloat16
F32 = jnp.float32


def _params(*sem):
    return pltpu.CompilerParams(dimension_semantics=sem, vmem_limit_bytes=VMEM_LIMIT)


def _rms(x, g):
    return x * lax.rsqrt(jnp.mean(x * x, axis=-1, keepdims=True) + NORM_EPS) * g


def _log2(n):
    assert n & (n - 1) == 0
    return n.bit_length() - 1


def _dot(a, b):
    return jnp.dot(a, b, preferred_element_type=F32)


def _dot_t(a, b):
    return lax.dot_general(a, b, (((1,), (1,)), ((), ())), preferred_element_type=F32)


def _ffn_kernel(x_ref, g_ref, wg_ref, wu_ref, wo_ref, *rest, nf, final):
    if final:
        fg_ref, o_ref, n_scr, acc_scr = rest
    else:
        o_ref, n_scr, acc_scr = rest
    f = pl.program_id(1)

    @pl.when(f == 0)
    def _():
        n_scr[...] = _rms(x_ref[...], g_ref[...]).astype(BF16)
        acc_scr[...] = jnp.zeros_like(acc_scr)

    n = n_scr[...]
    gate = _dot(n, wg_ref[...])
    up = _dot(n, wu_ref[...])
    act = (gate * jax.nn.sigmoid(gate) * up).astype(BF16)
    acc_scr[...] += _dot(act, wo_ref[...])

    @pl.when(f == nf - 1)
    def _():
        h = x_ref[...] + 0.5 * acc_scr[...]
        if final:
            h = _rms(h, fg_ref[...])
        o_ref[...] = h


def _ffn(h, gain, wi, wo, final_gain=None):
    m = h.shape[0]
    tm, tf = FFN_TM, FFN_TF
    nf = D_FF // tf
    final = final_gain is not None
    in_specs = [
        pl.BlockSpec((tm, D_MODEL), lambda i, f: (i, 0)),
        pl.BlockSpec((1, D_MODEL), lambda i, f: (0, 0)),
        pl.BlockSpec((D_MODEL, tf), lambda i, f: (0, f)),
        pl.BlockSpec((D_MODEL, tf), lambda i, f: (0, nf + f)),
        pl.BlockSpec((tf, D_MODEL), lambda i, f: (f, 0)),
    ]
    args = [h, gain.reshape(1, D_MODEL), wi, wi, wo]
    if final:
        in_specs.append(pl.BlockSpec((1, D_MODEL), lambda i, f: (0, 0)))
        args.append(final_gain.reshape(1, D_MODEL))
    return pl.pallas_call(
        functools.partial(_ffn_kernel, nf=nf, final=final),
        grid=(m // tm, nf),
        in_specs=in_specs,
        out_specs=pl.BlockSpec((tm, D_MODEL), lambda i, f: (i, 0)),
        out_shape=jax.ShapeDtypeStruct((m, D_MODEL), F32),
        scratch_shapes=[pltpu.VMEM((tm, D_MODEL), BF16), pltpu.VMEM((tm, D_MODEL), F32)],
        compiler_params=_params("parallel", "arbitrary"),
    )(*args)


def _rope_tables(t):
    inv = 1.0 / (ROPE_THETA ** (jnp.arange(0, HEAD_DIM, 2, dtype=F32) / HEAD_DIM))
    ang = jnp.arange(t, dtype=F32)[:, None] * inv[None, :]
    cos = jnp.cos(ang)
    sin = jnp.sin(ang)
    return (jnp.concatenate([cos, cos, cos, cos], axis=-1),
            jnp.concatenate([-sin, sin, -sin, sin], axis=-1))


def _proj_kernel(x_ref, g_ref, w_ref, cos_ref, sin_ref, *out_refs, plan):
    n = _rms(x_ref[...], g_ref[...]).astype(BF16)
    cos = cos_ref[...]
    sin = sin_ref[...]
    lane = lax.broadcasted_iota(jnp.int32, cos.shape, 1)
    first_half = (lane & (HEAD_DIM - 1)) < HEAD_DIM // 2
    for off, width, outs in plan:
        y = _dot(n, w_ref[:, off:off + width])
        for o_idx, sub_off, sub_w, mode, n_split in outs:
            o_ref = out_refs[o_idx]
            for c in range(sub_w // LANES):
                piece = y[:, sub_off + c * LANES: sub_off + (c + 1) * LANES]
                if mode == "rope":
                    rot = jnp.where(first_half, pltpu.roll(piece, LANES - HEAD_DIM // 2, 1),
                                    pltpu.roll(piece, HEAD_DIM // 2, 1))
                    piece = piece * cos + rot * sin
                elif mode == "sigmoid":
                    piece = jax.nn.sigmoid(piece)
                piece = piece.astype(o_ref.dtype)
                if n_split == 0:
                    o_ref[:, c * LANES:(c + 1) * LANES] = piece
                else:
                    hw = sub_w // n_split
                    per = LANES // hw
                    for s in range(per):
                        o_ref[c * per + s] = piece[:, s * hw:(s + 1) * hw]


def _proj(h, gain, w, cos, sin, plan, out_defs, t_len):
    m = h.shape[0]
    tm = PROJ_TM
    nt = t_len // tm
    out_shapes, out_specs = [], []
    for tail, dt in out_defs:
        if len(tail) == 1:
            out_shapes.append(jax.ShapeDtypeStruct((m, tail[0]), dt))
            out_specs.append(pl.BlockSpec((tm, tail[0]), lambda i: (i, 0)))
        else:
            out_shapes.append(jax.ShapeDtypeStruct((tail[0], m, tail[1]), dt))
            out_specs.append(pl.BlockSpec((tail[0], tm, tail[1]), lambda i: (0, i, 0)))
    return pl.pallas_call(
        functools.partial(_proj_kernel, plan=plan),
        grid=(m // tm,),
        in_specs=[
            pl.BlockSpec((tm, D_MODEL), lambda i: (i, 0)),
            pl.BlockSpec((1, D_MODEL), lambda i: (0, 0)),
            pl.BlockSpec(w.shape, lambda i: (0, 0)),
            pl.BlockSpec((tm, LANES), lambda i: (i % nt, 0)),
            pl.BlockSpec((tm, LANES), lambda i: (i % nt, 0)),
        ],
        out_specs=out_specs,
        out_shape=out_shapes,
        compiler_params=_params("parallel"),
    )(h, gain.reshape(1, D_MODEL), w, cos, sin)


def _banded_kernel(q_ref, kp_ref, kc_ref, vp_ref, vc_ref, *rest, n_kv, group, tl, pk, span, max_dist,
                   has_sink, gate_branch, want_lse):
    rest = list(rest)
    sink_ref = rest.pop(0) if has_sink else None
    gate_ref = rest.pop(0) if gate_branch is not None else None
    o_ref = rest.pop(0)
    lse_ref = rest.pop(0) if want_lse else None
    kcat, vcat = rest
    sb = BAND_SB
    t0 = pl.program_id(2) * tl
    kcat[0:pk, :] = kp_ref[0]
    kcat[pk:, :] = kc_ref[0]
    vcat[0:pk, :] = vp_ref[0]
    vcat[pk:, :] = vc_ref[0]
    nk = span + sb
    row = lax.broadcasted_iota(jnp.int32, (sb, nk), 0)
    col = lax.broadcasted_iota(jnp.int32, (sb, nk), 1)
    dist = row + span - col
    in_band = (dist >= 0) & (dist <= max_dist)
    for j in range(tl // sb):
        lo = pk + j * sb - span
        ok = in_band & (col >= span - j * sb - t0)
        outs, lses = [], []
        for h in range(n_kv):
            kk = kcat[lo:lo + nk, h * HEAD_DIM:(h + 1) * HEAD_DIM]
            vv = vcat[lo:lo + nk, h * HEAD_DIM:(h + 1) * HEAD_DIM]
            for g in range(group):
                hq = h * group + g
                qg = q_ref[0, j * sb:(j + 1) * sb, hq * HEAD_DIM:(hq + 1) * HEAD_DIM]
                s = _dot_t(qg, kk) * SCALE
                s = jnp.where(ok, s, NEG_INF)
                m = jnp.max(s, axis=-1, keepdims=True)
                if has_sink:
                    sink = sink_ref[hq]
                    m = jnp.maximum(m, sink)
                p = jnp.exp(s - m)
                d = jnp.sum(p, axis=-1, keepdims=True)
                if has_sink:
                    d = d + jnp.exp(sink - m)
                d = jnp.where(d > 0, d, 1.0)
                o = _dot(p.astype(BF16), vv) / d
                if gate_ref is not None:
                    gc = g * 3 + gate_branch
                    o = o * gate_ref[h, 0, j * sb:(j + 1) * sb, gc:gc + 1]
                outs.append(o)
                if want_lse:
                    lses.append(jnp.broadcast_to(m + jnp.log(d), (sb, HEAD_DIM)))
        o_ref[0, j * sb:(j + 1) * sb, :] = jnp.concatenate(outs, axis=-1).astype(o_ref.dtype)
        if want_lse:
            lse_ref[0, j * sb:(j + 1) * sb, :] = jnp.concatenate(lses, axis=-1)


def _banded(q, k, v, *, n_kv, group, max_dist, tl, pk, span, n_res=1, q_col=None, kv_col=None,
            sink=None, gate=None, gate_branch=None, want_lse=False):
    bsz, length, _ = q.shape
    qw = n_kv * group * HEAD_DIM
    kw = n_kv * HEAD_DIM
    q_col = q_col or (lambda i: i)
    kv_col = kv_col or (lambda i: i)
    ratio = tl // pk
    prev = lambda b, i, l: (b, jnp.maximum(l * ratio - 1, 0), kv_col(i))
    cur = lambda b, i, l: (b, l, kv_col(i))
    in_specs = [
        pl.BlockSpec((1, tl, qw), lambda b, i, l: (b, l, q_col(i))),
        pl.BlockSpec((1, pk, kw), prev),
        pl.BlockSpec((1, tl, kw), cur),
        pl.BlockSpec((1, pk, kw), prev),
        pl.BlockSpec((1, tl, kw), cur),
    ]
    args = [q, k, k, v, v]
    if sink is not None:
        in_specs.append(pl.BlockSpec(memory_space=pltpu.SMEM))
        args.append(sink)
    if gate is not None:
        in_specs.append(pl.BlockSpec((n_kv, 1, tl, LANES), lambda b, i, l: (0, b, l, 0)))
        args.append(gate)
    out_spec = pl.BlockSpec((1, tl, qw), lambda b, i, l: (b, l, i))
    out_shape = [jax.ShapeDtypeStruct((bsz, length, n_res * qw), BF16)]
    out_specs = [out_spec]
    if want_lse:
        out_shape.append(jax.ShapeDtypeStruct((bsz, length, n_res * qw), F32))
        out_specs.append(out_spec)
    res = pl.pallas_call(
        functools.partial(_banded_kernel, n_kv=n_kv, group=group, tl=tl, pk=pk, span=span,
                          max_dist=max_dist, has_sink=sink is not None, gate_branch=gate_branch,
                          want_lse=want_lse),
        grid=(bsz, n_res, length // tl),
        in_specs=in_specs,
        out_specs=out_specs,
        out_shape=out_shape,
        scratch_shapes=[pltpu.VMEM((pk + tl, kw), BF16), pltpu.VMEM((pk + tl, kw), BF16)],
        compiler_params=_params("parallel", "parallel", "parallel"),
    )(*args)
    return res if want_lse else res[0]


def _compress_one(z_ref, pe_ref, w1_ref, w2_ref, o_ref):
    half = NSA_CMP_STRIDE * HEAD_DIM
    z = z_ref[0, 0].astype(F32)
    za = (z + pe_ref[0, :, 0:half]).astype(BF16)
    zb = (z + pe_ref[0, :, half:2 * half]).astype(BF16)
    ya = _dot(za, w1_ref[0, 0:half, :])
    yb = _dot(zb, w1_ref[0, half:2 * half, :])
    nrow = ya.shape[0]
    hid = jax.nn.gelu(ya + pltpu.roll(yb, nrow - 1, 0))
    o_ref[0, 0] = _dot(hid.astype(BF16), w2_ref[0]).astype(o_ref.dtype)


def _compress_kernel(zk_ref, zv_ref, pek_ref, pev_ref, w1k_ref, w1v_ref, w2k_ref, w2v_ref, ok_ref, ov_ref):
    _compress_one(zk_ref, pek_ref, w1k_ref, w2k_ref, ok_ref)
    _compress_one(zv_ref, pev_ref, w1v_ref, w2v_ref, ov_ref)


def _compress(zk, zv, pe, w1, w2):
    hk, bsz, nch, width = zk.shape
    zspec = pl.BlockSpec((1, 1, nch, width), lambda h, b: (h, b, 0, 0))
    ospec = pl.BlockSpec((1, 1, nch, HEAD_DIM), lambda h, b: (h, b, 0, 0))
    oshape = jax.ShapeDtypeStruct((hk, bsz, nch, HEAD_DIM), BF16)

    def wspec(arr, which):
        return pl.BlockSpec((1,) + arr.shape[1:], lambda h, b: (which, 0, 0))

    return pl.pallas_call(
        _compress_kernel,
        grid=(hk, bsz),
        in_specs=[zspec, zspec, wspec(pe, 0), wspec(pe, 1), wspec(w1, 0), wspec(w1, 1), wspec(w2, 0), wspec(w2, 1)],
        out_specs=[ospec, ospec],
        out_shape=[oshape, oshape],
        compiler_params=_params("parallel", "parallel"),
    )(zk, zv, pe, pe, w1, w1, w2, w2)


def _topk_select(score, k, colf):
    ncol = score.shape[-1]
    rem = score
    sel = jnp.zeros_like(score)
    for _ in range(k):
        mx = jnp.max(rem, axis=-1, keepdims=True)
        first = jnp.min(jnp.where(rem == mx, colf, float(ncol)), axis=-1, keepdims=True)
        pick = colf == first
        sel = jnp.where(pick, jnp.where(mx > 0.5 * NEG_INF, 1.0, 0.0), sel)
        rem = jnp.where(pick, REMOVED, rem)
    return sel


def _cmp_kernel(q_ref, kc_ref, vc_ref, gate_ref, o_ref, bias_ref, *, tq, group):
    qi = pl.program_id(2)
    kc = kc_ref[0, 0]
    vc = vc_ref[0, 0]
    ncp = kc.shape[0]
    nsb = bias_ref.shape[-1]
    t = qi * tq + lax.broadcasted_iota(jnp.int32, (tq, ncp), 0)
    n = lax.broadcasted_iota(jnp.int32, (tq, ncp), 1)
    visible = n * NSA_CMP_STRIDE + (NSA_CMP_LEN - 1) <= t
    psum = jnp.zeros((tq, ncp), F32)
    outs = []
    for g in range(group):
        qg = q_ref[0, :, g * HEAD_DIM:(g + 1) * HEAD_DIM]
        s = _dot_t(qg, kc) * SCALE
        s = jnp.where(visible, s, NEG_INF)
        m = jnp.max(s, axis=-1, keepdims=True)
        p = jnp.where(visible, jnp.exp(s - m), 0.0)
        d = jnp.sum(p, axis=-1, keepdims=True)
        d = jnp.where(d > 0, d, 1.0)
        pc = p / d
        psum = psum + pc
        o = _dot(pc.astype(BF16), vc)
        outs.append(o * gate_ref[0, 0, :, g * 3:g * 3 + 1])
    o_ref[0] = jnp.concatenate(outs, axis=-1).astype(o_ref.dtype)

    nn = lax.broadcasted_iota(jnp.int32, (ncp, nsb), 0) * NSA_CMP_STRIDE
    jj = lax.broadcasted_iota(jnp.int32, (ncp, nsb), 1) * NSA_SEL_LEN
    ov = jnp.maximum(jnp.minimum(nn + NSA_CMP_LEN, jj + NSA_SEL_LEN) - jnp.maximum(nn, jj), 0)
    ov = (ov.astype(F32) / NSA_CMP_LEN).astype(BF16)
    p_hi = psum.astype(BF16)
    p_lo = (psum - p_hi.astype(F32)).astype(BF16)
    imp = _dot(p_hi, ov) + _dot(p_lo, ov)

    tt = qi * tq + lax.broadcasted_iota(jnp.int32, (tq, nsb), 0)
    j = lax.broadcasted_iota(jnp.int32, (tq, nsb), 1)
    cb = tt >> _log2(NSA_SEL_LEN)
    forced = (j == 0) | (j == cb) | (j == cb - 1)
    score = jnp.where(j <= cb, jnp.where(forced, FORCE_SCORE, imp), NEG_INF)
    sel = _topk_select(score, min(NSA_SEL_COUNT, nsb), j.astype(F32))
    bias_ref[0, 0] = jnp.where(sel > 0.5, 0.0, MASK_BIAS).astype(bias_ref.dtype)


def _cmp_attention(qu, kc, vc, gate, t_len):
    bsz = qu.shape[0]
    hk, _, ncp, _ = kc.shape
    group = B_Q_HEADS // B_KV_HEADS
    tq = CMP_TQ
    nsb = t_len // NSA_SEL_LEN
    gw = group * HEAD_DIM
    cspec = pl.BlockSpec((1, 1, ncp, HEAD_DIM), lambda b, h, i: (h, b, 0, 0))
    return pl.pallas_call(
        functools.partial(_cmp_kernel, tq=tq, group=group),
        grid=(bsz, hk, t_len // tq),
        in_specs=[
            pl.BlockSpec((1, tq, gw), lambda b, h, i: (b, i, h)),
            cspec, cspec,
            pl.BlockSpec((1, 1, tq, LANES), lambda b, h, i: (h, b, i, 0)),
        ],
        out_specs=[
            pl.BlockSpec((1, tq, gw), lambda b, h, i: (b, i, h)),
            pl.BlockSpec((1, 1, tq, nsb), lambda b, h, i: (h, b, i, 0)),
        ],
        out_shape=[
            jax.ShapeDtypeStruct((bsz, t_len, hk * gw), BF16),
            jax.ShapeDtypeStruct((hk, bsz, t_len, nsb), BF16),
        ],
        compiler_params=_params("parallel", "parallel", "parallel"),
    )(qu, kc, vc, gate)


def _moba_gate_kernel(q_ref, k_ref, bias_ref, *, tq):
    qi = pl.program_id(2)
    k = k_ref[0, 0]
    t_len = k.shape[0]
    ncol = bias_ref.shape[-1]
    blk = lax.broadcasted_iota(jnp.int32, (ncol, t_len), 0)
    pos = lax.broadcasted_iota(jnp.int32, (ncol, t_len), 1)
    member = jnp.where((pos >> _log2(MOBA_BLOCK)) == blk, 1.0, 0.0).astype(BF16)
    kmean = _dot(member, k) * (1.0 / MOBA_BLOCK)
    km_hi = kmean.astype(BF16)
    km_lo = (kmean - km_hi.astype(F32)).astype(BF16)
    q = q_ref[0, 0]
    gate = _dot_t(q, km_hi) + _dot_t(q, km_lo)
    t = qi * tq + lax.broadcasted_iota(jnp.int32, (tq, ncol), 0)
    j = lax.broadcasted_iota(jnp.int32, (tq, ncol), 1)
    cb = t >> _log2(MOBA_BLOCK)
    nb = t_len // MOBA_BLOCK
    score = jnp.where(j < cb, gate, jnp.where(j < nb, NEG_INF, REMOVED))
    sel = _topk_select(score, min(MOBA_TOPK, nb - 1), j.astype(F32))
    keep = (sel > 0.5) | (j == cb)
    bias_ref[0, 0] = jnp.where(keep, 0.0, MASK_BIAS).astype(bias_ref.dtype)


def _moba_gate(q, k):
    nh, bsz, t_len, _ = q.shape
    tq = MOBA_T
    return pl.pallas_call(
        functools.partial(_moba_gate_kernel, tq=tq),
        grid=(nh, bsz, t_len // tq),
        in_specs=[
            pl.BlockSpec((1, 1, tq, HEAD_DIM), lambda h, b, i: (h, b, i, 0)),
            pl.BlockSpec((1, 1, t_len, HEAD_DIM), lambda h, b, i: (h, b, 0, 0)),
        ],
        out_specs=pl.BlockSpec((1, 1, tq, HEAD_DIM), lambda h, b, i: (h, b, i, 0)),
        out_shape=jax.ShapeDtypeStruct((nh, bsz, t_len, HEAD_DIM), BF16),
        compiler_params=_params("parallel", "parallel", "parallel"),
    )(q, k)


def _flash_kernel(q_ref, bias_ref, k_ref, v_ref, *rest, group, tq, blk_len, gate_branch):
    if gate_branch is not None:
        gate_ref, o_ref, qa, ka, m_s, l_s, acc = rest
    else:
        gate_ref = None
        o_ref, qa, ka, m_s, l_s, acc = rest
    qi = pl.program_id(2)
    rows = group * tq
    bias = bias_ref[0, 0]
    for g in range(group):
        qg = q_ref[0, 0, :, g * HEAD_DIM:(g + 1) * HEAD_DIM].astype(F32) * SCALE
        qa[g * tq:(g + 1) * tq, 0:HEAD_DIM] = qg.astype(BF16)
        qa[g * tq:(g + 1) * tq, HEAD_DIM:2 * HEAD_DIM] = bias
    m_s[...] = jnp.full_like(m_s, NEG_INF)
    l_s[...] = jnp.zeros_like(l_s)
    acc[...] = jnp.zeros_like(acc)

    def step(ki, diagonal):
        start = pl.multiple_of(ki * tq, tq)
        ka[:, 0:HEAD_DIM] = k_ref[0, 0, pl.ds(start, tq), :]
        kpos = start + lax.broadcasted_iota(jnp.int32, (tq, HEAD_DIM), 0)
        kblk = lax.broadcasted_iota(jnp.int32, (tq, HEAD_DIM), 1)
        ka[:, HEAD_DIM:2 * HEAD_DIM] = jnp.where((kpos >> _log2(blk_len)) == kblk, 1.0, 0.0).astype(BF16)
        v = v_ref[0, 0, pl.ds(start, tq), :]
        s = _dot_t(qa[...], ka[...])
        if diagonal:
            qpos = lax.broadcasted_iota(jnp.int32, (rows, tq), 0) & (tq - 1)
            kcol = lax.broadcasted_iota(jnp.int32, (rows, tq), 1)
            s = jnp.where(kcol <= qpos, s, NEG_INF)
        m_old = m_s[...]
        m_new = jnp.maximum(m_old, jnp.max(s, axis=-1, keepdims=True))
        alpha = jnp.exp(m_old - m_new)
        p = jnp.exp(s - m_new)
        l_s[...] = alpha * l_s[...] + jnp.sum(p, axis=-1, keepdims=True)
        acc[...] = alpha * acc[...] + _dot(p.astype(BF16), v)
        m_s[...] = m_new

    def body(ki, carry):
        step(ki, False)
        return carry

    lax.fori_loop(0, qi, body, 0)
    step(qi, True)
    d = l_s[...]
    d = jnp.where(d > 0, d, 1.0)
    o = acc[...] / d
    outs = []
    for g in range(group):
        og = o[g * tq:(g + 1) * tq]
        if gate_ref is not None:
            gc = g * 3 + gate_branch
            og = og * gate_ref[0, 0, :, gc:gc + 1]
        outs.append(og)
    res = outs[0] if group == 1 else jnp.concatenate(outs, axis=-1)
    o_ref[0, 0] = res.astype(o_ref.dtype)


def _flash(q, bias, k, v, *, group, tq, blk_len, q_index, gate=None, gate_branch=None):
    nh, bsz, t_len, _ = k.shape
    gw = group * HEAD_DIM
    in_specs = [
        pl.BlockSpec((1, 1, tq, gw), q_index),
        pl.BlockSpec((1, 1, tq, HEAD_DIM), lambda h, b, i: (h, b, i, 0)),
        pl.BlockSpec((1, 1, t_len, HEAD_DIM), lambda h, b, i: (h, b, 0, 0)),
        pl.BlockSpec((1, 1, t_len, HEAD_DIM), lambda h, b, i: (h, b, 0, 0)),
    ]
    args = [q, bias, k, v]
    if gate is not None:
        in_specs.append(pl.BlockSpec((1, 1, tq, LANES), lambda h, b, i: (h, b, i, 0)))
        args.append(gate)
    rows = group * tq
    return pl.pallas_call(
        functools.partial(_flash_kernel, group=group, tq=tq, blk_len=blk_len, gate_branch=gate_branch),
        grid=(nh, bsz, t_len // tq),
        in_specs=in_specs,
        out_specs=pl.BlockSpec((1, 1, tq, gw), q_index),
        out_shape=jax.ShapeDtypeStruct(q.shape, BF16),
        scratch_shapes=[
            pltpu.VMEM((rows, 2 * HEAD_DIM), BF16),
            pltpu.VMEM((tq, 2 * HEAD_DIM), BF16),
            pltpu.VMEM((rows, 1), F32),
            pltpu.VMEM((rows, 1), F32),
            pltpu.VMEM((rows, HEAD_DIM), F32),
        ],
        compiler_params=_params("parallel", "parallel", "arbitrary"),
    )(*args)


def _even_out_kernel(h_ref, oa_ref, oc_ref, os_ref, ow_ref, w_ref, o_ref):
    na = oa_ref.shape[-1]
    ob = (oc_ref[...].astype(F32) + os_ref[...].astype(F32) + ow_ref[...].astype(F32)).astype(BF16)
    o_ref[...] = h_ref[...] + _dot(oa_ref[...], w_ref[0:na, :]) + _dot(ob, w_ref[na:, :])


def _even_out(h, oa, ocmp, osel, owin, w):
    m = h.shape[0]
    tm = PROJ_TM
    hspec = pl.BlockSpec((tm, D_MODEL), lambda i: (i, 0))
    aspec = pl.BlockSpec((tm, oa.shape[-1]), lambda i: (i, 0))
    return pl.pallas_call(
        _even_out_kernel,
        grid=(m // tm,),
        in_specs=[hspec, aspec, aspec, aspec, aspec, pl.BlockSpec(w.shape, lambda i: (0, 0))],
        out_specs=hspec,
        out_shape=jax.ShapeDtypeStruct((m, D_MODEL), F32),
        compiler_params=_params("parallel"),
    )(h, oa, ocmp, osel, owin, w)


def _odd_out_kernel(h_ref, o0_ref, o1_ref, o2_ref, l0_ref, l1_ref, l2_ref, od_ref, w_ref, o_ref):
    l0, l1, l2 = l0_ref[...], l1_ref[...], l2_ref[...]
    mx = jnp.maximum(jnp.maximum(l0, l1), l2)
    e0, e1, e2 = jnp.exp(l0 - mx), jnp.exp(l1 - mx), jnp.exp(l2 - mx)
    tot = e0 + e1 + e2
    oc = ((e0 / tot) * o0_ref[...].astype(F32) + (e1 / tot) * o1_ref[...].astype(F32)
          + (e2 / tot) * o2_ref[...].astype(F32))
    nc = o0_ref.shape[-1]
    out = h_ref[...] + _dot(oc.astype(BF16), w_ref[0:nc, :])
    for hh in range(od_ref.shape[0]):
        out = out + _dot(od_ref[hh], w_ref[nc + hh * HEAD_DIM: nc + (hh + 1) * HEAD_DIM, :])
    o_ref[...] = out


def _odd_out(h, outs, lses, od, w):
    m = h.shape[0]
    tm = PROJ_TM
    hspec = pl.BlockSpec((tm, D_MODEL), lambda i: (i, 0))
    cspec = pl.BlockSpec((tm, outs[0].shape[-1]), lambda i: (i, 0))
    return pl.pallas_call(
        _odd_out_kernel,
        grid=(m // tm,),
        in_specs=[hspec] + [cspec] * 6 + [
            pl.BlockSpec((od.shape[0], tm, HEAD_DIM), lambda i: (0, i, 0)),
            pl.BlockSpec(w.shape, lambda i: (0, 0)),
        ],
        out_specs=hspec,
        out_shape=jax.ShapeDtypeStruct((m, D_MODEL), F32),
        compiler_params=_params("parallel"),
    )(h, *outs, *lses, od, w)


def _col_ranges(sizes):
    offs, acc = [], 0
    for s in sizes:
        offs.append((acc, acc + s))
        acc += s
    return offs


def _even_mixer(h, gain, w_in, w_out, sinks, cmp_pe, cmp_w1, cmp_w2, cos, sin, bsz, t_len):
    m = bsz * t_len
    qa_w, kva_w = A_Q_HEADS * HEAD_DIM, A_KV_HEADS * HEAD_DIM
    qb_w, kvb_w = B_Q_HEADS * HEAD_DIM, B_KV_HEADS * HEAD_DIM
    sizes = [qa_w, kva_w, kva_w, qb_w] + [kvb_w] * 6 + [3 * B_Q_HEADS]
    (aq, ak, av, bq, bkc, bvc, bks, bvs, bkw, bvw, bg) = [w_in[:, a:b] for a, b in _col_ranges(sizes)]
    group = B_Q_HEADS // B_KV_HEADS
    gpad = jnp.zeros((D_MODEL, LANES - 3 * group), w_in.dtype)
    gates = [x for hk in range(B_KV_HEADS) for x in (bg[:, hk * 3 * group:(hk + 1) * 3 * group], gpad)]
    w = jnp.concatenate([aq, bq, ak, bkw, bks, bq, av, bvw, bvs, bkc, bvc] + gates, axis=1).astype(BF16)
    c = [0]

    def take(width):
        c[0] += width
        return c[0] - width

    plan = [
        (take(qa_w), qa_w, [(0, 0, qa_w, "rope", 0)]),
        (take(qb_w), qb_w, [(1, 0, qb_w, "rope", 0)]),
        (take(3 * kva_w), 3 * kva_w, [(2, 0, kva_w, "rope", 0), (3, kva_w, kvb_w, "rope", 0),
                                      (4, kva_w + kvb_w, kvb_w, "rope", B_KV_HEADS)]),
        (take(qb_w), qb_w, [(5, 0, qb_w, "none", 0)]),
        (take(5 * kvb_w), 5 * kvb_w, [(6, 0, kva_w, "none", 0), (7, kva_w, kvb_w, "none", 0),
                                      (8, 2 * kvb_w, kvb_w, "none", B_KV_HEADS),
                                      (9, 3 * kvb_w, kvb_w, "none", B_KV_HEADS),
                                      (10, 4 * kvb_w, kvb_w, "none", B_KV_HEADS)]),
        (take(B_KV_HEADS * LANES), B_KV_HEADS * LANES, [(11, 0, B_KV_HEADS * LANES, "sigmoid", B_KV_HEADS)]),
    ]
    split = ((B_KV_HEADS, HEAD_DIM), BF16)
    out_defs = [((qa_w,), BF16), ((qb_w,), BF16), ((kva_w,), BF16), ((kvb_w,), BF16), split,
                ((qb_w,), BF16), ((kva_w,), BF16), ((kvb_w,), BF16), split, split, split,
                ((B_KV_HEADS, LANES), F32)]
    (aq_r, bq_r, ak_r, bkw_r, bks_r, bq_u, av_, bvw_, bvs_, bkc_, bvc_, gate) = _proj(
        h, gain, w, cos, sin, plan, out_defs, t_len)

    def seq(x):
        return x.reshape(bsz, t_len, x.shape[-1])

    def heads(x):
        return x.reshape(x.shape[0], bsz, t_len, x.shape[-1])

    gate4 = heads(gate)
    oa = _banded(seq(aq_r), seq(ak_r), seq(av_), n_kv=A_KV_HEADS, group=A_Q_HEADS // A_KV_HEADS,
                 max_dist=A_WINDOW - 1, tl=BAND_TL, pk=128, span=128, sink=sinks)
    nch = t_len // NSA_CMP_STRIDE
    zk = bkc_.reshape(B_KV_HEADS, bsz, nch, NSA_CMP_STRIDE * HEAD_DIM)
    zv = bvc_.reshape(B_KV_HEADS, bsz, nch, NSA_CMP_STRIDE * HEAD_DIM)
    kc, vc = _compress(zk, zv, cmp_pe.reshape(2, 1, NSA_CMP_LEN * HEAD_DIM), cmp_w1.astype(BF16),
                       cmp_w2.astype(BF16))
    ocmp, bias = _cmp_attention(seq(bq_u), kc, vc, gate4, t_len)
    osel = _flash(seq(bq_r)[None], bias, heads(bks_r), heads(bvs_), group=group, tq=FLASH_T,
                  blk_len=NSA_SEL_LEN, q_index=lambda hk, b, i: (0, b, i, hk), gate=gate4, gate_branch=1)[0]
    owin = _banded(seq(bq_r), seq(bkw_r), seq(bvw_), n_kv=B_KV_HEADS, group=group,
                   max_dist=NSA_WINDOW - 1, tl=BAND_TL, pk=512, span=512, gate=gate4, gate_branch=2)
    return _even_out(h, oa.reshape(m, qa_w), ocmp.reshape(m, qb_w), osel.reshape(m, qb_w),
                     owin.reshape(m, qb_w), w_out.astype(BF16))


def _odd_mixer(h, gain, w_in, w_out, cos, sin, bsz, t_len):
    m = bsz * t_len
    cw = C_HEADS * HEAD_DIM
    dw = D_HEADS * HEAD_DIM
    gw = C_HEADS_PER_GROUP * HEAD_DIM
    plan = [
        (0, cw, [(0, 0, cw, "rope", 0)]),
        (cw, cw, [(1, 0, cw, "rope", 0)]),
        (2 * cw, cw, [(2, 0, cw, "none", 0)]),
        (3 * cw, 3 * dw, [(3, 0, dw, "rope", D_HEADS), (4, dw, dw, "rope", D_HEADS), (5, 2 * dw, dw, "none", D_HEADS)]),
    ]
    split = ((D_HEADS, HEAD_DIM), BF16)
    out_defs = [((cw,), BF16), ((cw,), BF16), ((cw,), BF16), split, split, split]
    cq, ck, cv, dq, dk, dv = _proj(h, gain, w_in.astype(BF16), cos, sin, plan, out_defs, t_len)
    outs, lses = [], []
    n_groups = len(C_GROUPS)
    for gi, (wlen, r) in enumerate(C_GROUPS):
        length = t_len // r
        view = lambda x: x.reshape(bsz, length, r * cw)
        col = lambda i, gi=gi: i * n_groups + gi
        o, lse = _banded(view(cq), view(ck), view(cv), n_kv=C_HEADS_PER_GROUP, group=1, max_dist=wlen // r,
                         tl=min(BAND_TL, length), pk=128, span=128, n_res=r, q_col=col, kv_col=col,
                         want_lse=True)
        outs.append(o.reshape(m, gw))
        lses.append(lse.reshape(m, gw))

    def heads(x):
        return x.reshape(x.shape[0], bsz, t_len, x.shape[-1])

    bias = _moba_gate(heads(dq), heads(dk))
    od = _flash(heads(dq), bias, heads(dk), heads(dv), group=1, tq=MOBA_T, blk_len=MOBA_BLOCK,
                q_index=lambda hh, b, i: (hh, b, i, 0))
    return _odd_out(h, outs, lses, od.reshape(D_HEADS, m, HEAD_DIM), w_out.astype(BF16))


def kernel(x, ffn_norm_pre, mix_norm, ffn_norm_post, ffn_wi, ffn_wo, even_w_in, even_w_out, even_sinks,
           nsa_cmp_pe, nsa_cmp_w1, nsa_cmp_w2, odd_w_in, odd_w_out, final_norm):
    bsz, t_len, _ = x.shape
    depth = ffn_wi.shape[0]
    cos, sin = _rope_tables(t_len)
    wi = ffn_wi.astype(BF16)
    wo = ffn_wo.astype(BF16)
    h = x.reshape(bsz * t_len, D_MODEL)
    for layer in range(depth):
        i = layer // 2
        h = _ffn(h, ffn_norm_pre[layer], wi[layer, 0], wo[layer, 0])
        if layer % 2 == 0:
            h = _even_mixer(h, mix_norm[layer], even_w_in[i], even_w_out[i], even_sinks[i], nsa_cmp_pe[i],
                            nsa_cmp_w1[i], nsa_cmp_w2[i], cos, sin, bsz, t_len)
        else:
            h = _odd_mixer(h, mix_norm[layer], odd_w_in[i], odd_w_out[i], cos, sin, bsz, t_len)
        last = layer == depth - 1
        h = _ffn(h, ffn_norm_post[layer], wi[layer, 1], wo[layer, 1], final_norm if last else None)
    return h.reshape(bsz, t_len, D_MODEL)
```

```python
import functools
import math

import jax
import jax.numpy as jnp
from jax import lax
from jax.experimental import pallas as pl
from jax.experimental.pallas import tpu as pltpu

D_MODEL = 1024
HEAD_DIM = 64
ROPE_THETA = 10000.0
NORM_EPS = 1e-6
D_FF = 2816
NEG_INF = -1e30
FORCE_SCORE = 1e4

A_Q_HEADS = 8
A_KV_HEADS = 2
A_WINDOW = 128
B_Q_HEADS = 8
B_KV_HEADS = 2
NSA_CMP_LEN = 32
NSA_CMP_STRIDE = 16
NSA_CMP_HIDDEN = 256
NSA_SEL_LEN = 64
NSA_SEL_COUNT = 8
NSA_WINDOW = 512
C_GROUPS = ((128, 1), (512, 4), (2048, 16))
C_HEADS_PER_GROUP = 4
C_HEADS = len(C_GROUPS) * C_HEADS_PER_GROUP
D_HEADS = 4
MOBA_BLOCK = 256
MOBA_TOPK = 3

LANES = 128
SCALE = HEAD_DIM ** -0.5
MASK_BIAS = -32768.0
REMOVED = -3e38
VMEM_LIMIT = 52 * 1024 * 1024

FFN_TM = 512
FFN_TF = 1408
PROJ_TM = 512
BAND_TL = 512
BAND_SB = 128
FLASH_T = 256
MOBA_T = 512
CMP_TQ = 256

BF16 = jnp.bfloat16
F32 = jnp.float32


def _params(*sem):
    return pltpu.CompilerParams(dimension_semantics=sem, vmem_limit_bytes=VMEM_LIMIT)


def _rms(x, g):
    return x * lax.rsqrt(jnp.mean(x * x, axis=-1, keepdims=True) + NORM_EPS) * g


def _log2(n):
    assert n & (n - 1) == 0
    return n.bit_length() - 1


def _dot(a, b):
    return jnp.dot(a, b, preferred_element_type=F32)


def _dot_t(a, b):
    return lax.dot_general(a, b, (((1,), (1,)), ((), ())), preferred_element_type=F32)


def _ffn_kernel(x_ref, g_ref, wg_ref, wu_ref, wo_ref, *rest, nf, final):
    if final:
        fg_ref, o_ref, n_scr, acc_scr = rest
    else:
        o_ref, n_scr, acc_scr = rest
    f = pl.program_id(1)

    @pl.when(f == 0)
    def _():
        n_scr[...] = _rms(x_ref[...], g_ref[...]).astype(BF16)
        acc_scr[...] = jnp.zeros_like(acc_scr)

    n = n_scr[...]
    gate = _dot(n, wg_ref[...])
    up = _dot(n, wu_ref[...])
    act = (gate * jax.nn.sigmoid(gate) * up).astype(BF16)
    acc_scr[...] += _dot(act, wo_ref[...])

    @pl.when(f == nf - 1)
    def _():
        h = x_ref[...] + 0.5 * acc_scr[...]
        if final:
            h = _rms(h, fg_ref[...])
        o_ref[...] = h


def _ffn(h, gain, wi, wo, final_gain=None):
    m = h.shape[0]
    tm, tf = FFN_TM, FFN_TF
    nf = D_FF // tf
    final = final_gain is not None
    in_specs = [
        pl.BlockSpec((tm, D_MODEL), lambda i, f: (i, 0)),
        pl.BlockSpec((1, D_MODEL), lambda i, f: (0, 0)),
        pl.BlockSpec((D_MODEL, tf), lambda i, f: (0, f)),
        pl.BlockSpec((D_MODEL, tf), lambda i, f: (0, nf + f)),
        pl.BlockSpec((tf, D_MODEL), lambda i, f: (f, 0)),
    ]
    args = [h, gain.reshape(1, D_MODEL), wi, wi, wo]
    if final:
        in_specs.append(pl.BlockSpec((1, D_MODEL), lambda i, f: (0, 0)))
        args.append(final_gain.reshape(1, D_MODEL))
    return pl.pallas_call(
        functools.partial(_ffn_kernel, nf=nf, final=final),
        grid=(m // tm, nf),
        in_specs=in_specs,
        out_specs=pl.BlockSpec((tm, D_MODEL), lambda i, f: (i, 0)),
        out_shape=jax.ShapeDtypeStruct((m, D_MODEL), F32),
        scratch_shapes=[pltpu.VMEM((tm, D_MODEL), BF16), pltpu.VMEM((tm, D_MODEL), F32)],
        compiler_params=_params("parallel", "arbitrary"),
    )(*args)


def _rope_tables(t):
    inv = 1.0 / (ROPE_THETA ** (jnp.arange(0, HEAD_DIM, 2, dtype=F32) / HEAD_DIM))
    ang = jnp.arange(t, dtype=F32)[:, None] * inv[None, :]
    cos = jnp.cos(ang)
    sin = jnp.sin(ang)
    return (jnp.concatenate([cos, cos, cos, cos], axis=-1),
            jnp.concatenate([-sin, sin, -sin, sin], axis=-1))


def _proj_kernel(x_ref, g_ref, w_ref, cos_ref, sin_ref, *out_refs, plan):
    n = _rms(x_ref[...], g_ref[...]).astype(BF16)
    cos = cos_ref[...]
    sin = sin_ref[...]
    lane = lax.broadcasted_iota(jnp.int32, cos.shape, 1)
    first_half = (lane & (HEAD_DIM - 1)) < HEAD_DIM // 2
    for off, width, outs in plan:
        y = _dot(n, w_ref[:, off:off + width])
        for o_idx, sub_off, sub_w, mode, n_split in outs:
            o_ref = out_refs[o_idx]
            for c in range(sub_w // LANES):
                piece = y[:, sub_off + c * LANES: sub_off + (c + 1) * LANES]
                if mode == "rope":
                    rot = jnp.where(first_half, pltpu.roll(piece, LANES - HEAD_DIM // 2, 1),
                                    pltpu.roll(piece, HEAD_DIM // 2, 1))
                    piece = piece * cos + rot * sin
                elif mode == "sigmoid":
                    piece = jax.nn.sigmoid(piece)
                piece = piece.astype(o_ref.dtype)
                if n_split == 0:
                    o_ref[:, c * LANES:(c + 1) * LANES] = piece
                else:
                    hw = sub_w // n_split
                    per = LANES // hw
                    for s in range(per):
                        o_ref[c * per + s] = piece[:, s * hw:(s + 1) * hw]


def _proj(h, gain, w, cos, sin, plan, out_defs, t_len):
    m = h.shape[0]
    tm = PROJ_TM
    nt = t_len // tm
    out_shapes, out_specs = [], []
    for tail, dt in out_defs:
        if len(tail) == 1:
            out_shapes.append(jax.ShapeDtypeStruct((m, tail[0]), dt))
            out_specs.append(pl.BlockSpec((tm, tail[0]), lambda i: (i, 0)))
        else:
            out_shapes.append(jax.ShapeDtypeStruct((tail[0], m, tail[1]), dt))
            out_specs.append(pl.BlockSpec((tail[0], tm, tail[1]), lambda i: (0, i, 0)))
    return pl.pallas_call(
        functools.partial(_proj_kernel, plan=plan),
        grid=(m // tm,),
        in_specs=[
            pl.BlockSpec((tm, D_MODEL), lambda i: (i, 0)),
            pl.BlockSpec((1, D_MODEL), lambda i: (0, 0)),
            pl.BlockSpec(w.shape, lambda i: (0, 0)),
            pl.BlockSpec((tm, LANES), lambda i: (i % nt, 0)),
            pl.BlockSpec((tm, LANES), lambda i: (i % nt, 0)),
        ],
        out_specs=out_specs,
        out_shape=out_shapes,
        compiler_params=_params("parallel"),
    )(h, gain.reshape(1, D_MODEL), w, cos, sin)


def _banded_kernel(q_ref, kp_ref, kc_ref, vp_ref, vc_ref, *rest, n_kv, group, tl, pk, span, max_dist,
                   has_sink, gate_branch, want_lse):
    rest = list(rest)
    sink_ref = rest.pop(0) if has_sink else None
    gate_ref = rest.pop(0) if gate_branch is not None else None
    o_ref = rest.pop(0)
    lse_ref = rest.pop(0) if want_lse else None
    kcat, vcat = rest
    sb = BAND_SB
    t0 = pl.program_id(2) * tl
    kcat[0:pk, :] = kp_ref[0]
    kcat[pk:, :] = kc_ref[0]
    vcat[0:pk, :] = vp_ref[0]
    vcat[pk:, :] = vc_ref[0]
    nk = span + sb
    row = lax.broadcasted_iota(jnp.int32, (sb, nk), 0)
    col = lax.broadcasted_iota(jnp.int32, (sb, nk), 1)
    dist = row + span - col
    in_band = (dist >= 0) & (dist <= max_dist)
    for j in range(tl // sb):
        lo = pk + j * sb - span
        ok = in_band & (col >= span - j * sb - t0)
        outs, lses = [], []
        for h in range(n_kv):
            kk = kcat[lo:lo + nk, h * HEAD_DIM:(h + 1) * HEAD_DIM]
            vv = vcat[lo:lo + nk, h * HEAD_DIM:(h + 1) * HEAD_DIM]
            for g in range(group):
                hq = h * group + g
                qg = q_ref[0, j * sb:(j + 1) * sb, hq * HEAD_DIM:(hq + 1) * HEAD_DIM]
                s = _dot_t(qg, kk) * SCALE
                s = jnp.where(ok, s, NEG_INF)
                m = jnp.max(s, axis=-1, keepdims=True)
                if has_sink:
                    sink = sink_ref[hq]
                    m = jnp.maximum(m, sink)
                p = jnp.exp(s - m)
                d = jnp.sum(p, axis=-1, keepdims=True)
                if has_sink:
                    d = d + jnp.exp(sink - m)
                d = jnp.where(d > 0, d, 1.0)
                o = _dot(p.astype(BF16), vv) / d
                if gate_ref is not None:
                    gc = g * 3 + gate_branch
                    o = o * gate_ref[h, 0, j * sb:(j + 1) * sb, gc:gc + 1]
                outs.append(o)
                if want_lse:
                    lses.append(jnp.broadcast_to(m + jnp.log(d), (sb, HEAD_DIM)))
        o_ref[0, j * sb:(j + 1) * sb, :] = jnp.concatenate(outs, axis=-1).astype(o_ref.dtype)
        if want_lse:
            lse_ref[0, j * sb:(j + 1) * sb, :] = jnp.concatenate(lses, axis=-1)


def _banded(q, k, v, *, n_kv, group, max_dist, tl, pk, span, n_res=1, q_col=None, kv_col=None,
            sink=None, gate=None, gate_branch=None, want_lse=False):
    bsz, length, _ = q.shape
    qw = n_kv * group * HEAD_DIM
    kw = n_kv * HEAD_DIM
    q_col = q_col or (lambda i: i)
    kv_col = kv_col or (lambda i: i)
    ratio = tl // pk
    prev = lambda b, i, l: (b, jnp.maximum(l * ratio - 1, 0), kv_col(i))
    cur = lambda b, i, l: (b, l, kv_col(i))
    in_specs = [
        pl.BlockSpec((1, tl, qw), lambda b, i, l: (b, l, q_col(i))),
        pl.BlockSpec((1, pk, kw), prev),
        pl.BlockSpec((1, tl, kw), cur),
        pl.BlockSpec((1, pk, kw), prev),
        pl.BlockSpec((1, tl, kw), cur),
    ]
    args = [q, k, k, v, v]
    if sink is not None:
        in_specs.append(pl.BlockSpec(memory_space=pltpu.SMEM))
        args.append(sink)
    if gate is not None:
        in_specs.append(pl.BlockSpec((n_kv, 1, tl, LANES), lambda b, i, l: (0, b, l, 0)))
        args.append(gate)
    out_spec = pl.BlockSpec((1, tl, qw), lambda b, i, l: (b, l, i))
    out_shape = [jax.ShapeDtypeStruct((bsz, length, n_res * qw), BF16)]
    out_specs = [out_spec]
    if want_lse:
        out_shape.append(jax.ShapeDtypeStruct((bsz, length, n_res * qw), F32))
        out_specs.append(out_spec)
    res = pl.pallas_call(
        functools.partial(_banded_kernel, n_kv=n_kv, group=group, tl=tl, pk=pk, span=span,
                          max_dist=max_dist, has_sink=sink is not None, gate_branch=gate_branch,
                          want_lse=want_lse),
        grid=(bsz, n_res, length // tl),
        in_specs=in_specs,
        out_specs=out_specs,
        out_shape=out_shape,
        scratch_shapes=[pltpu.VMEM((pk + tl, kw), BF16), pltpu.VMEM((pk + tl, kw), BF16)],
        compiler_params=_params("parallel", "parallel", "parallel"),
    )(*args)
    return res if want_lse else res[0]


def _compress_one(z_ref, pe_ref, w1_ref, w2_ref, o_ref):
    half = NSA_CMP_STRIDE * HEAD_DIM
    z = z_ref[0, 0].astype(F32)
    za = (z + pe_ref[0, :, 0:half]).astype(BF16)
    zb = (z + pe_ref[0, :, half:2 * half]).astype(BF16)
    ya = _dot(za, w1_ref[0, 0:half, :])
    yb = _dot(zb, w1_ref[0, half:2 * half, :])
    nrow = ya.shape[0]
    hid = jax.nn.gelu(ya + pltpu.roll(yb, nrow - 1, 0))
    o_ref[0, 0] = _dot(hid.astype(BF16), w2_ref[0]).astype(o_ref.dtype)


def _compress_kernel(zk_ref, zv_ref, pek_ref, pev_ref, w1k_ref, w1v_ref, w2k_ref, w2v_ref, ok_ref, ov_ref):
    _compress_one(zk_ref, pek_ref, w1k_ref, w2k_ref, ok_ref)
    _compress_one(zv_ref, pev_ref, w1v_ref, w2v_ref, ov_ref)


def _compress(zk, zv, pe, w1, w2):
    hk, bsz, nch, width = zk.shape
    zspec = pl.BlockSpec((1, 1, nch, width), lambda h, b: (h, b, 0, 0))
    ospec = pl.BlockSpec((1, 1, nch, HEAD_DIM), lambda h, b: (h, b, 0, 0))
    oshape = jax.ShapeDtypeStruct((hk, bsz, nch, HEAD_DIM), BF16)

    def wspec(arr, which):
        return pl.BlockSpec((1,) + arr.shape[1:], lambda h, b: (which, 0, 0))

    return pl.pallas_call(
        _compress_kernel,
        grid=(hk, bsz),
        in_specs=[zspec, zspec, wspec(pe, 0), wspec(pe, 1), wspec(w1, 0), wspec(w1, 1), wspec(w2, 0), wspec(w2, 1)],
        out_specs=[ospec, ospec],
        out_shape=[oshape, oshape],
        compiler_params=_params("parallel", "parallel"),
    )(zk, zv, pe, pe, w1, w1, w2, w2)


def _topk_select(score, k, colf):
    ncol = score.shape[-1]
    rem = score
    sel = jnp.zeros_like(score)
    for _ in range(k):
        mx = jnp.max(rem, axis=-1, keepdims=True)
        first = jnp.min(jnp.where(rem == mx, colf, float(ncol)), axis=-1, keepdims=True)
        pick = colf == first
        sel = jnp.where(pick, jnp.where(mx > 0.5 * NEG_INF, 1.0, 0.0), sel)
        rem = jnp.where(pick, REMOVED, rem)
    return sel


def _cmp_kernel(q_ref, kc_ref, vc_ref, gate_ref, o_ref, bias_ref, *, tq, group):
    qi = pl.program_id(2)
    kc = kc_ref[0, 0]
    vc = vc_ref[0, 0]
    ncp = kc.shape[0]
    nsb = bias_ref.shape[-1]
    t = qi * tq + lax.broadcasted_iota(jnp.int32, (tq, ncp), 0)
    n = lax.broadcasted_iota(jnp.int32, (tq, ncp), 1)
    visible = n * NSA_CMP_STRIDE + (NSA_CMP_LEN - 1) <= t
    psum = jnp.zeros((tq, ncp), F32)
    outs = []
    for g in range(group):
        qg = q_ref[0, :, g * HEAD_DIM:(g + 1) * HEAD_DIM]
        s = _dot_t(qg, kc) * SCALE
        s = jnp.where(visible, s, NEG_INF)
        m = jnp.max(s, axis=-1, keepdims=True)
        p = jnp.where(visible, jnp.exp(s - m), 0.0)
        d = jnp.sum(p, axis=-1, keepdims=True)
        d = jnp.where(d > 0, d, 1.0)
        pc = p / d
        psum = psum + pc
        o = _dot(pc.astype(BF16), vc)
        outs.append(o * gate_ref[0, 0, :, g * 3:g * 3 + 1])
    o_ref[0] = jnp.concatenate(outs, axis=-1).astype(o_ref.dtype)

    nn = lax.broadcasted_iota(jnp.int32, (ncp, nsb), 0) * NSA_CMP_STRIDE
    jj = lax.broadcasted_iota(jnp.int32, (ncp, nsb), 1) * NSA_SEL_LEN
    ov = jnp.maximum(jnp.minimum(nn + NSA_CMP_LEN, jj + NSA_SEL_LEN) - jnp.maximum(nn, jj), 0)
    ov = (ov.astype(F32) / NSA_CMP_LEN).astype(BF16)
    p_hi = psum.astype(BF16)
    p_lo = (psum - p_hi.astype(F32)).astype(BF16)
    imp = _dot(p_hi, ov) + _dot(p_lo, ov)

    tt = qi * tq + lax.broadcasted_iota(jnp.int32, (tq, nsb), 0)
    j = lax.broadcasted_iota(jnp.int32, (tq, nsb), 1)
    cb = tt >> _log2(NSA_SEL_LEN)
    forced = (j == 0) | (j == cb) | (j == cb - 1)
    score = jnp.where(j <= cb, jnp.where(forced, FORCE_SCORE, imp), NEG_INF)
    sel = _topk_select(score, min(NSA_SEL_COUNT, nsb), j.astype(F32))
    bias_ref[0, 0] = jnp.where(sel > 0.5, 0.0, MASK_BIAS).astype(bias_ref.dtype)


def _cmp_attention(qu, kc, vc, gate, t_len):
    bsz = qu.shape[0]
    hk, _, ncp, _ = kc.shape
    group = B_Q_HEADS // B_KV_HEADS
    tq = CMP_TQ
    nsb = t_len // NSA_SEL_LEN
    gw = group * HEAD_DIM
    cspec = pl.BlockSpec((1, 1, ncp, HEAD_DIM), lambda b, h, i: (h, b, 0, 0))
    return pl.pallas_call(
        functools.partial(_cmp_kernel, tq=tq, group=group),
        grid=(bsz, hk, t_len // tq),
        in_specs=[
            pl.BlockSpec((1, tq, gw), lambda b, h, i: (b, i, h)),
            cspec, cspec,
            pl.BlockSpec((1, 1, tq, LANES), lambda b, h, i: (h, b, i, 0)),
        ],
        out_specs=[
            pl.BlockSpec((1, tq, gw), lambda b, h, i: (b, i, h)),
            pl.BlockSpec((1, 1, tq, nsb), lambda b, h, i: (h, b, i, 0)),
        ],
        out_shape=[
            jax.ShapeDtypeStruct((bsz, t_len, hk * gw), BF16),
            jax.ShapeDtypeStruct((hk, bsz, t_len, nsb), BF16),
        ],
        compiler_params=_params("parallel", "parallel", "parallel"),
    )(qu, kc, vc, gate)


def _moba_gate_kernel(q_ref, k_ref, bias_ref, *, tq):
    qi = pl.program_id(2)
    k = k_ref[0, 0]
    t_len = k.shape[0]
    ncol = bias_ref.shape[-1]
    blk = lax.broadcasted_iota(jnp.int32, (ncol, t_len), 0)
    pos = lax.broadcasted_iota(jnp.int32, (ncol, t_len), 1)
    member = jnp.where((pos >> _log2(MOBA_BLOCK)) == blk, 1.0, 0.0).astype(BF16)
    kmean = _dot(member, k) * (1.0 / MOBA_BLOCK)
    km_hi = kmean.astype(BF16)
    km_lo = (kmean - km_hi.astype(F32)).astype(BF16)
    q = q_ref[0, 0]
    gate = _dot_t(q, km_hi) + _dot_t(q, km_lo)
    t = qi * tq + lax.broadcasted_iota(jnp.int32, (tq, ncol), 0)
    j = lax.broadcasted_iota(jnp.int32, (tq, ncol), 1)
    cb = t >> _log2(MOBA_BLOCK)
    nb = t_len // MOBA_BLOCK
    score = jnp.where(j < cb, gate, jnp.where(j < nb, NEG_INF, REMOVED))
    sel = _topk_select(score, min(MOBA_TOPK, nb - 1), j.astype(F32))
    keep = (sel > 0.5) | (j == cb)
    bias_ref[0, 0] = jnp.where(keep, 0.0, MASK_BIAS).astype(bias_ref.dtype)


def _moba_gate(q, k):
    nh, bsz, t_len, _ = q.shape
    tq = MOBA_T
    return pl.pallas_call(
        functools.partial(_moba_gate_kernel, tq=tq),
        grid=(nh, bsz, t_len // tq),
        in_specs=[
            pl.BlockSpec((1, 1, tq, HEAD_DIM), lambda h, b, i: (h, b, i, 0)),
            pl.BlockSpec((1, 1, t_len, HEAD_DIM), lambda h, b, i: (h, b, 0, 0)),
        ],
        out_specs=pl.BlockSpec((1, 1, tq, HEAD_DIM), lambda h, b, i: (h, b, i, 0)),
        out_shape=jax.ShapeDtypeStruct((nh, bsz, t_len, HEAD_DIM), BF16),
        compiler_params=_params("parallel", "parallel", "parallel"),
    )(q, k)


def _flash_kernel(q_ref, bias_ref, k_ref, vt_ref, *rest, group, tq, blk_len, gate_branch):
    if gate_branch is not None:
        gate_ref, o_ref, qa, ka, m_s, l_s, acc = rest
    else:
        gate_ref = None
        o_ref, qa, ka, m_s, l_s, acc = rest
    qi = pl.program_id(2)
    rows = group * tq
    t_len = k_ref.shape[2]

    @pl.when(qi == 0)
    def _():
        ka[:, 0:HEAD_DIM] = k_ref[0, 0]
        kpos = lax.broadcasted_iota(jnp.int32, (t_len, HEAD_DIM), 0)
        kblk = lax.broadcasted_iota(jnp.int32, (t_len, HEAD_DIM), 1)
        ka[:, HEAD_DIM:2 * HEAD_DIM] = jnp.where((kpos >> _log2(blk_len)) == kblk, 1.0, 0.0).astype(BF16)

    bias = bias_ref[0, 0]
    for g in range(group):
        qg = q_ref[0, 0, :, g * HEAD_DIM:(g + 1) * HEAD_DIM].astype(F32) * SCALE
        qa[g * tq:(g + 1) * tq, 0:HEAD_DIM] = qg.astype(BF16)
        qa[g * tq:(g + 1) * tq, HEAD_DIM:2 * HEAD_DIM] = bias
    m_s[...] = jnp.full_like(m_s, NEG_INF)
    l_s[...] = jnp.zeros_like(l_s)
    acc[...] = jnp.zeros_like(acc)

    def step(ki, diagonal):
        start = pl.multiple_of(ki * tq, tq)
        s = _dot_t(ka[pl.ds(start, tq), :], qa[...])
        if diagonal:
            kpos = lax.broadcasted_iota(jnp.int32, (tq, rows), 0)
            qpos = lax.broadcasted_iota(jnp.int32, (tq, rows), 1) & (tq - 1)
            s = jnp.where(kpos <= qpos, s, NEG_INF)
        m_old = m_s[...]
        m_new = jnp.maximum(m_old, jnp.max(s, axis=0, keepdims=True))
        alpha = jnp.exp(m_old - m_new)
        p = jnp.exp(s - m_new)
        l_s[...] = alpha * l_s[...] + jnp.sum(p, axis=0, keepdims=True)
        acc[...] = alpha * acc[...] + _dot(vt_ref[0, 0, ki], p.astype(BF16))
        m_s[...] = m_new

    def body(ki, carry):
        step(ki, False)
        return carry

    lax.fori_loop(0, qi, body, 0)
    step(qi, True)
    d = l_s[...]
    d = jnp.where(d > 0, d, 1.0)
    o_t = acc[...] / d
    outs = []
    for g in range(group):
        og = o_t[:, g * tq:(g + 1) * tq].T
        if gate_ref is not None:
            gc = g * 3 + gate_branch
            og = og * gate_ref[0, 0, :, gc:gc + 1]
        outs.append(og)
    res = outs[0] if group == 1 else jnp.concatenate(outs, axis=-1)
    o_ref[0, 0] = res.astype(o_ref.dtype)


def _flash(q, bias, k, v, *, group, tq, blk_len, q_index, gate=None, gate_branch=None):
    nh, bsz, t_len, _ = k.shape
    gw = group * HEAD_DIM
    nt = t_len // tq
    vt = v.reshape(nh, bsz, nt, tq, HEAD_DIM).transpose(0, 1, 2, 4, 3)
    in_specs = [
        pl.BlockSpec((1, 1, tq, gw), q_index),
        pl.BlockSpec((1, 1, tq, HEAD_DIM), lambda h, b, i: (h, b, i, 0)),
        pl.BlockSpec((1, 1, t_len, HEAD_DIM), lambda h, b, i: (h, b, 0, 0)),
        pl.BlockSpec((1, 1, nt, HEAD_DIM, tq), lambda h, b, i: (h, b, 0, 0, 0)),
    ]
    args = [q, bias, k, vt]
    if gate is not None:
        in_specs.append(pl.BlockSpec((1, 1, tq, LANES), lambda h, b, i: (h, b, i, 0)))
        args.append(gate)
    rows = group * tq
    return pl.pallas_call(
        functools.partial(_flash_kernel, group=group, tq=tq, blk_len=blk_len, gate_branch=gate_branch),
        grid=(nh, bsz, t_len // tq),
        in_specs=in_specs,
        out_specs=pl.BlockSpec((1, 1, tq, gw), q_index),
        out_shape=jax.ShapeDtypeStruct(q.shape, BF16),
        scratch_shapes=[
            pltpu.VMEM((rows, 2 * HEAD_DIM), BF16),
            pltpu.VMEM((t_len, 2 * HEAD_DIM), BF16),
            pltpu.VMEM((1, rows), F32),
            pltpu.VMEM((1, rows), F32),
            pltpu.VMEM((HEAD_DIM, rows), F32),
        ],
        compiler_params=_params("parallel", "parallel", "arbitrary"),
    )(*args)


def _even_out_kernel(h_ref, oa_ref, oc_ref, os_ref, ow_ref, w_ref, o_ref):
    na = oa_ref.shape[-1]
    ob = (oc_ref[...].astype(F32) + os_ref[...].astype(F32) + ow_ref[...].astype(F32)).astype(BF16)
    o_ref[...] = h_ref[...] + _dot(oa_ref[...], w_ref[0:na, :]) + _dot(ob, w_ref[na:, :])


def _even_out(h, oa, ocmp, osel, owin, w):
    m = h.shape[0]
    tm = PROJ_TM
    hspec = pl.BlockSpec((tm, D_MODEL), lambda i: (i, 0))
    aspec = pl.BlockSpec((tm, oa.shape[-1]), lambda i: (i, 0))
    return pl.pallas_call(
        _even_out_kernel,
        grid=(m // tm,),
        in_specs=[hspec, aspec, aspec, aspec, aspec, pl.BlockSpec(w.shape, lambda i: (0, 0))],
        out_specs=hspec,
        out_shape=jax.ShapeDtypeStruct((m, D_MODEL), F32),
        compiler_params=_params("parallel"),
    )(h, oa, ocmp, osel, owin, w)


def _odd_out_kernel(h_ref, o0_ref, o1_ref, o2_ref, l0_ref, l1_ref, l2_ref, od_ref, w_ref, o_ref):
    l0, l1, l2 = l0_ref[...], l1_ref[...], l2_ref[...]
    mx = jnp.maximum(jnp.maximum(l0, l1), l2)
    e0, e1, e2 = jnp.exp(l0 - mx), jnp.exp(l1 - mx), jnp.exp(l2 - mx)
    tot = e0 + e1 + e2
    oc = ((e0 / tot) * o0_ref[...].astype(F32) + (e1 / tot) * o1_ref[...].astype(F32)
          + (e2 / tot) * o2_ref[...].astype(F32))
    nc = o0_ref.shape[-1]
    out = h_ref[...] + _dot(oc.astype(BF16), w_ref[0:nc, :])
    for hh in range(od_ref.shape[0]):
        out = out + _dot(od_ref[hh], w_ref[nc + hh * HEAD_DIM: nc + (hh + 1) * HEAD_DIM, :])
    o_ref[...] = out


def _odd_out(h, outs, lses, od, w):
    m = h.shape[0]
    tm = PROJ_TM
    hspec = pl.BlockSpec((tm, D_MODEL), lambda i: (i, 0))
    cspec = pl.BlockSpec((tm, outs[0].shape[-1]), lambda i: (i, 0))
    return pl.pallas_call(
        _odd_out_kernel,
        grid=(m // tm,),
        in_specs=[hspec] + [cspec] * 6 + [
            pl.BlockSpec((od.shape[0], tm, HEAD_DIM), lambda i: (0, i, 0)),
            pl.BlockSpec(w.shape, lambda i: (0, 0)),
        ],
        out_specs=hspec,
        out_shape=jax.ShapeDtypeStruct((m, D_MODEL), F32),
        compiler_params=_params("parallel"),
    )(h, *outs, *lses, od, w)


def _col_ranges(sizes):
    offs, acc = [], 0
    for s in sizes:
        offs.append((acc, acc + s))
        acc += s
    return offs


def _even_mixer(h, gain, w_in, w_out, sinks, cmp_pe, cmp_w1, cmp_w2, cos, sin, bsz, t_len):
    m = bsz * t_len
    qa_w, kva_w = A_Q_HEADS * HEAD_DIM, A_KV_HEADS * HEAD_DIM
    qb_w, kvb_w = B_Q_HEADS * HEAD_DIM, B_KV_HEADS * HEAD_DIM
    sizes = [qa_w, kva_w, kva_w, qb_w] + [kvb_w] * 6 + [3 * B_Q_HEADS]
    (aq, ak, av, bq, bkc, bvc, bks, bvs, bkw, bvw, bg) = [w_in[:, a:b] for a, b in _col_ranges(sizes)]
    group = B_Q_HEADS // B_KV_HEADS
    gpad = jnp.zeros((D_MODEL, LANES - 3 * group), w_in.dtype)
    gates = [x for hk in range(B_KV_HEADS) for x in (bg[:, hk * 3 * group:(hk + 1) * 3 * group], gpad)]
    w = jnp.concatenate([aq, bq, ak, bkw, bks, bq, av, bvw, bvs, bkc, bvc] + gates, axis=1).astype(BF16)
    c = [0]

    def take(width):
        c[0] += width
        return c[0] - width

    plan = [
        (take(qa_w), qa_w, [(0, 0, qa_w, "rope", 0)]),
        (take(qb_w), qb_w, [(1, 0, qb_w, "rope", 0)]),
        (take(3 * kva_w), 3 * kva_w, [(2, 0, kva_w, "rope", 0), (3, kva_w, kvb_w, "rope", 0),
                                      (4, kva_w + kvb_w, kvb_w, "rope", B_KV_HEADS)]),
        (take(qb_w), qb_w, [(5, 0, qb_w, "none", 0)]),
        (take(5 * kvb_w), 5 * kvb_w, [(6, 0, kva_w, "none", 0), (7, kva_w, kvb_w, "none", 0),
                                      (8, 2 * kvb_w, kvb_w, "none", B_KV_HEADS),
                                      (9, 3 * kvb_w, kvb_w, "none", B_KV_HEADS),
                                      (10, 4 * kvb_w, kvb_w, "none", B_KV_HEADS)]),
        (take(B_KV_HEADS * LANES), B_KV_HEADS * LANES, [(11, 0, B_KV_HEADS * LANES, "sigmoid", B_KV_HEADS)]),
    ]
    split = ((B_KV_HEADS, HEAD_DIM), BF16)
    out_defs = [((qa_w,), BF16), ((qb_w,), BF16), ((kva_w,), BF16), ((kvb_w,), BF16), split,
                ((qb_w,), BF16), ((kva_w,), BF16), ((kvb_w,), BF16), split, split, split,
                ((B_KV_HEADS, LANES), F32)]
    (aq_r, bq_r, ak_r, bkw_r, bks_r, bq_u, av_, bvw_, bvs_, bkc_, bvc_, gate) = _proj(
        h, gain, w, cos, sin, plan, out_defs, t_len)

    def seq(x):
        return x.reshape(bsz, t_len, x.shape[-1])

    def heads(x):
        return x.reshape(x.shape[0], bsz, t_len, x.shape[-1])

    gate4 = heads(gate)
    oa = _banded(seq(aq_r), seq(ak_r), seq(av_), n_kv=A_KV_HEADS, group=A_Q_HEADS // A_KV_HEADS,
                 max_dist=A_WINDOW - 1, tl=BAND_TL, pk=128, span=128, sink=sinks)
    nch = t_len // NSA_CMP_STRIDE
    zk = bkc_.reshape(B_KV_HEADS, bsz, nch, NSA_CMP_STRIDE * HEAD_DIM)
    zv = bvc_.reshape(B_KV_HEADS, bsz, nch, NSA_CMP_STRIDE * HEAD_DIM)
    kc, vc = _compress(zk, zv, cmp_pe.reshape(2, 1, NSA_CMP_LEN * HEAD_DIM), cmp_w1.astype(BF16),
                       cmp_w2.astype(BF16))
    ocmp, bias = _cmp_attention(seq(bq_u), kc, vc, gate4, t_len)
    osel = _flash(seq(bq_r)[None], bias, heads(bks_r), heads(bvs_), group=group, tq=FLASH_T,
                  blk_len=NSA_SEL_LEN, q_index=lambda hk, b, i: (0, b, i, hk), gate=gate4, gate_branch=1)[0]
    owin = _banded(seq(bq_r), seq(bkw_r), seq(bvw_), n_kv=B_KV_HEADS, group=group,
                   max_dist=NSA_WINDOW - 1, tl=BAND_TL, pk=512, span=512, gate=gate4, gate_branch=2)
    return _even_out(h, oa.reshape(m, qa_w), ocmp.reshape(m, qb_w), osel.reshape(m, qb_w),
                     owin.reshape(m, qb_w), w_out.astype(BF16))


def _odd_mixer(h, gain, w_in, w_out, cos, sin, bsz, t_len):
    m = bsz * t_len
    cw = C_HEADS * HEAD_DIM
    dw = D_HEADS * HEAD_DIM
    gw = C_HEADS_PER_GROUP * HEAD_DIM
    plan = [
        (0, cw, [(0, 0, cw, "rope", 0)]),
        (cw, cw, [(1, 0, cw, "rope", 0)]),
        (2 * cw, cw, [(2, 0, cw, "none", 0)]),
        (3 * cw, 3 * dw, [(3, 0, dw, "rope", D_HEADS), (4, dw, dw, "rope", D_HEADS), (5, 2 * dw, dw, "none", D_HEADS)]),
    ]
    split = ((D_HEADS, HEAD_DIM), BF16)
    out_defs = [((cw,), BF16), ((cw,), BF16), ((cw,), BF16), split, split, split]
    cq, ck, cv, dq, dk, dv = _proj(h, gain, w_in.astype(BF16), cos, sin, plan, out_defs, t_len)
    outs, lses = [], []
    n_groups = len(C_GROUPS)
    for gi, (wlen, r) in enumerate(C_GROUPS):
        length = t_len // r
        view = lambda x: x.reshape(bsz, length, r * cw)
        col = lambda i, gi=gi: i * n_groups + gi
        o, lse = _banded(view(cq), view(ck), view(cv), n_kv=C_HEADS_PER_GROUP, group=1, max_dist=wlen // r,
                         tl=min(BAND_TL, length), pk=128, span=128, n_res=r, q_col=col, kv_col=col,
                         want_lse=True)
        outs.append(o.reshape(m, gw))
        lses.append(lse.reshape(m, gw))

    def heads(x):
        return x.reshape(x.shape[0], bsz, t_len, x.shape[-1])

    bias = _moba_gate(heads(dq), heads(dk))
    od = _flash(heads(dq), bias, heads(dk), heads(dv), group=1, tq=MOBA_T, blk_len=MOBA_BLOCK,
                q_index=lambda hh, b, i: (hh, b, i, 0))
    return _odd_out(h, outs, lses, od.reshape(D_HEADS, m, HEAD_DIM), w_out.astype(BF16))


def kernel(x, ffn_norm_pre, mix_norm, ffn_norm_post, ffn_wi, ffn_wo, even_w_in, even_w_out, even_sinks,
           nsa_cmp_pe, nsa_cmp_w1, nsa_cmp_w2, odd_w_in, odd_w_out, final_norm):
    bsz, t_len, _ = x.shape
    depth = ffn_wi.shape[0]
    cos, sin = _rope_tables(t_len)
    wi = ffn_wi.astype(BF16)
    wo = ffn_wo.astype(BF16)
    h = x.reshape(bsz * t_len, D_MODEL)
    for layer in range(depth):
        i = layer // 2
        h = _ffn(h, ffn_norm_pre[layer], wi[layer, 0], wo[layer, 0])
        if layer % 2 == 0:
            h = _even_mixer(h, mix_norm[layer], even_w_in[i], even_w_out[i], even_sinks[i], nsa_cmp_pe[i],
                            nsa_cmp_w1[i], nsa_cmp_w2[i], cos, sin, bsz, t_len)
        else:
            h = _odd_mixer(h, mix_norm[layer], odd_w_in[i], odd_w_out[i], cos, sin, bsz, t_len)
        last = layer == depth - 1
        h = _ffn(h, ffn_norm_post[layer], wi[layer, 1], wo[layer, 1], final_norm if last else None)
    return h.reshape(bsz, t_len, D_MODEL)
```

```python
import functools

import jax
import jax.numpy as jnp
from jax import lax
from jax.experimental import pallas as pl
from jax.experimental.pallas import tpu as pltpu

D_MODEL = 1024
HEAD_DIM = 64
ROPE_THETA = 10000.0
NORM_EPS = 1e-6
D_FF = 2816
NEG_INF = -1e30
FORCE_SCORE = 1e4

A_Q_HEADS = 8
A_KV_HEADS = 2
A_WINDOW = 128
B_Q_HEADS = 8
B_KV_HEADS = 2
NSA_CMP_LEN = 32
NSA_CMP_STRIDE = 16
NSA_CMP_HIDDEN = 256
NSA_SEL_LEN = 64
NSA_SEL_COUNT = 8
NSA_WINDOW = 512
C_GROUPS = ((128, 1), (512, 4), (2048, 16))
C_HEADS_PER_GROUP = 4
C_HEADS = len(C_GROUPS) * C_HEADS_PER_GROUP
D_HEADS = 4
MOBA_BLOCK = 256
MOBA_TOPK = 3

LANES = 128
SCALE = HEAD_DIM ** -0.5
MASK_BIAS = -32768.0
REMOVED = -3e38
VMEM_LIMIT = 52 * 1024 * 1024

FFN_TM = 512
FFN_TF = 1408
PROJ_TM = 512
BAND_TL = 512
BAND_SB = 128
FLASH_T = 256
FLASH_TK = 256
MOBA_T = 512
MOBA_TK = 512
CMP_TQ = 256

BF16 = jnp.bfloat16
F32 = jnp.float32


def _params(*sem):
    return pltpu.CompilerParams(dimension_semantics=sem, vmem_limit_bytes=VMEM_LIMIT)


def _rms(x, g):
    return x * lax.rsqrt(jnp.mean(x * x, axis=-1, keepdims=True) + NORM_EPS) * g


def _log2(n):
    assert n & (n - 1) == 0
    return n.bit_length() - 1


def _dot(a, b):
    return jnp.dot(a, b, preferred_element_type=F32)


def _dot_t(a, b):
    return lax.dot_general(a, b, (((1,), (1,)), ((), ())), preferred_element_type=F32)


def _dot_tn(a, b):
    return lax.dot_general(a, b, (((0,), (0,)), ((), ())), preferred_element_type=F32)


def _ffn_kernel(x_ref, g_ref, wg_ref, wu_ref, wo_ref, *rest, nf, final):
    if final:
        fg_ref, o_ref, n_scr, acc_scr = rest
    else:
        o_ref, n_scr, acc_scr = rest
    f = pl.program_id(1)

    @pl.when(f == 0)
    def _():
        n_scr[...] = _rms(x_ref[...], g_ref[...]).astype(BF16)
        acc_scr[...] = jnp.zeros_like(acc_scr)

    n = n_scr[...]
    gate = _dot(n, wg_ref[...])
    up = _dot(n, wu_ref[...])
    act = (gate * jax.nn.sigmoid(gate) * up).astype(BF16)
    acc_scr[...] += _dot(act, wo_ref[...])

    @pl.when(f == nf - 1)
    def _():
        h = x_ref[...] + 0.5 * acc_scr[...]
        if final:
            h = _rms(h, fg_ref[...])
        o_ref[...] = h


def _ffn(h, gain, wi, wo, final_gain=None):
    m = h.shape[0]
    tm, tf = FFN_TM, FFN_TF
    nf = D_FF // tf
    final = final_gain is not None
    in_specs = [
        pl.BlockSpec((tm, D_MODEL), lambda i, f: (i, 0)),
        pl.BlockSpec((1, D_MODEL), lambda i, f: (0, 0)),
        pl.BlockSpec((D_MODEL, tf), lambda i, f: (0, f)),
        pl.BlockSpec((D_MODEL, tf), lambda i, f: (0, nf + f)),
        pl.BlockSpec((tf, D_MODEL), lambda i, f: (f, 0)),
    ]
    args = [h, gain.reshape(1, D_MODEL), wi, wi, wo]
    if final:
        in_specs.append(pl.BlockSpec((1, D_MODEL), lambda i, f: (0, 0)))
        args.append(final_gain.reshape(1, D_MODEL))
    return pl.pallas_call(
        functools.partial(_ffn_kernel, nf=nf, final=final),
        grid=(m // tm, nf),
        in_specs=in_specs,
        out_specs=pl.BlockSpec((tm, D_MODEL), lambda i, f: (i, 0)),
        out_shape=jax.ShapeDtypeStruct((m, D_MODEL), F32),
        scratch_shapes=[pltpu.VMEM((tm, D_MODEL), BF16), pltpu.VMEM((tm, D_MODEL), F32)],
        compiler_params=_params("parallel", "arbitrary"),
    )(*args)


def _rope_tables(t):
    inv = 1.0 / (ROPE_THETA ** (jnp.arange(0, HEAD_DIM, 2, dtype=F32) / HEAD_DIM))
    ang = jnp.arange(t, dtype=F32)[:, None] * inv[None, :]
    cos = jnp.cos(ang)
    sin = jnp.sin(ang)
    return (jnp.concatenate([cos, cos, cos, cos], axis=-1),
            jnp.concatenate([-sin, sin, -sin, sin], axis=-1))


def _proj_kernel(x_ref, g_ref, w_ref, cos_ref, sin_ref, *rest, plan, n_out):
    out_refs, (ys,) = rest[:n_out], rest[n_out:]
    n = _rms(x_ref[...], g_ref[...]).astype(BF16)
    tm = n.shape[0]
    cos = cos_ref[...]
    sin = sin_ref[...]
    lane = lax.broadcasted_iota(jnp.int32, cos.shape, 1)
    first_half = (lane & (HEAD_DIM - 1)) < HEAD_DIM // 2
    heads_per_tile = LANES // HEAD_DIM
    for off, width, outs in plan:
        y = _dot(n, w_ref[:, off:off + width])
        for o in outs:
            o_ref = out_refs[o["idx"]]
            dt = o_ref.dtype
            kind = o["kind"]
            for c in range(o["width"] // LANES):
                piece = y[:, o["off"] + c * LANES: o["off"] + (c + 1) * LANES]
                if o.get("rope"):
                    rot = jnp.where(first_half, pltpu.roll(piece, LANES - HEAD_DIM // 2, 1),
                                    pltpu.roll(piece, HEAD_DIM // 2, 1))
                    piece = piece * cos + rot * sin
                if o.get("scale"):
                    piece = piece * SCALE
                if o.get("sigmoid"):
                    piece = jax.nn.sigmoid(piece)
                lanes = slice(c * LANES, (c + 1) * LANES)
                if kind == "flat":
                    o_ref[:, lanes] = piece.astype(dt)
                elif kind == "split":
                    hw = o["hw"]
                    per = LANES // hw
                    for k in range(per):
                        o_ref[c * per + k] = piece[:, k * hw:(k + 1) * hw].astype(dt)
                elif kind == "dilate":
                    r = o["r"]
                    if r == 1:
                        o_ref[0, 0, :, lanes] = piece.astype(dt)
                    else:
                        ys[...] = piece
                        for i in range(r):
                            o_ref[0, i, :, lanes] = ys[pl.ds(i, tm // r, stride=r), :].astype(dt)
                elif kind == "chunk":
                    ys[...] = piece
                    for l in range(NSA_CMP_STRIDE):
                        rows = ys[pl.ds(l, tm // NSA_CMP_STRIDE, stride=NSA_CMP_STRIDE), :].astype(dt)
                        for k in range(heads_per_tile):
                            o_ref[c * heads_per_tile + k, :, l * HEAD_DIM:(l + 1) * HEAD_DIM] = rows[
                                :, k * HEAD_DIM:(k + 1) * HEAD_DIM]
                else:
                    tile = o["tile"]
                    pt = piece.T
                    for k in range(heads_per_tile):
                        for kt in range(tm // tile):
                            o_ref[c * heads_per_tile + k, 0, kt] = pt[
                                k * HEAD_DIM:(k + 1) * HEAD_DIM, kt * tile:(kt + 1) * tile].astype(dt)


def _proj(h, gain, w, cos, sin, plan, out_defs, bsz, t_len):
    m = h.shape[0]
    tm = PROJ_TM
    nt = t_len // tm
    out_shapes, out_specs = [], []
    for d in out_defs:
        kind, dt = d[0], d[-1]
        if kind == "flat":
            shape, block, index = (m, d[1]), (tm, d[1]), (lambda i: (i, 0))
        elif kind == "split":
            shape, block, index = (d[1], m, d[2]), (d[1], tm, d[2]), (lambda i: (0, i, 0))
        elif kind == "dilate":
            r = d[1]
            shape, block = (bsz, r, t_len // r, d[2]), (1, r, tm // r, d[2])
            index = lambda i: (i // nt, 0, i % nt, 0)
        elif kind == "chunk":
            width = NSA_CMP_STRIDE * HEAD_DIM
            shape, block = (d[1], m // NSA_CMP_STRIDE, width), (d[1], tm // NSA_CMP_STRIDE, width)
            index = lambda i: (0, i, 0)
        else:
            tile = d[2]
            shape, block = (d[1], bsz, t_len // tile, HEAD_DIM, tile), (d[1], 1, tm // tile, HEAD_DIM, tile)
            index = lambda i: (0, i // nt, i % nt, 0, 0)
        out_shapes.append(jax.ShapeDtypeStruct(shape, dt))
        out_specs.append(pl.BlockSpec(block, index))
    return pl.pallas_call(
        functools.partial(_proj_kernel, plan=plan, n_out=len(out_defs)),
        grid=(m // tm,),
        in_specs=[
            pl.BlockSpec((tm, D_MODEL), lambda i: (i, 0)),
            pl.BlockSpec((1, D_MODEL), lambda i: (0, 0)),
            pl.BlockSpec(w.shape, lambda i: (0, 0)),
            pl.BlockSpec((tm, LANES), lambda i: (i % nt, 0)),
            pl.BlockSpec((tm, LANES), lambda i: (i % nt, 0)),
        ],
        out_specs=out_specs,
        out_shape=out_shapes,
        scratch_shapes=[pltpu.VMEM((tm, LANES), F32)],
        compiler_params=_params("parallel"),
    )(h, gain.reshape(1, D_MODEL), w, cos, sin)


def _banded_kernel(q_ref, kp_ref, kc_ref, vp_ref, vc_ref, *rest, n_kv, group, tl, pk, span, max_dist,
                   has_sink, gate_branch, want_lse):
    rest = list(rest)
    sink_ref = rest.pop(0) if has_sink else None
    gate_ref = rest.pop(0) if gate_branch is not None else None
    o_ref = rest.pop(0)
    lse_ref = rest.pop(0) if want_lse else None
    kcat, vcat = rest
    sb = BAND_SB
    t0 = pl.program_id(2) * tl
    kcat[0:pk, :] = kp_ref[0, 0]
    kcat[pk:, :] = kc_ref[0, 0]
    vcat[0:pk, :] = vp_ref[0, 0]
    vcat[pk:, :] = vc_ref[0, 0]
    nk = span + sb
    rows = group * sb
    key = lax.broadcasted_iota(jnp.int32, (nk, rows), 0)
    qry = lax.broadcasted_iota(jnp.int32, (nk, rows), 1) & (sb - 1)
    dist = qry + span - key
    in_band = (dist >= 0) & (dist <= max_dist)
    lane_head = lax.broadcasted_iota(jnp.int32, (1, rows), 1) >> _log2(sb)
    for j in range(tl // sb):
        lo = pk + j * sb - span
        qrows = slice(j * sb, (j + 1) * sb)
        ok = in_band & (key >= span - j * sb - t0)
        outs, lses = [], []
        for h in range(n_kv):
            kk = kcat[lo:lo + nk, h * HEAD_DIM:(h + 1) * HEAD_DIM]
            vv = vcat[lo:lo + nk, h * HEAD_DIM:(h + 1) * HEAD_DIM]
            qs = [q_ref[0, 0, qrows, (h * group + g) * HEAD_DIM:(h * group + g + 1) * HEAD_DIM]
                  for g in range(group)]
            qs = qs[0] if group == 1 else jnp.concatenate(qs, axis=0)
            s = jnp.where(ok, _dot_t(kk, qs), NEG_INF)
            m = jnp.max(s, axis=0, keepdims=True)
            if has_sink:
                sink = jnp.full((1, rows), sink_ref[h * group], F32)
                for g in range(1, group):
                    sink = jnp.where(lane_head == g, sink_ref[h * group + g], sink)
                m = jnp.maximum(m, sink)
            p = jnp.exp(s - m)
            d = jnp.sum(p, axis=0, keepdims=True)
            if has_sink:
                d = d + jnp.exp(sink - m)
            d = jnp.where(d > 0, d, 1.0)
            o_t = _dot_tn(vv, p.astype(BF16)) / d
            if want_lse:
                lse_t = jnp.broadcast_to(m + jnp.log(d), (HEAD_DIM, rows))
            for g in range(group):
                o = o_t[:, g * sb:(g + 1) * sb].T
                if gate_ref is not None:
                    gc = g * 3 + gate_branch
                    o = o * gate_ref[h, 0, qrows, gc:gc + 1]
                outs.append(o)
                if want_lse:
                    lses.append(lse_t[:, g * sb:(g + 1) * sb].T)
        o_ref[0, 0, qrows, :] = jnp.concatenate(outs, axis=-1).astype(o_ref.dtype)
        if want_lse:
            lse_ref[0, 0, qrows, :] = jnp.concatenate(lses, axis=-1)


def _banded(q, k, v, *, n_kv, group, max_dist, tl, pk, span, sink=None, gate=None, gate_branch=None,
            want_lse=False):
    bsz, n_res, length, qw = q.shape
    kw = n_kv * HEAD_DIM
    ratio = tl // pk
    prev = lambda b, i, l: (b, i, jnp.maximum(l * ratio - 1, 0), 0)
    cur = lambda b, i, l: (b, i, l, 0)
    in_specs = [
        pl.BlockSpec((1, 1, tl, qw), cur),
        pl.BlockSpec((1, 1, pk, kw), prev),
        pl.BlockSpec((1, 1, tl, kw), cur),
        pl.BlockSpec((1, 1, pk, kw), prev),
        pl.BlockSpec((1, 1, tl, kw), cur),
    ]
    args = [q, k, k, v, v]
    if sink is not None:
        in_specs.append(pl.BlockSpec(memory_space=pltpu.SMEM))
        args.append(sink)
    if gate is not None:
        in_specs.append(pl.BlockSpec((n_kv, 1, tl, LANES), lambda b, i, l: (0, b, l, 0)))
        args.append(gate)
    out_spec = pl.BlockSpec((1, 1, tl, qw), cur)
    out_shape = [jax.ShapeDtypeStruct(q.shape, BF16)]
    out_specs = [out_spec]
    if want_lse:
        out_shape.append(jax.ShapeDtypeStruct(q.shape, F32))
        out_specs.append(out_spec)
    res = pl.pallas_call(
        functools.partial(_banded_kernel, n_kv=n_kv, group=group, tl=tl, pk=pk, span=span,
                          max_dist=max_dist, has_sink=sink is not None, gate_branch=gate_branch,
                          want_lse=want_lse),
        grid=(bsz, n_res, length // tl),
        in_specs=in_specs,
        out_specs=out_specs,
        out_shape=out_shape,
        scratch_shapes=[pltpu.VMEM((pk + tl, kw), BF16), pltpu.VMEM((pk + tl, kw), BF16)],
        compiler_params=_params("parallel", "parallel", "parallel"),
    )(*args)
    return res if want_lse else res[0]


def _compress_one(z_ref, pe_ref, w1_ref, w2_ref, o_ref):
    half = NSA_CMP_STRIDE * HEAD_DIM
    z = z_ref[0, 0].astype(F32)
    za = (z + pe_ref[0, :, 0:half]).astype(BF16)
    zb = (z + pe_ref[0, :, half:2 * half]).astype(BF16)
    ya = _dot(za, w1_ref[0, 0:half, :])
    yb = _dot(zb, w1_ref[0, half:2 * half, :])
    nrow = ya.shape[0]
    hid = jax.nn.gelu(ya + pltpu.roll(yb, nrow - 1, 0))
    o_ref[0, 0] = _dot(hid.astype(BF16), w2_ref[0]).astype(o_ref.dtype)


def _compress_kernel(zk_ref, zv_ref, pek_ref, pev_ref, w1k_ref, w1v_ref, w2k_ref, w2v_ref, ok_ref, ov_ref):
    _compress_one(zk_ref, pek_ref, w1k_ref, w2k_ref, ok_ref)
    _compress_one(zv_ref, pev_ref, w1v_ref, w2v_ref, ov_ref)


def _compress(zk, zv, pe, w1, w2):
    hk, bsz, nch, width = zk.shape
    zspec = pl.BlockSpec((1, 1, nch, width), lambda h, b: (h, b, 0, 0))
    ospec = pl.BlockSpec((1, 1, nch, HEAD_DIM), lambda h, b: (h, b, 0, 0))
    oshape = jax.ShapeDtypeStruct((hk, bsz, nch, HEAD_DIM), BF16)

    def wspec(arr, which):
        return pl.BlockSpec((1,) + arr.shape[1:], lambda h, b: (which, 0, 0))

    return pl.pallas_call(
        _compress_kernel,
        grid=(hk, bsz),
        in_specs=[zspec, zspec, wspec(pe, 0), wspec(pe, 1), wspec(w1, 0), wspec(w1, 1), wspec(w2, 0), wspec(w2, 1)],
        out_specs=[ospec, ospec],
        out_shape=[oshape, oshape],
        compiler_params=_params("parallel", "parallel"),
    )(zk, zv, pe, pe, w1, w1, w2, w2)


def _topk_select(score, k, idxf):
    ncand = score.shape[0]
    rem = score
    sel = jnp.zeros_like(score)
    for _ in range(k):
        mx = jnp.max(rem, axis=0, keepdims=True)
        first = jnp.min(jnp.where(rem == mx, idxf, float(ncand)), axis=0, keepdims=True)
        pick = idxf == first
        sel = jnp.where(pick, jnp.where(mx > 0.5 * NEG_INF, 1.0, 0.0), sel)
        rem = jnp.where(pick, REMOVED, rem)
    return sel


def _cmp_kernel(q_ref, kc_ref, vc_ref, gate_ref, o_ref, bias_ref, *, tq, group):
    qi = pl.program_id(2)
    kc = kc_ref[0, 0]
    vc = vc_ref[0, 0]
    ncp = kc.shape[0]
    nsb = bias_ref.shape[-1]
    rows = group * tq
    n = lax.broadcasted_iota(jnp.int32, (ncp, rows), 0)
    t = qi * tq + (lax.broadcasted_iota(jnp.int32, (ncp, rows), 1) & (tq - 1))
    visible = n * NSA_CMP_STRIDE + (NSA_CMP_LEN - 1) <= t
    qs = jnp.concatenate([q_ref[0, :, g * HEAD_DIM:(g + 1) * HEAD_DIM] for g in range(group)], axis=0)
    s = jnp.where(visible, _dot_t(kc, qs), NEG_INF)
    m = jnp.max(s, axis=0, keepdims=True)
    p = jnp.where(visible, jnp.exp(s - m), 0.0)
    d = jnp.sum(p, axis=0, keepdims=True)
    d = jnp.where(d > 0, d, 1.0)
    pc = p / d
    o_t = _dot_tn(vc, pc.astype(BF16))
    outs = []
    psum = pc[:, 0:tq]
    for g in range(group):
        if g:
            psum = psum + pc[:, g * tq:(g + 1) * tq]
        outs.append(o_t[:, g * tq:(g + 1) * tq].T * gate_ref[0, 0, :, g * 3:g * 3 + 1])
    o_ref[0] = jnp.concatenate(outs, axis=-1).astype(o_ref.dtype)

    jj = lax.broadcasted_iota(jnp.int32, (nsb, ncp), 0) * NSA_SEL_LEN
    nn = lax.broadcasted_iota(jnp.int32, (nsb, ncp), 1) * NSA_CMP_STRIDE
    ov = jnp.maximum(jnp.minimum(nn + NSA_CMP_LEN, jj + NSA_SEL_LEN) - jnp.maximum(nn, jj), 0)
    ov = (ov.astype(F32) / NSA_CMP_LEN).astype(BF16)
    p_hi = psum.astype(BF16)
    p_lo = (psum - p_hi.astype(F32)).astype(BF16)
    imp = _dot(ov, p_hi) + _dot(ov, p_lo)

    j = lax.broadcasted_iota(jnp.int32, (nsb, tq), 0)
    tt = qi * tq + lax.broadcasted_iota(jnp.int32, (nsb, tq), 1)
    cb = tt >> _log2(NSA_SEL_LEN)
    forced = (j == 0) | (j == cb) | (j == cb - 1)
    score = jnp.where(j <= cb, jnp.where(forced, FORCE_SCORE, imp), NEG_INF)
    sel = _topk_select(score, min(NSA_SEL_COUNT, nsb), j.astype(F32))
    bias_ref[0, 0] = jnp.where(sel > 0.5, 0.0, MASK_BIAS).T.astype(bias_ref.dtype)


def _cmp_attention(qu, kc, vc, gate, t_len):
    bsz = qu.shape[0]
    hk, _, ncp, _ = kc.shape
    group = B_Q_HEADS // B_KV_HEADS
    tq = CMP_TQ
    nsb = t_len // NSA_SEL_LEN
    gw = group * HEAD_DIM
    cspec = pl.BlockSpec((1, 1, ncp, HEAD_DIM), lambda b, h, i: (h, b, 0, 0))
    return pl.pallas_call(
        functools.partial(_cmp_kernel, tq=tq, group=group),
        grid=(bsz, hk, t_len // tq),
        in_specs=[
            pl.BlockSpec((1, tq, gw), lambda b, h, i: (b, i, h)),
            cspec, cspec,
            pl.BlockSpec((1, 1, tq, LANES), lambda b, h, i: (h, b, i, 0)),
        ],
        out_specs=[
            pl.BlockSpec((1, tq, gw), lambda b, h, i: (b, i, h)),
            pl.BlockSpec((1, 1, tq, nsb), lambda b, h, i: (h, b, i, 0)),
        ],
        out_shape=[
            jax.ShapeDtypeStruct((bsz, t_len, hk * gw), BF16),
            jax.ShapeDtypeStruct((hk, bsz, t_len, nsb), BF16),
        ],
        compiler_params=_params("parallel", "parallel", "parallel"),
    )(qu, kc, vc, gate)


def _moba_gate_kernel(q_ref, k_ref, bias_ref, km_hi, km_lo, *, tq):
    qi = pl.program_id(2)
    t_len = k_ref.shape[2]
    ncol = bias_ref.shape[-1]
    nb = t_len // MOBA_BLOCK

    @pl.when(qi == 0)
    def _():
        blk = lax.broadcasted_iota(jnp.int32, (ncol, t_len), 0)
        pos = lax.broadcasted_iota(jnp.int32, (ncol, t_len), 1)
        member = jnp.where((pos >> _log2(MOBA_BLOCK)) == blk, 1.0, 0.0).astype(BF16)
        kmean = _dot(member, k_ref[0, 0]) * (1.0 / MOBA_BLOCK)
        hi = kmean.astype(BF16)
        km_hi[...] = hi
        km_lo[...] = (kmean - hi.astype(F32)).astype(BF16)

    q = q_ref[0, 0]
    gate = _dot_t(km_hi[...], q) + _dot_t(km_lo[...], q)
    j = lax.broadcasted_iota(jnp.int32, (ncol, tq), 0)
    t = qi * tq + lax.broadcasted_iota(jnp.int32, (ncol, tq), 1)
    cb = t >> _log2(MOBA_BLOCK)
    score = jnp.where(j < cb, gate, jnp.where(j < nb, NEG_INF, REMOVED))
    sel = _topk_select(score, min(MOBA_TOPK, nb - 1), j.astype(F32))
    keep = (sel > 0.5) | (j == cb)
    bias_ref[0, 0] = jnp.where(keep, 0.0, MASK_BIAS).T.astype(bias_ref.dtype)


def _moba_gate(q, k):
    nh, bsz, t_len, _ = q.shape
    tq = MOBA_T
    return pl.pallas_call(
        functools.partial(_moba_gate_kernel, tq=tq),
        grid=(nh, bsz, t_len // tq),
        in_specs=[
            pl.BlockSpec((1, 1, tq, HEAD_DIM), lambda h, b, i: (h, b, i, 0)),
            pl.BlockSpec((1, 1, t_len, HEAD_DIM), lambda h, b, i: (h, b, 0, 0)),
        ],
        out_specs=pl.BlockSpec((1, 1, tq, HEAD_DIM), lambda h, b, i: (h, b, i, 0)),
        out_shape=jax.ShapeDtypeStruct((nh, bsz, t_len, HEAD_DIM), BF16),
        scratch_shapes=[pltpu.VMEM((HEAD_DIM, HEAD_DIM), BF16), pltpu.VMEM((HEAD_DIM, HEAD_DIM), BF16)],
        compiler_params=_params("parallel", "parallel", "arbitrary"),
    )(q, k)


def _flash_kernel(q_ref, bias_ref, k_ref, vt_ref, *rest, group, tq, tk, blk_len, q_scale, gate_branch):
    if gate_branch is not None:
        gate_ref, o_ref, qa, ka, m_s, l_s, acc, s_a, s_b = rest
    else:
        gate_ref = None
        o_ref, qa, ka, m_s, l_s, acc, s_a, s_b = rest
    qi = pl.program_id(2)
    rows = group * tq
    t_len = k_ref.shape[2]

    @pl.when(qi == 0)
    def _():
        ka[:, 0:HEAD_DIM] = k_ref[0, 0]
        kpos = lax.broadcasted_iota(jnp.int32, (t_len, HEAD_DIM), 0)
        kblk = lax.broadcasted_iota(jnp.int32, (t_len, HEAD_DIM), 1)
        ka[:, HEAD_DIM:2 * HEAD_DIM] = jnp.where((kpos >> _log2(blk_len)) == kblk, 1.0, 0.0).astype(BF16)

    bias = bias_ref[0, 0]
    for g in range(group):
        qg = q_ref[0, 0, :, g * HEAD_DIM:(g + 1) * HEAD_DIM]
        if q_scale != 1.0:
            qg = (qg.astype(F32) * q_scale).astype(BF16)
        qa[g * tq:(g + 1) * tq, 0:HEAD_DIM] = qg
        qa[g * tq:(g + 1) * tq, HEAD_DIM:2 * HEAD_DIM] = bias
    m_s[...] = jnp.full_like(m_s, NEG_INF)
    l_s[...] = jnp.zeros_like(l_s)
    acc[...] = jnp.zeros_like(acc)

    def scores(ki, s_ref):
        start = pl.multiple_of(ki * tk, tk)
        s_ref[...] = _dot_t(ka[pl.ds(start, tk), :], qa[...])

    def softmax_pv(ki, s_ref, masked):
        s = s_ref[...]
        if masked:
            kpos = ki * tk + lax.broadcasted_iota(jnp.int32, (tk, rows), 0)
            qpos = qi * tq + (lax.broadcasted_iota(jnp.int32, (tk, rows), 1) & (tq - 1))
            s = jnp.where(kpos <= qpos, s, NEG_INF)
        m_old = m_s[...]
        m_new = jnp.maximum(m_old, jnp.max(s, axis=0, keepdims=True))
        alpha = jnp.exp(m_old - m_new)
        p = jnp.exp(s - m_new)
        l_s[...] = alpha * l_s[...] + jnp.sum(p, axis=0, keepdims=True)
        acc[...] = alpha * acc[...] + _dot(vt_ref[0, 0, ki], p.astype(BF16))
        m_s[...] = m_new

    n_full = (qi * tq) >> _log2(tk)
    scores(0, s_a)

    def pair(j, carry):
        k0 = 2 * j
        scores(k0 + 1, s_b)
        softmax_pv(k0, s_a, False)
        scores(k0 + 2, s_a)
        softmax_pv(k0 + 1, s_b, False)
        return carry

    lax.fori_loop(0, n_full >> 1, pair, 0)
    cur = (n_full >> 1) << 1

    @pl.when((n_full & 1) == 1)
    def _():
        scores(cur + 1, s_b)
        softmax_pv(cur, s_a, False)
        softmax_pv(cur + 1, s_b, True)

    @pl.when((n_full & 1) == 0)
    def _():
        softmax_pv(cur, s_a, True)

    d = l_s[...]
    d = jnp.where(d > 0, d, 1.0)
    o_t = acc[...] / d
    outs = []
    for g in range(group):
        og = o_t[:, g * tq:(g + 1) * tq].T
        if gate_ref is not None:
            gc = g * 3 + gate_branch
            og = og * gate_ref[0, 0, :, gc:gc + 1]
        outs.append(og)
    res = outs[0] if group == 1 else jnp.concatenate(outs, axis=-1)
    o_ref[0, 0] = res.astype(o_ref.dtype)


def _flash(q, bias, k, vt, *, group, tq, blk_len, q_scale, q_index, gate=None, gate_branch=None):
    nh, bsz, t_len, _ = k.shape
    gw = group * HEAD_DIM
    nt, tk = vt.shape[2], vt.shape[4]
    in_specs = [
        pl.BlockSpec((1, 1, tq, gw), q_index),
        pl.BlockSpec((1, 1, tq, HEAD_DIM), lambda h, b, i: (h, b, i, 0)),
        pl.BlockSpec((1, 1, t_len, HEAD_DIM), lambda h, b, i: (h, b, 0, 0)),
        pl.BlockSpec((1, 1, nt, HEAD_DIM, tk), lambda h, b, i: (h, b, 0, 0, 0)),
    ]
    args = [q, bias, k, vt]
    if gate is not None:
        in_specs.append(pl.BlockSpec((1, 1, tq, LANES), lambda h, b, i: (h, b, i, 0)))
        args.append(gate)
    rows = group * tq
    return pl.pallas_call(
        functools.partial(_flash_kernel, group=group, tq=tq, tk=tk, blk_len=blk_len, q_scale=q_scale,
                          gate_branch=gate_branch),
        grid=(nh, bsz, t_len // tq),
        in_specs=in_specs,
        out_specs=pl.BlockSpec((1, 1, tq, gw), q_index),
        out_shape=jax.ShapeDtypeStruct(q.shape, BF16),
        scratch_shapes=[
            pltpu.VMEM((rows, 2 * HEAD_DIM), BF16),
            pltpu.VMEM((t_len, 2 * HEAD_DIM), BF16),
            pltpu.VMEM((1, rows), F32),
            pltpu.VMEM((1, rows), F32),
            pltpu.VMEM((HEAD_DIM, rows), F32),
            pltpu.VMEM((tk, rows), F32),
            pltpu.VMEM((tk, rows), F32),
        ],
        compiler_params=_params("parallel", "parallel", "arbitrary"),
    )(*args)


def _even_out_kernel(h_ref, oa_ref, oc_ref, os_ref, ow_ref, w_ref, o_ref):
    na = oa_ref.shape[-1]
    ob = (oc_ref[...].astype(F32) + os_ref[...].astype(F32) + ow_ref[...].astype(F32)).astype(BF16)
    o_ref[...] = h_ref[...] + _dot(oa_ref[...], w_ref[0:na, :]) + _dot(ob, w_ref[na:, :])


def _even_out(h, oa, ocmp, osel, owin, w):
    m = h.shape[0]
    tm = PROJ_TM
    hspec = pl.BlockSpec((tm, D_MODEL), lambda i: (i, 0))
    aspec = pl.BlockSpec((tm, oa.shape[-1]), lambda i: (i, 0))
    return pl.pallas_call(
        _even_out_kernel,
        grid=(m // tm,),
        in_specs=[hspec, aspec, aspec, aspec, aspec, pl.BlockSpec(w.shape, lambda i: (0, 0))],
        out_specs=hspec,
        out_shape=jax.ShapeDtypeStruct((m, D_MODEL), F32),
        compiler_params=_params("parallel"),
    )(h, oa, ocmp, osel, owin, w)


def _odd_out_kernel(h_ref, *rest, dilations):
    ng = len(dilations)
    o_refs, l_refs = rest[:ng], rest[ng:2 * ng]
    od_ref, w_ref, out_ref = rest[2 * ng:2 * ng + 3]
    scratch = list(rest[2 * ng + 3:])
    tm = h_ref.shape[0]

    def tokens(ref, r):
        if r == 1:
            return ref[0, 0].astype(F32)
        pieces = []
        for c in range(ref.shape[-1] // LANES):
            scr = scratch.pop(0)
            for i in range(r):
                scr[pl.ds(i, tm // r, stride=r), :] = ref[0, i, :, c * LANES:(c + 1) * LANES].astype(F32)
            pieces.append(scr[...])
        return jnp.concatenate(pieces, axis=-1)

    outs = [tokens(ref, r) for ref, r in zip(o_refs, dilations)]
    lses = [tokens(ref, r) for ref, r in zip(l_refs, dilations)]
    mx = functools.reduce(jnp.maximum, lses)
    es = [jnp.exp(l - mx) for l in lses]
    tot = functools.reduce(lambda a, b: a + b, es)
    oc = functools.reduce(lambda a, b: a + b, [(e / tot) * o for e, o in zip(es, outs)])
    nc = oc.shape[-1]
    out = h_ref[...] + _dot(oc.astype(BF16), w_ref[0:nc, :])
    for hh in range(od_ref.shape[0]):
        out = out + _dot(od_ref[hh], w_ref[nc + hh * HEAD_DIM: nc + (hh + 1) * HEAD_DIM, :])
    out_ref[...] = out


def _odd_out(h, outs, lses, od, w, t_len):
    m = h.shape[0]
    tm = PROJ_TM
    nt = t_len // tm
    gw = outs[0].shape[-1]
    dilations = tuple(o.shape[1] for o in outs)
    hspec = pl.BlockSpec((tm, D_MODEL), lambda i: (i, 0))
    gspecs = [pl.BlockSpec((1, r, tm // r, gw), lambda i: (i // nt, 0, i % nt, 0)) for r in dilations]
    n_scr = 2 * (gw // LANES) * sum(1 for r in dilations if r > 1)
    return pl.pallas_call(
        functools.partial(_odd_out_kernel, dilations=dilations),
        grid=(m // tm,),
        in_specs=[hspec] + gspecs + gspecs + [
            pl.BlockSpec((od.shape[0], tm, HEAD_DIM), lambda i: (0, i, 0)),
            pl.BlockSpec(w.shape, lambda i: (0, 0)),
        ],
        out_specs=hspec,
        out_shape=jax.ShapeDtypeStruct((m, D_MODEL), F32),
        scratch_shapes=[pltpu.VMEM((tm, LANES), F32)] * n_scr,
        compiler_params=_params("parallel"),
    )(h, *outs, *lses, od, w)


def _col_ranges(sizes):
    offs, acc = [], 0
    for s in sizes:
        offs.append((acc, acc + s))
        acc += s
    return offs


def _even_mixer(h, gain, w_in, w_out, sinks, cmp_pe, cmp_w1, cmp_w2, cos, sin, bsz, t_len):
    m = bsz * t_len
    qa_w, kva_w = A_Q_HEADS * HEAD_DIM, A_KV_HEADS * HEAD_DIM
    qb_w, kvb_w = B_Q_HEADS * HEAD_DIM, B_KV_HEADS * HEAD_DIM
    sizes = [qa_w, kva_w, kva_w, qb_w] + [kvb_w] * 6 + [3 * B_Q_HEADS]
    (aq, ak, av, bq, bkc, bvc, bks, bvs, bkw, bvw, bg) = [w_in[:, a:b] for a, b in _col_ranges(sizes)]
    group = B_Q_HEADS // B_KV_HEADS
    gpad = jnp.zeros((D_MODEL, LANES - 3 * group), w_in.dtype)
    gates = [x for hk in range(B_KV_HEADS) for x in (bg[:, hk * 3 * group:(hk + 1) * 3 * group], gpad)]
    w = jnp.concatenate([aq, bq, ak, bkw, bks, bq, av, bvw, bvs, bkc, bvc] + gates, axis=1).astype(BF16)
    c = [0]

    def take(width):
        c[0] += width
        return c[0] - width

    def out(idx, off, width, kind, **kw):
        return dict(idx=idx, off=off, width=width, kind=kind, **kw)

    plan = [
        (take(qa_w), qa_w, [out(0, 0, qa_w, "flat", rope=True, scale=True)]),
        (take(qb_w), qb_w, [out(1, 0, qb_w, "flat", rope=True, scale=True)]),
        (take(3 * kva_w), 3 * kva_w, [out(2, 0, kva_w, "flat", rope=True),
                                      out(3, kva_w, kvb_w, "flat", rope=True),
                                      out(4, kva_w + kvb_w, kvb_w, "split", hw=HEAD_DIM, rope=True)]),
        (take(qb_w), qb_w, [out(5, 0, qb_w, "flat", scale=True)]),
        (take(5 * kvb_w), 5 * kvb_w, [out(6, 0, kva_w, "flat"), out(7, kva_w, kvb_w, "flat"),
                                      out(8, 2 * kvb_w, kvb_w, "vt", tile=FLASH_TK),
                                      out(9, 3 * kvb_w, kvb_w, "chunk"),
                                      out(10, 4 * kvb_w, kvb_w, "chunk")]),
        (take(B_KV_HEADS * LANES), B_KV_HEADS * LANES,
         [out(11, 0, B_KV_HEADS * LANES, "split", hw=LANES, sigmoid=True)]),
    ]
    out_defs = [("flat", qa_w, BF16), ("flat", qb_w, BF16), ("flat", kva_w, BF16), ("flat", kvb_w, BF16),
                ("split", B_KV_HEADS, HEAD_DIM, BF16), ("flat", qb_w, BF16), ("flat", kva_w, BF16),
                ("flat", kvb_w, BF16), ("vt", B_KV_HEADS, FLASH_TK, BF16), ("chunk", B_KV_HEADS, BF16),
                ("chunk", B_KV_HEADS, BF16), ("split", B_KV_HEADS, LANES, F32)]
    (aq_r, bq_r, ak_r, bkw_r, bks_r, bq_u, av_, bvw_, bvs_t, zk, zv, gate) = _proj(
        h, gain, w, cos, sin, plan, out_defs, bsz, t_len)

    def seq(x):
        return x.reshape(bsz, 1, t_len, x.shape[-1])

    def heads(x):
        return x.reshape(x.shape[0], bsz, t_len, x.shape[-1])

    gate4 = heads(gate)
    oa = _banded(seq(aq_r), seq(ak_r), seq(av_), n_kv=A_KV_HEADS, group=A_Q_HEADS // A_KV_HEADS,
                 max_dist=A_WINDOW - 1, tl=BAND_TL, pk=128, span=128, sink=sinks)
    nch = t_len // NSA_CMP_STRIDE
    zshape = (B_KV_HEADS, bsz, nch, NSA_CMP_STRIDE * HEAD_DIM)
    kc, vc = _compress(zk.reshape(zshape), zv.reshape(zshape), cmp_pe.reshape(2, 1, NSA_CMP_LEN * HEAD_DIM),
                       cmp_w1.astype(BF16), cmp_w2.astype(BF16))
    ocmp, bias = _cmp_attention(bq_u.reshape(bsz, t_len, qb_w), kc, vc, gate4, t_len)
    osel = _flash(seq(bq_r).reshape(1, bsz, t_len, qb_w), bias, heads(bks_r), bvs_t, group=group, tq=FLASH_T,
                  blk_len=NSA_SEL_LEN, q_scale=1.0, q_index=lambda hk, b, i: (0, b, i, hk), gate=gate4,
                  gate_branch=1)
    owin = _banded(seq(bq_r), seq(bkw_r), seq(bvw_), n_kv=B_KV_HEADS, group=group,
                   max_dist=NSA_WINDOW - 1, tl=BAND_TL, pk=512, span=512, gate=gate4, gate_branch=2)
    return _even_out(h, oa.reshape(m, qa_w), ocmp.reshape(m, qb_w), osel.reshape(m, qb_w),
                     owin.reshape(m, qb_w), w_out.astype(BF16))


def _odd_mixer(h, gain, w_in, w_out, cos, sin, bsz, t_len):
    m = bsz * t_len
    cw = C_HEADS * HEAD_DIM
    dw = D_HEADS * HEAD_DIM
    gw = C_HEADS_PER_GROUP * HEAD_DIM
    n_groups = len(C_GROUPS)

    def group_outs(base, **kw):
        return [dict(idx=base + gi, off=gi * gw, width=gw, kind="dilate", r=r, **kw)
                for gi, (_, r) in enumerate(C_GROUPS)]

    plan = [
        (0, cw, group_outs(0, rope=True, scale=True)),
        (cw, cw, group_outs(n_groups, rope=True)),
        (2 * cw, cw, group_outs(2 * n_groups)),
        (3 * cw, 3 * dw, [dict(idx=3 * n_groups, off=0, width=dw, kind="split", hw=HEAD_DIM, rope=True),
                          dict(idx=3 * n_groups + 1, off=dw, width=dw, kind="split", hw=HEAD_DIM, rope=True),
                          dict(idx=3 * n_groups + 2, off=2 * dw, width=dw, kind="vt", tile=MOBA_TK)]),
    ]
    out_defs = [("dilate", r, gw, BF16) for _ in range(3) for _, r in C_GROUPS]
    out_defs += [("split", D_HEADS, HEAD_DIM, BF16), ("split", D_HEADS, HEAD_DIM, BF16),
                 ("vt", D_HEADS, MOBA_TK, BF16)]
    res = _proj(h, gain, w_in.astype(BF16), cos, sin, plan, out_defs, bsz, t_len)
    cq, ck, cv = res[0:n_groups], res[n_groups:2 * n_groups], res[2 * n_groups:3 * n_groups]
    dq, dk, dv_t = res[3 * n_groups:]
    outs, lses = [], []
    for gi, (wlen, r) in enumerate(C_GROUPS):
        o, lse = _banded(cq[gi], ck[gi], cv[gi], n_kv=C_HEADS_PER_GROUP, group=1, max_dist=wlen // r,
                         tl=min(BAND_TL, t_len // r), pk=128, span=128, want_lse=True)
        outs.append(o)
        lses.append(lse)

    def heads(x):
        return x.reshape(x.shape[0], bsz, t_len, x.shape[-1])

    bias = _moba_gate(heads(dq), heads(dk))
    od = _flash(heads(dq), bias, heads(dk), dv_t, group=1, tq=MOBA_T, blk_len=MOBA_BLOCK, q_scale=SCALE,
                q_index=lambda hh, b, i: (hh, b, i, 0))
    return _odd_out(h, outs, lses, od.reshape(D_HEADS, m, HEAD_DIM), w_out.astype(BF16), t_len)


def kernel(x, ffn_norm_pre, mix_norm, ffn_norm_post, ffn_wi, ffn_wo, even_w_in, even_w_out, even_sinks,
           nsa_cmp_pe, nsa_cmp_w1, nsa_cmp_w2, odd_w_in, odd_w_out, final_norm):
    bsz, t_len, _ = x.shape
    depth = ffn_wi.shape[0]
    cos, sin = _rope_tables(t_len)
    wi = ffn_wi.astype(BF16)
    wo = ffn_wo.astype(BF16)
    h = x.reshape(bsz * t_len, D_MODEL)
    for layer in range(depth):
        i = layer // 2
        h = _ffn(h, ffn_norm_pre[layer], wi[layer, 0], wo[layer, 0])
        if layer % 2 == 0:
            h = _even_mixer(h, mix_norm[layer], even_w_in[i], even_w_out[i], even_sinks[i], nsa_cmp_pe[i],
                            nsa_cmp_w1[i], nsa_cmp_w2[i], cos, sin, bsz, t_len)
        else:
            h = _odd_mixer(h, mix_norm[layer], odd_w_in[i], odd_w_out[i], cos, sin, bsz, t_len)
        last = layer == depth - 1
        h = _ffn(h, ffn_norm_post[layer], wi[layer, 1], wo[layer, 1], final_norm if last else None)
    return h.reshape(bsz, t_len, D_MODEL)
```

```python
import functools

import jax
import jax.numpy as jnp
from jax import lax
from jax.experimental import pallas as pl
from jax.experimental.pallas import tpu as pltpu

D_MODEL = 1024
HEAD_DIM = 64
ROPE_THETA = 10000.0
NORM_EPS = 1e-6
D_FF = 2816
NEG_INF = -1e30
FORCE_SCORE = 1e4

A_Q_HEADS = 8
A_KV_HEADS = 2
A_WINDOW = 128
B_Q_HEADS = 8
B_KV_HEADS = 2
NSA_CMP_LEN = 32
NSA_CMP_STRIDE = 16
NSA_CMP_HIDDEN = 256
NSA_SEL_LEN = 64
NSA_SEL_COUNT = 8
NSA_WINDOW = 512
C_GROUPS = ((128, 1), (512, 4), (2048, 16))
C_HEADS_PER_GROUP = 4
C_HEADS = len(C_GROUPS) * C_HEADS_PER_GROUP
D_HEADS = 4
MOBA_BLOCK = 256
MOBA_TOPK = 3

LANES = 128
LOG2E = 1.4426950408889634
LN2 = 0.6931471805599453
QK_SCALE = HEAD_DIM ** -0.5 * LOG2E
MASK_BIAS = -32768.0
V_ROWS = HEAD_DIM + 16
REMOVED = -3e38
VMEM_LIMIT = 52 * 1024 * 1024

FFN_TM = 512
FFN_TF = 256
PROJ_TM = 512
BAND_TL = 512
BAND_SB = 128
FLASH_T = 256
FLASH_TK = 256
MOBA_T = 512
MOBA_TK = 512
CMP_TQ = 256

BF16 = jnp.bfloat16
F32 = jnp.float32


def _params(*sem):
    return pltpu.CompilerParams(dimension_semantics=sem, vmem_limit_bytes=VMEM_LIMIT)


def _rms(x, g):
    return x * lax.rsqrt(jnp.mean(x * x, axis=-1, keepdims=True) + NORM_EPS) * g


def _log2(n):
    assert n & (n - 1) == 0
    return n.bit_length() - 1


def _dot(a, b):
    return jnp.dot(a, b, preferred_element_type=F32)


def _dot_t(a, b):
    return lax.dot_general(a, b, (((1,), (1,)), ((), ())), preferred_element_type=F32)


def _dot_tn(a, b):
    return lax.dot_general(a, b, (((0,), (0,)), ((), ())), preferred_element_type=F32)


def _ffn_kernel(x_ref, g_ref, wi_ref, wo_ref, *rest, final):
    if final:
        fg_ref, o_ref, act_scr = rest
    else:
        o_ref, act_scr = rest
    n = _rms(x_ref[...], g_ref[...]).astype(BF16)
    for c in range(D_FF // FFN_TF):
        cols = slice(c * FFN_TF, (c + 1) * FFN_TF)
        gate = _dot(n, wi_ref[:, cols])
        up = _dot(n, wi_ref[:, D_FF + c * FFN_TF:D_FF + (c + 1) * FFN_TF])
        act_scr[:, cols] = (gate * jax.nn.sigmoid(gate) * up).astype(BF16)
    h = x_ref[...] + 0.5 * _dot(act_scr[...], wo_ref[...])
    if final:
        h = _rms(h, fg_ref[...])
    o_ref[...] = h


def _ffn(h, gain, wi, wo, final_gain=None):
    m = h.shape[0]
    tm = FFN_TM
    final = final_gain is not None
    resident = pl.Buffered(1)
    in_specs = [
        pl.BlockSpec((tm, D_MODEL), lambda i: (i, 0)),
        pl.BlockSpec((1, D_MODEL), lambda i: (0, 0)),
        pl.BlockSpec(wi.shape, lambda i: (0, 0), pipeline_mode=resident),
        pl.BlockSpec(wo.shape, lambda i: (0, 0), pipeline_mode=resident),
    ]
    args = [h, gain.reshape(1, D_MODEL), wi, wo]
    if final:
        in_specs.append(pl.BlockSpec((1, D_MODEL), lambda i: (0, 0)))
        args.append(final_gain.reshape(1, D_MODEL))
    return pl.pallas_call(
        functools.partial(_ffn_kernel, final=final),
        grid=(m // tm,),
        in_specs=in_specs,
        out_specs=pl.BlockSpec((tm, D_MODEL), lambda i: (i, 0)),
        out_shape=jax.ShapeDtypeStruct((m, D_MODEL), F32),
        scratch_shapes=[pltpu.VMEM((tm, D_FF), BF16)],
        compiler_params=_params("parallel"),
    )(*args)


def _rope_tables(t):
    inv = 1.0 / (ROPE_THETA ** (jnp.arange(0, HEAD_DIM, 2, dtype=F32) / HEAD_DIM))
    ang = jnp.arange(t, dtype=F32)[:, None] * inv[None, :]
    cos = jnp.cos(ang)
    sin = jnp.sin(ang)
    return (jnp.concatenate([cos, cos, cos, cos], axis=-1),
            jnp.concatenate([-sin, sin, -sin, sin], axis=-1))


def _proj_kernel(x_ref, g_ref, w_ref, cos_ref, sin_ref, *rest, plan, n_out):
    out_refs, (ys,) = rest[:n_out], rest[n_out:]
    n = _rms(x_ref[...], g_ref[...]).astype(BF16)
    tm = n.shape[0]
    cos = cos_ref[...]
    sin = sin_ref[...]
    lane = lax.broadcasted_iota(jnp.int32, cos.shape, 1)
    first_half = (lane & (HEAD_DIM - 1)) < HEAD_DIM // 2
    heads_per_tile = LANES // HEAD_DIM
    for off, width, outs in plan:
        y = _dot(n, w_ref[:, off:off + width])
        for o in outs:
            o_ref = out_refs[o["idx"]]
            dt = o_ref.dtype
            kind = o["kind"]
            for c in range(o["width"] // LANES):
                piece = y[:, o["off"] + c * LANES: o["off"] + (c + 1) * LANES]
                if o.get("rope"):
                    rot = jnp.where(first_half, pltpu.roll(piece, LANES - HEAD_DIM // 2, 1),
                                    pltpu.roll(piece, HEAD_DIM // 2, 1))
                    piece = piece * cos + rot * sin
                if o.get("scale"):
                    piece = piece * QK_SCALE
                if o.get("sigmoid"):
                    piece = jax.nn.sigmoid(piece)
                lanes = slice(c * LANES, (c + 1) * LANES)
                if kind == "flat":
                    o_ref[:, lanes] = piece.astype(dt)
                elif kind == "split":
                    hw = o["hw"]
                    per = LANES // hw
                    for k in range(per):
                        o_ref[c * per + k] = piece[:, k * hw:(k + 1) * hw].astype(dt)
                elif kind == "dilate":
                    r = o["r"]
                    if r == 1:
                        o_ref[0, 0, :, lanes] = piece.astype(dt)
                    else:
                        ys[...] = piece
                        for i in range(r):
                            o_ref[0, i, :, lanes] = ys[pl.ds(i, tm // r, stride=r), :].astype(dt)
                elif kind == "chunk":
                    ys[...] = piece
                    for l in range(NSA_CMP_STRIDE):
                        rows = ys[pl.ds(l, tm // NSA_CMP_STRIDE, stride=NSA_CMP_STRIDE), :].astype(dt)
                        for k in range(heads_per_tile):
                            o_ref[c * heads_per_tile + k, :, l * HEAD_DIM:(l + 1) * HEAD_DIM] = rows[
                                :, k * HEAD_DIM:(k + 1) * HEAD_DIM]
                else:
                    tile = o["tile"]
                    pt = piece.T
                    for k in range(heads_per_tile):
                        for kt in range(tm // tile):
                            o_ref[c * heads_per_tile + k, 0, kt, 0:HEAD_DIM, :] = pt[
                                k * HEAD_DIM:(k + 1) * HEAD_DIM, kt * tile:(kt + 1) * tile].astype(dt)
                            o_ref[c * heads_per_tile + k, 0, kt, HEAD_DIM:V_ROWS, :] = jnp.ones(
                                (V_ROWS - HEAD_DIM, tile), dt)


def _proj(h, gain, w, cos, sin, plan, out_defs, bsz, t_len):
    m = h.shape[0]
    tm = PROJ_TM
    nt = t_len // tm
    out_shapes, out_specs = [], []
    for d in out_defs:
        kind, dt = d[0], d[-1]
        if kind == "flat":
            shape, block, index = (m, d[1]), (tm, d[1]), (lambda i: (i, 0))
        elif kind == "split":
            shape, block, index = (d[1], m, d[2]), (d[1], tm, d[2]), (lambda i: (0, i, 0))
        elif kind == "dilate":
            r = d[1]
            shape, block = (bsz, r, t_len // r, d[2]), (1, r, tm // r, d[2])
            index = lambda i: (i // nt, 0, i % nt, 0)
        elif kind == "chunk":
            width = NSA_CMP_STRIDE * HEAD_DIM
            shape, block = (d[1], m // NSA_CMP_STRIDE, width), (d[1], tm // NSA_CMP_STRIDE, width)
            index = lambda i: (0, i, 0)
        else:
            tile = d[2]
            shape, block = (d[1], bsz, t_len // tile, V_ROWS, tile), (d[1], 1, tm // tile, V_ROWS, tile)
            index = lambda i: (0, i // nt, i % nt, 0, 0)
        out_shapes.append(jax.ShapeDtypeStruct(shape, dt))
        out_specs.append(pl.BlockSpec(block, index))
    return pl.pallas_call(
        functools.partial(_proj_kernel, plan=plan, n_out=len(out_defs)),
        grid=(m // tm,),
        in_specs=[
            pl.BlockSpec((tm, D_MODEL), lambda i: (i, 0)),
            pl.BlockSpec((1, D_MODEL), lambda i: (0, 0)),
            pl.BlockSpec(w.shape, lambda i: (0, 0)),
            pl.BlockSpec((tm, LANES), lambda i: (i % nt, 0)),
            pl.BlockSpec((tm, LANES), lambda i: (i % nt, 0)),
        ],
        out_specs=out_specs,
        out_shape=out_shapes,
        scratch_shapes=[pltpu.VMEM((tm, LANES), F32)],
        compiler_params=_params("parallel"),
    )(h, gain.reshape(1, D_MODEL), w, cos, sin)


def _banded_kernel(q_ref, kp_ref, kc_ref, vp_ref, vc_ref, *rest, n_kv, group, tl, pk, span, max_dist,
                   has_sink, gate_branch, want_lse):
    rest = list(rest)
    sink_ref = rest.pop(0) if has_sink else None
    gate_ref = rest.pop(0) if gate_branch is not None else None
    o_ref = rest.pop(0)
    lse_ref = rest.pop(0) if want_lse else None
    kcat, vcat = rest
    sb = BAND_SB
    kcat[0:pk, :] = kp_ref[0, 0]
    kcat[pk:, :] = kc_ref[0, 0]
    vcat[0:pk, :] = vp_ref[0, 0]
    vcat[pk:, :] = vc_ref[0, 0]
    nk = span + sb
    rows = group * sb
    key = lax.broadcasted_iota(jnp.int32, (nk, rows), 0)
    qry = lax.broadcasted_iota(jnp.int32, (nk, rows), 1) & (sb - 1)
    dist = qry + span - key
    in_band = (dist >= 0) & (dist <= max_dist)
    lane_head = lax.broadcasted_iota(jnp.int32, (1, rows), 1) >> _log2(sb)
    items = [(j, h) for j in range(tl // sb) for h in range(n_kv)]

    def compute(first_tile):
        def scores(j, h):
            lo = pk + j * sb - span
            kk = kcat[lo:lo + nk, h * HEAD_DIM:(h + 1) * HEAD_DIM]
            qs = [q_ref[0, 0, j * sb:(j + 1) * sb, (h * group + g) * HEAD_DIM:(h * group + g + 1) * HEAD_DIM]
                  for g in range(group)]
            qs = qs[0] if group == 1 else jnp.concatenate(qs, axis=0)
            ok = in_band
            if first_tile and j * sb < span:
                ok = ok & (key >= span - j * sb)
            return jnp.where(ok, _dot_t(kk, qs), NEG_INF)

        outs, lses = [], []
        s_next = scores(*items[0])
        for n, (j, h) in enumerate(items):
            s = s_next
            if n + 1 < len(items):
                s_next = scores(*items[n + 1])
            lo = pk + j * sb - span
            qrows = slice(j * sb, (j + 1) * sb)
            vv = vcat[lo:lo + nk, h * HEAD_DIM:(h + 1) * HEAD_DIM]
            m = jnp.max(s, axis=0, keepdims=True)
            if has_sink:
                sink = jnp.full((1, rows), sink_ref[h * group], F32)
                for g in range(1, group):
                    sink = jnp.where(lane_head == g, sink_ref[h * group + g], sink)
                sink = sink * LOG2E
                m = jnp.maximum(m, sink)
            p = jnp.exp2(s - m)
            d = jnp.sum(p, axis=0, keepdims=True)
            if has_sink:
                d = d + jnp.exp2(sink - m)
            d = jnp.where(d > 0, d, 1.0)
            o_t = _dot_tn(vv, p.astype(BF16)) / d
            if want_lse:
                lse_t = jnp.broadcast_to(m * LN2 + jnp.log(d), (HEAD_DIM, rows))
            for g in range(group):
                o = o_t[:, g * sb:(g + 1) * sb].T
                if gate_ref is not None:
                    gc = g * 3 + gate_branch
                    o = o * gate_ref[h, 0, qrows, gc:gc + 1]
                outs.append(o)
                if want_lse:
                    lses.append(lse_t[:, g * sb:(g + 1) * sb].T)
            if h == n_kv - 1:
                o_ref[0, 0, qrows, :] = jnp.concatenate(outs, axis=-1).astype(o_ref.dtype)
                if want_lse:
                    lse_ref[0, 0, qrows, :] = jnp.concatenate(lses, axis=-1)
                outs, lses = [], []

    first = pl.program_id(2) == 0
    pl.when(first)(functools.partial(compute, True))
    pl.when(jnp.logical_not(first))(functools.partial(compute, False))


def _banded(q, k, v, *, n_kv, group, max_dist, tl, pk, span, sink=None, gate=None, gate_branch=None,
            want_lse=False):
    bsz, n_res, length, qw = q.shape
    kw = n_kv * HEAD_DIM
    ratio = tl // pk
    prev = lambda b, i, l: (b, i, jnp.maximum(l * ratio - 1, 0), 0)
    cur = lambda b, i, l: (b, i, l, 0)
    in_specs = [
        pl.BlockSpec((1, 1, tl, qw), cur),
        pl.BlockSpec((1, 1, pk, kw), prev),
        pl.BlockSpec((1, 1, tl, kw), cur),
        pl.BlockSpec((1, 1, pk, kw), prev),
        pl.BlockSpec((1, 1, tl, kw), cur),
    ]
    args = [q, k, k, v, v]
    if sink is not None:
        in_specs.append(pl.BlockSpec(memory_space=pltpu.SMEM))
        args.append(sink)
    if gate is not None:
        in_specs.append(pl.BlockSpec((n_kv, 1, tl, LANES), lambda b, i, l: (0, b, l, 0)))
        args.append(gate)
    out_spec = pl.BlockSpec((1, 1, tl, qw), cur)
    out_shape = [jax.ShapeDtypeStruct(q.shape, BF16)]
    out_specs = [out_spec]
    if want_lse:
        out_shape.append(jax.ShapeDtypeStruct(q.shape, F32))
        out_specs.append(out_spec)
    res = pl.pallas_call(
        functools.partial(_banded_kernel, n_kv=n_kv, group=group, tl=tl, pk=pk, span=span,
                          max_dist=max_dist, has_sink=sink is not None, gate_branch=gate_branch,
                          want_lse=want_lse),
        grid=(bsz, n_res, length // tl),
        in_specs=in_specs,
        out_specs=out_specs,
        out_shape=out_shape,
        scratch_shapes=[pltpu.VMEM((pk + tl, kw), BF16), pltpu.VMEM((pk + tl, kw), BF16)],
        compiler_params=_params("parallel", "parallel", "parallel"),
    )(*args)
    return res if want_lse else res[0]


def _compress_one(z_ref, pe_ref, w1_ref, w2_ref, o_ref):
    half = NSA_CMP_STRIDE * HEAD_DIM
    z = z_ref[0, 0].astype(F32)
    za = (z + pe_ref[0, :, 0:half]).astype(BF16)
    zb = (z + pe_ref[0, :, half:2 * half]).astype(BF16)
    ya = _dot(za, w1_ref[0, 0:half, :])
    yb = _dot(zb, w1_ref[0, half:2 * half, :])
    nrow = ya.shape[0]
    hid = jax.nn.gelu(ya + pltpu.roll(yb, nrow - 1, 0))
    o_ref[0, 0] = _dot(hid.astype(BF16), w2_ref[0]).astype(o_ref.dtype)


def _compress_kernel(zk_ref, zv_ref, pek_ref, pev_ref, w1k_ref, w1v_ref, w2k_ref, w2v_ref, ok_ref, ov_ref):
    _compress_one(zk_ref, pek_ref, w1k_ref, w2k_ref, ok_ref)
    _compress_one(zv_ref, pev_ref, w1v_ref, w2v_ref, ov_ref)


def _compress(zk, zv, pe, w1, w2):
    hk, bsz, nch, width = zk.shape
    zspec = pl.BlockSpec((1, 1, nch, width), lambda h, b: (h, b, 0, 0))
    ospec = pl.BlockSpec((1, 1, nch, HEAD_DIM), lambda h, b: (h, b, 0, 0))
    oshape = jax.ShapeDtypeStruct((hk, bsz, nch, HEAD_DIM), BF16)

    def wspec(arr, which):
        return pl.BlockSpec((1,) + arr.shape[1:], lambda h, b: (which, 0, 0))

    return pl.pallas_call(
        _compress_kernel,
        grid=(hk, bsz),
        in_specs=[zspec, zspec, wspec(pe, 0), wspec(pe, 1), wspec(w1, 0), wspec(w1, 1), wspec(w2, 0), wspec(w2, 1)],
        out_specs=[ospec, ospec],
        out_shape=[oshape, oshape],
        compiler_params=_params("parallel", "parallel"),
    )(zk, zv, pe, pe, w1, w1, w2, w2)


def _topk_select(score, k, idxf):
    ncand = score.shape[0]
    rem = score
    sel = jnp.zeros_like(score)
    for _ in range(k):
        mx = jnp.max(rem, axis=0, keepdims=True)
        first = jnp.min(jnp.where(rem == mx, idxf, float(ncand)), axis=0, keepdims=True)
        pick = idxf == first
        sel = jnp.where(pick, jnp.where(mx > 0.5 * NEG_INF, 1.0, 0.0), sel)
        rem = jnp.where(pick, REMOVED, rem)
    return sel


def _cmp_kernel(q_ref, kc_ref, vc_ref, gate_ref, o_ref, bias_ref, *, tq, group):
    qi = pl.program_id(2)
    kc = kc_ref[0, 0]
    vc = vc_ref[0, 0]
    ncp = kc.shape[0]
    nsb = bias_ref.shape[-1]
    rows = group * tq
    n = lax.broadcasted_iota(jnp.int32, (ncp, rows), 0)
    t = qi * tq + (lax.broadcasted_iota(jnp.int32, (ncp, rows), 1) & (tq - 1))
    visible = n * NSA_CMP_STRIDE + (NSA_CMP_LEN - 1) <= t
    qs = jnp.concatenate([q_ref[0, :, g * HEAD_DIM:(g + 1) * HEAD_DIM] for g in range(group)], axis=0)
    s = jnp.where(visible, _dot_t(kc, qs), NEG_INF)
    m = jnp.max(s, axis=0, keepdims=True)
    p = jnp.where(visible, jnp.exp2(s - m), 0.0)
    d = jnp.sum(p, axis=0, keepdims=True)
    d = jnp.where(d > 0, d, 1.0)
    pc = p / d
    o_t = _dot_tn(vc, pc.astype(BF16))
    outs = []
    psum = pc[:, 0:tq]
    for g in range(group):
        if g:
            psum = psum + pc[:, g * tq:(g + 1) * tq]
        outs.append(o_t[:, g * tq:(g + 1) * tq].T * gate_ref[0, 0, :, g * 3:g * 3 + 1])
    o_ref[0] = jnp.concatenate(outs, axis=-1).astype(o_ref.dtype)

    jj = lax.broadcasted_iota(jnp.int32, (nsb, ncp), 0) * NSA_SEL_LEN
    nn = lax.broadcasted_iota(jnp.int32, (nsb, ncp), 1) * NSA_CMP_STRIDE
    ov = jnp.maximum(jnp.minimum(nn + NSA_CMP_LEN, jj + NSA_SEL_LEN) - jnp.maximum(nn, jj), 0)
    ov = (ov.astype(F32) / NSA_CMP_LEN).astype(BF16)
    p_hi = psum.astype(BF16)
    p_lo = (psum - p_hi.astype(F32)).astype(BF16)
    imp = _dot(ov, p_hi) + _dot(ov, p_lo)

    j = lax.broadcasted_iota(jnp.int32, (nsb, tq), 0)
    tt = qi * tq + lax.broadcasted_iota(jnp.int32, (nsb, tq), 1)
    cb = tt >> _log2(NSA_SEL_LEN)
    forced = (j == 0) | (j == cb) | (j == cb - 1)
    score = jnp.where(j <= cb, jnp.where(forced, FORCE_SCORE, imp), NEG_INF)
    sel = _topk_select(score, min(NSA_SEL_COUNT, nsb), j.astype(F32))
    bias_ref[0, 0] = jnp.where(sel > 0.5, 0.0, MASK_BIAS).T.astype(bias_ref.dtype)


def _cmp_attention(qu, kc, vc, gate, t_len):
    bsz = qu.shape[0]
    hk, _, ncp, _ = kc.shape
    group = B_Q_HEADS // B_KV_HEADS
    tq = CMP_TQ
    nsb = t_len // NSA_SEL_LEN
    gw = group * HEAD_DIM
    cspec = pl.BlockSpec((1, 1, ncp, HEAD_DIM), lambda b, h, i: (h, b, 0, 0))
    return pl.pallas_call(
        functools.partial(_cmp_kernel, tq=tq, group=group),
        grid=(bsz, hk, t_len // tq),
        in_specs=[
            pl.BlockSpec((1, tq, gw), lambda b, h, i: (b, i, h)),
            cspec, cspec,
            pl.BlockSpec((1, 1, tq, LANES), lambda b, h, i: (h, b, i, 0)),
        ],
        out_specs=[
            pl.BlockSpec((1, tq, gw), lambda b, h, i: (b, i, h)),
            pl.BlockSpec((1, 1, tq, nsb), lambda b, h, i: (h, b, i, 0)),
        ],
        out_shape=[
            jax.ShapeDtypeStruct((bsz, t_len, hk * gw), BF16),
            jax.ShapeDtypeStruct((hk, bsz, t_len, nsb), BF16),
        ],
        compiler_params=_params("parallel", "parallel", "parallel"),
    )(qu, kc, vc, gate)


def _moba_gate_kernel(q_ref, k_ref, bias_ref, km_hi, km_lo, *, tq):
    qi = pl.program_id(2)
    t_len = k_ref.shape[2]
    ncol = bias_ref.shape[-1]
    nb = t_len // MOBA_BLOCK

    @pl.when(qi == 0)
    def _():
        blk = lax.broadcasted_iota(jnp.int32, (ncol, t_len), 0)
        pos = lax.broadcasted_iota(jnp.int32, (ncol, t_len), 1)
        member = jnp.where((pos >> _log2(MOBA_BLOCK)) == blk, 1.0, 0.0).astype(BF16)
        kmean = _dot(member, k_ref[0, 0]) * (1.0 / MOBA_BLOCK)
        hi = kmean.astype(BF16)
        km_hi[...] = hi
        km_lo[...] = (kmean - hi.astype(F32)).astype(BF16)

    q = q_ref[0, 0]
    gate = _dot_t(km_hi[...], q) + _dot_t(km_lo[...], q)
    j = lax.broadcasted_iota(jnp.int32, (ncol, tq), 0)
    t = qi * tq + lax.broadcasted_iota(jnp.int32, (ncol, tq), 1)
    cb = t >> _log2(MOBA_BLOCK)
    score = jnp.where(j < cb, gate, jnp.where(j < nb, NEG_INF, REMOVED))
    sel = _topk_select(score, min(MOBA_TOPK, nb - 1), j.astype(F32))
    keep = (sel > 0.5) | (j == cb)
    bias_ref[0, 0] = jnp.where(keep, 0.0, MASK_BIAS).T.astype(bias_ref.dtype)


def _moba_gate(q, k):
    nh, bsz, t_len, _ = q.shape
    tq = MOBA_T
    return pl.pallas_call(
        functools.partial(_moba_gate_kernel, tq=tq),
        grid=(nh, bsz, t_len // tq),
        in_specs=[
            pl.BlockSpec((1, 1, tq, HEAD_DIM), lambda h, b, i: (h, b, i, 0)),
            pl.BlockSpec((1, 1, t_len, HEAD_DIM), lambda h, b, i: (h, b, 0, 0)),
        ],
        out_specs=pl.BlockSpec((1, 1, tq, HEAD_DIM), lambda h, b, i: (h, b, i, 0)),
        out_shape=jax.ShapeDtypeStruct((nh, bsz, t_len, HEAD_DIM), BF16),
        scratch_shapes=[pltpu.VMEM((HEAD_DIM, HEAD_DIM), BF16), pltpu.VMEM((HEAD_DIM, HEAD_DIM), BF16)],
        compiler_params=_params("parallel", "parallel", "arbitrary"),
    )(q, k)


def _flash_kernel(q_ref, bias_ref, k_ref, vt_ref, *rest, group, tq, tk, blk_len, q_scale, gate_branch):
    if gate_branch is not None:
        gate_ref, o_ref, qa, ka, m_s, acc, s_a, s_b = rest
    else:
        gate_ref = None
        o_ref, qa, ka, m_s, acc, s_a, s_b = rest
    qi = pl.program_id(2)
    rows = group * tq
    t_len = k_ref.shape[2]

    @pl.when(qi == 0)
    def _():
        ka[:, 0:HEAD_DIM] = k_ref[0, 0]
        kpos = lax.broadcasted_iota(jnp.int32, (t_len, HEAD_DIM), 0)
        kblk = lax.broadcasted_iota(jnp.int32, (t_len, HEAD_DIM), 1)
        ka[:, HEAD_DIM:2 * HEAD_DIM] = jnp.where((kpos >> _log2(blk_len)) == kblk, 1.0, 0.0).astype(BF16)

    bias = bias_ref[0, 0]
    for g in range(group):
        qg = q_ref[0, 0, :, g * HEAD_DIM:(g + 1) * HEAD_DIM]
        if q_scale != 1.0:
            qg = (qg.astype(F32) * q_scale).astype(BF16)
        qa[g * tq:(g + 1) * tq, 0:HEAD_DIM] = qg
        qa[g * tq:(g + 1) * tq, HEAD_DIM:2 * HEAD_DIM] = bias
    m_s[...] = jnp.full_like(m_s, NEG_INF)
    acc[...] = jnp.zeros_like(acc)

    def scores(ki, s_ref):
        start = pl.multiple_of(ki * tk, tk)
        s_ref[...] = _dot_t(ka[pl.ds(start, tk), :], qa[...])

    def softmax_pv(ki, s_ref, masked):
        s = s_ref[...]
        if masked:
            kpos = ki * tk + lax.broadcasted_iota(jnp.int32, (tk, rows), 0)
            qpos = qi * tq + (lax.broadcasted_iota(jnp.int32, (tk, rows), 1) & (tq - 1))
            s = jnp.where(kpos <= qpos, s, NEG_INF)
        m_old = m_s[...]
        m_new = jnp.maximum(m_old, jnp.max(s, axis=0, keepdims=True))
        alpha = jnp.exp2(m_old - m_new)
        p = jnp.exp2(s - m_new)
        acc[...] = alpha * acc[...] + _dot(vt_ref[0, 0, ki], p.astype(BF16))
        m_s[...] = m_new

    n_full = (qi * tq) >> _log2(tk)
    scores(0, s_a)

    def pair(j, carry):
        k0 = 2 * j
        scores(k0 + 1, s_b)
        softmax_pv(k0, s_a, False)
        scores(k0 + 2, s_a)
        softmax_pv(k0 + 1, s_b, False)
        return carry

    lax.fori_loop(0, n_full >> 1, pair, 0)
    cur = (n_full >> 1) << 1

    @pl.when((n_full & 1) == 1)
    def _():
        scores(cur + 1, s_b)
        softmax_pv(cur, s_a, False)
        softmax_pv(cur + 1, s_b, True)

    @pl.when((n_full & 1) == 0)
    def _():
        softmax_pv(cur, s_a, True)

    d = acc[HEAD_DIM:HEAD_DIM + 1, :]
    d = jnp.where(d > 0, d, 1.0)
    o_t = acc[0:HEAD_DIM, :] / d
    outs = []
    for g in range(group):
        og = o_t[:, g * tq:(g + 1) * tq].T
        if gate_ref is not None:
            gc = g * 3 + gate_branch
            og = og * gate_ref[0, 0, :, gc:gc + 1]
        outs.append(og)
    res = outs[0] if group == 1 else jnp.concatenate(outs, axis=-1)
    o_ref[0, 0] = res.astype(o_ref.dtype)


def _flash(q, bias, k, vt, *, group, tq, blk_len, q_scale, q_index, gate=None, gate_branch=None):
    nh, bsz, t_len, _ = k.shape
    gw = group * HEAD_DIM
    nt, tk = vt.shape[2], vt.shape[4]
    in_specs = [
        pl.BlockSpec((1, 1, tq, gw), q_index),
        pl.BlockSpec((1, 1, tq, HEAD_DIM), lambda h, b, i: (h, b, i, 0)),
        pl.BlockSpec((1, 1, t_len, HEAD_DIM), lambda h, b, i: (h, b, 0, 0)),
        pl.BlockSpec((1, 1, nt, V_ROWS, tk), lambda h, b, i: (h, b, 0, 0, 0)),
    ]
    args = [q, bias, k, vt]
    if gate is not None:
        in_specs.append(pl.BlockSpec((1, 1, tq, LANES), lambda h, b, i: (h, b, i, 0)))
        args.append(gate)
    rows = group * tq
    return pl.pallas_call(
        functools.partial(_flash_kernel, group=group, tq=tq, tk=tk, blk_len=blk_len, q_scale=q_scale,
                          gate_branch=gate_branch),
        grid=(nh, bsz, t_len // tq),
        in_specs=in_specs,
        out_specs=pl.BlockSpec((1, 1, tq, gw), q_index),
        out_shape=jax.ShapeDtypeStruct(q.shape, BF16),
        scratch_shapes=[
            pltpu.VMEM((rows, 2 * HEAD_DIM), BF16),
            pltpu.VMEM((t_len, 2 * HEAD_DIM), BF16),
            pltpu.VMEM((1, rows), F32),
            pltpu.VMEM((V_ROWS, rows), F32),
            pltpu.VMEM((tk, rows), F32),
            pltpu.VMEM((tk, rows), F32),
        ],
        compiler_params=_params("parallel", "parallel", "arbitrary"),
    )(*args)


def _even_out_kernel(h_ref, oa_ref, oc_ref, os_ref, ow_ref, w_ref, o_ref):
    na = oa_ref.shape[-1]
    ob = (oc_ref[...].astype(F32) + os_ref[...].astype(F32) + ow_ref[...].astype(F32)).astype(BF16)
    o_ref[...] = h_ref[...] + _dot(oa_ref[...], w_ref[0:na, :]) + _dot(ob, w_ref[na:, :])


def _even_out(h, oa, ocmp, osel, owin, w):
    m = h.shape[0]
    tm = PROJ_TM
    hspec = pl.BlockSpec((tm, D_MODEL), lambda i: (i, 0))
    aspec = pl.BlockSpec((tm, oa.shape[-1]), lambda i: (i, 0))
    return pl.pallas_call(
        _even_out_kernel,
        grid=(m // tm,),
        in_specs=[hspec, aspec, aspec, aspec, aspec, pl.BlockSpec(w.shape, lambda i: (0, 0))],
        out_specs=hspec,
        out_shape=jax.ShapeDtypeStruct((m, D_MODEL), F32),
        compiler_params=_params("parallel"),
    )(h, oa, ocmp, osel, owin, w)


def _odd_out_kernel(h_ref, *rest, dilations):
    ng = len(dilations)
    o_refs, l_refs = rest[:ng], rest[ng:2 * ng]
    od_ref, w_ref, out_ref = rest[2 * ng:2 * ng + 3]
    scratch = list(rest[2 * ng + 3:])
    tm = h_ref.shape[0]

    def tokens(ref, r):
        if r == 1:
            return ref[0, 0].astype(F32)
        pieces = []
        for c in range(ref.shape[-1] // LANES):
            scr = scratch.pop(0)
            for i in range(r):
                scr[pl.ds(i, tm // r, stride=r), :] = ref[0, i, :, c * LANES:(c + 1) * LANES].astype(F32)
            pieces.append(scr[...])
        return jnp.concatenate(pieces, axis=-1)

    outs = [tokens(ref, r) for ref, r in zip(o_refs, dilations)]
    lses = [tokens(ref, r) for ref, r in zip(l_refs, dilations)]
    mx = functools.reduce(jnp.maximum, lses)
    es = [jnp.exp(l - mx) for l in lses]
    tot = functools.reduce(lambda a, b: a + b, es)
    oc = functools.reduce(lambda a, b: a + b, [(e / tot) * o for e, o in zip(es, outs)])
    nc = oc.shape[-1]
    out = h_ref[...] + _dot(oc.astype(BF16), w_ref[0:nc, :])
    for hh in range(od_ref.shape[0]):
        out = out + _dot(od_ref[hh], w_ref[nc + hh * HEAD_DIM: nc + (hh + 1) * HEAD_DIM, :])
    out_ref[...] = out


def _odd_out(h, outs, lses, od, w, t_len):
    m = h.shape[0]
    tm = PROJ_TM
    nt = t_len // tm
    gw = outs[0].shape[-1]
    dilations = tuple(o.shape[1] for o in outs)
    hspec = pl.BlockSpec((tm, D_MODEL), lambda i: (i, 0))
    gspecs = [pl.BlockSpec((1, r, tm // r, gw), lambda i: (i // nt, 0, i % nt, 0)) for r in dilations]
    n_scr = 2 * (gw // LANES) * sum(1 for r in dilations if r > 1)
    return pl.pallas_call(
        functools.partial(_odd_out_kernel, dilations=dilations),
        grid=(m // tm,),
        in_specs=[hspec] + gspecs + gspecs + [
            pl.BlockSpec((od.shape[0], tm, HEAD_DIM), lambda i: (0, i, 0)),
            pl.BlockSpec(w.shape, lambda i: (0, 0)),
        ],
        out_specs=hspec,
        out_shape=jax.ShapeDtypeStruct((m, D_MODEL), F32),
        scratch_shapes=[pltpu.VMEM((tm, LANES), F32)] * n_scr,
        compiler_params=_params("parallel"),
    )(h, *outs, *lses, od, w)


def _col_ranges(sizes):
    offs, acc = [], 0
    for s in sizes:
        offs.append((acc, acc + s))
        acc += s
    return offs


def _even_mixer(h, gain, w_in, w_out, sinks, cmp_pe, cmp_w1, cmp_w2, cos, sin, bsz, t_len):
    m = bsz * t_len
    qa_w, kva_w = A_Q_HEADS * HEAD_DIM, A_KV_HEADS * HEAD_DIM
    qb_w, kvb_w = B_Q_HEADS * HEAD_DIM, B_KV_HEADS * HEAD_DIM
    sizes = [qa_w, kva_w, kva_w, qb_w] + [kvb_w] * 6 + [3 * B_Q_HEADS]
    (aq, ak, av, bq, bkc, bvc, bks, bvs, bkw, bvw, bg) = [w_in[:, a:b] for a, b in _col_ranges(sizes)]
    group = B_Q_HEADS // B_KV_HEADS
    gpad = jnp.zeros((D_MODEL, LANES - 3 * group), w_in.dtype)
    gates = [x for hk in range(B_KV_HEADS) for x in (bg[:, hk * 3 * group:(hk + 1) * 3 * group], gpad)]
    w = jnp.concatenate([aq, bq, ak, bkw, bks, bq, av, bvw, bvs, bkc, bvc] + gates, axis=1).astype(BF16)
    c = [0]

    def take(width):
        c[0] += width
        return c[0] - width

    def out(idx, off, width, kind, **kw):
        return dict(idx=idx, off=off, width=width, kind=kind, **kw)

    plan = [
        (take(qa_w), qa_w, [out(0, 0, qa_w, "flat", rope=True, scale=True)]),
        (take(qb_w), qb_w, [out(1, 0, qb_w, "flat", rope=True, scale=True)]),
        (take(3 * kva_w), 3 * kva_w, [out(2, 0, kva_w, "flat", rope=True),
                                      out(3, kva_w, kvb_w, "flat", rope=True),
                                      out(4, kva_w + kvb_w, kvb_w, "split", hw=HEAD_DIM, rope=True)]),
        (take(qb_w), qb_w, [out(5, 0, qb_w, "flat", scale=True)]),
        (take(5 * kvb_w), 5 * kvb_w, [out(6, 0, kva_w, "flat"), out(7, kva_w, kvb_w, "flat"),
                                      out(8, 2 * kvb_w, kvb_w, "vt", tile=FLASH_TK),
                                      out(9, 3 * kvb_w, kvb_w, "chunk"),
                                      out(10, 4 * kvb_w, kvb_w, "chunk")]),
        (take(B_KV_HEADS * LANES), B_KV_HEADS * LANES,
         [out(11, 0, B_KV_HEADS * LANES, "split", hw=LANES, sigmoid=True)]),
    ]
    out_defs = [("flat", qa_w, BF16), ("flat", qb_w, BF16), ("flat", kva_w, BF16), ("flat", kvb_w, BF16),
                ("split", B_KV_HEADS, HEAD_DIM, BF16), ("flat", qb_w, BF16), ("flat", kva_w, BF16),
                ("flat", kvb_w, BF16), ("vt", B_KV_HEADS, FLASH_TK, BF16), ("chunk", B_KV_HEADS, BF16),
                ("chunk", B_KV_HEADS, BF16), ("split", B_KV_HEADS, LANES, F32)]
    (aq_r, bq_r, ak_r, bkw_r, bks_r, bq_u, av_, bvw_, bvs_t, zk, zv, gate) = _proj(
        h, gain, w, cos, sin, plan, out_defs, bsz, t_len)

    def seq(x):
        return x.reshape(bsz, 1, t_len, x.shape[-1])

    def heads(x):
        return x.reshape(x.shape[0], bsz, t_len, x.shape[-1])

    gate4 = heads(gate)
    oa = _banded(seq(aq_r), seq(ak_r), seq(av_), n_kv=A_KV_HEADS, group=A_Q_HEADS // A_KV_HEADS,
                 max_dist=A_WINDOW - 1, tl=BAND_TL, pk=128, span=128, sink=sinks)
    nch = t_len // NSA_CMP_STRIDE
    zshape = (B_KV_HEADS, bsz, nch, NSA_CMP_STRIDE * HEAD_DIM)
    kc, vc = _compress(zk.reshape(zshape), zv.reshape(zshape), cmp_pe.reshape(2, 1, NSA_CMP_LEN * HEAD_DIM),
                       cmp_w1.astype(BF16), cmp_w2.astype(BF16))
    ocmp, bias = _cmp_attention(bq_u.reshape(bsz, t_len, qb_w), kc, vc, gate4, t_len)
    osel = _flash(seq(bq_r).reshape(1, bsz, t_len, qb_w), bias, heads(bks_r), bvs_t, group=group, tq=FLASH_T,
                  blk_len=NSA_SEL_LEN, q_scale=1.0, q_index=lambda hk, b, i: (0, b, i, hk), gate=gate4,
                  gate_branch=1)
    owin = _banded(seq(bq_r), seq(bkw_r), seq(bvw_), n_kv=B_KV_HEADS, group=group,
                   max_dist=NSA_WINDOW - 1, tl=BAND_TL, pk=512, span=512, gate=gate4, gate_branch=2)
    return _even_out(h, oa.reshape(m, qa_w), ocmp.reshape(m, qb_w), osel.reshape(m, qb_w),
                     owin.reshape(m, qb_w), w_out.astype(BF16))


def _odd_mixer(h, gain, w_in, w_out, cos, sin, bsz, t_len):
    m = bsz * t_len
    cw = C_HEADS * HEAD_DIM
    dw = D_HEADS * HEAD_DIM
    gw = C_HEADS_PER_GROUP * HEAD_DIM
    n_groups = len(C_GROUPS)

    def group_outs(base, **kw):
        return [dict(idx=base + gi, off=gi * gw, width=gw, kind="dilate", r=r, **kw)
                for gi, (_, r) in enumerate(C_GROUPS)]

    plan = [
        (0, cw, group_outs(0, rope=True, scale=True)),
        (cw, cw, group_outs(n_groups, rope=True)),
        (2 * cw, cw, group_outs(2 * n_groups)),
        (3 * cw, 3 * dw, [dict(idx=3 * n_groups, off=0, width=dw, kind="split", hw=HEAD_DIM, rope=True),
                          dict(idx=3 * n_groups + 1, off=dw, width=dw, kind="split", hw=HEAD_DIM, rope=True),
                          dict(idx=3 * n_groups + 2, off=2 * dw, width=dw, kind="vt", tile=MOBA_TK)]),
    ]
    out_defs = [("dilate", r, gw, BF16) for _ in range(3) for _, r in C_GROUPS]
    out_defs += [("split", D_HEADS, HEAD_DIM, BF16), ("split", D_HEADS, HEAD_DIM, BF16),
                 ("vt", D_HEADS, MOBA_TK, BF16)]
    res = _proj(h, gain, w_in.astype(BF16), cos, sin, plan, out_defs, bsz, t_len)
    cq, ck, cv = res[0:n_groups], res[n_groups:2 * n_groups], res[2 * n_groups:3 * n_groups]
    dq, dk, dv_t = res[3 * n_groups:]
    outs, lses = [], []
    for gi, (wlen, r) in enumerate(C_GROUPS):
        o, lse = _banded(cq[gi], ck[gi], cv[gi], n_kv=C_HEADS_PER_GROUP, group=1, max_dist=wlen // r,
                         tl=min(BAND_TL, t_len // r), pk=128, span=128, want_lse=True)
        outs.append(o)
        lses.append(lse)

    def heads(x):
        return x.reshape(x.shape[0], bsz, t_len, x.shape[-1])

    bias = _moba_gate(heads(dq), heads(dk))
    od = _flash(heads(dq), bias, heads(dk), dv_t, group=1, tq=MOBA_T, blk_len=MOBA_BLOCK, q_scale=QK_SCALE,
                q_index=lambda hh, b, i: (hh, b, i, 0))
    return _odd_out(h, outs, lses, od.reshape(D_HEADS, m, HEAD_DIM), w_out.astype(BF16), t_len)


def kernel(x, ffn_norm_pre, mix_norm, ffn_norm_post, ffn_wi, ffn_wo, even_w_in, even_w_out, even_sinks,
           nsa_cmp_pe, nsa_cmp_w1, nsa_cmp_w2, odd_w_in, odd_w_out, final_norm):
    bsz, t_len, _ = x.shape
    depth = ffn_wi.shape[0]
    cos, sin = _rope_tables(t_len)
    wi = ffn_wi.astype(BF16)
    wo = ffn_wo.astype(BF16)
    h = x.reshape(bsz * t_len, D_MODEL)
    for layer in range(depth):
        i = layer // 2
        h = _ffn(h, ffn_norm_pre[layer], wi[layer, 0], wo[layer, 0])
        if layer % 2 == 0:
            h = _even_mixer(h, mix_norm[layer], even_w_in[i], even_w_out[i], even_sinks[i], nsa_cmp_pe[i],
                            nsa_cmp_w1[i], nsa_cmp_w2[i], cos, sin, bsz, t_len)
        else:
            h = _odd_mixer(h, mix_norm[layer], odd_w_in[i], odd_w_out[i], cos, sin, bsz, t_len)
        last = layer == depth - 1
        h = _ffn(h, ffn_norm_post[layer], wi[layer, 1], wo[layer, 1], final_norm if last else None)
    return h.reshape(bsz, t_len, D_MODEL)
```

```python
import functools

import jax
import jax.numpy as jnp
from jax import lax
from jax.experimental import pallas as pl
from jax.experimental.pallas import tpu as pltpu

D_MODEL = 1024
HEAD_DIM = 64
ROPE_THETA = 10000.0
NORM_EPS = 1e-6
D_FF = 2816
NEG_INF = -1e30
FORCE_SCORE = 1e4

A_Q_HEADS = 8
A_KV_HEADS = 2
A_WINDOW = 128
B_Q_HEADS = 8
B_KV_HEADS = 2
NSA_CMP_LEN = 32
NSA_CMP_STRIDE = 16
NSA_CMP_HIDDEN = 256
NSA_SEL_LEN = 64
NSA_SEL_COUNT = 8
NSA_WINDOW = 512
C_GROUPS = ((128, 1), (512, 4), (2048, 16))
C_HEADS_PER_GROUP = 4
C_HEADS = len(C_GROUPS) * C_HEADS_PER_GROUP
D_HEADS = 4
MOBA_BLOCK = 256
MOBA_TOPK = 3

LANES = 128
LOG2E = 1.4426950408889634
LN2 = 0.6931471805599453
QK_SCALE = HEAD_DIM ** -0.5 * LOG2E
MASK_BIAS = -32768.0
V_ROWS = HEAD_DIM + 16
REMOVED = -3e38
VMEM_LIMIT = 52 * 1024 * 1024

FFN_TM = 512
FFN_TF = 256
PROJ_TM = 512
BAND_TL = 512
BAND_SB = 128
FLASH_T = 256
FLASH_TK = 256
MOBA_T = 512
MOBA_TK = 512
CMP_TQ = 256

BF16 = jnp.bfloat16
F32 = jnp.float32


def _params(*sem):
    return pltpu.CompilerParams(dimension_semantics=sem, vmem_limit_bytes=VMEM_LIMIT)


def _rms(x, g):
    return x * lax.rsqrt(jnp.mean(x * x, axis=-1, keepdims=True) + NORM_EPS) * g


def _log2(n):
    assert n & (n - 1) == 0
    return n.bit_length() - 1


def _dot(a, b):
    return jnp.dot(a, b, preferred_element_type=F32)


def _dot_t(a, b):
    return lax.dot_general(a, b, (((1,), (1,)), ((), ())), preferred_element_type=F32)


def _dot_tn(a, b):
    return lax.dot_general(a, b, (((0,), (0,)), ((), ())), preferred_element_type=F32)


def _ffn_kernel(x_ref, g_ref, wi_ref, wo_ref, *rest, final, mix_fn, n_mix):
    rest = list(rest)
    fg_ref = rest.pop(0) if final else None
    mix_refs = [rest.pop(0) for _ in range(n_mix)]
    o_ref, act_scr = rest.pop(0), rest.pop(0)
    x = x_ref[...]
    if mix_fn is not None:
        x = x + mix_fn(mix_refs, rest)
    o_ref[...] = x
    n = _rms(x, g_ref[...]).astype(BF16)
    for c in range(D_FF // FFN_TF):
        cols = slice(c * FFN_TF, (c + 1) * FFN_TF)
        gate = _dot(n, wi_ref[:, cols])
        up = _dot(n, wi_ref[:, D_FF + c * FFN_TF:D_FF + (c + 1) * FFN_TF])
        act_scr[:, cols] = (gate * jax.nn.sigmoid(gate) * up).astype(BF16)
    h = o_ref[...] + 0.5 * _dot(act_scr[...], wo_ref[...])
    if final:
        h = _rms(h, fg_ref[...])
    o_ref[...] = h


def _resident(shape):
    return pl.BlockSpec(shape, lambda i: (0,) * len(shape), pipeline_mode=pl.Buffered(1))


def _ffn(h, gain, wi, wo, final_gain=None, mix=None):
    m = h.shape[0]
    tm = FFN_TM
    final = final_gain is not None
    in_specs = [
        pl.BlockSpec((tm, D_MODEL), lambda i: (i, 0)),
        pl.BlockSpec((1, D_MODEL), lambda i: (0, 0)),
        _resident(wi.shape),
        _resident(wo.shape),
    ]
    args = [h, gain.reshape(1, D_MODEL), wi, wo]
    if final:
        in_specs.append(pl.BlockSpec((1, D_MODEL), lambda i: (0, 0)))
        args.append(final_gain.reshape(1, D_MODEL))
    scratch = [pltpu.VMEM((tm, D_FF), BF16)]
    if mix is not None:
        in_specs += mix["specs"]
        args += mix["args"]
        scratch += mix["scratch"]
    return pl.pallas_call(
        functools.partial(_ffn_kernel, final=final, mix_fn=mix and mix["fn"], n_mix=len(mix["args"]) if mix else 0),
        grid=(m // tm,),
        in_specs=in_specs,
        out_specs=pl.BlockSpec((tm, D_MODEL), lambda i: (i, 0)),
        out_shape=jax.ShapeDtypeStruct((m, D_MODEL), F32),
        scratch_shapes=scratch,
        compiler_params=_params("parallel"),
    )(*args)


def _rope_tables(t):
    inv = 1.0 / (ROPE_THETA ** (jnp.arange(0, HEAD_DIM, 2, dtype=F32) / HEAD_DIM))
    ang = jnp.arange(t, dtype=F32)[:, None] * inv[None, :]
    cos = jnp.cos(ang)
    sin = jnp.sin(ang)
    return (jnp.concatenate([cos, cos, cos, cos], axis=-1),
            jnp.concatenate([-sin, sin, -sin, sin], axis=-1))


def _proj_kernel(x_ref, g_ref, w_ref, cos_ref, sin_ref, *rest, plan, n_out):
    out_refs, (ys,) = rest[:n_out], rest[n_out:]
    n = _rms(x_ref[...], g_ref[...]).astype(BF16)
    tm = n.shape[0]
    cos = cos_ref[...]
    sin = sin_ref[...]
    lane = lax.broadcasted_iota(jnp.int32, cos.shape, 1)
    first_half = (lane & (HEAD_DIM - 1)) < HEAD_DIM // 2
    heads_per_tile = LANES // HEAD_DIM
    for off, width, outs in plan:
        y = _dot(n, w_ref[:, off:off + width])
        for o in outs:
            o_ref = out_refs[o["idx"]]
            dt = o_ref.dtype
            kind = o["kind"]
            for c in range(o["width"] // LANES):
                piece = y[:, o["off"] + c * LANES: o["off"] + (c + 1) * LANES]
                if o.get("rope"):
                    rot = jnp.where(first_half, pltpu.roll(piece, LANES - HEAD_DIM // 2, 1),
                                    pltpu.roll(piece, HEAD_DIM // 2, 1))
                    piece = piece * cos + rot * sin
                if o.get("scale"):
                    piece = piece * QK_SCALE
                if o.get("sigmoid"):
                    piece = jax.nn.sigmoid(piece)
                lanes = slice(c * LANES, (c + 1) * LANES)
                if kind == "flat":
                    o_ref[:, lanes] = piece.astype(dt)
                elif kind == "split":
                    hw = o["hw"]
                    per = LANES // hw
                    for k in range(per):
                        o_ref[c * per + k] = piece[:, k * hw:(k + 1) * hw].astype(dt)
                elif kind == "dilate":
                    r = o["r"]
                    if r == 1:
                        o_ref[0, 0, :, lanes] = piece.astype(dt)
                    else:
                        ys[...] = piece
                        for i in range(r):
                            o_ref[0, i, :, lanes] = ys[pl.ds(i, tm // r, stride=r), :].astype(dt)
                elif kind == "chunk":
                    ys[...] = piece
                    for l in range(NSA_CMP_STRIDE):
                        rows = ys[pl.ds(l, tm // NSA_CMP_STRIDE, stride=NSA_CMP_STRIDE), :].astype(dt)
                        for k in range(heads_per_tile):
                            o_ref[c * heads_per_tile + k, :, l * HEAD_DIM:(l + 1) * HEAD_DIM] = rows[
                                :, k * HEAD_DIM:(k + 1) * HEAD_DIM]
                else:
                    tile = o["tile"]
                    pt = piece.T
                    for k in range(heads_per_tile):
                        for kt in range(tm // tile):
                            o_ref[c * heads_per_tile + k, 0, kt, 0:HEAD_DIM, :] = pt[
                                k * HEAD_DIM:(k + 1) * HEAD_DIM, kt * tile:(kt + 1) * tile].astype(dt)
                            o_ref[c * heads_per_tile + k, 0, kt, HEAD_DIM:V_ROWS, :] = jnp.ones(
                                (V_ROWS - HEAD_DIM, tile), dt)


def _proj(h, gain, w, cos, sin, plan, out_defs, bsz, t_len):
    m = h.shape[0]
    tm = PROJ_TM
    nt = t_len // tm
    out_shapes, out_specs = [], []
    for d in out_defs:
        kind, dt = d[0], d[-1]
        if kind == "flat":
            shape, block, index = (m, d[1]), (tm, d[1]), (lambda i: (i, 0))
        elif kind == "split":
            shape, block, index = (d[1], m, d[2]), (d[1], tm, d[2]), (lambda i: (0, i, 0))
        elif kind == "dilate":
            r = d[1]
            shape, block = (bsz, r, t_len // r, d[2]), (1, r, tm // r, d[2])
            index = lambda i: (i // nt, 0, i % nt, 0)
        elif kind == "chunk":
            width = NSA_CMP_STRIDE * HEAD_DIM
            shape, block = (d[1], m // NSA_CMP_STRIDE, width), (d[1], tm // NSA_CMP_STRIDE, width)
            index = lambda i: (0, i, 0)
        else:
            tile = d[2]
            shape, block = (d[1], bsz, t_len // tile, V_ROWS, tile), (d[1], 1, tm // tile, V_ROWS, tile)
            index = lambda i: (0, i // nt, i % nt, 0, 0)
        out_shapes.append(jax.ShapeDtypeStruct(shape, dt))
        out_specs.append(pl.BlockSpec(block, index))
    return pl.pallas_call(
        functools.partial(_proj_kernel, plan=plan, n_out=len(out_defs)),
        grid=(m // tm,),
        in_specs=[
            pl.BlockSpec((tm, D_MODEL), lambda i: (i, 0)),
            pl.BlockSpec((1, D_MODEL), lambda i: (0, 0)),
            pl.BlockSpec(w.shape, lambda i: (0, 0)),
            pl.BlockSpec((tm, LANES), lambda i: (i % nt, 0)),
            pl.BlockSpec((tm, LANES), lambda i: (i % nt, 0)),
        ],
        out_specs=out_specs,
        out_shape=out_shapes,
        scratch_shapes=[pltpu.VMEM((tm, LANES), F32)],
        compiler_params=_params("parallel"),
    )(h, gain.reshape(1, D_MODEL), w, cos, sin)


def _banded_kernel(q_ref, kp_ref, kc_ref, vp_ref, vc_ref, *rest, n_kv, group, tl, pk, span, max_dist,
                   has_sink, gate_branch, want_lse):
    rest = list(rest)
    sink_ref = rest.pop(0) if has_sink else None
    gate_ref = rest.pop(0) if gate_branch is not None else None
    o_ref = rest.pop(0)
    lse_ref = rest.pop(0) if want_lse else None
    kcat, vcat, s_a, s_b = rest
    sb = BAND_SB
    kcat[0:pk, :] = kp_ref[0, 0]
    kcat[pk:, :] = kc_ref[0, 0]
    vcat[0:pk, :] = vp_ref[0, 0]
    vcat[pk:, :] = vc_ref[0, 0]
    nk = span + sb
    kw = n_kv * HEAD_DIM
    merged = group == 1
    stacked = n_kv if merged else group
    rows = stacked * sb
    key = lax.broadcasted_iota(jnp.int32, (nk, rows), 0)
    qry = lax.broadcasted_iota(jnp.int32, (nk, rows), 1) & (sb - 1)
    dist = qry + span - key
    in_band = (dist >= 0) & (dist <= max_dist)
    lane_head = lax.broadcasted_iota(jnp.int32, (1, rows), 1) >> _log2(sb)
    q_head = lax.broadcasted_iota(jnp.int32, (sb, kw), 1) >> _log2(HEAD_DIM)
    items = [(j, h) for j in range(tl // sb) for h in ([0] if merged else range(n_kv))]
    s_bufs = (s_a, s_b)

    def compute(first_tile):
        def scores(n):
            j, h = items[n]
            lo = pk + j * sb - span
            qrows = slice(j * sb, (j + 1) * sb)
            if merged:
                qj = q_ref[0, 0, qrows, :].astype(F32)
                qs = jnp.concatenate([jnp.where(q_head == r, qj, 0.0).astype(BF16) for r in range(n_kv)], axis=0)
                kk = kcat[lo:lo + nk, :]
            else:
                qs = jnp.concatenate([q_ref[0, 0, qrows, (h * group + g) * HEAD_DIM:(h * group + g + 1) * HEAD_DIM]
                                      for g in range(group)], axis=0)
                kk = kcat[lo:lo + nk, h * HEAD_DIM:(h + 1) * HEAD_DIM]
            ok = in_band
            if first_tile and j * sb < span:
                ok = ok & (key >= span - j * sb)
            s_bufs[n % 2][...] = jnp.where(ok, _dot_t(kk, qs), NEG_INF)

        outs, lses = [], []
        scores(0)
        for n, (j, h) in enumerate(items):
            if n + 1 < len(items):
                scores(n + 1)
            s = s_bufs[n % 2][...]
            lo = pk + j * sb - span
            qrows = slice(j * sb, (j + 1) * sb)
            m = jnp.max(s, axis=0, keepdims=True)
            if has_sink:
                sink = jnp.full((1, rows), sink_ref[h * group], F32)
                for g in range(1, group):
                    sink = jnp.where(lane_head == g, sink_ref[h * group + g], sink)
                sink = sink * LOG2E
                m = jnp.maximum(m, sink)
            p = jnp.exp2(s - m)
            d = jnp.sum(p, axis=0, keepdims=True)
            if has_sink:
                d = d + jnp.exp2(sink - m)
            d = jnp.where(d > 0, d, 1.0)
            inv = 1.0 / d
            vv = vcat[lo:lo + nk, :] if merged else vcat[lo:lo + nk, h * HEAD_DIM:(h + 1) * HEAD_DIM]
            o_t = _dot_tn(vv, p.astype(BF16))
            if want_lse:
                lse_t = jnp.broadcast_to(m * LN2 + jnp.log(d), (HEAD_DIM, rows))
            for r in range(stacked):
                cols = slice(r * sb, (r + 1) * sb)
                vrow = r * HEAD_DIM if merged else 0
                o = (o_t[vrow:vrow + HEAD_DIM, cols] * inv[:, cols]).T
                if gate_ref is not None:
                    gc = r * 3 + gate_branch
                    o = o * gate_ref[h, 0, qrows, gc:gc + 1]
                outs.append(o)
                if want_lse:
                    lses.append(lse_t[:, cols].T)
            if merged or h == n_kv - 1:
                o_ref[0, 0, qrows, :] = jnp.concatenate(outs, axis=-1).astype(o_ref.dtype)
                if want_lse:
                    lse_ref[0, 0, qrows, :] = jnp.concatenate(lses, axis=-1)
                outs, lses = [], []

    first = pl.program_id(2) == 0
    pl.when(first)(functools.partial(compute, True))
    pl.when(jnp.logical_not(first))(functools.partial(compute, False))


def _banded(q, k, v, *, n_kv, group, max_dist, tl, pk, span, sink=None, gate=None, gate_branch=None,
            want_lse=False):
    bsz, n_res, length, qw = q.shape
    kw = n_kv * HEAD_DIM
    s_shape = (span + BAND_SB, (n_kv if group == 1 else group) * BAND_SB)
    ratio = tl // pk
    prev = lambda b, i, l: (b, i, jnp.maximum(l * ratio - 1, 0), 0)
    cur = lambda b, i, l: (b, i, l, 0)
    in_specs = [
        pl.BlockSpec((1, 1, tl, qw), cur),
        pl.BlockSpec((1, 1, pk, kw), prev),
        pl.BlockSpec((1, 1, tl, kw), cur),
        pl.BlockSpec((1, 1, pk, kw), prev),
        pl.BlockSpec((1, 1, tl, kw), cur),
    ]
    args = [q, k, k, v, v]
    if sink is not None:
        in_specs.append(pl.BlockSpec(memory_space=pltpu.SMEM))
        args.append(sink)
    if gate is not None:
        in_specs.append(pl.BlockSpec((n_kv, 1, tl, LANES), lambda b, i, l: (0, b, l, 0)))
        args.append(gate)
    out_spec = pl.BlockSpec((1, 1, tl, qw), cur)
    out_shape = [jax.ShapeDtypeStruct(q.shape, BF16)]
    out_specs = [out_spec]
    if want_lse:
        out_shape.append(jax.ShapeDtypeStruct(q.shape, F32))
        out_specs.append(out_spec)
    res = pl.pallas_call(
        functools.partial(_banded_kernel, n_kv=n_kv, group=group, tl=tl, pk=pk, span=span,
                          max_dist=max_dist, has_sink=sink is not None, gate_branch=gate_branch,
                          want_lse=want_lse),
        grid=(bsz, n_res, length // tl),
        in_specs=in_specs,
        out_specs=out_specs,
        out_shape=out_shape,
        scratch_shapes=[pltpu.VMEM((pk + tl, kw), BF16), pltpu.VMEM((pk + tl, kw), BF16),
                        pltpu.VMEM(s_shape, F32), pltpu.VMEM(s_shape, F32)],
        compiler_params=_params("parallel", "parallel", "parallel"),
    )(*args)
    return res if want_lse else res[0]


def _compress_one(z_ref, pe_ref, w1_ref, w2_ref, o_ref):
    half = NSA_CMP_STRIDE * HEAD_DIM
    z = z_ref[0, 0].astype(F32)
    za = (z + pe_ref[0, :, 0:half]).astype(BF16)
    zb = (z + pe_ref[0, :, half:2 * half]).astype(BF16)
    ya = _dot(za, w1_ref[0, 0:half, :])
    yb = _dot(zb, w1_ref[0, half:2 * half, :])
    nrow = ya.shape[0]
    hid = jax.nn.gelu(ya + pltpu.roll(yb, nrow - 1, 0))
    o_ref[0, 0] = _dot(hid.astype(BF16), w2_ref[0]).astype(o_ref.dtype)


def _compress_kernel(zk_ref, zv_ref, pek_ref, pev_ref, w1k_ref, w1v_ref, w2k_ref, w2v_ref, ok_ref, ov_ref):
    _compress_one(zk_ref, pek_ref, w1k_ref, w2k_ref, ok_ref)
    _compress_one(zv_ref, pev_ref, w1v_ref, w2v_ref, ov_ref)


def _compress(zk, zv, pe, w1, w2):
    hk, bsz, nch, width = zk.shape
    zspec = pl.BlockSpec((1, 1, nch, width), lambda h, b: (h, b, 0, 0))
    ospec = pl.BlockSpec((1, 1, nch, HEAD_DIM), lambda h, b: (h, b, 0, 0))
    oshape = jax.ShapeDtypeStruct((hk, bsz, nch, HEAD_DIM), BF16)

    def wspec(arr, which):
        return pl.BlockSpec((1,) + arr.shape[1:], lambda h, b: (which, 0, 0))

    return pl.pallas_call(
        _compress_kernel,
        grid=(hk, bsz),
        in_specs=[zspec, zspec, wspec(pe, 0), wspec(pe, 1), wspec(w1, 0), wspec(w1, 1), wspec(w2, 0), wspec(w2, 1)],
        out_specs=[ospec, ospec],
        out_shape=[oshape, oshape],
        compiler_params=_params("parallel", "parallel"),
    )(zk, zv, pe, pe, w1, w1, w2, w2)


def _topk_select(score, k, idxf):
    ncand = score.shape[0]
    rem = score
    sel = jnp.zeros_like(score)
    for _ in range(k):
        mx = jnp.max(rem, axis=0, keepdims=True)
        first = jnp.min(jnp.where(rem == mx, idxf, float(ncand)), axis=0, keepdims=True)
        pick = idxf == first
        sel = jnp.where(pick, jnp.where(mx > 0.5 * NEG_INF, 1.0, 0.0), sel)
        rem = jnp.where(pick, REMOVED, rem)
    return sel


def _cmp_kernel(q_ref, kc_ref, vc_ref, gate_ref, o_ref, bias_ref, *, tq, group):
    qi = pl.program_id(2)
    kc = kc_ref[0, 0]
    vc = vc_ref[0, 0]
    ncp = kc.shape[0]
    nsb = bias_ref.shape[-1]
    rows = group * tq
    n = lax.broadcasted_iota(jnp.int32, (ncp, rows), 0)
    t = qi * tq + (lax.broadcasted_iota(jnp.int32, (ncp, rows), 1) & (tq - 1))
    visible = n * NSA_CMP_STRIDE + (NSA_CMP_LEN - 1) <= t
    qs = jnp.concatenate([q_ref[0, :, g * HEAD_DIM:(g + 1) * HEAD_DIM] for g in range(group)], axis=0)
    s = jnp.where(visible, _dot_t(kc, qs), NEG_INF)
    m = jnp.max(s, axis=0, keepdims=True)
    p = jnp.where(visible, jnp.exp2(s - m), 0.0)
    d = jnp.sum(p, axis=0, keepdims=True)
    d = jnp.where(d > 0, d, 1.0)
    pc = p * (1.0 / d)
    o_t = _dot_tn(vc, pc.astype(BF16))
    outs = []
    psum = pc[:, 0:tq]
    for g in range(group):
        if g:
            psum = psum + pc[:, g * tq:(g + 1) * tq]
        outs.append(o_t[:, g * tq:(g + 1) * tq].T * gate_ref[0, 0, :, g * 3:g * 3 + 1])
    o_ref[0] = jnp.concatenate(outs, axis=-1).astype(o_ref.dtype)

    jj = lax.broadcasted_iota(jnp.int32, (nsb, ncp), 0) * NSA_SEL_LEN
    nn = lax.broadcasted_iota(jnp.int32, (nsb, ncp), 1) * NSA_CMP_STRIDE
    ov = jnp.maximum(jnp.minimum(nn + NSA_CMP_LEN, jj + NSA_SEL_LEN) - jnp.maximum(nn, jj), 0)
    ov = (ov.astype(F32) / NSA_CMP_LEN).astype(BF16)
    p_hi = psum.astype(BF16)
    p_lo = (psum - p_hi.astype(F32)).astype(BF16)
    imp = _dot(ov, p_hi) + _dot(ov, p_lo)

    j = lax.broadcasted_iota(jnp.int32, (nsb, tq), 0)
    tt = qi * tq + lax.broadcasted_iota(jnp.int32, (nsb, tq), 1)
    cb = tt >> _log2(NSA_SEL_LEN)
    forced = (j == 0) | (j == cb) | (j == cb - 1)
    score = jnp.where(j <= cb, jnp.where(forced, FORCE_SCORE, imp), NEG_INF)
    sel = _topk_select(score, min(NSA_SEL_COUNT, nsb), j.astype(F32))
    bias_ref[0, 0] = jnp.where(sel > 0.5, 0.0, MASK_BIAS).T.astype(bias_ref.dtype)


def _cmp_attention(qu, kc, vc, gate, t_len):
    bsz = qu.shape[0]
    hk, _, ncp, _ = kc.shape
    group = B_Q_HEADS // B_KV_HEADS
    tq = CMP_TQ
    nsb = t_len // NSA_SEL_LEN
    gw = group * HEAD_DIM
    cspec = pl.BlockSpec((1, 1, ncp, HEAD_DIM), lambda b, h, i: (h, b, 0, 0))
    return pl.pallas_call(
        functools.partial(_cmp_kernel, tq=tq, group=group),
        grid=(bsz, hk, t_len // tq),
        in_specs=[
            pl.BlockSpec((1, tq, gw), lambda b, h, i: (b, i, h)),
            cspec, cspec,
            pl.BlockSpec((1, 1, tq, LANES), lambda b, h, i: (h, b, i, 0)),
        ],
        out_specs=[
            pl.BlockSpec((1, tq, gw), lambda b, h, i: (b, i, h)),
            pl.BlockSpec((1, 1, tq, nsb), lambda b, h, i: (h, b, i, 0)),
        ],
        out_shape=[
            jax.ShapeDtypeStruct((bsz, t_len, hk * gw), BF16),
            jax.ShapeDtypeStruct((hk, bsz, t_len, nsb), BF16),
        ],
        compiler_params=_params("parallel", "parallel", "parallel"),
    )(qu, kc, vc, gate)


def _moba_gate_kernel(q_ref, k_ref, bias_ref, km_hi, km_lo, *, tq):
    qi = pl.program_id(2)
    t_len = k_ref.shape[2]
    ncol = bias_ref.shape[-1]
    nb = t_len // MOBA_BLOCK

    @pl.when(qi == 0)
    def _():
        blk = lax.broadcasted_iota(jnp.int32, (ncol, t_len), 0)
        pos = lax.broadcasted_iota(jnp.int32, (ncol, t_len), 1)
        member = jnp.where((pos >> _log2(MOBA_BLOCK)) == blk, 1.0, 0.0).astype(BF16)
        kmean = _dot(member, k_ref[0, 0]) * (1.0 / MOBA_BLOCK)
        hi = kmean.astype(BF16)
        km_hi[...] = hi
        km_lo[...] = (kmean - hi.astype(F32)).astype(BF16)

    q = q_ref[0, 0]
    gate = _dot_t(km_hi[...], q) + _dot_t(km_lo[...], q)
    j = lax.broadcasted_iota(jnp.int32, (ncol, tq), 0)
    t = qi * tq + lax.broadcasted_iota(jnp.int32, (ncol, tq), 1)
    cb = t >> _log2(MOBA_BLOCK)
    score = jnp.where(j < cb, gate, jnp.where(j < nb, NEG_INF, REMOVED))
    sel = _topk_select(score, min(MOBA_TOPK, nb - 1), j.astype(F32))
    keep = (sel > 0.5) | (j == cb)
    bias_ref[0, 0] = jnp.where(keep, 0.0, MASK_BIAS).T.astype(bias_ref.dtype)


def _moba_gate(q, k):
    nh, bsz, t_len, _ = q.shape
    tq = MOBA_T
    return pl.pallas_call(
        functools.partial(_moba_gate_kernel, tq=tq),
        grid=(nh, bsz, t_len // tq),
        in_specs=[
            pl.BlockSpec((1, 1, tq, HEAD_DIM), lambda h, b, i: (h, b, i, 0)),
            pl.BlockSpec((1, 1, t_len, HEAD_DIM), lambda h, b, i: (h, b, 0, 0)),
        ],
        out_specs=pl.BlockSpec((1, 1, tq, HEAD_DIM), lambda h, b, i: (h, b, i, 0)),
        out_shape=jax.ShapeDtypeStruct((nh, bsz, t_len, HEAD_DIM), BF16),
        scratch_shapes=[pltpu.VMEM((HEAD_DIM, HEAD_DIM), BF16), pltpu.VMEM((HEAD_DIM, HEAD_DIM), BF16)],
        compiler_params=_params("parallel", "parallel", "arbitrary"),
    )(q, k)


def _flash_kernel(q_ref, bias_ref, k_ref, vt_ref, *rest, group, tq, tk, blk_len, q_scale, gate_branch):
    if gate_branch is not None:
        gate_ref, o_ref, qa, ka, m_s, acc, s_a, s_b = rest
    else:
        gate_ref = None
        o_ref, qa, ka, m_s, acc, s_a, s_b = rest
    qi = pl.program_id(2)
    rows = group * tq
    t_len = k_ref.shape[2]

    @pl.when(qi == 0)
    def _():
        ka[:, 0:HEAD_DIM] = k_ref[0, 0]
        kpos = lax.broadcasted_iota(jnp.int32, (t_len, HEAD_DIM), 0)
        kblk = lax.broadcasted_iota(jnp.int32, (t_len, HEAD_DIM), 1)
        ka[:, HEAD_DIM:2 * HEAD_DIM] = jnp.where((kpos >> _log2(blk_len)) == kblk, 1.0, 0.0).astype(BF16)

    bias = bias_ref[0, 0]
    for g in range(group):
        qg = q_ref[0, 0, :, g * HEAD_DIM:(g + 1) * HEAD_DIM]
        if q_scale != 1.0:
            qg = (qg.astype(F32) * q_scale).astype(BF16)
        qa[g * tq:(g + 1) * tq, 0:HEAD_DIM] = qg
        qa[g * tq:(g + 1) * tq, HEAD_DIM:2 * HEAD_DIM] = bias
    m_s[...] = jnp.full_like(m_s, NEG_INF)
    acc[...] = jnp.zeros_like(acc)

    def scores(ki, s_ref):
        start = pl.multiple_of(ki * tk, tk)
        s_ref[...] = _dot_t(ka[pl.ds(start, tk), :], qa[...])

    def softmax_pv(ki, s_ref, masked):
        s = s_ref[...]
        if masked:
            kpos = ki * tk + lax.broadcasted_iota(jnp.int32, (tk, rows), 0)
            qpos = qi * tq + (lax.broadcasted_iota(jnp.int32, (tk, rows), 1) & (tq - 1))
            s = jnp.where(kpos <= qpos, s, NEG_INF)
        m_old = m_s[...]
        m_new = jnp.maximum(m_old, jnp.max(s, axis=0, keepdims=True))
        alpha = jnp.exp2(m_old - m_new)
        p = jnp.exp2(s - m_new)
        acc[...] = alpha * acc[...] + _dot(vt_ref[0, 0, ki], p.astype(BF16))
        m_s[...] = m_new

    n_full = (qi * tq) >> _log2(tk)
    scores(0, s_a)

    def pair(j, carry):
        k0 = 2 * j
        scores(k0 + 1, s_b)
        softmax_pv(k0, s_a, False)
        scores(k0 + 2, s_a)
        softmax_pv(k0 + 1, s_b, False)
        return carry

    lax.fori_loop(0, n_full >> 1, pair, 0)
    cur = (n_full >> 1) << 1

    @pl.when((n_full & 1) == 1)
    def _():
        scores(cur + 1, s_b)
        softmax_pv(cur, s_a, False)
        softmax_pv(cur + 1, s_b, True)

    @pl.when((n_full & 1) == 0)
    def _():
        softmax_pv(cur, s_a, True)

    d = acc[HEAD_DIM:HEAD_DIM + 1, :]
    d = jnp.where(d > 0, d, 1.0)
    o_t = acc[0:HEAD_DIM, :] * (1.0 / d)
    outs = []
    for g in range(group):
        og = o_t[:, g * tq:(g + 1) * tq].T
        if gate_ref is not None:
            gc = g * 3 + gate_branch
            og = og * gate_ref[0, 0, :, gc:gc + 1]
        outs.append(og)
    res = outs[0] if group == 1 else jnp.concatenate(outs, axis=-1)
    o_ref[0, 0] = res.astype(o_ref.dtype)


def _flash(q, bias, k, vt, *, group, tq, blk_len, q_scale, q_index, gate=None, gate_branch=None):
    nh, bsz, t_len, _ = k.shape
    gw = group * HEAD_DIM
    nt, tk = vt.shape[2], vt.shape[4]
    in_specs = [
        pl.BlockSpec((1, 1, tq, gw), q_index),
        pl.BlockSpec((1, 1, tq, HEAD_DIM), lambda h, b, i: (h, b, i, 0)),
        pl.BlockSpec((1, 1, t_len, HEAD_DIM), lambda h, b, i: (h, b, 0, 0)),
        pl.BlockSpec((1, 1, nt, V_ROWS, tk), lambda h, b, i: (h, b, 0, 0, 0)),
    ]
    args = [q, bias, k, vt]
    if gate is not None:
        in_specs.append(pl.BlockSpec((1, 1, tq, LANES), lambda h, b, i: (h, b, i, 0)))
        args.append(gate)
    rows = group * tq
    return pl.pallas_call(
        functools.partial(_flash_kernel, group=group, tq=tq, tk=tk, blk_len=blk_len, q_scale=q_scale,
                          gate_branch=gate_branch),
        grid=(nh, bsz, t_len // tq),
        in_specs=in_specs,
        out_specs=pl.BlockSpec((1, 1, tq, gw), q_index),
        out_shape=jax.ShapeDtypeStruct(q.shape, BF16),
        scratch_shapes=[
            pltpu.VMEM((rows, 2 * HEAD_DIM), BF16),
            pltpu.VMEM((t_len, 2 * HEAD_DIM), BF16),
            pltpu.VMEM((1, rows), F32),
            pltpu.VMEM((V_ROWS, rows), F32),
            pltpu.VMEM((tk, rows), F32),
            pltpu.VMEM((tk, rows), F32),
        ],
        compiler_params=_params("parallel", "parallel", "arbitrary"),
    )(*args)


def _even_delta(refs, scratch):
    oa_ref, oc_ref, os_ref, ow_ref, w_ref = refs
    na = oa_ref.shape[-1]
    ob = (oc_ref[...].astype(F32) + os_ref[...].astype(F32) + ow_ref[...].astype(F32)).astype(BF16)
    return _dot(oa_ref[...], w_ref[0:na, :]) + _dot(ob, w_ref[na:, :])


def _even_mix(oa, ocmp, osel, owin, w):
    aspec = pl.BlockSpec((FFN_TM, oa.shape[-1]), lambda i: (i, 0))
    return dict(args=[oa, ocmp, osel, owin, w], specs=[aspec] * 4 + [_resident(w.shape)], scratch=[],
                fn=_even_delta)


def _odd_delta(refs, scratch, *, dilations):
    ng = len(dilations)
    o_refs, l_refs = refs[:ng], refs[ng:2 * ng]
    od_ref, w_ref = refs[2 * ng:]
    scratch = list(scratch)
    tm = od_ref.shape[1]

    def tokens(ref, r):
        if r == 1:
            return ref[0, 0].astype(F32)
        pieces = []
        for c in range(ref.shape[-1] // LANES):
            scr = scratch.pop(0)
            for i in range(r):
                scr[pl.ds(i, tm // r, stride=r), :] = ref[0, i, :, c * LANES:(c + 1) * LANES].astype(F32)
            pieces.append(scr[...])
        return jnp.concatenate(pieces, axis=-1)

    outs = [tokens(ref, r) for ref, r in zip(o_refs, dilations)]
    lses = [tokens(ref, r) for ref, r in zip(l_refs, dilations)]
    mx = functools.reduce(jnp.maximum, lses)
    es = [jnp.exp(l - mx) for l in lses]
    tot = functools.reduce(lambda a, b: a + b, es)
    oc = functools.reduce(lambda a, b: a + b, [(e / tot) * o for e, o in zip(es, outs)])
    nc = oc.shape[-1]
    delta = _dot(oc.astype(BF16), w_ref[0:nc, :])
    for hh in range(od_ref.shape[0]):
        delta = delta + _dot(od_ref[hh], w_ref[nc + hh * HEAD_DIM: nc + (hh + 1) * HEAD_DIM, :])
    return delta


def _odd_mix(outs, lses, od, w, t_len):
    tm = FFN_TM
    nt = t_len // tm
    gw = outs[0].shape[-1]
    dilations = tuple(o.shape[1] for o in outs)
    gspecs = [pl.BlockSpec((1, r, tm // r, gw), lambda i: (i // nt, 0, i % nt, 0)) for r in dilations]
    n_scr = 2 * (gw // LANES) * sum(1 for r in dilations if r > 1)
    specs = gspecs + gspecs + [pl.BlockSpec((od.shape[0], tm, HEAD_DIM), lambda i: (0, i, 0)), _resident(w.shape)]
    return dict(args=[*outs, *lses, od, w], specs=specs, scratch=[pltpu.VMEM((tm, LANES), F32)] * n_scr,
                fn=functools.partial(_odd_delta, dilations=dilations))


def _col_ranges(sizes):
    offs, acc = [], 0
    for s in sizes:
        offs.append((acc, acc + s))
        acc += s
    return offs


def _even_mixer(h, gain, w_in, w_out, sinks, cmp_pe, cmp_w1, cmp_w2, cos, sin, bsz, t_len):
    m = bsz * t_len
    qa_w, kva_w = A_Q_HEADS * HEAD_DIM, A_KV_HEADS * HEAD_DIM
    qb_w, kvb_w = B_Q_HEADS * HEAD_DIM, B_KV_HEADS * HEAD_DIM
    sizes = [qa_w, kva_w, kva_w, qb_w] + [kvb_w] * 6 + [3 * B_Q_HEADS]
    (aq, ak, av, bq, bkc, bvc, bks, bvs, bkw, bvw, bg) = [w_in[:, a:b] for a, b in _col_ranges(sizes)]
    group = B_Q_HEADS // B_KV_HEADS
    gpad = jnp.zeros((D_MODEL, LANES - 3 * group), w_in.dtype)
    gates = [x for hk in range(B_KV_HEADS) for x in (bg[:, hk * 3 * group:(hk + 1) * 3 * group], gpad)]
    w = jnp.concatenate([aq, bq, ak, bkw, bks, bq, av, bvw, bvs, bkc, bvc] + gates, axis=1).astype(BF16)
    c = [0]

    def take(width):
        c[0] += width
        return c[0] - width

    def out(idx, off, width, kind, **kw):
        return dict(idx=idx, off=off, width=width, kind=kind, **kw)

    plan = [
        (take(qa_w), qa_w, [out(0, 0, qa_w, "flat", rope=True, scale=True)]),
        (take(qb_w), qb_w, [out(1, 0, qb_w, "flat", rope=True, scale=True)]),
        (take(3 * kva_w), 3 * kva_w, [out(2, 0, kva_w, "flat", rope=True),
                                      out(3, kva_w, kvb_w, "flat", rope=True),
                                      out(4, kva_w + kvb_w, kvb_w, "split", hw=HEAD_DIM, rope=True)]),
        (take(qb_w), qb_w, [out(5, 0, qb_w, "flat", scale=True)]),
        (take(5 * kvb_w), 5 * kvb_w, [out(6, 0, kva_w, "flat"), out(7, kva_w, kvb_w, "flat"),
                                      out(8, 2 * kvb_w, kvb_w, "vt", tile=FLASH_TK),
                                      out(9, 3 * kvb_w, kvb_w, "chunk"),
                                      out(10, 4 * kvb_w, kvb_w, "chunk")]),
        (take(B_KV_HEADS * LANES), B_KV_HEADS * LANES,
         [out(11, 0, B_KV_HEADS * LANES, "split", hw=LANES, sigmoid=True)]),
    ]
    out_defs = [("flat", qa_w, BF16), ("flat", qb_w, BF16), ("flat", kva_w, BF16), ("flat", kvb_w, BF16),
                ("split", B_KV_HEADS, HEAD_DIM, BF16), ("flat", qb_w, BF16), ("flat", kva_w, BF16),
                ("flat", kvb_w, BF16), ("vt", B_KV_HEADS, FLASH_TK, BF16), ("chunk", B_KV_HEADS, BF16),
                ("chunk", B_KV_HEADS, BF16), ("split", B_KV_HEADS, LANES, F32)]
    (aq_r, bq_r, ak_r, bkw_r, bks_r, bq_u, av_, bvw_, bvs_t, zk, zv, gate) = _proj(
        h, gain, w, cos, sin, plan, out_defs, bsz, t_len)

    def seq(x):
        return x.reshape(bsz, 1, t_len, x.shape[-1])

    def heads(x):
        return x.reshape(x.shape[0], bsz, t_len, x.shape[-1])

    gate4 = heads(gate)
    oa = _banded(seq(aq_r), seq(ak_r), seq(av_), n_kv=A_KV_HEADS, group=A_Q_HEADS // A_KV_HEADS,
                 max_dist=A_WINDOW - 1, tl=BAND_TL, pk=128, span=128, sink=sinks)
    nch = t_len // NSA_CMP_STRIDE
    zshape = (B_KV_HEADS, bsz, nch, NSA_CMP_STRIDE * HEAD_DIM)
    kc, vc = _compress(zk.reshape(zshape), zv.reshape(zshape), cmp_pe.reshape(2, 1, NSA_CMP_LEN * HEAD_DIM),
                       cmp_w1.astype(BF16), cmp_w2.astype(BF16))
    ocmp, bias = _cmp_attention(bq_u.reshape(bsz, t_len, qb_w), kc, vc, gate4, t_len)
    osel = _flash(seq(bq_r).reshape(1, bsz, t_len, qb_w), bias, heads(bks_r), bvs_t, group=group, tq=FLASH_T,
                  blk_len=NSA_SEL_LEN, q_scale=1.0, q_index=lambda hk, b, i: (0, b, i, hk), gate=gate4,
                  gate_branch=1)
    owin = _banded(seq(bq_r), seq(bkw_r), seq(bvw_), n_kv=B_KV_HEADS, group=group,
                   max_dist=NSA_WINDOW - 1, tl=BAND_TL, pk=512, span=512, gate=gate4, gate_branch=2)
    return _even_mix(oa.reshape(m, qa_w), ocmp.reshape(m, qb_w), osel.reshape(m, qb_w),
                     owin.reshape(m, qb_w), w_out.astype(BF16))


def _odd_mixer(h, gain, w_in, w_out, cos, sin, bsz, t_len):
    m = bsz * t_len
    cw = C_HEADS * HEAD_DIM
    dw = D_HEADS * HEAD_DIM
    gw = C_HEADS_PER_GROUP * HEAD_DIM
    n_groups = len(C_GROUPS)

    def group_outs(base, **kw):
        return [dict(idx=base + gi, off=gi * gw, width=gw, kind="dilate", r=r, **kw)
                for gi, (_, r) in enumerate(C_GROUPS)]

    plan = [
        (0, cw, group_outs(0, rope=True, scale=True)),
        (cw, cw, group_outs(n_groups, rope=True)),
        (2 * cw, cw, group_outs(2 * n_groups)),
        (3 * cw, 3 * dw, [dict(idx=3 * n_groups, off=0, width=dw, kind="split", hw=HEAD_DIM, rope=True),
                          dict(idx=3 * n_groups + 1, off=dw, width=dw, kind="split", hw=HEAD_DIM, rope=True),
                          dict(idx=3 * n_groups + 2, off=2 * dw, width=dw, kind="vt", tile=MOBA_TK)]),
    ]
    out_defs = [("dilate", r, gw, BF16) for _ in range(3) for _, r in C_GROUPS]
    out_defs += [("split", D_HEADS, HEAD_DIM, BF16), ("split", D_HEADS, HEAD_DIM, BF16),
                 ("vt", D_HEADS, MOBA_TK, BF16)]
    res = _proj(h, gain, w_in.astype(BF16), cos, sin, plan, out_defs, bsz, t_len)
    cq, ck, cv = res[0:n_groups], res[n_groups:2 * n_groups], res[2 * n_groups:3 * n_groups]
    dq, dk, dv_t = res[3 * n_groups:]
    outs, lses = [], []
    for gi, (wlen, r) in enumerate(C_GROUPS):
        o, lse = _banded(cq[gi], ck[gi], cv[gi], n_kv=C_HEADS_PER_GROUP, group=1, max_dist=wlen // r,
                         tl=min(BAND_TL, t_len // r), pk=128, span=128, want_lse=True)
        outs.append(o)
        lses.append(lse)

    def heads(x):
        return x.reshape(x.shape[0], bsz, t_len, x.shape[-1])

    bias = _moba_gate(heads(dq), heads(dk))
    od = _flash(heads(dq), bias, heads(dk), dv_t, group=1, tq=MOBA_T, blk_len=MOBA_BLOCK, q_scale=QK_SCALE,
                q_index=lambda hh, b, i: (hh, b, i, 0))
    return _odd_mix(outs, lses, od.reshape(D_HEADS, m, HEAD_DIM), w_out.astype(BF16), t_len)


def kernel(x, ffn_norm_pre, mix_norm, ffn_norm_post, ffn_wi, ffn_wo, even_w_in, even_w_out, even_sinks,
           nsa_cmp_pe, nsa_cmp_w1, nsa_cmp_w2, odd_w_in, odd_w_out, final_norm):
    bsz, t_len, _ = x.shape
    depth = ffn_wi.shape[0]
    cos, sin = _rope_tables(t_len)
    wi = ffn_wi.astype(BF16)
    wo = ffn_wo.astype(BF16)
    h = x.reshape(bsz * t_len, D_MODEL)
    for layer in range(depth):
        i = layer // 2
        h = _ffn(h, ffn_norm_pre[layer], wi[layer, 0], wo[layer, 0])
        if layer % 2 == 0:
            mix = _even_mixer(h, mix_norm[layer], even_w_in[i], even_w_out[i], even_sinks[i], nsa_cmp_pe[i],
                              nsa_cmp_w1[i], nsa_cmp_w2[i], cos, sin, bsz, t_len)
        else:
            mix = _odd_mixer(h, mix_norm[layer], odd_w_in[i], odd_w_out[i], cos, sin, bsz, t_len)
        last = layer == depth - 1
        h = _ffn(h, ffn_norm_post[layer], wi[layer, 1], wo[layer, 1], final_norm if last else None, mix=mix)
    return h.reshape(bsz, t_len, D_MODEL)
```

```python
import functools

import jax
import jax.numpy as jnp
from jax import lax
from jax.experimental import pallas as pl
from jax.experimental.pallas import tpu as pltpu

D_MODEL = 1024
HEAD_DIM = 64
ROPE_THETA = 10000.0
NORM_EPS = 1e-6
D_FF = 2816
NEG_INF = -1e30
FORCE_SCORE = 1e4

A_Q_HEADS = 8
A_KV_HEADS = 2
A_WINDOW = 128
B_Q_HEADS = 8
B_KV_HEADS = 2
NSA_CMP_LEN = 32
NSA_CMP_STRIDE = 16
NSA_CMP_HIDDEN = 256
NSA_SEL_LEN = 64
NSA_SEL_COUNT = 8
NSA_WINDOW = 512
C_GROUPS = ((128, 1), (512, 4), (2048, 16))
C_HEADS_PER_GROUP = 4
C_HEADS = len(C_GROUPS) * C_HEADS_PER_GROUP
D_HEADS = 4
MOBA_BLOCK = 256
MOBA_TOPK = 3

LANES = 128
LOG2E = 1.4426950408889634
LN2 = 0.6931471805599453
QK_SCALE = HEAD_DIM ** -0.5 * LOG2E
MASK_BIAS = -32768.0
V_ROWS = HEAD_DIM + 16
REMOVED = -3e38
VMEM_LIMIT = 52 * 1024 * 1024

FFN_TM = 512
FFN_TF = 256
PROJ_TM = 512
BAND_TL = 512
BAND_SB = 128
FLASH_T = 256
FLASH_TK = 256
MOBA_T = 512
MOBA_TK = 512
GATE_TQ = 4096
CMP_TQ = 1024

BF16 = jnp.bfloat16
F32 = jnp.float32


def _params(*sem):
    return pltpu.CompilerParams(dimension_semantics=sem, vmem_limit_bytes=VMEM_LIMIT)


def _rms(x, g):
    return x * lax.rsqrt(jnp.mean(x * x, axis=-1, keepdims=True) + NORM_EPS) * g


def _log2(n):
    assert n & (n - 1) == 0
    return n.bit_length() - 1


def _dot(a, b):
    return jnp.dot(a, b, preferred_element_type=F32)


def _dot_t(a, b):
    return lax.dot_general(a, b, (((1,), (1,)), ((), ())), preferred_element_type=F32)


def _dot_tn(a, b):
    return lax.dot_general(a, b, (((0,), (0,)), ((), ())), preferred_element_type=F32)


def _ffn_kernel(x_ref, g_ref, wi_ref, wo_ref, *rest, final, mix_fn, n_mix):
    rest = list(rest)
    fg_ref = rest.pop(0) if final else None
    mix_refs = [rest.pop(0) for _ in range(n_mix)]
    o_ref, act_scr = rest.pop(0), rest.pop(0)
    x = x_ref[...]
    if mix_fn is not None:
        x = x + mix_fn(mix_refs, rest)
    o_ref[...] = x
    n = _rms(x, g_ref[...]).astype(BF16)
    for c in range(D_FF // FFN_TF):
        cols = slice(c * FFN_TF, (c + 1) * FFN_TF)
        gate = _dot(n, wi_ref[:, cols])
        up = _dot(n, wi_ref[:, D_FF + c * FFN_TF:D_FF + (c + 1) * FFN_TF])
        act_scr[:, cols] = (gate * jax.nn.sigmoid(gate) * up).astype(BF16)
    h = o_ref[...] + 0.5 * _dot(act_scr[...], wo_ref[...])
    if final:
        h = _rms(h, fg_ref[...])
    o_ref[...] = h


def _resident(shape):
    return pl.BlockSpec(shape, lambda i: (0,) * len(shape), pipeline_mode=pl.Buffered(1))


def _ffn(h, gain, wi, wo, final_gain=None, mix=None):
    m = h.shape[0]
    tm = FFN_TM
    final = final_gain is not None
    in_specs = [
        pl.BlockSpec((tm, D_MODEL), lambda i: (i, 0)),
        pl.BlockSpec((1, D_MODEL), lambda i: (0, 0)),
        _resident(wi.shape),
        _resident(wo.shape),
    ]
    args = [h, gain.reshape(1, D_MODEL), wi, wo]
    if final:
        in_specs.append(pl.BlockSpec((1, D_MODEL), lambda i: (0, 0)))
        args.append(final_gain.reshape(1, D_MODEL))
    scratch = [pltpu.VMEM((tm, D_FF), BF16)]
    if mix is not None:
        in_specs += mix["specs"]
        args += mix["args"]
        scratch += mix["scratch"]
    return pl.pallas_call(
        functools.partial(_ffn_kernel, final=final, mix_fn=mix and mix["fn"], n_mix=len(mix["args"]) if mix else 0),
        grid=(m // tm,),
        in_specs=in_specs,
        out_specs=pl.BlockSpec((tm, D_MODEL), lambda i: (i, 0)),
        out_shape=jax.ShapeDtypeStruct((m, D_MODEL), F32),
        scratch_shapes=scratch,
        compiler_params=_params("parallel"),
    )(*args)


def _rope_tables(t):
    inv = 1.0 / (ROPE_THETA ** (jnp.arange(0, HEAD_DIM, 2, dtype=F32) / HEAD_DIM))
    ang = jnp.arange(t, dtype=F32)[:, None] * inv[None, :]
    cos = jnp.cos(ang)
    sin = jnp.sin(ang)
    return (jnp.concatenate([cos, cos, cos, cos], axis=-1),
            jnp.concatenate([-sin, sin, -sin, sin], axis=-1))


def _proj_kernel(x_ref, g_ref, w_ref, cos_ref, sin_ref, *rest, plan, n_out):
    out_refs, (ys,) = rest[:n_out], rest[n_out:]
    n = _rms(x_ref[...], g_ref[...]).astype(BF16)
    tm = n.shape[0]
    cos = cos_ref[...]
    sin = sin_ref[...]
    lane = lax.broadcasted_iota(jnp.int32, cos.shape, 1)
    first_half = (lane & (HEAD_DIM - 1)) < HEAD_DIM // 2
    heads_per_tile = LANES // HEAD_DIM
    for off, width, outs in plan:
        y = _dot(n, w_ref[:, off:off + width])
        for o in outs:
            o_ref = out_refs[o["idx"]]
            dt = o_ref.dtype
            kind = o["kind"]
            for c in range(o["width"] // LANES):
                piece = y[:, o["off"] + c * LANES: o["off"] + (c + 1) * LANES]
                if o.get("rope"):
                    rot = jnp.where(first_half, pltpu.roll(piece, LANES - HEAD_DIM // 2, 1),
                                    pltpu.roll(piece, HEAD_DIM // 2, 1))
                    piece = piece * cos + rot * sin
                if o.get("scale"):
                    piece = piece * QK_SCALE
                if o.get("sigmoid"):
                    piece = jax.nn.sigmoid(piece)
                lanes = slice(c * LANES, (c + 1) * LANES)
                if kind == "flat":
                    o_ref[:, lanes] = piece.astype(dt)
                elif kind == "split":
                    hw = o["hw"]
                    per = LANES // hw
                    for k in range(per):
                        o_ref[c * per + k] = piece[:, k * hw:(k + 1) * hw].astype(dt)
                elif kind == "dilate":
                    r = o["r"]
                    if r == 1:
                        o_ref[0, 0, :, lanes] = piece.astype(dt)
                    else:
                        ys[...] = piece
                        for i in range(r):
                            o_ref[0, i, :, lanes] = ys[pl.ds(i, tm // r, stride=r), :].astype(dt)
                elif kind == "chunk":
                    ys[...] = piece
                    for l in range(NSA_CMP_STRIDE):
                        rows = ys[pl.ds(l, tm // NSA_CMP_STRIDE, stride=NSA_CMP_STRIDE), :].astype(dt)
                        for k in range(heads_per_tile):
                            o_ref[c * heads_per_tile + k, :, l * HEAD_DIM:(l + 1) * HEAD_DIM] = rows[
                                :, k * HEAD_DIM:(k + 1) * HEAD_DIM]
                else:
                    tile = o["tile"]
                    pt = piece.T
                    for k in range(heads_per_tile):
                        for kt in range(tm // tile):
                            o_ref[c * heads_per_tile + k, 0, kt, 0:HEAD_DIM, :] = pt[
                                k * HEAD_DIM:(k + 1) * HEAD_DIM, kt * tile:(kt + 1) * tile].astype(dt)
                            o_ref[c * heads_per_tile + k, 0, kt, HEAD_DIM:V_ROWS, :] = jnp.ones(
                                (V_ROWS - HEAD_DIM, tile), dt)


def _proj(h, gain, w, cos, sin, plan, out_defs, bsz, t_len):
    m = h.shape[0]
    tm = PROJ_TM
    nt = t_len // tm
    out_shapes, out_specs = [], []
    for d in out_defs:
        kind, dt = d[0], d[-1]
        if kind == "flat":
            shape, block, index = (m, d[1]), (tm, d[1]), (lambda i: (i, 0))
        elif kind == "split":
            shape, block, index = (d[1], m, d[2]), (d[1], tm, d[2]), (lambda i: (0, i, 0))
        elif kind == "dilate":
            r = d[1]
            shape, block = (bsz, r, t_len // r, d[2]), (1, r, tm // r, d[2])
            index = lambda i: (i // nt, 0, i % nt, 0)
        elif kind == "chunk":
            width = NSA_CMP_STRIDE * HEAD_DIM
            shape, block = (d[1], m // NSA_CMP_STRIDE, width), (d[1], tm // NSA_CMP_STRIDE, width)
            index = lambda i: (0, i, 0)
        else:
            tile = d[2]
            shape, block = (d[1], bsz, t_len // tile, V_ROWS, tile), (d[1], 1, tm // tile, V_ROWS, tile)
            index = lambda i: (0, i // nt, i % nt, 0, 0)
        out_shapes.append(jax.ShapeDtypeStruct(shape, dt))
        out_specs.append(pl.BlockSpec(block, index))
    return pl.pallas_call(
        functools.partial(_proj_kernel, plan=plan, n_out=len(out_defs)),
        grid=(m // tm,),
        in_specs=[
            pl.BlockSpec((tm, D_MODEL), lambda i: (i, 0)),
            pl.BlockSpec((1, D_MODEL), lambda i: (0, 0)),
            pl.BlockSpec(w.shape, lambda i: (0, 0)),
            pl.BlockSpec((tm, LANES), lambda i: (i % nt, 0)),
            pl.BlockSpec((tm, LANES), lambda i: (i % nt, 0)),
        ],
        out_specs=out_specs,
        out_shape=out_shapes,
        scratch_shapes=[pltpu.VMEM((tm, LANES), F32)],
        compiler_params=_params("parallel"),
    )(h, gain.reshape(1, D_MODEL), w, cos, sin)


def _banded_kernel(q_ref, kp_ref, kc_ref, vp_ref, vc_ref, *rest, n_kv, group, tl, pk, span, max_dist,
                   has_sink, gate_branch, want_lse):
    rest = list(rest)
    sink_ref = rest.pop(0) if has_sink else None
    gate_ref = rest.pop(0) if gate_branch is not None else None
    o_ref = rest.pop(0)
    lse_ref = rest.pop(0) if want_lse else None
    kcat, vcat, s_a, s_b = rest
    sb = BAND_SB
    n_seq = q_ref.shape[1]
    for rr in range(n_seq):
        kcat[rr, 0:pk, :] = kp_ref[0, rr]
        kcat[rr, pk:, :] = kc_ref[0, rr]
        vcat[rr, 0:pk, :] = vp_ref[0, rr]
        vcat[rr, pk:, :] = vc_ref[0, rr]
    nk = span + sb
    kw = n_kv * HEAD_DIM
    merged = group == 1
    stacked = n_kv if merged else group
    rows = stacked * sb
    key = lax.broadcasted_iota(jnp.int32, (nk, rows), 0)
    qry = lax.broadcasted_iota(jnp.int32, (nk, rows), 1) & (sb - 1)
    dist = qry + span - key
    in_band = (dist >= 0) & (dist <= max_dist)
    lane_head = lax.broadcasted_iota(jnp.int32, (1, rows), 1) >> _log2(sb)
    q_head = lax.broadcasted_iota(jnp.int32, (sb, kw), 1) >> _log2(HEAD_DIM)
    items = [(rr, j, h) for rr in range(n_seq) for j in range(tl // sb)
             for h in ([0] if merged else range(n_kv))]
    s_bufs = (s_a, s_b)

    def compute(first_tile):
        def scores(n):
            rr, j, h = items[n]
            lo = pk + j * sb - span
            qrows = slice(j * sb, (j + 1) * sb)
            if merged:
                qj = q_ref[0, rr, qrows, :].astype(F32)
                qs = jnp.concatenate([jnp.where(q_head == r, qj, 0.0).astype(BF16) for r in range(n_kv)], axis=0)
                kk = kcat[rr, lo:lo + nk, :]
            else:
                qs = jnp.concatenate([q_ref[0, rr, qrows, (h * group + g) * HEAD_DIM:(h * group + g + 1) * HEAD_DIM]
                                      for g in range(group)], axis=0)
                kk = kcat[rr, lo:lo + nk, h * HEAD_DIM:(h + 1) * HEAD_DIM]
            ok = in_band
            if first_tile and j * sb < span:
                ok = ok & (key >= span - j * sb)
            s_bufs[n % 2][...] = jnp.where(ok, _dot_t(kk, qs), NEG_INF)

        outs, lses = [], []
        scores(0)
        for n, (rr, j, h) in enumerate(items):
            if n + 1 < len(items):
                scores(n + 1)
            s = s_bufs[n % 2][...]
            lo = pk + j * sb - span
            qrows = slice(j * sb, (j + 1) * sb)
            m = jnp.max(s, axis=0, keepdims=True)
            if has_sink:
                sink = jnp.full((1, rows), sink_ref[h * group], F32)
                for g in range(1, group):
                    sink = jnp.where(lane_head == g, sink_ref[h * group + g], sink)
                sink = sink * LOG2E
                m = jnp.maximum(m, sink)
            p = jnp.exp2(s - m)
            d = jnp.sum(p, axis=0, keepdims=True)
            if has_sink:
                d = d + jnp.exp2(sink - m)
            d = jnp.where(d > 0, d, 1.0)
            inv = 1.0 / d
            vv = vcat[rr, lo:lo + nk, :] if merged else vcat[rr, lo:lo + nk, h * HEAD_DIM:(h + 1) * HEAD_DIM]
            o_t = _dot_tn(vv, p.astype(BF16))
            if want_lse:
                lse_t = jnp.broadcast_to(m * LN2 + jnp.log(d), (HEAD_DIM, rows))
            for r in range(stacked):
                cols = slice(r * sb, (r + 1) * sb)
                vrow = r * HEAD_DIM if merged else 0
                o = (o_t[vrow:vrow + HEAD_DIM, cols] * inv[:, cols]).T
                if gate_ref is not None:
                    gc = r * 3 + gate_branch
                    o = o * gate_ref[h, 0, qrows, gc:gc + 1]
                outs.append(o)
                if want_lse:
                    lses.append(lse_t[:, cols].T)
            if merged or h == n_kv - 1:
                o_ref[0, rr, qrows, :] = jnp.concatenate(outs, axis=-1).astype(o_ref.dtype)
                if want_lse:
                    lse_ref[0, rr, qrows, :] = jnp.concatenate(lses, axis=-1)
                outs, lses = [], []

    first = pl.program_id(2) == 0
    pl.when(first)(functools.partial(compute, True))
    pl.when(jnp.logical_not(first))(functools.partial(compute, False))


def _banded(q, k, v, *, n_kv, group, max_dist, tl, pk, span, sink=None, gate=None, gate_branch=None,
            want_lse=False):
    bsz, n_res, length, qw = q.shape
    kw = n_kv * HEAD_DIM
    s_shape = (span + BAND_SB, (n_kv if group == 1 else group) * BAND_SB)
    ratio = tl // pk
    n_seq = min(n_res, max(1, BAND_TL // tl))
    prev = lambda b, i, l: (b, i, jnp.maximum(l * ratio - 1, 0), 0)
    cur = lambda b, i, l: (b, i, l, 0)
    in_specs = [
        pl.BlockSpec((1, n_seq, tl, qw), cur),
        pl.BlockSpec((1, n_seq, pk, kw), prev),
        pl.BlockSpec((1, n_seq, tl, kw), cur),
        pl.BlockSpec((1, n_seq, pk, kw), prev),
        pl.BlockSpec((1, n_seq, tl, kw), cur),
    ]
    args = [q, k, k, v, v]
    if sink is not None:
        in_specs.append(pl.BlockSpec(memory_space=pltpu.SMEM))
        args.append(sink)
    if gate is not None:
        in_specs.append(pl.BlockSpec((n_kv, 1, tl, LANES), lambda b, i, l: (0, b, l, 0)))
        args.append(gate)
    out_spec = pl.BlockSpec((1, n_seq, tl, qw), cur)
    out_shape = [jax.ShapeDtypeStruct(q.shape, BF16)]
    out_specs = [out_spec]
    if want_lse:
        out_shape.append(jax.ShapeDtypeStruct(q.shape, F32))
        out_specs.append(out_spec)
    res = pl.pallas_call(
        functools.partial(_banded_kernel, n_kv=n_kv, group=group, tl=tl, pk=pk, span=span,
                          max_dist=max_dist, has_sink=sink is not None, gate_branch=gate_branch,
                          want_lse=want_lse),
        grid=(bsz, n_res // n_seq, length // tl),
        in_specs=in_specs,
        out_specs=out_specs,
        out_shape=out_shape,
        scratch_shapes=[pltpu.VMEM((n_seq, pk + tl, kw), BF16), pltpu.VMEM((n_seq, pk + tl, kw), BF16),
                        pltpu.VMEM(s_shape, F32), pltpu.VMEM(s_shape, F32)],
        compiler_params=_params("parallel", "parallel", "parallel"),
    )(*args)
    return res if want_lse else res[0]


def _compress_one(z_ref, pe_ref, w1_ref, w2_ref, o_ref):
    half = NSA_CMP_STRIDE * HEAD_DIM
    z = z_ref[0, 0].astype(F32)
    za = (z + pe_ref[0, :, 0:half]).astype(BF16)
    zb = (z + pe_ref[0, :, half:2 * half]).astype(BF16)
    ya = _dot(za, w1_ref[0, 0:half, :])
    yb = _dot(zb, w1_ref[0, half:2 * half, :])
    nrow = ya.shape[0]
    hid = jax.nn.gelu(ya + pltpu.roll(yb, nrow - 1, 0))
    o_ref[0, 0] = _dot(hid.astype(BF16), w2_ref[0]).astype(o_ref.dtype)


def _compress_kernel(zk_ref, zv_ref, pek_ref, pev_ref, w1k_ref, w1v_ref, w2k_ref, w2v_ref, ok_ref, ov_ref):
    _compress_one(zk_ref, pek_ref, w1k_ref, w2k_ref, ok_ref)
    _compress_one(zv_ref, pev_ref, w1v_ref, w2v_ref, ov_ref)


def _compress(zk, zv, pe, w1, w2):
    hk, bsz, nch, width = zk.shape
    zspec = pl.BlockSpec((1, 1, nch, width), lambda h, b: (h, b, 0, 0))
    ospec = pl.BlockSpec((1, 1, nch, HEAD_DIM), lambda h, b: (h, b, 0, 0))
    oshape = jax.ShapeDtypeStruct((hk, bsz, nch, HEAD_DIM), BF16)

    def wspec(arr, which):
        return pl.BlockSpec((1,) + arr.shape[1:], lambda h, b: (which, 0, 0))

    return pl.pallas_call(
        _compress_kernel,
        grid=(hk, bsz),
        in_specs=[zspec, zspec, wspec(pe, 0), wspec(pe, 1), wspec(w1, 0), wspec(w1, 1), wspec(w2, 0), wspec(w2, 1)],
        out_specs=[ospec, ospec],
        out_shape=[oshape, oshape],
        compiler_params=_params("parallel", "parallel"),
    )(zk, zv, pe, pe, w1, w1, w2, w2)


def _topk_select(score, k, idxf):
    ncand = score.shape[0]
    rem = score
    sel = jnp.zeros_like(score)
    for _ in range(k):
        mx = jnp.max(rem, axis=0, keepdims=True)
        first = jnp.min(jnp.where(rem == mx, idxf, float(ncand)), axis=0, keepdims=True)
        pick = idxf == first
        sel = jnp.where(pick, jnp.where(mx > 0.5 * NEG_INF, 1.0, 0.0), sel)
        rem = jnp.where(pick, REMOVED, rem)
    return sel


def _cmp_kernel(q_ref, kc_ref, vc_ref, gate_ref, o_ref, bias_ref, *, tq, group):
    qi = pl.program_id(2)
    kc = kc_ref[0, 0]
    vc = vc_ref[0, 0]
    ncp = kc.shape[0]
    nsb = bias_ref.shape[-1]
    rows = group * tq
    last = lax.broadcasted_iota(jnp.int32, (ncp, 1), 0) * NSA_CMP_STRIDE + (NSA_CMP_LEN - 1)
    t_in_tile = lax.broadcasted_iota(jnp.int32, (1, rows), 1) & (tq - 1)
    visible = (last - t_in_tile) <= qi * tq
    qs = jnp.concatenate([q_ref[0, :, g * HEAD_DIM:(g + 1) * HEAD_DIM] for g in range(group)], axis=0)
    s = jnp.where(visible, _dot_t(kc, qs), NEG_INF)
    m = jnp.max(s, axis=0, keepdims=True)
    m = jnp.where(m > 0.5 * NEG_INF, m, 0.0)
    p = jnp.exp2(s - m)
    d = jnp.sum(p, axis=0, keepdims=True)
    d = jnp.where(d > 0, d, 1.0)
    pc = p * (1.0 / d)
    o_t = _dot_tn(vc, pc.astype(BF16))
    outs = []
    psum = pc[:, 0:tq]
    for g in range(group):
        if g:
            psum = psum + pc[:, g * tq:(g + 1) * tq]
        outs.append(o_t[:, g * tq:(g + 1) * tq].T * gate_ref[0, 0, :, g * 3:g * 3 + 1])
    o_ref[0] = jnp.concatenate(outs, axis=-1).astype(o_ref.dtype)

    jj = lax.broadcasted_iota(jnp.int32, (nsb, ncp), 0) * NSA_SEL_LEN
    nn = lax.broadcasted_iota(jnp.int32, (nsb, ncp), 1) * NSA_CMP_STRIDE
    ov = jnp.maximum(jnp.minimum(nn + NSA_CMP_LEN, jj + NSA_SEL_LEN) - jnp.maximum(nn, jj), 0)
    ov = (ov.astype(F32) / NSA_CMP_LEN).astype(BF16)
    p_hi = psum.astype(BF16)
    p_lo = (psum - p_hi.astype(F32)).astype(BF16)
    imp = _dot(ov, p_hi) + _dot(ov, p_lo)

    j = lax.broadcasted_iota(jnp.int32, (nsb, tq), 0)
    tt = qi * tq + lax.broadcasted_iota(jnp.int32, (nsb, tq), 1)
    cb = tt >> _log2(NSA_SEL_LEN)
    forced = (j == 0) | (j == cb) | (j == cb - 1)
    score = jnp.where(j <= cb, jnp.where(forced, FORCE_SCORE, imp), NEG_INF)
    sel = _topk_select(score, min(NSA_SEL_COUNT, nsb), j.astype(F32))
    bias_ref[0, 0] = jnp.where(sel > 0.5, 0.0, MASK_BIAS).T.astype(bias_ref.dtype)


def _cmp_attention(qu, kc, vc, gate, t_len):
    bsz = qu.shape[0]
    hk, _, ncp, _ = kc.shape
    group = B_Q_HEADS // B_KV_HEADS
    tq = CMP_TQ
    nsb = t_len // NSA_SEL_LEN
    gw = group * HEAD_DIM
    cspec = pl.BlockSpec((1, 1, ncp, HEAD_DIM), lambda b, h, i: (h, b, 0, 0))
    return pl.pallas_call(
        functools.partial(_cmp_kernel, tq=tq, group=group),
        grid=(bsz, hk, t_len // tq),
        in_specs=[
            pl.BlockSpec((1, tq, gw), lambda b, h, i: (b, i, h)),
            cspec, cspec,
            pl.BlockSpec((1, 1, tq, LANES), lambda b, h, i: (h, b, i, 0)),
        ],
        out_specs=[
            pl.BlockSpec((1, tq, gw), lambda b, h, i: (b, i, h)),
            pl.BlockSpec((1, 1, tq, nsb), lambda b, h, i: (h, b, i, 0)),
        ],
        out_shape=[
            jax.ShapeDtypeStruct((bsz, t_len, hk * gw), BF16),
            jax.ShapeDtypeStruct((hk, bsz, t_len, nsb), BF16),
        ],
        compiler_params=_params("parallel", "parallel", "parallel"),
    )(qu, kc, vc, gate)


def _moba_gate_kernel(q_ref, k_ref, bias_ref, km_hi, km_lo, *, tq):
    qi = pl.program_id(2)
    t_len = k_ref.shape[2]
    ncol = bias_ref.shape[-1]
    nb = t_len // MOBA_BLOCK

    @pl.when(qi == 0)
    def _():
        blk = lax.broadcasted_iota(jnp.int32, (ncol, t_len), 0)
        pos = lax.broadcasted_iota(jnp.int32, (ncol, t_len), 1)
        member = jnp.where((pos >> _log2(MOBA_BLOCK)) == blk, 1.0, 0.0).astype(BF16)
        kmean = _dot(member, k_ref[0, 0]) * (1.0 / MOBA_BLOCK)
        hi = kmean.astype(BF16)
        km_hi[...] = hi
        km_lo[...] = (kmean - hi.astype(F32)).astype(BF16)

    q = q_ref[0, 0]
    gate = (_dot_t(km_hi[...], q) + _dot_t(km_lo[...], q))[0:nb]
    j = lax.broadcasted_iota(jnp.int32, (nb, tq), 0)
    t = qi * tq + lax.broadcasted_iota(jnp.int32, (nb, tq), 1)
    cb = t >> _log2(MOBA_BLOCK)
    score = jnp.where(j < cb, gate, NEG_INF)
    sel = _topk_select(score, min(MOBA_TOPK, nb - 1), j.astype(F32))
    keep = (sel > 0.5) | (j == cb)
    bias = jnp.where(keep, 0.0, MASK_BIAS)
    bias = jnp.concatenate([bias, jnp.zeros((ncol - nb, tq), F32)], axis=0)
    bias_ref[0, 0] = bias.T.astype(bias_ref.dtype)


def _moba_gate(q, k):
    nh, bsz, t_len, _ = q.shape
    tq = min(GATE_TQ, t_len)
    return pl.pallas_call(
        functools.partial(_moba_gate_kernel, tq=tq),
        grid=(nh, bsz, t_len // tq),
        in_specs=[
            pl.BlockSpec((1, 1, tq, HEAD_DIM), lambda h, b, i: (h, b, i, 0)),
            pl.BlockSpec((1, 1, t_len, HEAD_DIM), lambda h, b, i: (h, b, 0, 0)),
        ],
        out_specs=pl.BlockSpec((1, 1, tq, HEAD_DIM), lambda h, b, i: (h, b, i, 0)),
        out_shape=jax.ShapeDtypeStruct((nh, bsz, t_len, HEAD_DIM), BF16),
        scratch_shapes=[pltpu.VMEM((HEAD_DIM, HEAD_DIM), BF16), pltpu.VMEM((HEAD_DIM, HEAD_DIM), BF16)],
        compiler_params=_params("parallel", "parallel", "arbitrary"),
    )(q, k)


def _flash_kernel(q_ref, bias_ref, k_ref, vt_ref, *rest, group, tq, tk, blk_len, q_scale, gate_branch):
    if gate_branch is not None:
        gate_ref, o_ref, qa, ka, m_s, acc, s_a, s_b = rest
    else:
        gate_ref = None
        o_ref, qa, ka, m_s, acc, s_a, s_b = rest
    qi = pl.program_id(2)
    rows = group * tq
    t_len = k_ref.shape[2]

    @pl.when(qi == 0)
    def _():
        ka[:, 0:HEAD_DIM] = k_ref[0, 0]
        kpos = lax.broadcasted_iota(jnp.int32, (t_len, HEAD_DIM), 0)
        kblk = lax.broadcasted_iota(jnp.int32, (t_len, HEAD_DIM), 1)
        ka[:, HEAD_DIM:2 * HEAD_DIM] = jnp.where((kpos >> _log2(blk_len)) == kblk, 1.0, 0.0).astype(BF16)

    bias = bias_ref[0, 0]
    for g in range(group):
        qg = q_ref[0, 0, :, g * HEAD_DIM:(g + 1) * HEAD_DIM]
        if q_scale != 1.0:
            qg = (qg.astype(F32) * q_scale).astype(BF16)
        qa[g * tq:(g + 1) * tq, 0:HEAD_DIM] = qg
        qa[g * tq:(g + 1) * tq, HEAD_DIM:2 * HEAD_DIM] = bias
    m_s[...] = jnp.full_like(m_s, NEG_INF)
    acc[...] = jnp.zeros_like(acc)

    def scores(ki, s_ref):
        start = pl.multiple_of(ki * tk, tk)
        s_ref[...] = _dot_t(ka[pl.ds(start, tk), :], qa[...])

    def softmax_pv(ki, s_ref, masked):
        s = s_ref[...]
        if masked:
            kpos = ki * tk + lax.broadcasted_iota(jnp.int32, (tk, rows), 0)
            qpos = qi * tq + (lax.broadcasted_iota(jnp.int32, (tk, rows), 1) & (tq - 1))
            s = jnp.where(kpos <= qpos, s, NEG_INF)
        m_old = m_s[...]
        m_new = jnp.maximum(m_old, jnp.max(s, axis=0, keepdims=True))
        alpha = jnp.exp2(m_old - m_new)
        p = jnp.exp2(s - m_new)
        acc[...] = alpha * acc[...] + _dot(vt_ref[0, 0, ki], p.astype(BF16))
        m_s[...] = m_new

    n_full = (qi * tq) >> _log2(tk)
    scores(0, s_a)

    def pair(j, carry):
        k0 = 2 * j
        scores(k0 + 1, s_b)
        softmax_pv(k0, s_a, False)
        scores(k0 + 2, s_a)
        softmax_pv(k0 + 1, s_b, False)
        return carry

    lax.fori_loop(0, n_full >> 1, pair, 0)
    cur = (n_full >> 1) << 1

    @pl.when((n_full & 1) == 1)
    def _():
        scores(cur + 1, s_b)
        softmax_pv(cur, s_a, False)
        softmax_pv(cur + 1, s_b, True)

    @pl.when((n_full & 1) == 0)
    def _():
        softmax_pv(cur, s_a, True)

    d = acc[HEAD_DIM:HEAD_DIM + 1, :]
    d = jnp.where(d > 0, d, 1.0)
    o_t = acc[0:HEAD_DIM, :] * (1.0 / d)
    outs = []
    for g in range(group):
        og = o_t[:, g * tq:(g + 1) * tq].T
        if gate_ref is not None:
            gc = g * 3 + gate_branch
            og = og * gate_ref[0, 0, :, gc:gc + 1]
        outs.append(og)
    res = outs[0] if group == 1 else jnp.concatenate(outs, axis=-1)
    o_ref[0, 0] = res.astype(o_ref.dtype)


def _flash(q, bias, k, vt, *, group, tq, blk_len, q_scale, q_index, gate=None, gate_branch=None):
    nh, bsz, t_len, _ = k.shape
    gw = group * HEAD_DIM
    nt, tk = vt.shape[2], vt.shape[4]
    in_specs = [
        pl.BlockSpec((1, 1, tq, gw), q_index),
        pl.BlockSpec((1, 1, tq, HEAD_DIM), lambda h, b, i: (h, b, i, 0)),
        pl.BlockSpec((1, 1, t_len, HEAD_DIM), lambda h, b, i: (h, b, 0, 0)),
        pl.BlockSpec((1, 1, nt, V_ROWS, tk), lambda h, b, i: (h, b, 0, 0, 0)),
    ]
    args = [q, bias, k, vt]
    if gate is not None:
        in_specs.append(pl.BlockSpec((1, 1, tq, LANES), lambda h, b, i: (h, b, i, 0)))
        args.append(gate)
    rows = group * tq
    return pl.pallas_call(
        functools.partial(_flash_kernel, group=group, tq=tq, tk=tk, blk_len=blk_len, q_scale=q_scale,
                          gate_branch=gate_branch),
        grid=(nh, bsz, t_len // tq),
        in_specs=in_specs,
        out_specs=pl.BlockSpec((1, 1, tq, gw), q_index),
        out_shape=jax.ShapeDtypeStruct(q.shape, BF16),
        scratch_shapes=[
            pltpu.VMEM((rows, 2 * HEAD_DIM), BF16),
            pltpu.VMEM((t_len, 2 * HEAD_DIM), BF16),
            pltpu.VMEM((1, rows), F32),
            pltpu.VMEM((V_ROWS, rows), F32),
            pltpu.VMEM((tk, rows), F32),
            pltpu.VMEM((tk, rows), F32),
        ],
        compiler_params=_params("parallel", "parallel", "arbitrary"),
    )(*args)


def _even_delta(refs, scratch):
    oa_ref, oc_ref, os_ref, ow_ref, w_ref = refs
    na = oa_ref.shape[-1]
    ob = (oc_ref[...].astype(F32) + os_ref[...].astype(F32) + ow_ref[...].astype(F32)).astype(BF16)
    return _dot(oa_ref[...], w_ref[0:na, :]) + _dot(ob, w_ref[na:, :])


def _even_mix(oa, ocmp, osel, owin, w):
    aspec = pl.BlockSpec((FFN_TM, oa.shape[-1]), lambda i: (i, 0))
    return dict(args=[oa, ocmp, osel, owin, w], specs=[aspec] * 4 + [_resident(w.shape)], scratch=[],
                fn=_even_delta)


def _odd_delta(refs, scratch, *, dilations):
    ng = len(dilations)
    o_refs, l_refs = refs[:ng], refs[ng:2 * ng]
    od_ref, w_ref = refs[2 * ng:]
    scratch = list(scratch)
    tm = od_ref.shape[1]

    def tokens(ref, r):
        if r == 1:
            return ref[0, 0].astype(F32)
        pieces = []
        for c in range(ref.shape[-1] // LANES):
            scr = scratch.pop(0)
            for i in range(r):
                scr[pl.ds(i, tm // r, stride=r), :] = ref[0, i, :, c * LANES:(c + 1) * LANES].astype(F32)
            pieces.append(scr[...])
        return jnp.concatenate(pieces, axis=-1)

    outs = [tokens(ref, r) for ref, r in zip(o_refs, dilations)]
    lses = [tokens(ref, r) for ref, r in zip(l_refs, dilations)]
    mx = functools.reduce(jnp.maximum, lses)
    es = [jnp.exp(l - mx) for l in lses]
    tot = functools.reduce(lambda a, b: a + b, es)
    oc = functools.reduce(lambda a, b: a + b, [(e / tot) * o for e, o in zip(es, outs)])
    nc = oc.shape[-1]
    delta = _dot(oc.astype(BF16), w_ref[0:nc, :])
    for hh in range(od_ref.shape[0]):
        delta = delta + _dot(od_ref[hh], w_ref[nc + hh * HEAD_DIM: nc + (hh + 1) * HEAD_DIM, :])
    return delta


def _odd_mix(outs, lses, od, w, t_len):
    tm = FFN_TM
    nt = t_len // tm
    gw = outs[0].shape[-1]
    dilations = tuple(o.shape[1] for o in outs)
    gspecs = [pl.BlockSpec((1, r, tm // r, gw), lambda i: (i // nt, 0, i % nt, 0)) for r in dilations]
    n_scr = 2 * (gw // LANES) * sum(1 for r in dilations if r > 1)
    specs = gspecs + gspecs + [pl.BlockSpec((od.shape[0], tm, HEAD_DIM), lambda i: (0, i, 0)), _resident(w.shape)]
    return dict(args=[*outs, *lses, od, w], specs=specs, scratch=[pltpu.VMEM((tm, LANES), F32)] * n_scr,
                fn=functools.partial(_odd_delta, dilations=dilations))


def _col_ranges(sizes):
    offs, acc = [], 0
    for s in sizes:
        offs.append((acc, acc + s))
        acc += s
    return offs


def _even_mixer(h, gain, w_in, w_out, sinks, cmp_pe, cmp_w1, cmp_w2, cos, sin, bsz, t_len):
    m = bsz * t_len
    qa_w, kva_w = A_Q_HEADS * HEAD_DIM, A_KV_HEADS * HEAD_DIM
    qb_w, kvb_w = B_Q_HEADS * HEAD_DIM, B_KV_HEADS * HEAD_DIM
    sizes = [qa_w, kva_w, kva_w, qb_w] + [kvb_w] * 6 + [3 * B_Q_HEADS]
    (aq, ak, av, bq, bkc, bvc, bks, bvs, bkw, bvw, bg) = [w_in[:, a:b] for a, b in _col_ranges(sizes)]
    group = B_Q_HEADS // B_KV_HEADS
    gpad = jnp.zeros((D_MODEL, LANES - 3 * group), w_in.dtype)
    gates = [x for hk in range(B_KV_HEADS) for x in (bg[:, hk * 3 * group:(hk + 1) * 3 * group], gpad)]
    w = jnp.concatenate([aq, bq, ak, bkw, bks, bq, av, bvw, bvs, bkc, bvc] + gates, axis=1).astype(BF16)
    c = [0]

    def take(width):
        c[0] += width
        return c[0] - width

    def out(idx, off, width, kind, **kw):
        return dict(idx=idx, off=off, width=width, kind=kind, **kw)

    plan = [
        (take(qa_w), qa_w, [out(0, 0, qa_w, "flat", rope=True, scale=True)]),
        (take(qb_w), qb_w, [out(1, 0, qb_w, "flat", rope=True, scale=True)]),
        (take(3 * kva_w), 3 * kva_w, [out(2, 0, kva_w, "flat", rope=True),
                                      out(3, kva_w, kvb_w, "flat", rope=True),
                                      out(4, kva_w + kvb_w, kvb_w, "split", hw=HEAD_DIM, rope=True)]),
        (take(qb_w), qb_w, [out(5, 0, qb_w, "flat", scale=True)]),
        (take(5 * kvb_w), 5 * kvb_w, [out(6, 0, kva_w, "flat"), out(7, kva_w, kvb_w, "flat"),
                                      out(8, 2 * kvb_w, kvb_w, "vt", tile=FLASH_TK),
                                      out(9, 3 * kvb_w, kvb_w, "chunk"),
                                      out(10, 4 * kvb_w, kvb_w, "chunk")]),
        (take(B_KV_HEADS * LANES), B_KV_HEADS * LANES,
         [out(11, 0, B_KV_HEADS * LANES, "split", hw=LANES, sigmoid=True)]),
    ]
    out_defs = [("flat", qa_w, BF16), ("flat", qb_w, BF16), ("flat", kva_w, BF16), ("flat", kvb_w, BF16),
                ("split", B_KV_HEADS, HEAD_DIM, BF16), ("flat", qb_w, BF16), ("flat", kva_w, BF16),
                ("flat", kvb_w, BF16), ("vt", B_KV_HEADS, FLASH_TK, BF16), ("chunk", B_KV_HEADS, BF16),
                ("chunk", B_KV_HEADS, BF16), ("split", B_KV_HEADS, LANES, F32)]
    (aq_r, bq_r, ak_r, bkw_r, bks_r, bq_u, av_, bvw_, bvs_t, zk, zv, gate) = _proj(
        h, gain, w, cos, sin, plan, out_defs, bsz, t_len)

    def seq(x):
        return x.reshape(bsz, 1, t_len, x.shape[-1])

    def heads(x):
        return x.reshape(x.shape[0], bsz, t_len, x.shape[-1])

    gate4 = heads(gate)
    oa = _banded(seq(aq_r), seq(ak_r), seq(av_), n_kv=A_KV_HEADS, group=A_Q_HEADS // A_KV_HEADS,
                 max_dist=A_WINDOW - 1, tl=BAND_TL, pk=128, span=128, sink=sinks)
    nch = t_len // NSA_CMP_STRIDE
    zshape = (B_KV_HEADS, bsz, nch, NSA_CMP_STRIDE * HEAD_DIM)
    kc, vc = _compress(zk.reshape(zshape), zv.reshape(zshape), cmp_pe.reshape(2, 1, NSA_CMP_LEN * HEAD_DIM),
                       cmp_w1.astype(BF16), cmp_w2.astype(BF16))
    ocmp, bias = _cmp_attention(bq_u.reshape(bsz, t_len, qb_w), kc, vc, gate4, t_len)
    osel = _flash(seq(bq_r).reshape(1, bsz, t_len, qb_w), bias, heads(bks_r), bvs_t, group=group, tq=FLASH_T,
                  blk_len=NSA_SEL_LEN, q_scale=1.0, q_index=lambda hk, b, i: (0, b, i, hk), gate=gate4,
                  gate_branch=1)
    owin = _banded(seq(bq_r), seq(bkw_r), seq(bvw_), n_kv=B_KV_HEADS, group=group,
                   max_dist=NSA_WINDOW - 1, tl=BAND_TL, pk=512, span=512, gate=gate4, gate_branch=2)
    return _even_mix(oa.reshape(m, qa_w), ocmp.reshape(m, qb_w), osel.reshape(m, qb_w),
                     owin.reshape(m, qb_w), w_out.astype(BF16))


def _odd_mixer(h, gain, w_in, w_out, cos, sin, bsz, t_len):
    m = bsz * t_len
    cw = C_HEADS * HEAD_DIM
    dw = D_HEADS * HEAD_DIM
    gw = C_HEADS_PER_GROUP * HEAD_DIM
    n_groups = len(C_GROUPS)

    def group_outs(base, **kw):
        return [dict(idx=base + gi, off=gi * gw, width=gw, kind="dilate", r=r, **kw)
                for gi, (_, r) in enumerate(C_GROUPS)]

    plan = [
        (0, cw, group_outs(0, rope=True, scale=True)),
        (cw, cw, group_outs(n_groups, rope=True)),
        (2 * cw, cw, group_outs(2 * n_groups)),
        (3 * cw, 3 * dw, [dict(idx=3 * n_groups, off=0, width=dw, kind="split", hw=HEAD_DIM, rope=True),
                          dict(idx=3 * n_groups + 1, off=dw, width=dw, kind="split", hw=HEAD_DIM, rope=True),
                          dict(idx=3 * n_groups + 2, off=2 * dw, width=dw, kind="vt", tile=MOBA_TK)]),
    ]
    out_defs = [("dilate", r, gw, BF16) for _ in range(3) for _, r in C_GROUPS]
    out_defs += [("split", D_HEADS, HEAD_DIM, BF16), ("split", D_HEADS, HEAD_DIM, BF16),
                 ("vt", D_HEADS, MOBA_TK, BF16)]
    res = _proj(h, gain, w_in.astype(BF16), cos, sin, plan, out_defs, bsz, t_len)
    cq, ck, cv = res[0:n_groups], res[n_groups:2 * n_groups], res[2 * n_groups:3 * n_groups]
    dq, dk, dv_t = res[3 * n_groups:]
    outs, lses = [], []
    for gi, (wlen, r) in enumerate(C_GROUPS):
        o, lse = _banded(cq[gi], ck[gi], cv[gi], n_kv=C_HEADS_PER_GROUP, group=1, max_dist=wlen // r,
                         tl=min(BAND_TL, t_len // r), pk=128, span=128, want_lse=True)
        outs.append(o)
        lses.append(lse)

    def heads(x):
        return x.reshape(x.shape[0], bsz, t_len, x.shape[-1])

    bias = _moba_gate(heads(dq), heads(dk))
    od = _flash(heads(dq), bias, heads(dk), dv_t, group=1, tq=MOBA_T, blk_len=MOBA_BLOCK, q_scale=QK_SCALE,
                q_index=lambda hh, b, i: (hh, b, i, 0))
    return _odd_mix(outs, lses, od.reshape(D_HEADS, m, HEAD_DIM), w_out.astype(BF16), t_len)


def kernel(x, ffn_norm_pre, mix_norm, ffn_norm_post, ffn_wi, ffn_wo, even_w_in, even_w_out, even_sinks,
           nsa_cmp_pe, nsa_cmp_w1, nsa_cmp_w2, odd_w_in, odd_w_out, final_norm):
    bsz, t_len, _ = x.shape
    depth = ffn_wi.shape[0]
    cos, sin = _rope_tables(t_len)
    wi = ffn_wi.astype(BF16)
    wo = ffn_wo.astype(BF16)
    h = x.reshape(bsz * t_len, D_MODEL)
    for layer in range(depth):
        i = layer // 2
        h = _ffn(h, ffn_norm_pre[layer], wi[layer, 0], wo[layer, 0])
        if layer % 2 == 0:
            mix = _even_mixer(h, mix_norm[layer], even_w_in[i], even_w_out[i], even_sinks[i], nsa_cmp_pe[i],
                              nsa_cmp_w1[i], nsa_cmp_w2[i], cos, sin, bsz, t_len)
        else:
            mix = _odd_mixer(h, mix_norm[layer], odd_w_in[i], odd_w_out[i], cos, sin, bsz, t_len)
        last = layer == depth - 1
        h = _ffn(h, ffn_norm_post[layer], wi[layer, 1], wo[layer, 1], final_norm if last else None, mix=mix)
    return h.reshape(bsz, t_len, D_MODEL)
```

```python
import functools

import jax
import jax.numpy as jnp
from jax import lax
from jax.experimental import pallas as pl
from jax.experimental.pallas import tpu as pltpu

D_MODEL = 1024
HEAD_DIM = 64
ROPE_THETA = 10000.0
NORM_EPS = 1e-6
D_FF = 2816
NEG_INF = -1e30
FORCE_SCORE = 1e4

A_Q_HEADS = 8
A_KV_HEADS = 2
A_WINDOW = 128
B_Q_HEADS = 8
B_KV_HEADS = 2
NSA_CMP_LEN = 32
NSA_CMP_STRIDE = 16
NSA_CMP_HIDDEN = 256
NSA_SEL_LEN = 64
NSA_SEL_COUNT = 8
NSA_WINDOW = 512
C_GROUPS = ((128, 1), (512, 4), (2048, 16))
C_HEADS_PER_GROUP = 4
C_HEADS = len(C_GROUPS) * C_HEADS_PER_GROUP
D_HEADS = 4
MOBA_BLOCK = 256
MOBA_TOPK = 3

LANES = 128
LOG2E = 1.4426950408889634
LN2 = 0.6931471805599453
QK_SCALE = HEAD_DIM ** -0.5 * LOG2E
MASK_BIAS = -32768.0
V_ROWS = HEAD_DIM + 16
REMOVED = -3e38
VMEM_LIMIT = 52 * 1024 * 1024

FFN_TM = 512
FFN_TF = 256
PROJ_TM = 512
BAND_TL = 512
BAND_SB = 128
FLASH_T = 256
FLASH_TK = 256
MOBA_T = 512
MOBA_TK = 512
GATE_TQ = 4096
CMP_TQ = 1024

BF16 = jnp.bfloat16
F32 = jnp.float32


def _params(*sem):
    return pltpu.CompilerParams(dimension_semantics=sem, vmem_limit_bytes=VMEM_LIMIT)


def _rms(x, g):
    return x * lax.rsqrt(jnp.mean(x * x, axis=-1, keepdims=True) + NORM_EPS) * g


def _log2(n):
    assert n & (n - 1) == 0
    return n.bit_length() - 1


def _dot(a, b):
    return jnp.dot(a, b, preferred_element_type=F32)


def _dot_t(a, b):
    return lax.dot_general(a, b, (((1,), (1,)), ((), ())), preferred_element_type=F32)


def _dot_tn(a, b):
    return lax.dot_general(a, b, (((0,), (0,)), ((), ())), preferred_element_type=F32)


def _ffn_kernel(x_ref, g_ref, wi_ref, wo_ref, *rest, final, mix_fn, n_mix):
    rest = list(rest)
    fg_ref = rest.pop(0) if final else None
    mix_refs = [rest.pop(0) for _ in range(n_mix)]
    o_ref, act_scr = rest.pop(0), rest.pop(0)
    x = x_ref[...]
    if mix_fn is not None:
        x = x + mix_fn(mix_refs, rest)
    o_ref[...] = x
    n = _rms(x, g_ref[...]).astype(BF16)
    for c in range(D_FF // FFN_TF):
        cols = slice(c * FFN_TF, (c + 1) * FFN_TF)
        gate = _dot(n, wi_ref[:, cols])
        up = _dot(n, wi_ref[:, D_FF + c * FFN_TF:D_FF + (c + 1) * FFN_TF])
        act_scr[:, cols] = (gate * jax.nn.sigmoid(gate) * up).astype(BF16)
    h = o_ref[...] + 0.5 * _dot(act_scr[...], wo_ref[...])
    if final:
        h = _rms(h, fg_ref[...])
    o_ref[...] = h


def _resident(shape):
    return pl.BlockSpec(shape, lambda i: (0,) * len(shape), pipeline_mode=pl.Buffered(1))


def _ffn(h, gain, wi, wo, final_gain=None, mix=None):
    m = h.shape[0]
    tm = FFN_TM
    final = final_gain is not None
    in_specs = [
        pl.BlockSpec((tm, D_MODEL), lambda i: (i, 0)),
        pl.BlockSpec((1, D_MODEL), lambda i: (0, 0)),
        _resident(wi.shape),
        _resident(wo.shape),
    ]
    args = [h, gain.reshape(1, D_MODEL), wi, wo]
    if final:
        in_specs.append(pl.BlockSpec((1, D_MODEL), lambda i: (0, 0)))
        args.append(final_gain.reshape(1, D_MODEL))
    scratch = [pltpu.VMEM((tm, D_FF), BF16)]
    if mix is not None:
        in_specs += mix["specs"]
        args += mix["args"]
        scratch += mix["scratch"]
    return pl.pallas_call(
        functools.partial(_ffn_kernel, final=final, mix_fn=mix and mix["fn"], n_mix=len(mix["args"]) if mix else 0),
        grid=(m // tm,),
        in_specs=in_specs,
        out_specs=pl.BlockSpec((tm, D_MODEL), lambda i: (i, 0)),
        out_shape=jax.ShapeDtypeStruct((m, D_MODEL), F32),
        scratch_shapes=scratch,
        compiler_params=_params("parallel"),
    )(*args)


def _rope_tables(t):
    inv = 1.0 / (ROPE_THETA ** (jnp.arange(0, HEAD_DIM, 2, dtype=F32) / HEAD_DIM))
    ang = jnp.arange(t, dtype=F32)[:, None] * inv[None, :]
    cos = jnp.cos(ang)
    sin = jnp.sin(ang)
    return (jnp.concatenate([cos, cos, cos, cos], axis=-1),
            jnp.concatenate([-sin, sin, -sin, sin], axis=-1))


def _proj_kernel(x_ref, g_ref, w_ref, cos_ref, sin_ref, *rest, plan, n_out):
    out_refs, (ys,) = rest[:n_out], rest[n_out:]
    n = _rms(x_ref[...], g_ref[...]).astype(BF16)
    tm = n.shape[0]
    cos = cos_ref[...]
    sin = sin_ref[...]
    lane = lax.broadcasted_iota(jnp.int32, cos.shape, 1)
    first_half = (lane & (HEAD_DIM - 1)) < HEAD_DIM // 2
    heads_per_tile = LANES // HEAD_DIM
    for off, width, outs in plan:
        y = _dot(n, w_ref[:, off:off + width])
        for o in outs:
            o_ref = out_refs[o["idx"]]
            dt = o_ref.dtype
            kind = o["kind"]
            for c in range(o["width"] // LANES):
                piece = y[:, o["off"] + c * LANES: o["off"] + (c + 1) * LANES]
                if o.get("rope"):
                    rot = jnp.where(first_half, pltpu.roll(piece, LANES - HEAD_DIM // 2, 1),
                                    pltpu.roll(piece, HEAD_DIM // 2, 1))
                    piece = piece * cos + rot * sin
                if o.get("scale"):
                    piece = piece * QK_SCALE
                if o.get("sigmoid"):
                    piece = jax.nn.sigmoid(piece)
                lanes = slice(c * LANES, (c + 1) * LANES)
                if kind == "flat":
                    o_ref[:, lanes] = piece.astype(dt)
                elif kind == "split":
                    hw = o["hw"]
                    per = LANES // hw
                    for k in range(per):
                        o_ref[c * per + k] = piece[:, k * hw:(k + 1) * hw].astype(dt)
                elif kind == "dilate":
                    r = o["r"]
                    if r == 1:
                        o_ref[0, 0, :, lanes] = piece.astype(dt)
                    else:
                        ys[...] = piece
                        for i in range(r):
                            o_ref[0, i, :, lanes] = ys[pl.ds(i, tm // r, stride=r), :].astype(dt)
                elif kind == "chunk":
                    ys[...] = piece
                    for l in range(NSA_CMP_STRIDE):
                        rows = ys[pl.ds(l, tm // NSA_CMP_STRIDE, stride=NSA_CMP_STRIDE), :].astype(dt)
                        for k in range(heads_per_tile):
                            o_ref[c * heads_per_tile + k, :, l * HEAD_DIM:(l + 1) * HEAD_DIM] = rows[
                                :, k * HEAD_DIM:(k + 1) * HEAD_DIM]
                else:
                    tile = o["tile"]
                    pt = piece.T
                    for k in range(heads_per_tile):
                        for kt in range(tm // tile):
                            o_ref[c * heads_per_tile + k, 0, kt, 0:HEAD_DIM, :] = pt[
                                k * HEAD_DIM:(k + 1) * HEAD_DIM, kt * tile:(kt + 1) * tile].astype(dt)
                            o_ref[c * heads_per_tile + k, 0, kt, HEAD_DIM:V_ROWS, :] = jnp.ones(
                                (V_ROWS - HEAD_DIM, tile), dt)


def _proj(h, gain, w, cos, sin, plan, out_defs, bsz, t_len):
    m = h.shape[0]
    tm = PROJ_TM
    nt = t_len // tm
    out_shapes, out_specs = [], []
    for d in out_defs:
        kind, dt = d[0], d[-1]
        if kind == "flat":
            shape, block, index = (m, d[1]), (tm, d[1]), (lambda i: (i, 0))
        elif kind == "split":
            shape, block, index = (d[1], m, d[2]), (d[1], tm, d[2]), (lambda i: (0, i, 0))
        elif kind == "dilate":
            r = d[1]
            shape, block = (bsz, r, t_len // r, d[2]), (1, r, tm // r, d[2])
            index = lambda i: (i // nt, 0, i % nt, 0)
        elif kind == "chunk":
            width = NSA_CMP_STRIDE * HEAD_DIM
            shape, block = (d[1], m // NSA_CMP_STRIDE, width), (d[1], tm // NSA_CMP_STRIDE, width)
            index = lambda i: (0, i, 0)
        else:
            tile = d[2]
            shape, block = (d[1], bsz, t_len // tile, V_ROWS, tile), (d[1], 1, tm // tile, V_ROWS, tile)
            index = lambda i: (0, i // nt, i % nt, 0, 0)
        out_shapes.append(jax.ShapeDtypeStruct(shape, dt))
        out_specs.append(pl.BlockSpec(block, index))
    return pl.pallas_call(
        functools.partial(_proj_kernel, plan=plan, n_out=len(out_defs)),
        grid=(m // tm,),
        in_specs=[
            pl.BlockSpec((tm, D_MODEL), lambda i: (i, 0)),
            pl.BlockSpec((1, D_MODEL), lambda i: (0, 0)),
            pl.BlockSpec(w.shape, lambda i: (0, 0)),
            pl.BlockSpec((tm, LANES), lambda i: (i % nt, 0)),
            pl.BlockSpec((tm, LANES), lambda i: (i % nt, 0)),
        ],
        out_specs=out_specs,
        out_shape=out_shapes,
        scratch_shapes=[pltpu.VMEM((tm, LANES), F32)],
        compiler_params=_params("parallel"),
    )(h, gain.reshape(1, D_MODEL), w, cos, sin)


def _banded_kernel(q_ref, kp_ref, kc_ref, vp_ref, vc_ref, *rest, n_kv, group, tl, pk, span, max_dist,
                   has_sink, gate_branch, want_lse):
    rest = list(rest)
    sink_ref = rest.pop(0) if has_sink else None
    gate_ref = rest.pop(0) if gate_branch is not None else None
    o_ref = rest.pop(0)
    lse_ref = rest.pop(0) if want_lse else None
    kcat, vcat, s_a, s_b, band = rest
    sb = BAND_SB
    n_seq = kc_ref.shape[1]
    for rr in range(n_seq):
        kcat[rr, 0:pk, :] = kp_ref[0, rr]
        kcat[rr, pk:, :] = kc_ref[0, rr]
        vcat[rr, 0:pk, :] = vp_ref[0, rr]
        vcat[rr, pk:, :] = vc_ref[0, rr]
    nk = span + sb
    kw = n_kv * HEAD_DIM
    merged = group == 1
    stacked = n_kv if merged else group
    rows = stacked * sb
    key = lax.broadcasted_iota(jnp.int32, (nk, rows), 0)
    qry = lax.broadcasted_iota(jnp.int32, (nk, rows), 1) & (sb - 1)
    dist = qry + span - key
    in_band = (dist >= 0) & (dist <= max_dist)
    lane_head = lax.broadcasted_iota(jnp.int32, (1, rows), 1) >> _log2(sb)
    q_head = lax.broadcasted_iota(jnp.int32, (sb, kw), 1) >> _log2(HEAD_DIM)
    v_head = lax.broadcasted_iota(jnp.int32, (nk, kw), 1) >> _log2(HEAD_DIM)
    assert merged or kw == LANES
    items = [(rr, j, h) for rr in range(n_seq) for j in range(tl // sb)
             for h in ([0] if merged else range(n_kv))]
    s_bufs = (s_a, s_b)

    def compute(first_tile):
        def scores(n):
            rr, j, h = items[n]
            lo = pk + j * sb - span
            qrows = slice(j * sb, (j + 1) * sb)
            if merged:
                qj = q_ref[0, rr, qrows, :].astype(F32)
                qs = jnp.concatenate([jnp.where(q_head == r, qj, 0.0).astype(BF16) for r in range(n_kv)], axis=0)
                kk = kcat[rr, lo:lo + nk, :]
            else:
                qs = jnp.concatenate([q_ref[h * group + g, 0, qrows, :] for g in range(group)], axis=0)
                kk = kcat[rr, lo:lo + nk, h * HEAD_DIM:(h + 1) * HEAD_DIM]
            variant = 1 + j if first_tile and j * sb < span else 0
            s_bufs[n % 2][...] = _dot_t(kk, qs) + band[variant]

        band[0] = jnp.where(in_band, 0.0, NEG_INF)
        if first_tile:
            for j in range(span // sb):
                band[1 + j] = jnp.where(in_band & (key >= span - j * sb), 0.0, NEG_INF)

        outs, lses = [], []
        scores(0)
        for n, (rr, j, h) in enumerate(items):
            if n + 1 < len(items):
                scores(n + 1)
            s = s_bufs[n % 2][...]
            lo = pk + j * sb - span
            qrows = slice(j * sb, (j + 1) * sb)
            m = jnp.max(s, axis=0, keepdims=True)
            if has_sink:
                sink = jnp.full((1, rows), sink_ref[h * group], F32)
                for g in range(1, group):
                    sink = jnp.where(lane_head == g, sink_ref[h * group + g], sink)
                sink = sink * LOG2E
                m = jnp.maximum(m, sink)
            p = jnp.exp2(s - m).astype(BF16)
            vwin = vcat[rr, lo:lo + nk, :]
            if merged:
                vv = jnp.concatenate([vwin, jnp.ones((nk, LANES), BF16)], axis=1)
                ones_row = kw
            else:
                vv = jnp.where(v_head == h, vwin.astype(F32), 1.0).astype(BF16)
                ones_row = (1 - h) * HEAD_DIM
            o_t = _dot_tn(vv, p)
            d = o_t[ones_row:ones_row + 1, :]
            if has_sink:
                d = d + jnp.exp2(sink - m)
            d = jnp.where(d > 0, d, 1.0)
            inv = 1.0 / d
            if want_lse:
                lse_t = jnp.broadcast_to(m * LN2 + jnp.log(d), (HEAD_DIM, rows))
            for r in range(stacked):
                cols = slice(r * sb, (r + 1) * sb)
                vrow = (r if merged else h) * HEAD_DIM
                o = (o_t[vrow:vrow + HEAD_DIM, cols] * inv[:, cols]).T
                if gate_ref is not None:
                    gc = r * 3 + gate_branch
                    o = o * gate_ref[h, 0, qrows, gc:gc + 1]
                outs.append(o)
                if want_lse:
                    lses.append(lse_t[:, cols].T)
            if merged or h == n_kv - 1:
                o_ref[0, rr, qrows, :] = jnp.concatenate(outs, axis=-1).astype(o_ref.dtype)
                if want_lse:
                    lse_ref[0, rr, qrows, :] = jnp.concatenate(lses, axis=-1)
                outs, lses = [], []

    first = pl.program_id(2) == 0
    pl.when(first)(functools.partial(compute, True))
    pl.when(jnp.logical_not(first))(functools.partial(compute, False))


def _banded(q, k, v, *, n_kv, group, max_dist, tl, pk, span, sink=None, gate=None, gate_branch=None,
            want_lse=False):
    bsz, n_res, length, kw = k.shape
    qw = n_kv * group * HEAD_DIM
    s_shape = (span + BAND_SB, (n_kv if group == 1 else group) * BAND_SB)
    ratio = tl // pk
    n_seq = min(n_res, max(1, BAND_TL // tl))
    prev = lambda b, i, l: (b, i, jnp.maximum(l * ratio - 1, 0), 0)
    cur = lambda b, i, l: (b, i, l, 0)
    if group == 1:
        q_spec = pl.BlockSpec((1, n_seq, tl, qw), cur)
    else:
        q_spec = pl.BlockSpec((n_kv * group, 1, tl, HEAD_DIM), lambda b, i, l: (0, b, l, 0))
    o_shape = (bsz, n_res, length, qw)
    in_specs = [
        q_spec,
        pl.BlockSpec((1, n_seq, pk, kw), prev),
        pl.BlockSpec((1, n_seq, tl, kw), cur),
        pl.BlockSpec((1, n_seq, pk, kw), prev),
        pl.BlockSpec((1, n_seq, tl, kw), cur),
    ]
    args = [q, k, k, v, v]
    if sink is not None:
        in_specs.append(pl.BlockSpec(memory_space=pltpu.SMEM))
        args.append(sink)
    if gate is not None:
        in_specs.append(pl.BlockSpec((n_kv, 1, tl, LANES), lambda b, i, l: (0, b, l, 0)))
        args.append(gate)
    out_spec = pl.BlockSpec((1, n_seq, tl, qw), cur)
    out_shape = [jax.ShapeDtypeStruct(o_shape, BF16)]
    out_specs = [out_spec]
    if want_lse:
        out_shape.append(jax.ShapeDtypeStruct(o_shape, F32))
        out_specs.append(out_spec)
    res = pl.pallas_call(
        functools.partial(_banded_kernel, n_kv=n_kv, group=group, tl=tl, pk=pk, span=span,
                          max_dist=max_dist, has_sink=sink is not None, gate_branch=gate_branch,
                          want_lse=want_lse),
        grid=(bsz, n_res // n_seq, length // tl),
        in_specs=in_specs,
        out_specs=out_specs,
        out_shape=out_shape,
        scratch_shapes=[pltpu.VMEM((n_seq, pk + tl, kw), BF16), pltpu.VMEM((n_seq, pk + tl, kw), BF16),
                        pltpu.VMEM(s_shape, F32), pltpu.VMEM(s_shape, F32),
                        pltpu.VMEM((1 + span // BAND_SB,) + s_shape, F32)],
        compiler_params=_params("parallel", "parallel", "parallel"),
    )(*args)
    return res if want_lse else res[0]


def _compress_one(z_ref, pe_ref, w1_ref, w2_ref, o_ref):
    half = NSA_CMP_STRIDE * HEAD_DIM
    z = z_ref[0, 0].astype(F32)
    za = (z + pe_ref[0, :, 0:half]).astype(BF16)
    zb = (z + pe_ref[0, :, half:2 * half]).astype(BF16)
    ya = _dot(za, w1_ref[0, 0:half, :])
    yb = _dot(zb, w1_ref[0, half:2 * half, :])
    nrow = ya.shape[0]
    hid = jax.nn.gelu(ya + pltpu.roll(yb, nrow - 1, 0))
    o_ref[0, 0] = _dot(hid.astype(BF16), w2_ref[0]).astype(o_ref.dtype)


def _compress_kernel(zk_ref, zv_ref, pek_ref, pev_ref, w1k_ref, w1v_ref, w2k_ref, w2v_ref, ok_ref, ov_ref):
    _compress_one(zk_ref, pek_ref, w1k_ref, w2k_ref, ok_ref)
    _compress_one(zv_ref, pev_ref, w1v_ref, w2v_ref, ov_ref)


def _compress(zk, zv, pe, w1, w2):
    hk, bsz, nch, width = zk.shape
    zspec = pl.BlockSpec((1, 1, nch, width), lambda h, b: (h, b, 0, 0))
    ospec = pl.BlockSpec((1, 1, nch, HEAD_DIM), lambda h, b: (h, b, 0, 0))
    oshape = jax.ShapeDtypeStruct((hk, bsz, nch, HEAD_DIM), BF16)

    def wspec(arr, which):
        return pl.BlockSpec((1,) + arr.shape[1:], lambda h, b: (which, 0, 0))

    return pl.pallas_call(
        _compress_kernel,
        grid=(hk, bsz),
        in_specs=[zspec, zspec, wspec(pe, 0), wspec(pe, 1), wspec(w1, 0), wspec(w1, 1), wspec(w2, 0), wspec(w2, 1)],
        out_specs=[ospec, ospec],
        out_shape=[oshape, oshape],
        compiler_params=_params("parallel", "parallel"),
    )(zk, zv, pe, pe, w1, w1, w2, w2)


def _topk_select(score, k, idxf):
    ncand = score.shape[0]
    rem = score
    sel = jnp.zeros_like(score)
    for _ in range(k):
        mx = jnp.max(rem, axis=0, keepdims=True)
        first = jnp.min(jnp.where(rem == mx, idxf, float(ncand)), axis=0, keepdims=True)
        pick = idxf == first
        sel = jnp.where(pick, jnp.where(mx > 0.5 * NEG_INF, 1.0, 0.0), sel)
        rem = jnp.where(pick, REMOVED, rem)
    return sel


def _cmp_kernel(q_ref, kc_ref, vc_ref, gate_ref, o_ref, bias_ref, *, tq, group):
    qi = pl.program_id(2)
    kc = kc_ref[0, 0]
    vc = vc_ref[0, 0]
    ncp = kc.shape[0]
    nsb = bias_ref.shape[-1]
    rows = group * tq
    last = lax.broadcasted_iota(jnp.int32, (ncp, 1), 0) * NSA_CMP_STRIDE + (NSA_CMP_LEN - 1)
    t_in_tile = lax.broadcasted_iota(jnp.int32, (1, rows), 1) & (tq - 1)
    visible = (last - t_in_tile) <= qi * tq
    qs = jnp.concatenate([q_ref[0, :, g * HEAD_DIM:(g + 1) * HEAD_DIM] for g in range(group)], axis=0)
    s = jnp.where(visible, _dot_t(kc, qs), NEG_INF)
    m = jnp.max(s, axis=0, keepdims=True)
    m = jnp.where(m > 0.5 * NEG_INF, m, 0.0)
    p = jnp.exp2(s - m)
    vc_ones = jnp.concatenate([vc, jnp.ones_like(vc)], axis=1)
    o_t = _dot_tn(vc_ones, p.astype(BF16))
    d = o_t[HEAD_DIM:HEAD_DIM + 1, :]
    d = jnp.where(d > 0, d, 1.0)
    inv = 1.0 / d
    pc = p * inv
    o_t = o_t[0:HEAD_DIM, :] * inv
    outs = []
    psum = pc[:, 0:tq]
    for g in range(group):
        if g:
            psum = psum + pc[:, g * tq:(g + 1) * tq]
        outs.append(o_t[:, g * tq:(g + 1) * tq].T * gate_ref[0, 0, :, g * 3:g * 3 + 1])
    o_ref[0] = jnp.concatenate(outs, axis=-1).astype(o_ref.dtype)

    jj = lax.broadcasted_iota(jnp.int32, (nsb, ncp), 0) * NSA_SEL_LEN
    nn = lax.broadcasted_iota(jnp.int32, (nsb, ncp), 1) * NSA_CMP_STRIDE
    ov = jnp.maximum(jnp.minimum(nn + NSA_CMP_LEN, jj + NSA_SEL_LEN) - jnp.maximum(nn, jj), 0)
    ov = (ov.astype(F32) / NSA_CMP_LEN).astype(BF16)
    p_hi = psum.astype(BF16)
    p_lo = (psum - p_hi.astype(F32)).astype(BF16)
    imp = _dot(ov, p_hi) + _dot(ov, p_lo)

    j = lax.broadcasted_iota(jnp.int32, (nsb, tq), 0)
    tt = qi * tq + lax.broadcasted_iota(jnp.int32, (nsb, tq), 1)
    cb = tt >> _log2(NSA_SEL_LEN)
    forced = (j == 0) | (j == cb) | (j == cb - 1)
    score = jnp.where(j <= cb, jnp.where(forced, FORCE_SCORE, imp), NEG_INF)
    sel = _topk_select(score, min(NSA_SEL_COUNT, nsb), j.astype(F32))
    bias_ref[0, 0] = jnp.where(sel > 0.5, 0.0, MASK_BIAS).T.astype(bias_ref.dtype)


def _cmp_attention(qu, kc, vc, gate, t_len):
    bsz = qu.shape[0]
    hk, _, ncp, _ = kc.shape
    group = B_Q_HEADS // B_KV_HEADS
    tq = CMP_TQ
    nsb = t_len // NSA_SEL_LEN
    gw = group * HEAD_DIM
    cspec = pl.BlockSpec((1, 1, ncp, HEAD_DIM), lambda b, h, i: (h, b, 0, 0))
    return pl.pallas_call(
        functools.partial(_cmp_kernel, tq=tq, group=group),
        grid=(bsz, hk, t_len // tq),
        in_specs=[
            pl.BlockSpec((1, tq, gw), lambda b, h, i: (b, i, h)),
            cspec, cspec,
            pl.BlockSpec((1, 1, tq, LANES), lambda b, h, i: (h, b, i, 0)),
        ],
        out_specs=[
            pl.BlockSpec((1, tq, gw), lambda b, h, i: (b, i, h)),
            pl.BlockSpec((1, 1, tq, nsb), lambda b, h, i: (h, b, i, 0)),
        ],
        out_shape=[
            jax.ShapeDtypeStruct((bsz, t_len, hk * gw), BF16),
            jax.ShapeDtypeStruct((hk, bsz, t_len, nsb), BF16),
        ],
        compiler_params=_params("parallel", "parallel", "parallel"),
    )(qu, kc, vc, gate)


def _moba_gate_kernel(q_ref, k_ref, bias_ref, km_hi, km_lo, *, tq):
    qi = pl.program_id(2)
    t_len = k_ref.shape[2]
    ncol = bias_ref.shape[-1]
    nb = t_len // MOBA_BLOCK

    @pl.when(qi == 0)
    def _():
        blk = lax.broadcasted_iota(jnp.int32, (ncol, t_len), 0)
        pos = lax.broadcasted_iota(jnp.int32, (ncol, t_len), 1)
        member = jnp.where((pos >> _log2(MOBA_BLOCK)) == blk, 1.0, 0.0).astype(BF16)
        kmean = _dot(member, k_ref[0, 0]) * (1.0 / MOBA_BLOCK)
        hi = kmean.astype(BF16)
        km_hi[...] = hi
        km_lo[...] = (kmean - hi.astype(F32)).astype(BF16)

    q = q_ref[0, 0]
    gate = (_dot_t(km_hi[...], q) + _dot_t(km_lo[...], q))[0:nb]
    j = lax.broadcasted_iota(jnp.int32, (nb, tq), 0)
    t = qi * tq + lax.broadcasted_iota(jnp.int32, (nb, tq), 1)
    cb = t >> _log2(MOBA_BLOCK)
    score = jnp.where(j < cb, gate, NEG_INF)
    sel = _topk_select(score, min(MOBA_TOPK, nb - 1), j.astype(F32))
    keep = (sel > 0.5) | (j == cb)
    bias = jnp.where(keep, 0.0, MASK_BIAS)
    bias = jnp.concatenate([bias, jnp.zeros((ncol - nb, tq), F32)], axis=0)
    bias_ref[0, 0] = bias.T.astype(bias_ref.dtype)


def _moba_gate(q, k):
    nh, bsz, t_len, _ = q.shape
    tq = min(GATE_TQ, t_len)
    return pl.pallas_call(
        functools.partial(_moba_gate_kernel, tq=tq),
        grid=(nh, bsz, t_len // tq),
        in_specs=[
            pl.BlockSpec((1, 1, tq, HEAD_DIM), lambda h, b, i: (h, b, i, 0)),
            pl.BlockSpec((1, 1, t_len, HEAD_DIM), lambda h, b, i: (h, b, 0, 0)),
        ],
        out_specs=pl.BlockSpec((1, 1, tq, HEAD_DIM), lambda h, b, i: (h, b, i, 0)),
        out_shape=jax.ShapeDtypeStruct((nh, bsz, t_len, HEAD_DIM), BF16),
        scratch_shapes=[pltpu.VMEM((HEAD_DIM, HEAD_DIM), BF16), pltpu.VMEM((HEAD_DIM, HEAD_DIM), BF16)],
        compiler_params=_params("parallel", "parallel", "arbitrary"),
    )(q, k)


def _flash_kernel(q_ref, bias_ref, k_ref, vt_ref, *rest, group, tq, tk, blk_len, q_scale, gate_branch):
    if gate_branch is not None:
        gate_ref, o_ref, qa, ka, m_s, acc, s_a, s_b = rest
    else:
        gate_ref = None
        o_ref, qa, ka, m_s, acc, s_a, s_b = rest
    qi = pl.program_id(2)
    rows = group * tq
    t_len = k_ref.shape[2]

    @pl.when(qi == 0)
    def _():
        ka[:, 0:HEAD_DIM] = k_ref[0, 0]
        kpos = lax.broadcasted_iota(jnp.int32, (t_len, HEAD_DIM), 0)
        kblk = lax.broadcasted_iota(jnp.int32, (t_len, HEAD_DIM), 1)
        ka[:, HEAD_DIM:2 * HEAD_DIM] = jnp.where((kpos >> _log2(blk_len)) == kblk, 1.0, 0.0).astype(BF16)

    bias = bias_ref[0, 0]
    for g in range(group):
        qg = q_ref[g, 0]
        if q_scale != 1.0:
            qg = (qg.astype(F32) * q_scale).astype(BF16)
        qa[g * tq:(g + 1) * tq, 0:HEAD_DIM] = qg
        qa[g * tq:(g + 1) * tq, HEAD_DIM:2 * HEAD_DIM] = bias
    m_s[...] = jnp.full_like(m_s, NEG_INF)
    acc[...] = jnp.zeros_like(acc)

    def scores(ki, s_ref):
        start = pl.multiple_of(ki * tk, tk)
        s_ref[...] = _dot_t(ka[pl.ds(start, tk), :], qa[...])

    def softmax_pv(ki, s_ref, masked):
        s = s_ref[...]
        if masked:
            kpos = ki * tk + lax.broadcasted_iota(jnp.int32, (tk, rows), 0)
            qpos = qi * tq + (lax.broadcasted_iota(jnp.int32, (tk, rows), 1) & (tq - 1))
            s = jnp.where(kpos <= qpos, s, NEG_INF)
        m_old = m_s[...]
        m_new = jnp.maximum(m_old, jnp.max(s, axis=0, keepdims=True))
        alpha = jnp.exp2(m_old - m_new)
        p = jnp.exp2(s - m_new)
        acc[...] = alpha * acc[...] + _dot(vt_ref[0, 0, ki], p.astype(BF16))
        m_s[...] = m_new

    n_full = (qi * tq) >> _log2(tk)
    scores(0, s_a)

    def pair(j, carry):
        k0 = 2 * j
        scores(k0 + 1, s_b)
        softmax_pv(k0, s_a, False)
        scores(k0 + 2, s_a)
        softmax_pv(k0 + 1, s_b, False)
        return carry

    lax.fori_loop(0, n_full >> 1, pair, 0)
    cur = (n_full >> 1) << 1

    @pl.when((n_full & 1) == 1)
    def _():
        scores(cur + 1, s_b)
        softmax_pv(cur, s_a, False)
        softmax_pv(cur + 1, s_b, True)

    @pl.when((n_full & 1) == 0)
    def _():
        softmax_pv(cur, s_a, True)

    d = acc[HEAD_DIM:HEAD_DIM + 1, :]
    d = jnp.where(d > 0, d, 1.0)
    o_t = acc[0:HEAD_DIM, :] * (1.0 / d)
    outs = []
    for g in range(group):
        og = o_t[:, g * tq:(g + 1) * tq].T
        if gate_ref is not None:
            gc = g * 3 + gate_branch
            og = og * gate_ref[0, 0, :, gc:gc + 1]
        outs.append(og)
    res = outs[0] if group == 1 else jnp.concatenate(outs, axis=-1)
    o_ref[0, 0] = res.astype(o_ref.dtype)


def _flash(q, bias, k, vt, *, group, tq, blk_len, q_scale, out_shape, out_index, gate=None, gate_branch=None):
    nh, bsz, t_len, _ = k.shape
    gw = group * HEAD_DIM
    nt, tk = vt.shape[2], vt.shape[4]
    in_specs = [
        pl.BlockSpec((group, 1, tq, HEAD_DIM), lambda h, b, i: (h, b, i, 0)),
        pl.BlockSpec((1, 1, tq, HEAD_DIM), lambda h, b, i: (h, b, i, 0)),
        pl.BlockSpec((1, 1, t_len, HEAD_DIM), lambda h, b, i: (h, b, 0, 0)),
        pl.BlockSpec((1, 1, nt, V_ROWS, tk), lambda h, b, i: (h, b, 0, 0, 0)),
    ]
    args = [q, bias, k, vt]
    if gate is not None:
        in_specs.append(pl.BlockSpec((1, 1, tq, LANES), lambda h, b, i: (h, b, i, 0)))
        args.append(gate)
    rows = group * tq
    return pl.pallas_call(
        functools.partial(_flash_kernel, group=group, tq=tq, tk=tk, blk_len=blk_len, q_scale=q_scale,
                          gate_branch=gate_branch),
        grid=(nh, bsz, t_len // tq),
        in_specs=in_specs,
        out_specs=pl.BlockSpec((1, 1, tq, gw), out_index),
        out_shape=jax.ShapeDtypeStruct(out_shape, BF16),
        scratch_shapes=[
            pltpu.VMEM((rows, 2 * HEAD_DIM), BF16),
            pltpu.VMEM((t_len, 2 * HEAD_DIM), BF16),
            pltpu.VMEM((1, rows), F32),
            pltpu.VMEM((V_ROWS, rows), F32),
            pltpu.VMEM((tk, rows), F32),
            pltpu.VMEM((tk, rows), F32),
        ],
        compiler_params=_params("parallel", "parallel", "arbitrary"),
    )(*args)


def _even_delta(refs, scratch):
    oa_ref, oc_ref, os_ref, ow_ref, w_ref = refs
    na = oa_ref.shape[-1]
    ob = (oc_ref[...].astype(F32) + os_ref[...].astype(F32) + ow_ref[...].astype(F32)).astype(BF16)
    return _dot(oa_ref[...], w_ref[0:na, :]) + _dot(ob, w_ref[na:, :])


def _even_mix(oa, ocmp, osel, owin, w):
    aspec = pl.BlockSpec((FFN_TM, oa.shape[-1]), lambda i: (i, 0))
    return dict(args=[oa, ocmp, osel, owin, w], specs=[aspec] * 4 + [_resident(w.shape)], scratch=[],
                fn=_even_delta)


def _odd_delta(refs, scratch, *, dilations):
    ng = len(dilations)
    o_refs, l_refs = refs[:ng], refs[ng:2 * ng]
    od_ref, w_ref = refs[2 * ng:]
    scratch = list(scratch)
    tm = od_ref.shape[1]

    def tokens(ref, r):
        if r == 1:
            return ref[0, 0].astype(F32)
        pieces = []
        for c in range(ref.shape[-1] // LANES):
            scr = scratch.pop(0)
            for i in range(r):
                scr[pl.ds(i, tm // r, stride=r), :] = ref[0, i, :, c * LANES:(c + 1) * LANES].astype(F32)
            pieces.append(scr[...])
        return jnp.concatenate(pieces, axis=-1)

    outs = [tokens(ref, r) for ref, r in zip(o_refs, dilations)]
    lses = [tokens(ref, r) for ref, r in zip(l_refs, dilations)]
    mx = functools.reduce(jnp.maximum, lses)
    es = [jnp.exp(l - mx) for l in lses]
    tot = functools.reduce(lambda a, b: a + b, es)
    oc = functools.reduce(lambda a, b: a + b, [(e / tot) * o for e, o in zip(es, outs)])
    nc = oc.shape[-1]
    delta = _dot(oc.astype(BF16), w_ref[0:nc, :])
    for hh in range(od_ref.shape[0]):
        delta = delta + _dot(od_ref[hh], w_ref[nc + hh * HEAD_DIM: nc + (hh + 1) * HEAD_DIM, :])
    return delta


def _odd_mix(outs, lses, od, w, t_len):
    tm = FFN_TM
    nt = t_len // tm
    gw = outs[0].shape[-1]
    dilations = tuple(o.shape[1] for o in outs)
    gspecs = [pl.BlockSpec((1, r, tm // r, gw), lambda i: (i // nt, 0, i % nt, 0)) for r in dilations]
    n_scr = 2 * (gw // LANES) * sum(1 for r in dilations if r > 1)
    specs = gspecs + gspecs + [pl.BlockSpec((od.shape[0], tm, HEAD_DIM), lambda i: (0, i, 0)), _resident(w.shape)]
    return dict(args=[*outs, *lses, od, w], specs=specs, scratch=[pltpu.VMEM((tm, LANES), F32)] * n_scr,
                fn=functools.partial(_odd_delta, dilations=dilations))


def _col_ranges(sizes):
    offs, acc = [], 0
    for s in sizes:
        offs.append((acc, acc + s))
        acc += s
    return offs


def _even_mixer(h, gain, w_in, w_out, sinks, cmp_pe, cmp_w1, cmp_w2, cos, sin, bsz, t_len):
    m = bsz * t_len
    qa_w, kva_w = A_Q_HEADS * HEAD_DIM, A_KV_HEADS * HEAD_DIM
    qb_w, kvb_w = B_Q_HEADS * HEAD_DIM, B_KV_HEADS * HEAD_DIM
    sizes = [qa_w, kva_w, kva_w, qb_w] + [kvb_w] * 6 + [3 * B_Q_HEADS]
    (aq, ak, av, bq, bkc, bvc, bks, bvs, bkw, bvw, bg) = [w_in[:, a:b] for a, b in _col_ranges(sizes)]
    group = B_Q_HEADS // B_KV_HEADS
    gpad = jnp.zeros((D_MODEL, LANES - 3 * group), w_in.dtype)
    gates = [x for hk in range(B_KV_HEADS) for x in (bg[:, hk * 3 * group:(hk + 1) * 3 * group], gpad)]
    w = jnp.concatenate([aq, bq, ak, bkw, bks, av, bvw, bvs, bkc, bvc] + gates, axis=1).astype(BF16)
    c = [0]

    def take(width):
        c[0] += width
        return c[0] - width

    def out(idx, off, width, kind, **kw):
        return dict(idx=idx, off=off, width=width, kind=kind, **kw)

    plan = [
        (take(qa_w), qa_w, [out(0, 0, qa_w, "split", hw=HEAD_DIM, rope=True, scale=True)]),
        (take(qb_w), qb_w, [out(1, 0, qb_w, "split", hw=HEAD_DIM, rope=True, scale=True),
                            out(5, 0, qb_w, "flat", scale=True)]),
        (take(3 * kva_w), 3 * kva_w, [out(2, 0, kva_w, "flat", rope=True),
                                      out(3, kva_w, kvb_w, "flat", rope=True),
                                      out(4, kva_w + kvb_w, kvb_w, "split", hw=HEAD_DIM, rope=True)]),
        (take(5 * kvb_w), 5 * kvb_w, [out(6, 0, kva_w, "flat"), out(7, kva_w, kvb_w, "flat"),
                                      out(8, 2 * kvb_w, kvb_w, "vt", tile=FLASH_TK),
                                      out(9, 3 * kvb_w, kvb_w, "chunk"),
                                      out(10, 4 * kvb_w, kvb_w, "chunk")]),
        (take(B_KV_HEADS * LANES), B_KV_HEADS * LANES,
         [out(11, 0, B_KV_HEADS * LANES, "split", hw=LANES, sigmoid=True)]),
    ]
    out_defs = [("split", A_Q_HEADS, HEAD_DIM, BF16), ("split", B_Q_HEADS, HEAD_DIM, BF16), ("flat", kva_w, BF16),
                ("flat", kvb_w, BF16),
                ("split", B_KV_HEADS, HEAD_DIM, BF16), ("flat", qb_w, BF16), ("flat", kva_w, BF16),
                ("flat", kvb_w, BF16), ("vt", B_KV_HEADS, FLASH_TK, BF16), ("chunk", B_KV_HEADS, BF16),
                ("chunk", B_KV_HEADS, BF16), ("split", B_KV_HEADS, LANES, F32)]
    (aq_r, bq_r, ak_r, bkw_r, bks_r, bq_u, av_, bvw_, bvs_t, zk, zv, gate) = _proj(
        h, gain, w, cos, sin, plan, out_defs, bsz, t_len)

    def seq(x):
        return x.reshape(bsz, 1, t_len, x.shape[-1])

    def heads(x):
        return x.reshape(x.shape[0], bsz, t_len, x.shape[-1])

    gate4 = heads(gate)
    oa = _banded(heads(aq_r), seq(ak_r), seq(av_), n_kv=A_KV_HEADS, group=A_Q_HEADS // A_KV_HEADS,
                 max_dist=A_WINDOW - 1, tl=BAND_TL, pk=128, span=128, sink=sinks)
    nch = t_len // NSA_CMP_STRIDE
    zshape = (B_KV_HEADS, bsz, nch, NSA_CMP_STRIDE * HEAD_DIM)
    kc, vc = _compress(zk.reshape(zshape), zv.reshape(zshape), cmp_pe.reshape(2, 1, NSA_CMP_LEN * HEAD_DIM),
                       cmp_w1.astype(BF16), cmp_w2.astype(BF16))
    ocmp, bias = _cmp_attention(bq_u.reshape(bsz, t_len, qb_w), kc, vc, gate4, t_len)
    osel = _flash(heads(bq_r), bias, heads(bks_r), bvs_t, group=group, tq=FLASH_T, blk_len=NSA_SEL_LEN,
                  q_scale=1.0, out_shape=(1, bsz, t_len, qb_w), out_index=lambda hk, b, i: (0, b, i, hk),
                  gate=gate4, gate_branch=1)
    owin = _banded(heads(bq_r), seq(bkw_r), seq(bvw_), n_kv=B_KV_HEADS, group=group,
                   max_dist=NSA_WINDOW - 1, tl=BAND_TL, pk=512, span=512, gate=gate4, gate_branch=2)
    return _even_mix(oa.reshape(m, qa_w), ocmp.reshape(m, qb_w), osel.reshape(m, qb_w),
                     owin.reshape(m, qb_w), w_out.astype(BF16))


def _odd_mixer(h, gain, w_in, w_out, cos, sin, bsz, t_len):
    m = bsz * t_len
    cw = C_HEADS * HEAD_DIM
    dw = D_HEADS * HEAD_DIM
    gw = C_HEADS_PER_GROUP * HEAD_DIM
    n_groups = len(C_GROUPS)

    def group_outs(base, **kw):
        return [dict(idx=base + gi, off=gi * gw, width=gw, kind="dilate", r=r, **kw)
                for gi, (_, r) in enumerate(C_GROUPS)]

    plan = [
        (0, cw, group_outs(0, rope=True, scale=True)),
        (cw, cw, group_outs(n_groups, rope=True)),
        (2 * cw, cw, group_outs(2 * n_groups)),
        (3 * cw, 3 * dw, [dict(idx=3 * n_groups, off=0, width=dw, kind="split", hw=HEAD_DIM, rope=True),
                          dict(idx=3 * n_groups + 1, off=dw, width=dw, kind="split", hw=HEAD_DIM, rope=True),
                          dict(idx=3 * n_groups + 2, off=2 * dw, width=dw, kind="vt", tile=MOBA_TK)]),
    ]
    out_defs = [("dilate", r, gw, BF16) for _ in range(3) for _, r in C_GROUPS]
    out_defs += [("split", D_HEADS, HEAD_DIM, BF16), ("split", D_HEADS, HEAD_DIM, BF16),
                 ("vt", D_HEADS, MOBA_TK, BF16)]
    res = _proj(h, gain, w_in.astype(BF16), cos, sin, plan, out_defs, bsz, t_len)
    cq, ck, cv = res[0:n_groups], res[n_groups:2 * n_groups], res[2 * n_groups:3 * n_groups]
    dq, dk, dv_t = res[3 * n_groups:]
    outs, lses = [], []
    for gi, (wlen, r) in enumerate(C_GROUPS):
        o, lse = _banded(cq[gi], ck[gi], cv[gi], n_kv=C_HEADS_PER_GROUP, group=1, max_dist=wlen // r,
                         tl=min(BAND_TL, t_len // r), pk=128, span=128, want_lse=True)
        outs.append(o)
        lses.append(lse)

    def heads(x):
        return x.reshape(x.shape[0], bsz, t_len, x.shape[-1])

    bias = _moba_gate(heads(dq), heads(dk))
    od = _flash(heads(dq), bias, heads(dk), dv_t, group=1, tq=MOBA_T, blk_len=MOBA_BLOCK, q_scale=QK_SCALE,
                out_shape=(D_HEADS, bsz, t_len, HEAD_DIM), out_index=lambda hh, b, i: (hh, b, i, 0))
    return _odd_mix(outs, lses, od.reshape(D_HEADS, m, HEAD_DIM), w_out.astype(BF16), t_len)


def kernel(x, ffn_norm_pre, mix_norm, ffn_norm_post, ffn_wi, ffn_wo, even_w_in, even_w_out, even_sinks,
           nsa_cmp_pe, nsa_cmp_w1, nsa_cmp_w2, odd_w_in, odd_w_out, final_norm):
    bsz, t_len, _ = x.shape
    depth = ffn_wi.shape[0]
    cos, sin = _rope_tables(t_len)
    wi = ffn_wi.astype(BF16)
    wo = ffn_wo.astype(BF16)
    h = x.reshape(bsz * t_len, D_MODEL)
    for layer in range(depth):
        i = layer // 2
        h = _ffn(h, ffn_norm_pre[layer], wi[layer, 0], wo[layer, 0])
        if layer % 2 == 0:
            mix = _even_mixer(h, mix_norm[layer], even_w_in[i], even_w_out[i], even_sinks[i], nsa_cmp_pe[i],
                              nsa_cmp_w1[i], nsa_cmp_w2[i], cos, sin, bsz, t_len)
        else:
            mix = _odd_mixer(h, mix_norm[layer], odd_w_in[i], odd_w_out[i], cos, sin, bsz, t_len)
        last = layer == depth - 1
        h = _ffn(h, ffn_norm_post[layer], wi[layer, 1], wo[layer, 1], final_norm if last else None, mix=mix)
    return h.reshape(bsz, t_len, D_MODEL)
```

```python
import functools

import jax
import jax.numpy as jnp
from jax import lax
from jax.experimental import pallas as pl
from jax.experimental.pallas import tpu as pltpu

D_MODEL = 1024
HEAD_DIM = 64
ROPE_THETA = 10000.0
NORM_EPS = 1e-6
D_FF = 2816
NEG_INF = -1e30
FORCE_SCORE = 1e4

A_Q_HEADS = 8
A_KV_HEADS = 2
A_WINDOW = 128
B_Q_HEADS = 8
B_KV_HEADS = 2
NSA_CMP_LEN = 32
NSA_CMP_STRIDE = 16
NSA_CMP_HIDDEN = 256
NSA_SEL_LEN = 64
NSA_SEL_COUNT = 8
NSA_WINDOW = 512
C_GROUPS = ((128, 1), (512, 4), (2048, 16))
C_HEADS_PER_GROUP = 4
C_HEADS = len(C_GROUPS) * C_HEADS_PER_GROUP
D_HEADS = 4
MOBA_BLOCK = 256
MOBA_TOPK = 3

LANES = 128
LOG2E = 1.4426950408889634
LN2 = 0.6931471805599453
QK_SCALE = HEAD_DIM ** -0.5 * LOG2E
MASK_BIAS = -32768.0
V_ROWS = HEAD_DIM + 16
REMOVED = -3e38
VMEM_LIMIT = 52 * 1024 * 1024

FFN_TM = 512
FFN_TF = 256
PROJ_TM = 512
BAND_TL = 512
BAND_SB = 128
FLASH_T = 256
FLASH_TK = 256
FLASH_CHAINS = 2
MOBA_T = 512
MOBA_TK = 512
GATE_TQ = 4096
CMP_TQ = 1024

BF16 = jnp.bfloat16
F32 = jnp.float32


def _params(*sem):
    return pltpu.CompilerParams(dimension_semantics=sem, vmem_limit_bytes=VMEM_LIMIT)


def _rms(x, g):
    return x * lax.rsqrt(jnp.mean(x * x, axis=-1, keepdims=True) + NORM_EPS) * g


def _log2(n):
    assert n & (n - 1) == 0
    return n.bit_length() - 1


def _dot(a, b):
    return jnp.dot(a, b, preferred_element_type=F32)


def _dot_t(a, b):
    return lax.dot_general(a, b, (((1,), (1,)), ((), ())), preferred_element_type=F32)


def _dot_tn(a, b):
    return lax.dot_general(a, b, (((0,), (0,)), ((), ())), preferred_element_type=F32)


def _ffn_kernel(x_ref, g_ref, wi_ref, wo_ref, *rest, final, mix_fn, n_mix):
    rest = list(rest)
    fg_ref = rest.pop(0) if final else None
    mix_refs = [rest.pop(0) for _ in range(n_mix)]
    o_ref, act_scr = rest.pop(0), rest.pop(0)
    x = x_ref[...]
    if mix_fn is not None:
        x = x + mix_fn(mix_refs, rest)
    o_ref[...] = x
    n = _rms(x, g_ref[...]).astype(BF16)
    for c in range(D_FF // FFN_TF):
        cols = slice(c * FFN_TF, (c + 1) * FFN_TF)
        gate = _dot(n, wi_ref[:, cols])
        up = _dot(n, wi_ref[:, D_FF + c * FFN_TF:D_FF + (c + 1) * FFN_TF])
        act_scr[:, cols] = (gate * jax.nn.sigmoid(gate) * up).astype(BF16)
    h = o_ref[...] + 0.5 * _dot(act_scr[...], wo_ref[...])
    if final:
        h = _rms(h, fg_ref[...])
    o_ref[...] = h


def _resident(shape):
    return pl.BlockSpec(shape, lambda i: (0,) * len(shape), pipeline_mode=pl.Buffered(1))


def _ffn(h, gain, wi, wo, final_gain=None, mix=None):
    m = h.shape[0]
    tm = FFN_TM
    final = final_gain is not None
    in_specs = [
        pl.BlockSpec((tm, D_MODEL), lambda i: (i, 0)),
        pl.BlockSpec((1, D_MODEL), lambda i: (0, 0)),
        _resident(wi.shape),
        _resident(wo.shape),
    ]
    args = [h, gain.reshape(1, D_MODEL), wi, wo]
    if final:
        in_specs.append(pl.BlockSpec((1, D_MODEL), lambda i: (0, 0)))
        args.append(final_gain.reshape(1, D_MODEL))
    scratch = [pltpu.VMEM((tm, D_FF), BF16)]
    if mix is not None:
        in_specs += mix["specs"]
        args += mix["args"]
        scratch += mix["scratch"]
    return pl.pallas_call(
        functools.partial(_ffn_kernel, final=final, mix_fn=mix and mix["fn"], n_mix=len(mix["args"]) if mix else 0),
        grid=(m // tm,),
        in_specs=in_specs,
        out_specs=pl.BlockSpec((tm, D_MODEL), lambda i: (i, 0)),
        out_shape=jax.ShapeDtypeStruct((m, D_MODEL), F32),
        scratch_shapes=scratch,
        compiler_params=_params("parallel"),
    )(*args)


def _rope_tables(t):
    inv = 1.0 / (ROPE_THETA ** (jnp.arange(0, HEAD_DIM, 2, dtype=F32) / HEAD_DIM))
    ang = jnp.arange(t, dtype=F32)[:, None] * inv[None, :]
    cos = jnp.cos(ang)
    sin = jnp.sin(ang)
    return (jnp.concatenate([cos, cos, cos, cos], axis=-1),
            jnp.concatenate([-sin, sin, -sin, sin], axis=-1))


def _proj_kernel(x_ref, g_ref, w_ref, cos_ref, sin_ref, *rest, plan, n_out):
    out_refs, (ys,) = rest[:n_out], rest[n_out:]
    n = _rms(x_ref[...], g_ref[...]).astype(BF16)
    tm = n.shape[0]
    cos = cos_ref[...]
    sin = sin_ref[...]
    lane = lax.broadcasted_iota(jnp.int32, cos.shape, 1)
    first_half = (lane & (HEAD_DIM - 1)) < HEAD_DIM // 2
    heads_per_tile = LANES // HEAD_DIM
    for off, width, outs in plan:
        y = _dot(n, w_ref[:, off:off + width])
        for o in outs:
            o_ref = out_refs[o["idx"]]
            dt = o_ref.dtype
            kind = o["kind"]
            for c in range(o["width"] // LANES):
                piece = y[:, o["off"] + c * LANES: o["off"] + (c + 1) * LANES]
                if o.get("rope"):
                    rot = jnp.where(first_half, pltpu.roll(piece, LANES - HEAD_DIM // 2, 1),
                                    pltpu.roll(piece, HEAD_DIM // 2, 1))
                    piece = piece * cos + rot * sin
                if o.get("scale"):
                    piece = piece * QK_SCALE
                if o.get("sigmoid"):
                    piece = jax.nn.sigmoid(piece)
                lanes = slice(c * LANES, (c + 1) * LANES)
                if kind == "flat":
                    o_ref[:, lanes] = piece.astype(dt)
                elif kind == "split":
                    hw = o["hw"]
                    per = LANES // hw
                    for k in range(per):
                        o_ref[c * per + k] = piece[:, k * hw:(k + 1) * hw].astype(dt)
                elif kind == "dilate":
                    r = o["r"]
                    if r == 1:
                        o_ref[0, 0, :, lanes] = piece.astype(dt)
                    else:
                        ys[...] = piece
                        for i in range(r):
                            o_ref[0, i, :, lanes] = ys[pl.ds(i, tm // r, stride=r), :].astype(dt)
                elif kind == "chunk":
                    ys[...] = piece
                    for l in range(NSA_CMP_STRIDE):
                        rows = ys[pl.ds(l, tm // NSA_CMP_STRIDE, stride=NSA_CMP_STRIDE), :].astype(dt)
                        for k in range(heads_per_tile):
                            o_ref[c * heads_per_tile + k, :, l * HEAD_DIM:(l + 1) * HEAD_DIM] = rows[
                                :, k * HEAD_DIM:(k + 1) * HEAD_DIM]
                else:
                    tile = o["tile"]
                    pt = piece.T
                    for k in range(heads_per_tile):
                        for kt in range(tm // tile):
                            o_ref[c * heads_per_tile + k, 0, kt, 0:HEAD_DIM, :] = pt[
                                k * HEAD_DIM:(k + 1) * HEAD_DIM, kt * tile:(kt + 1) * tile].astype(dt)
                            o_ref[c * heads_per_tile + k, 0, kt, HEAD_DIM:V_ROWS, :] = jnp.ones(
                                (V_ROWS - HEAD_DIM, tile), dt)


def _proj(h, gain, w, cos, sin, plan, out_defs, bsz, t_len):
    m = h.shape[0]
    tm = PROJ_TM
    nt = t_len // tm
    out_shapes, out_specs = [], []
    for d in out_defs:
        kind, dt = d[0], d[-1]
        if kind == "flat":
            shape, block, index = (m, d[1]), (tm, d[1]), (lambda i: (i, 0))
        elif kind == "split":
            shape, block, index = (d[1], m, d[2]), (d[1], tm, d[2]), (lambda i: (0, i, 0))
        elif kind == "dilate":
            r = d[1]
            shape, block = (bsz, r, t_len // r, d[2]), (1, r, tm // r, d[2])
            index = lambda i: (i // nt, 0, i % nt, 0)
        elif kind == "chunk":
            width = NSA_CMP_STRIDE * HEAD_DIM
            shape, block = (d[1], m // NSA_CMP_STRIDE, width), (d[1], tm // NSA_CMP_STRIDE, width)
            index = lambda i: (0, i, 0)
        else:
            tile = d[2]
            shape, block = (d[1], bsz, t_len // tile, V_ROWS, tile), (d[1], 1, tm // tile, V_ROWS, tile)
            index = lambda i: (0, i // nt, i % nt, 0, 0)
        out_shapes.append(jax.ShapeDtypeStruct(shape, dt))
        out_specs.append(pl.BlockSpec(block, index))
    return pl.pallas_call(
        functools.partial(_proj_kernel, plan=plan, n_out=len(out_defs)),
        grid=(m // tm,),
        in_specs=[
            pl.BlockSpec((tm, D_MODEL), lambda i: (i, 0)),
            pl.BlockSpec((1, D_MODEL), lambda i: (0, 0)),
            pl.BlockSpec(w.shape, lambda i: (0, 0)),
            pl.BlockSpec((tm, LANES), lambda i: (i % nt, 0)),
            pl.BlockSpec((tm, LANES), lambda i: (i % nt, 0)),
        ],
        out_specs=out_specs,
        out_shape=out_shapes,
        scratch_shapes=[pltpu.VMEM((tm, LANES), F32)],
        compiler_params=_params("parallel"),
    )(h, gain.reshape(1, D_MODEL), w, cos, sin)


def _banded_kernel(q_ref, kp_ref, kc_ref, vp_ref, vc_ref, *rest, n_kv, group, tl, pk, span, max_dist,
                   has_sink, gate_branch, want_lse):
    rest = list(rest)
    sink_ref = rest.pop(0) if has_sink else None
    gate_ref = rest.pop(0) if gate_branch is not None else None
    o_ref = rest.pop(0)
    lse_ref = rest.pop(0) if want_lse else None
    kcat, vcat, s_a, s_b, band = rest
    sb = BAND_SB
    n_seq = kc_ref.shape[1]
    for rr in range(n_seq):
        kcat[rr, 0:pk, :] = kp_ref[0, rr]
        kcat[rr, pk:, :] = kc_ref[0, rr]
        vcat[rr, 0:pk, :] = vp_ref[0, rr]
        vcat[rr, pk:, :] = vc_ref[0, rr]
    nk = span + sb
    kw = n_kv * HEAD_DIM
    merged = group == 1
    stacked = n_kv if merged else group
    rows = stacked * sb
    key = lax.broadcasted_iota(jnp.int32, (nk, rows), 0)
    qry = lax.broadcasted_iota(jnp.int32, (nk, rows), 1) & (sb - 1)
    dist = qry + span - key
    in_band = (dist >= 0) & (dist <= max_dist)
    lane_head = lax.broadcasted_iota(jnp.int32, (1, rows), 1) >> _log2(sb)
    q_head = lax.broadcasted_iota(jnp.int32, (sb, kw), 1) >> _log2(HEAD_DIM)
    v_head = lax.broadcasted_iota(jnp.int32, (nk, kw), 1) >> _log2(HEAD_DIM)
    assert merged or kw == LANES
    items = [(rr, j, h) for rr in range(n_seq) for j in range(tl // sb)
             for h in ([0] if merged else range(n_kv))]
    s_bufs = (s_a, s_b)

    def compute(first_tile):
        def scores(n):
            rr, j, h = items[n]
            lo = pk + j * sb - span
            qrows = slice(j * sb, (j + 1) * sb)
            if merged:
                qj = q_ref[0, rr, qrows, :].astype(F32)
                qs = jnp.concatenate([jnp.where(q_head == r, qj, 0.0).astype(BF16) for r in range(n_kv)], axis=0)
                kk = kcat[rr, lo:lo + nk, :]
            else:
                qs = jnp.concatenate([q_ref[h * group + g, 0, qrows, :] for g in range(group)], axis=0)
                kk = kcat[rr, lo:lo + nk, h * HEAD_DIM:(h + 1) * HEAD_DIM]
            variant = 1 + j if first_tile and j * sb < span else 0
            s_bufs[n % 2][...] = _dot_t(kk, qs) + band[variant]

        band[0] = jnp.where(in_band, 0.0, NEG_INF)
        if first_tile:
            for j in range(span // sb):
                band[1 + j] = jnp.where(in_band & (key >= span - j * sb), 0.0, NEG_INF)

        outs, lses = [], []
        scores(0)
        for n, (rr, j, h) in enumerate(items):
            if n + 1 < len(items):
                scores(n + 1)
            s = s_bufs[n % 2][...]
            lo = pk + j * sb - span
            qrows = slice(j * sb, (j + 1) * sb)
            m = jnp.max(s, axis=0, keepdims=True)
            if has_sink:
                sink = jnp.full((1, rows), sink_ref[h * group], F32)
                for g in range(1, group):
                    sink = jnp.where(lane_head == g, sink_ref[h * group + g], sink)
                sink = sink * LOG2E
                m = jnp.maximum(m, sink)
            p = jnp.exp2(s - m).astype(BF16)
            vwin = vcat[rr, lo:lo + nk, :]
            if merged:
                vv = jnp.concatenate([vwin, jnp.ones((nk, LANES), BF16)], axis=1)
                ones_row = kw
            else:
                vv = jnp.where(v_head == h, vwin.astype(F32), 1.0).astype(BF16)
                ones_row = (1 - h) * HEAD_DIM
            o_t = _dot_tn(vv, p)
            d = o_t[ones_row:ones_row + 1, :]
            if has_sink:
                d = d + jnp.exp2(sink - m)
            d = jnp.where(d > 0, d, 1.0)
            inv = 1.0 / d
            if want_lse:
                lse_t = jnp.broadcast_to(m * LN2 + jnp.log(d), (HEAD_DIM, rows))
            for r in range(stacked):
                cols = slice(r * sb, (r + 1) * sb)
                vrow = (r if merged else h) * HEAD_DIM
                o = (o_t[vrow:vrow + HEAD_DIM, cols] * inv[:, cols]).T
                if gate_ref is not None:
                    gc = r * 3 + gate_branch
                    o = o * gate_ref[h, 0, qrows, gc:gc + 1]
                outs.append(o)
                if want_lse:
                    lses.append(lse_t[:, cols].T)
            if merged or h == n_kv - 1:
                o_ref[0, rr, qrows, :] = jnp.concatenate(outs, axis=-1).astype(o_ref.dtype)
                if want_lse:
                    lse_ref[0, rr, qrows, :] = jnp.concatenate(lses, axis=-1)
                outs, lses = [], []

    first = pl.program_id(2) == 0
    pl.when(first)(functools.partial(compute, True))
    pl.when(jnp.logical_not(first))(functools.partial(compute, False))


def _banded(q, k, v, *, n_kv, group, max_dist, tl, pk, span, sink=None, gate=None, gate_branch=None,
            want_lse=False):
    bsz, n_res, length, kw = k.shape
    qw = n_kv * group * HEAD_DIM
    s_shape = (span + BAND_SB, (n_kv if group == 1 else group) * BAND_SB)
    ratio = tl // pk
    n_seq = min(n_res, max(1, BAND_TL // tl))
    prev = lambda b, i, l: (b, i, jnp.maximum(l * ratio - 1, 0), 0)
    cur = lambda b, i, l: (b, i, l, 0)
    if group == 1:
        q_spec = pl.BlockSpec((1, n_seq, tl, qw), cur)
    else:
        q_spec = pl.BlockSpec((n_kv * group, 1, tl, HEAD_DIM), lambda b, i, l: (0, b, l, 0))
    o_shape = (bsz, n_res, length, qw)
    in_specs = [
        q_spec,
        pl.BlockSpec((1, n_seq, pk, kw), prev),
        pl.BlockSpec((1, n_seq, tl, kw), cur),
        pl.BlockSpec((1, n_seq, pk, kw), prev),
        pl.BlockSpec((1, n_seq, tl, kw), cur),
    ]
    args = [q, k, k, v, v]
    if sink is not None:
        in_specs.append(pl.BlockSpec(memory_space=pltpu.SMEM))
        args.append(sink)
    if gate is not None:
        in_specs.append(pl.BlockSpec((n_kv, 1, tl, LANES), lambda b, i, l: (0, b, l, 0)))
        args.append(gate)
    out_spec = pl.BlockSpec((1, n_seq, tl, qw), cur)
    out_shape = [jax.ShapeDtypeStruct(o_shape, BF16)]
    out_specs = [out_spec]
    if want_lse:
        out_shape.append(jax.ShapeDtypeStruct(o_shape, F32))
        out_specs.append(out_spec)
    res = pl.pallas_call(
        functools.partial(_banded_kernel, n_kv=n_kv, group=group, tl=tl, pk=pk, span=span,
                          max_dist=max_dist, has_sink=sink is not None, gate_branch=gate_branch,
                          want_lse=want_lse),
        grid=(bsz, n_res // n_seq, length // tl),
        in_specs=in_specs,
        out_specs=out_specs,
        out_shape=out_shape,
        scratch_shapes=[pltpu.VMEM((n_seq, pk + tl, kw), BF16), pltpu.VMEM((n_seq, pk + tl, kw), BF16),
                        pltpu.VMEM(s_shape, F32), pltpu.VMEM(s_shape, F32),
                        pltpu.VMEM((1 + span // BAND_SB,) + s_shape, F32)],
        compiler_params=_params("parallel", "parallel", "parallel"),
    )(*args)
    return res if want_lse else res[0]


def _compress_one(z_ref, pe_ref, w1_ref, w2_ref, o_ref):
    half = NSA_CMP_STRIDE * HEAD_DIM
    z = z_ref[0, 0].astype(F32)
    za = (z + pe_ref[0, :, 0:half]).astype(BF16)
    zb = (z + pe_ref[0, :, half:2 * half]).astype(BF16)
    ya = _dot(za, w1_ref[0, 0:half, :])
    yb = _dot(zb, w1_ref[0, half:2 * half, :])
    nrow = ya.shape[0]
    hid = jax.nn.gelu(ya + pltpu.roll(yb, nrow - 1, 0))
    o_ref[0, 0] = _dot(hid.astype(BF16), w2_ref[0]).astype(o_ref.dtype)


def _compress_kernel(zk_ref, zv_ref, pek_ref, pev_ref, w1k_ref, w1v_ref, w2k_ref, w2v_ref, ok_ref, ov_ref):
    _compress_one(zk_ref, pek_ref, w1k_ref, w2k_ref, ok_ref)
    _compress_one(zv_ref, pev_ref, w1v_ref, w2v_ref, ov_ref)


def _compress(zk, zv, pe, w1, w2):
    hk, bsz, nch, width = zk.shape
    zspec = pl.BlockSpec((1, 1, nch, width), lambda h, b: (h, b, 0, 0))
    ospec = pl.BlockSpec((1, 1, nch, HEAD_DIM), lambda h, b: (h, b, 0, 0))
    oshape = jax.ShapeDtypeStruct((hk, bsz, nch, HEAD_DIM), BF16)

    def wspec(arr, which):
        return pl.BlockSpec((1,) + arr.shape[1:], lambda h, b: (which, 0, 0))

    return pl.pallas_call(
        _compress_kernel,
        grid=(hk, bsz),
        in_specs=[zspec, zspec, wspec(pe, 0), wspec(pe, 1), wspec(w1, 0), wspec(w1, 1), wspec(w2, 0), wspec(w2, 1)],
        out_specs=[ospec, ospec],
        out_shape=[oshape, oshape],
        compiler_params=_params("parallel", "parallel"),
    )(zk, zv, pe, pe, w1, w1, w2, w2)


def _topk_select(score, k, idxf):
    ncand = score.shape[0]
    rem = score
    sel = jnp.zeros_like(score)
    for _ in range(k):
        mx = jnp.max(rem, axis=0, keepdims=True)
        first = jnp.min(jnp.where(rem == mx, idxf, float(ncand)), axis=0, keepdims=True)
        pick = idxf == first
        sel = jnp.where(pick, jnp.where(mx > 0.5 * NEG_INF, 1.0, 0.0), sel)
        rem = jnp.where(pick, REMOVED, rem)
    return sel


def _cmp_kernel(q_ref, kc_ref, vc_ref, gate_ref, o_ref, bias_ref, *, tq, group):
    qi = pl.program_id(2)
    kc = kc_ref[0, 0]
    vc = vc_ref[0, 0]
    ncp = kc.shape[0]
    nsb = bias_ref.shape[-1]
    rows = group * tq
    last = lax.broadcasted_iota(jnp.int32, (ncp, 1), 0) * NSA_CMP_STRIDE + (NSA_CMP_LEN - 1)
    t_in_tile = lax.broadcasted_iota(jnp.int32, (1, rows), 1) & (tq - 1)
    visible = (last - t_in_tile) <= qi * tq
    qs = jnp.concatenate([q_ref[0, :, g * HEAD_DIM:(g + 1) * HEAD_DIM] for g in range(group)], axis=0)
    s = jnp.where(visible, _dot_t(kc, qs), NEG_INF)
    m = jnp.max(s, axis=0, keepdims=True)
    m = jnp.where(m > 0.5 * NEG_INF, m, 0.0)
    p = jnp.exp2(s - m)
    vc_ones = jnp.concatenate([vc, jnp.ones_like(vc)], axis=1)
    o_t = _dot_tn(vc_ones, p.astype(BF16))
    d = o_t[HEAD_DIM:HEAD_DIM + 1, :]
    d = jnp.where(d > 0, d, 1.0)
    inv = 1.0 / d
    pc = p * inv
    o_t = o_t[0:HEAD_DIM, :] * inv
    outs = []
    psum = pc[:, 0:tq]
    for g in range(group):
        if g:
            psum = psum + pc[:, g * tq:(g + 1) * tq]
        outs.append(o_t[:, g * tq:(g + 1) * tq].T * gate_ref[0, 0, :, g * 3:g * 3 + 1])
    o_ref[0] = jnp.concatenate(outs, axis=-1).astype(o_ref.dtype)

    jj = lax.broadcasted_iota(jnp.int32, (nsb, ncp), 0) * NSA_SEL_LEN
    nn = lax.broadcasted_iota(jnp.int32, (nsb, ncp), 1) * NSA_CMP_STRIDE
    ov = jnp.maximum(jnp.minimum(nn + NSA_CMP_LEN, jj + NSA_SEL_LEN) - jnp.maximum(nn, jj), 0)
    ov = (ov.astype(F32) / NSA_CMP_LEN).astype(BF16)
    p_hi = psum.astype(BF16)
    p_lo = (psum - p_hi.astype(F32)).astype(BF16)
    imp = _dot(ov, p_hi) + _dot(ov, p_lo)

    j = lax.broadcasted_iota(jnp.int32, (nsb, tq), 0)
    tt = qi * tq + lax.broadcasted_iota(jnp.int32, (nsb, tq), 1)
    cb = tt >> _log2(NSA_SEL_LEN)
    forced = (j == 0) | (j == cb) | (j == cb - 1)
    score = jnp.where(j <= cb, jnp.where(forced, FORCE_SCORE, imp), NEG_INF)
    sel = _topk_select(score, min(NSA_SEL_COUNT, nsb), j.astype(F32))
    bias_ref[0, 0] = jnp.where(sel > 0.5, 0.0, MASK_BIAS).T.astype(bias_ref.dtype)


def _cmp_attention(qu, kc, vc, gate, t_len):
    bsz = qu.shape[0]
    hk, _, ncp, _ = kc.shape
    group = B_Q_HEADS // B_KV_HEADS
    tq = CMP_TQ
    nsb = t_len // NSA_SEL_LEN
    gw = group * HEAD_DIM
    cspec = pl.BlockSpec((1, 1, ncp, HEAD_DIM), lambda b, h, i: (h, b, 0, 0))
    return pl.pallas_call(
        functools.partial(_cmp_kernel, tq=tq, group=group),
        grid=(bsz, hk, t_len // tq),
        in_specs=[
            pl.BlockSpec((1, tq, gw), lambda b, h, i: (b, i, h)),
            cspec, cspec,
            pl.BlockSpec((1, 1, tq, LANES), lambda b, h, i: (h, b, i, 0)),
        ],
        out_specs=[
            pl.BlockSpec((1, tq, gw), lambda b, h, i: (b, i, h)),
            pl.BlockSpec((1, 1, tq, nsb), lambda b, h, i: (h, b, i, 0)),
        ],
        out_shape=[
            jax.ShapeDtypeStruct((bsz, t_len, hk * gw), BF16),
            jax.ShapeDtypeStruct((hk, bsz, t_len, nsb), BF16),
        ],
        compiler_params=_params("parallel", "parallel", "parallel"),
    )(qu, kc, vc, gate)


def _moba_gate_kernel(q_ref, k_ref, bias_ref, km_hi, km_lo, *, tq):
    qi = pl.program_id(2)
    t_len = k_ref.shape[2]
    ncol = bias_ref.shape[-1]
    nb = t_len // MOBA_BLOCK

    @pl.when(qi == 0)
    def _():
        blk = lax.broadcasted_iota(jnp.int32, (ncol, t_len), 0)
        pos = lax.broadcasted_iota(jnp.int32, (ncol, t_len), 1)
        member = jnp.where((pos >> _log2(MOBA_BLOCK)) == blk, 1.0, 0.0).astype(BF16)
        kmean = _dot(member, k_ref[0, 0]) * (1.0 / MOBA_BLOCK)
        hi = kmean.astype(BF16)
        km_hi[...] = hi
        km_lo[...] = (kmean - hi.astype(F32)).astype(BF16)

    q = q_ref[0, 0]
    gate = (_dot_t(km_hi[...], q) + _dot_t(km_lo[...], q))[0:nb]
    j = lax.broadcasted_iota(jnp.int32, (nb, tq), 0)
    t = qi * tq + lax.broadcasted_iota(jnp.int32, (nb, tq), 1)
    cb = t >> _log2(MOBA_BLOCK)
    score = jnp.where(j < cb, gate, NEG_INF)
    sel = _topk_select(score, min(MOBA_TOPK, nb - 1), j.astype(F32))
    keep = (sel > 0.5) | (j == cb)
    bias = jnp.where(keep, 0.0, MASK_BIAS)
    bias = jnp.concatenate([bias, jnp.zeros((ncol - nb, tq), F32)], axis=0)
    bias_ref[0, 0] = bias.T.astype(bias_ref.dtype)


def _moba_gate(q, k):
    nh, bsz, t_len, _ = q.shape
    tq = min(GATE_TQ, t_len)
    return pl.pallas_call(
        functools.partial(_moba_gate_kernel, tq=tq),
        grid=(nh, bsz, t_len // tq),
        in_specs=[
            pl.BlockSpec((1, 1, tq, HEAD_DIM), lambda h, b, i: (h, b, i, 0)),
            pl.BlockSpec((1, 1, t_len, HEAD_DIM), lambda h, b, i: (h, b, 0, 0)),
        ],
        out_specs=pl.BlockSpec((1, 1, tq, HEAD_DIM), lambda h, b, i: (h, b, i, 0)),
        out_shape=jax.ShapeDtypeStruct((nh, bsz, t_len, HEAD_DIM), BF16),
        scratch_shapes=[pltpu.VMEM((HEAD_DIM, HEAD_DIM), BF16), pltpu.VMEM((HEAD_DIM, HEAD_DIM), BF16)],
        compiler_params=_params("parallel", "parallel", "arbitrary"),
    )(q, k)


def _flash_kernel(q_ref, bias_ref, k_ref, vt_ref, *rest, group, chains, lanes_out, tq, tk, blk_len, q_scale,
                  gate_branch):
    rest = list(rest)
    gate_ref = rest.pop(0) if gate_branch is not None else None
    o_ref, qa, ka, m_s, acc = rest[:5]
    s_bufs = rest[5:]
    qi = pl.program_id(2)
    rows = group * tq
    t_len = k_ref.shape[2]

    @pl.when(qi == 0)
    def _():
        kpos = lax.broadcasted_iota(jnp.int32, (t_len, HEAD_DIM), 0)
        kblk = lax.broadcasted_iota(jnp.int32, (t_len, HEAD_DIM), 1)
        onehot = jnp.where((kpos >> _log2(blk_len)) == kblk, 1.0, 0.0).astype(BF16)
        for c in range(chains):
            ka[c, :, 0:HEAD_DIM] = k_ref[c, 0]
            ka[c, :, HEAD_DIM:2 * HEAD_DIM] = onehot

    for c in range(chains):
        bias = bias_ref[c, 0]
        for g in range(group):
            qg = q_ref[c * group + g, 0]
            if q_scale != 1.0:
                qg = (qg.astype(F32) * q_scale).astype(BF16)
            qa[c, g * tq:(g + 1) * tq, 0:HEAD_DIM] = qg
            qa[c, g * tq:(g + 1) * tq, HEAD_DIM:2 * HEAD_DIM] = bias
    m_s[...] = jnp.full_like(m_s, NEG_INF)
    acc[...] = jnp.zeros_like(acc)

    def scores(ki, slot):
        start = pl.multiple_of(ki * tk, tk)
        for c in range(chains):
            s_bufs[2 * c + slot][...] = _dot_t(ka[c, pl.ds(start, tk), :], qa[c])

    def softmax_pv(ki, slot, masked):
        for c in range(chains):
            s = s_bufs[2 * c + slot][...]
            if masked:
                kpos = ki * tk + lax.broadcasted_iota(jnp.int32, (tk, rows), 0)
                qpos = qi * tq + (lax.broadcasted_iota(jnp.int32, (tk, rows), 1) & (tq - 1))
                s = jnp.where(kpos <= qpos, s, NEG_INF)
            m_old = m_s[c]
            m_new = jnp.maximum(m_old, jnp.max(s, axis=0, keepdims=True))
            alpha = jnp.exp2(m_old - m_new)
            p = jnp.exp2(s - m_new)
            acc[c] = alpha * acc[c] + _dot(vt_ref[c, 0, ki], p.astype(BF16))
            m_s[c] = m_new

    n_full = (qi * tq) >> _log2(tk)
    scores(0, 0)

    def pair(j, carry):
        k0 = 2 * j
        scores(k0 + 1, 1)
        softmax_pv(k0, 0, False)
        scores(k0 + 2, 0)
        softmax_pv(k0 + 1, 1, False)
        return carry

    lax.fori_loop(0, n_full >> 1, pair, 0)
    cur = (n_full >> 1) << 1

    @pl.when((n_full & 1) == 1)
    def _():
        scores(cur + 1, 1)
        softmax_pv(cur, 0, False)
        softmax_pv(cur + 1, 1, True)

    @pl.when((n_full & 1) == 0)
    def _():
        softmax_pv(cur, 0, True)

    outs = []
    for c in range(chains):
        d = acc[c, HEAD_DIM:HEAD_DIM + 1, :]
        d = jnp.where(d > 0, d, 1.0)
        o_t = acc[c, 0:HEAD_DIM, :] * (1.0 / d)
        for g in range(group):
            og = o_t[:, g * tq:(g + 1) * tq].T
            if gate_ref is not None:
                gc = g * 3 + gate_branch
                og = og * gate_ref[c, 0, :, gc:gc + 1]
            outs.append(og)
    if lanes_out:
        o_ref[0, 0] = jnp.concatenate(outs, axis=-1).astype(o_ref.dtype)
    else:
        for n, og in enumerate(outs):
            o_ref[n, 0] = og.astype(o_ref.dtype)


def _flash(q, bias, k, vt, *, group, tq, blk_len, q_scale, lanes_out, gate=None, gate_branch=None):
    nh, bsz, t_len, _ = k.shape
    chains = FLASH_CHAINS
    nt, tk = vt.shape[2], vt.shape[4]
    per_step = lambda h, b, i: (h, b, i, 0)
    whole_seq = lambda h, b, i: (h, b, 0, 0)
    in_specs = [
        pl.BlockSpec((chains * group, 1, tq, HEAD_DIM), per_step),
        pl.BlockSpec((chains, 1, tq, HEAD_DIM), per_step),
        pl.BlockSpec((chains, 1, t_len, HEAD_DIM), whole_seq),
        pl.BlockSpec((chains, 1, nt, V_ROWS, tk), lambda h, b, i: (h, b, 0, 0, 0)),
    ]
    args = [q, bias, k, vt]
    if gate is not None:
        in_specs.append(pl.BlockSpec((chains, 1, tq, LANES), per_step))
        args.append(gate)
    if lanes_out:
        out_shape = (1, bsz, t_len, nh * group * HEAD_DIM)
        out_spec = pl.BlockSpec((1, 1, tq, chains * group * HEAD_DIM), lambda h, b, i: (0, b, i, h))
    else:
        out_shape = (nh * group, bsz, t_len, HEAD_DIM)
        out_spec = pl.BlockSpec((chains * group, 1, tq, HEAD_DIM), per_step)
    rows = group * tq
    return pl.pallas_call(
        functools.partial(_flash_kernel, group=group, chains=chains, lanes_out=lanes_out, tq=tq, tk=tk,
                          blk_len=blk_len, q_scale=q_scale, gate_branch=gate_branch),
        grid=(nh // chains, bsz, t_len // tq),
        in_specs=in_specs,
        out_specs=out_spec,
        out_shape=jax.ShapeDtypeStruct(out_shape, BF16),
        scratch_shapes=[
            pltpu.VMEM((chains, rows, 2 * HEAD_DIM), BF16),
            pltpu.VMEM((chains, t_len, 2 * HEAD_DIM), BF16),
            pltpu.VMEM((chains, 1, rows), F32),
            pltpu.VMEM((chains, V_ROWS, rows), F32),
        ] + [pltpu.VMEM((tk, rows), F32)] * (2 * chains),
        compiler_params=_params("parallel", "parallel", "arbitrary"),
    )(*args)


def _even_delta(refs, scratch):
    oa_ref, oc_ref, os_ref, ow_ref, w_ref = refs
    na = oa_ref.shape[-1]
    ob = (oc_ref[...].astype(F32) + os_ref[...].astype(F32) + ow_ref[...].astype(F32)).astype(BF16)
    return _dot(oa_ref[...], w_ref[0:na, :]) + _dot(ob, w_ref[na:, :])


def _even_mix(oa, ocmp, osel, owin, w):
    aspec = pl.BlockSpec((FFN_TM, oa.shape[-1]), lambda i: (i, 0))
    return dict(args=[oa, ocmp, osel, owin, w], specs=[aspec] * 4 + [_resident(w.shape)], scratch=[],
                fn=_even_delta)


def _odd_delta(refs, scratch, *, dilations):
    ng = len(dilations)
    o_refs, l_refs = refs[:ng], refs[ng:2 * ng]
    od_ref, w_ref = refs[2 * ng:]
    scratch = list(scratch)
    tm = od_ref.shape[1]

    def tokens(ref, r):
        if r == 1:
            return ref[0, 0].astype(F32)
        pieces = []
        for c in range(ref.shape[-1] // LANES):
            scr = scratch.pop(0)
            for i in range(r):
                scr[pl.ds(i, tm // r, stride=r), :] = ref[0, i, :, c * LANES:(c + 1) * LANES].astype(F32)
            pieces.append(scr[...])
        return jnp.concatenate(pieces, axis=-1)

    outs = [tokens(ref, r) for ref, r in zip(o_refs, dilations)]
    lses = [tokens(ref, r) for ref, r in zip(l_refs, dilations)]
    mx = functools.reduce(jnp.maximum, lses)
    es = [jnp.exp(l - mx) for l in lses]
    tot = functools.reduce(lambda a, b: a + b, es)
    oc = functools.reduce(lambda a, b: a + b, [(e / tot) * o for e, o in zip(es, outs)])
    nc = oc.shape[-1]
    delta = _dot(oc.astype(BF16), w_ref[0:nc, :])
    for hh in range(od_ref.shape[0]):
        delta = delta + _dot(od_ref[hh], w_ref[nc + hh * HEAD_DIM: nc + (hh + 1) * HEAD_DIM, :])
    return delta


def _odd_mix(outs, lses, od, w, t_len):
    tm = FFN_TM
    nt = t_len // tm
    gw = outs[0].shape[-1]
    dilations = tuple(o.shape[1] for o in outs)
    gspecs = [pl.BlockSpec((1, r, tm // r, gw), lambda i: (i // nt, 0, i % nt, 0)) for r in dilations]
    n_scr = 2 * (gw // LANES) * sum(1 for r in dilations if r > 1)
    specs = gspecs + gspecs + [pl.BlockSpec((od.shape[0], tm, HEAD_DIM), lambda i: (0, i, 0)), _resident(w.shape)]
    return dict(args=[*outs, *lses, od, w], specs=specs, scratch=[pltpu.VMEM((tm, LANES), F32)] * n_scr,
                fn=functools.partial(_odd_delta, dilations=dilations))


def _col_ranges(sizes):
    offs, acc = [], 0
    for s in sizes:
        offs.append((acc, acc + s))
        acc += s
    return offs


def _even_mixer(h, gain, w_in, w_out, sinks, cmp_pe, cmp_w1, cmp_w2, cos, sin, bsz, t_len):
    m = bsz * t_len
    qa_w, kva_w = A_Q_HEADS * HEAD_DIM, A_KV_HEADS * HEAD_DIM
    qb_w, kvb_w = B_Q_HEADS * HEAD_DIM, B_KV_HEADS * HEAD_DIM
    sizes = [qa_w, kva_w, kva_w, qb_w] + [kvb_w] * 6 + [3 * B_Q_HEADS]
    (aq, ak, av, bq, bkc, bvc, bks, bvs, bkw, bvw, bg) = [w_in[:, a:b] for a, b in _col_ranges(sizes)]
    group = B_Q_HEADS // B_KV_HEADS
    gpad = jnp.zeros((D_MODEL, LANES - 3 * group), w_in.dtype)
    gates = [x for hk in range(B_KV_HEADS) for x in (bg[:, hk * 3 * group:(hk + 1) * 3 * group], gpad)]
    w = jnp.concatenate([aq, bq, ak, bkw, bks, av, bvw, bvs, bkc, bvc] + gates, axis=1).astype(BF16)
    c = [0]

    def take(width):
        c[0] += width
        return c[0] - width

    def out(idx, off, width, kind, **kw):
        return dict(idx=idx, off=off, width=width, kind=kind, **kw)

    plan = [
        (take(qa_w), qa_w, [out(0, 0, qa_w, "split", hw=HEAD_DIM, rope=True, scale=True)]),
        (take(qb_w), qb_w, [out(1, 0, qb_w, "split", hw=HEAD_DIM, rope=True, scale=True),
                            out(5, 0, qb_w, "flat", scale=True)]),
        (take(3 * kva_w), 3 * kva_w, [out(2, 0, kva_w, "flat", rope=True),
                                      out(3, kva_w, kvb_w, "flat", rope=True),
                                      out(4, kva_w + kvb_w, kvb_w, "split", hw=HEAD_DIM, rope=True)]),
        (take(5 * kvb_w), 5 * kvb_w, [out(6, 0, kva_w, "flat"), out(7, kva_w, kvb_w, "flat"),
                                      out(8, 2 * kvb_w, kvb_w, "vt", tile=FLASH_TK),
                                      out(9, 3 * kvb_w, kvb_w, "chunk"),
                                      out(10, 4 * kvb_w, kvb_w, "chunk")]),
        (take(B_KV_HEADS * LANES), B_KV_HEADS * LANES,
         [out(11, 0, B_KV_HEADS * LANES, "split", hw=LANES, sigmoid=True)]),
    ]
    out_defs = [("split", A_Q_HEADS, HEAD_DIM, BF16), ("split", B_Q_HEADS, HEAD_DIM, BF16), ("flat", kva_w, BF16),
                ("flat", kvb_w, BF16),
                ("split", B_KV_HEADS, HEAD_DIM, BF16), ("flat", qb_w, BF16), ("flat", kva_w, BF16),
                ("flat", kvb_w, BF16), ("vt", B_KV_HEADS, FLASH_TK, BF16), ("chunk", B_KV_HEADS, BF16),
                ("chunk", B_KV_HEADS, BF16), ("split", B_KV_HEADS, LANES, F32)]
    (aq_r, bq_r, ak_r, bkw_r, bks_r, bq_u, av_, bvw_, bvs_t, zk, zv, gate) = _proj(
        h, gain, w, cos, sin, plan, out_defs, bsz, t_len)

    def seq(x):
        return x.reshape(bsz, 1, t_len, x.shape[-1])

    def heads(x):
        return x.reshape(x.shape[0], bsz, t_len, x.shape[-1])

    gate4 = heads(gate)
    oa = _banded(heads(aq_r), seq(ak_r), seq(av_), n_kv=A_KV_HEADS, group=A_Q_HEADS // A_KV_HEADS,
                 max_dist=A_WINDOW - 1, tl=BAND_TL, pk=128, span=128, sink=sinks)
    nch = t_len // NSA_CMP_STRIDE
    zshape = (B_KV_HEADS, bsz, nch, NSA_CMP_STRIDE * HEAD_DIM)
    kc, vc = _compress(zk.reshape(zshape), zv.reshape(zshape), cmp_pe.reshape(2, 1, NSA_CMP_LEN * HEAD_DIM),
                       cmp_w1.astype(BF16), cmp_w2.astype(BF16))
    ocmp, bias = _cmp_attention(bq_u.reshape(bsz, t_len, qb_w), kc, vc, gate4, t_len)
    osel = _flash(heads(bq_r), bias, heads(bks_r), bvs_t, group=group, tq=FLASH_T, blk_len=NSA_SEL_LEN,
                  q_scale=1.0, lanes_out=True, gate=gate4, gate_branch=1)
    owin = _banded(heads(bq_r), seq(bkw_r), seq(bvw_), n_kv=B_KV_HEADS, group=group,
                   max_dist=NSA_WINDOW - 1, tl=BAND_TL, pk=512, span=512, gate=gate4, gate_branch=2)
    return _even_mix(oa.reshape(m, qa_w), ocmp.reshape(m, qb_w), osel.reshape(m, qb_w),
                     owin.reshape(m, qb_w), w_out.astype(BF16))


def _odd_mixer(h, gain, w_in, w_out, cos, sin, bsz, t_len):
    m = bsz * t_len
    cw = C_HEADS * HEAD_DIM
    dw = D_HEADS * HEAD_DIM
    gw = C_HEADS_PER_GROUP * HEAD_DIM
    n_groups = len(C_GROUPS)

    def group_outs(base, **kw):
        return [dict(idx=base + gi, off=gi * gw, width=gw, kind="dilate", r=r, **kw)
                for gi, (_, r) in enumerate(C_GROUPS)]

    plan = [
        (0, cw, group_outs(0, rope=True, scale=True)),
        (cw, cw, group_outs(n_groups, rope=True)),
        (2 * cw, cw, group_outs(2 * n_groups)),
        (3 * cw, 3 * dw, [dict(idx=3 * n_groups, off=0, width=dw, kind="split", hw=HEAD_DIM, rope=True),
                          dict(idx=3 * n_groups + 1, off=dw, width=dw, kind="split", hw=HEAD_DIM, rope=True),
                          dict(idx=3 * n_groups + 2, off=2 * dw, width=dw, kind="vt", tile=MOBA_TK)]),
    ]
    out_defs = [("dilate", r, gw, BF16) for _ in range(3) for _, r in C_GROUPS]
    out_defs += [("split", D_HEADS, HEAD_DIM, BF16), ("split", D_HEADS, HEAD_DIM, BF16),
                 ("vt", D_HEADS, MOBA_TK, BF16)]
    res = _proj(h, gain, w_in.astype(BF16), cos, sin, plan, out_defs, bsz, t_len)
    cq, ck, cv = res[0:n_groups], res[n_groups:2 * n_groups], res[2 * n_groups:3 * n_groups]
    dq, dk, dv_t = res[3 * n_groups:]
    outs, lses = [], []
    for gi, (wlen, r) in enumerate(C_GROUPS):
        o, lse = _banded(cq[gi], ck[gi], cv[gi], n_kv=C_HEADS_PER_GROUP, group=1, max_dist=wlen // r,
                         tl=min(BAND_TL, t_len // r), pk=128, span=128, want_lse=True)
        outs.append(o)
        lses.append(lse)

    def heads(x):
        return x.reshape(x.shape[0], bsz, t_len, x.shape[-1])

    bias = _moba_gate(heads(dq), heads(dk))
    od = _flash(heads(dq), bias, heads(dk), dv_t, group=1, tq=MOBA_T, blk_len=MOBA_BLOCK, q_scale=QK_SCALE,
                lanes_out=False)
    return _odd_mix(outs, lses, od.reshape(D_HEADS, m, HEAD_DIM), w_out.astype(BF16), t_len)


def kernel(x, ffn_norm_pre, mix_norm, ffn_norm_post, ffn_wi, ffn_wo, even_w_in, even_w_out, even_sinks,
           nsa_cmp_pe, nsa_cmp_w1, nsa_cmp_w2, odd_w_in, odd_w_out, final_norm):
    bsz, t_len, _ = x.shape
    depth = ffn_wi.shape[0]
    cos, sin = _rope_tables(t_len)
    wi = ffn_wi.astype(BF16)
    wo = ffn_wo.astype(BF16)
    h = x.reshape(bsz * t_len, D_MODEL)
    for layer in range(depth):
        i = layer // 2
        h = _ffn(h, ffn_norm_pre[layer], wi[layer, 0], wo[layer, 0])
        if layer % 2 == 0:
            mix = _even_mixer(h, mix_norm[layer], even_w_in[i], even_w_out[i], even_sinks[i], nsa_cmp_pe[i],
                              nsa_cmp_w1[i], nsa_cmp_w2[i], cos, sin, bsz, t_len)
        else:
            mix = _odd_mixer(h, mix_norm[layer], odd_w_in[i], odd_w_out[i], cos, sin, bsz, t_len)
        last = layer == depth - 1
        h = _ffn(h, ffn_norm_post[layer], wi[layer, 1], wo[layer, 1], final_norm if last else None, mix=mix)
    return h.reshape(bsz, t_len, D_MODEL)
```

```python
import functools

import jax
import jax.numpy as jnp
from jax import lax
from jax.experimental import pallas as pl
from jax.experimental.pallas import tpu as pltpu

D_MODEL = 1024
HEAD_DIM = 64
ROPE_THETA = 10000.0
NORM_EPS = 1e-6
D_FF = 2816
NEG_INF = -1e30
FORCE_SCORE = 1e4

A_Q_HEADS = 8
A_KV_HEADS = 2
A_WINDOW = 128
B_Q_HEADS = 8
B_KV_HEADS = 2
NSA_CMP_LEN = 32
NSA_CMP_STRIDE = 16
NSA_CMP_HIDDEN = 256
NSA_SEL_LEN = 64
NSA_SEL_COUNT = 8
NSA_WINDOW = 512
C_GROUPS = ((128, 1), (512, 4), (2048, 16))
C_HEADS_PER_GROUP = 4
C_HEADS = len(C_GROUPS) * C_HEADS_PER_GROUP
D_HEADS = 4
MOBA_BLOCK = 256
MOBA_TOPK = 3

LANES = 128
LOG2E = 1.4426950408889634
LN2 = 0.6931471805599453
QK_SCALE = HEAD_DIM ** -0.5 * LOG2E
MASK_BIAS = -32768.0
V_ROWS = HEAD_DIM + 16
REMOVED = -3e38
VMEM_LIMIT = 52 * 1024 * 1024

FFN_TM = 512
FFN_TF = 256
PROJ_TM = 512
BAND_TL = 512
BAND_SB = 128
FLASH_T = 256
FLASH_TK = 256
FLASH_CHAINS = 4
MOBA_T = 512
MOBA_TK = 512
GATE_TQ = 4096
CMP_TQ = 1024

BF16 = jnp.bfloat16
F32 = jnp.float32


def _params(*sem):
    return pltpu.CompilerParams(dimension_semantics=sem, vmem_limit_bytes=VMEM_LIMIT)


def _rms(x, g):
    return x * lax.rsqrt(jnp.mean(x * x, axis=-1, keepdims=True) + NORM_EPS) * g


def _log2(n):
    assert n & (n - 1) == 0
    return n.bit_length() - 1


def _dot(a, b):
    return jnp.dot(a, b, preferred_element_type=F32)


def _dot_t(a, b):
    return lax.dot_general(a, b, (((1,), (1,)), ((), ())), preferred_element_type=F32)


def _dot_tn(a, b):
    return lax.dot_general(a, b, (((0,), (0,)), ((), ())), preferred_element_type=F32)


def _ffn_kernel(x_ref, g_ref, wi_ref, wo_ref, *rest, final, mix_fn, n_mix):
    rest = list(rest)
    fg_ref = rest.pop(0) if final else None
    mix_refs = [rest.pop(0) for _ in range(n_mix)]
    o_ref, act_scr = rest.pop(0), rest.pop(0)
    x = x_ref[...]
    if mix_fn is not None:
        x = x + mix_fn(mix_refs, rest)
    o_ref[...] = x
    n = _rms(x, g_ref[...]).astype(BF16)
    for c in range(D_FF // FFN_TF):
        cols = slice(c * FFN_TF, (c + 1) * FFN_TF)
        gate = _dot(n, wi_ref[:, cols])
        up = _dot(n, wi_ref[:, D_FF + c * FFN_TF:D_FF + (c + 1) * FFN_TF])
        act_scr[:, cols] = (gate * jax.nn.sigmoid(gate) * up).astype(BF16)
    h = o_ref[...] + 0.5 * _dot(act_scr[...], wo_ref[...])
    if final:
        h = _rms(h, fg_ref[...])
    o_ref[...] = h


def _resident(shape, lead=()):
    block = (None,) * len(lead) + tuple(shape[len(lead):])
    index = tuple(lead) + (0,) * (len(shape) - len(lead))
    return pl.BlockSpec(block, lambda i: index, pipeline_mode=pl.Buffered(1))


def _ffn(h, gain, wi, wo, which, final_gain=None, mix=None):
    m = h.shape[0]
    tm = FFN_TM
    final = final_gain is not None
    in_specs = [
        pl.BlockSpec((tm, D_MODEL), lambda i: (i, 0)),
        pl.BlockSpec((1, D_MODEL), lambda i: (0, 0)),
        _resident(wi.shape, which),
        _resident(wo.shape, which),
    ]
    args = [h, gain.reshape(1, D_MODEL), wi, wo]
    if final:
        in_specs.append(pl.BlockSpec((1, D_MODEL), lambda i: (0, 0)))
        args.append(final_gain.reshape(1, D_MODEL))
    scratch = [pltpu.VMEM((tm, D_FF), BF16)]
    if mix is not None:
        in_specs += mix["specs"]
        args += mix["args"]
        scratch += mix["scratch"]
    return pl.pallas_call(
        functools.partial(_ffn_kernel, final=final, mix_fn=mix and mix["fn"], n_mix=len(mix["args"]) if mix else 0),
        grid=(m // tm,),
        in_specs=in_specs,
        out_specs=pl.BlockSpec((tm, D_MODEL), lambda i: (i, 0)),
        out_shape=jax.ShapeDtypeStruct((m, D_MODEL), F32),
        scratch_shapes=scratch,
        compiler_params=_params("parallel"),
    )(*args)


def _rope_tables(t):
    inv = 1.0 / (ROPE_THETA ** (jnp.arange(0, HEAD_DIM, 2, dtype=F32) / HEAD_DIM))
    ang = jnp.arange(t, dtype=F32)[:, None] * inv[None, :]
    cos = jnp.cos(ang)
    sin = jnp.sin(ang)
    return (jnp.concatenate([cos, cos, cos, cos], axis=-1),
            jnp.concatenate([-sin, sin, -sin, sin], axis=-1))


def _proj_kernel(x_ref, g_ref, w_ref, cos_ref, sin_ref, *rest, plan, n_out):
    out_refs, (ys,) = rest[:n_out], rest[n_out:]
    n = _rms(x_ref[...], g_ref[...]).astype(BF16)
    tm = n.shape[0]
    cos = cos_ref[...]
    sin = sin_ref[...]
    lane = lax.broadcasted_iota(jnp.int32, cos.shape, 1)
    first_half = (lane & (HEAD_DIM - 1)) < HEAD_DIM // 2
    heads_per_tile = LANES // HEAD_DIM
    for off, width, outs in plan:
        y = _dot(n, w_ref[:, off:off + width])
        for o in outs:
            o_ref = out_refs[o["idx"]]
            dt = o_ref.dtype
            kind = o["kind"]
            for c in range(o["width"] // LANES):
                piece = y[:, o["off"] + c * LANES: o["off"] + (c + 1) * LANES]
                if o.get("rope"):
                    rot = jnp.where(first_half, pltpu.roll(piece, LANES - HEAD_DIM // 2, 1),
                                    pltpu.roll(piece, HEAD_DIM // 2, 1))
                    piece = piece * cos + rot * sin
                if o.get("scale"):
                    piece = piece * QK_SCALE
                if o.get("sigmoid"):
                    piece = jax.nn.sigmoid(piece)
                lanes = slice(c * LANES, (c + 1) * LANES)
                if kind == "flat":
                    o_ref[:, lanes] = piece.astype(dt)
                elif kind == "split":
                    hw = o["hw"]
                    per = LANES // hw
                    for k in range(per):
                        o_ref[c * per + k] = piece[:, k * hw:(k + 1) * hw].astype(dt)
                elif kind == "dilate":
                    r = o["r"]
                    if r == 1:
                        o_ref[0, 0, :, lanes] = piece.astype(dt)
                    else:
                        ys[...] = piece
                        for i in range(r):
                            o_ref[0, i, :, lanes] = ys[pl.ds(i, tm // r, stride=r), :].astype(dt)
                elif kind == "chunk":
                    ys[...] = piece
                    for l in range(NSA_CMP_STRIDE):
                        rows = ys[pl.ds(l, tm // NSA_CMP_STRIDE, stride=NSA_CMP_STRIDE), :].astype(dt)
                        for k in range(heads_per_tile):
                            o_ref[c * heads_per_tile + k, :, l * HEAD_DIM:(l + 1) * HEAD_DIM] = rows[
                                :, k * HEAD_DIM:(k + 1) * HEAD_DIM]
                else:
                    tile = o["tile"]
                    pt = piece.T
                    for k in range(heads_per_tile):
                        for kt in range(tm // tile):
                            o_ref[c * heads_per_tile + k, 0, kt, 0:HEAD_DIM, :] = pt[
                                k * HEAD_DIM:(k + 1) * HEAD_DIM, kt * tile:(kt + 1) * tile].astype(dt)
                            o_ref[c * heads_per_tile + k, 0, kt, HEAD_DIM:V_ROWS, :] = jnp.ones(
                                (V_ROWS - HEAD_DIM, tile), dt)


def _proj(h, gain, w, cos, sin, plan, out_defs, bsz, t_len):
    m = h.shape[0]
    tm = PROJ_TM
    nt = t_len // tm
    out_shapes, out_specs = [], []
    for d in out_defs:
        kind, dt = d[0], d[-1]
        if kind == "flat":
            shape, block, index = (m, d[1]), (tm, d[1]), (lambda i: (i, 0))
        elif kind == "split":
            shape, block, index = (d[1], m, d[2]), (d[1], tm, d[2]), (lambda i: (0, i, 0))
        elif kind == "dilate":
            r = d[1]
            shape, block = (bsz, r, t_len // r, d[2]), (1, r, tm // r, d[2])
            index = lambda i: (i // nt, 0, i % nt, 0)
        elif kind == "chunk":
            width = NSA_CMP_STRIDE * HEAD_DIM
            shape, block = (d[1], m // NSA_CMP_STRIDE, width), (d[1], tm // NSA_CMP_STRIDE, width)
            index = lambda i: (0, i, 0)
        else:
            tile = d[2]
            shape, block = (d[1], bsz, t_len // tile, V_ROWS, tile), (d[1], 1, tm // tile, V_ROWS, tile)
            index = lambda i: (0, i // nt, i % nt, 0, 0)
        out_shapes.append(jax.ShapeDtypeStruct(shape, dt))
        out_specs.append(pl.BlockSpec(block, index))
    return pl.pallas_call(
        functools.partial(_proj_kernel, plan=plan, n_out=len(out_defs)),
        grid=(m // tm,),
        in_specs=[
            pl.BlockSpec((tm, D_MODEL), lambda i: (i, 0)),
            pl.BlockSpec((1, D_MODEL), lambda i: (0, 0)),
            pl.BlockSpec(w.shape, lambda i: (0, 0)),
            pl.BlockSpec((tm, LANES), lambda i: (i % nt, 0)),
            pl.BlockSpec((tm, LANES), lambda i: (i % nt, 0)),
        ],
        out_specs=out_specs,
        out_shape=out_shapes,
        scratch_shapes=[pltpu.VMEM((tm, LANES), F32)],
        compiler_params=_params("parallel"),
    )(h, gain.reshape(1, D_MODEL), w, cos, sin)


def _banded_kernel(q_ref, kp_ref, kc_ref, vp_ref, vc_ref, *rest, n_kv, group, tl, pk, span, max_dist,
                   has_sink, gate_branch, want_lse):
    rest = list(rest)
    sink_ref = rest.pop(0) if has_sink else None
    gate_ref = rest.pop(0) if gate_branch is not None else None
    o_ref = rest.pop(0)
    lse_ref = rest.pop(0) if want_lse else None
    kcat, vcat, s_a, s_b, band = rest
    sb = BAND_SB
    n_seq = kc_ref.shape[1]
    for rr in range(n_seq):
        kcat[rr, 0:pk, :] = kp_ref[0, rr]
        kcat[rr, pk:, :] = kc_ref[0, rr]
        vcat[rr, 0:pk, :] = vp_ref[0, rr]
        vcat[rr, pk:, :] = vc_ref[0, rr]
    nk = span + sb
    kw = n_kv * HEAD_DIM
    merged = group == 1
    stacked = n_kv if merged else group
    rows = stacked * sb
    key = lax.broadcasted_iota(jnp.int32, (nk, rows), 0)
    qry = lax.broadcasted_iota(jnp.int32, (nk, rows), 1) & (sb - 1)
    dist = qry + span - key
    in_band = (dist >= 0) & (dist <= max_dist)
    lane_head = lax.broadcasted_iota(jnp.int32, (1, rows), 1) >> _log2(sb)
    q_head = lax.broadcasted_iota(jnp.int32, (sb, kw), 1) >> _log2(HEAD_DIM)
    v_head = lax.broadcasted_iota(jnp.int32, (nk, kw), 1) >> _log2(HEAD_DIM)
    assert merged or kw == LANES
    items = [(rr, j, h) for rr in range(n_seq) for j in range(tl // sb)
             for h in ([0] if merged else range(n_kv))]
    s_bufs = (s_a, s_b)

    def compute(first_tile):
        def scores(n):
            rr, j, h = items[n]
            lo = pk + j * sb - span
            qrows = slice(j * sb, (j + 1) * sb)
            if merged:
                qj = q_ref[0, rr, qrows, :].astype(F32)
                qs = jnp.concatenate([jnp.where(q_head == r, qj, 0.0).astype(BF16) for r in range(n_kv)], axis=0)
                kk = kcat[rr, lo:lo + nk, :]
            else:
                qs = jnp.concatenate([q_ref[h * group + g, 0, qrows, :] for g in range(group)], axis=0)
                kk = kcat[rr, lo:lo + nk, h * HEAD_DIM:(h + 1) * HEAD_DIM]
            variant = 1 + j if first_tile and j * sb < span else 0
            s_bufs[n % 2][...] = _dot_t(kk, qs) + band[variant]

        band[0] = jnp.where(in_band, 0.0, NEG_INF)
        if first_tile:
            for j in range(span // sb):
                band[1 + j] = jnp.where(in_band & (key >= span - j * sb), 0.0, NEG_INF)

        outs, lses = [], []
        scores(0)
        for n, (rr, j, h) in enumerate(items):
            if n + 1 < len(items):
                scores(n + 1)
            s = s_bufs[n % 2][...]
            lo = pk + j * sb - span
            qrows = slice(j * sb, (j + 1) * sb)
            m = jnp.max(s, axis=0, keepdims=True)
            if has_sink:
                sink = jnp.full((1, rows), sink_ref[h * group], F32)
                for g in range(1, group):
                    sink = jnp.where(lane_head == g, sink_ref[h * group + g], sink)
                sink = sink * LOG2E
                m = jnp.maximum(m, sink)
            p = jnp.exp2(s - m).astype(BF16)
            vwin = vcat[rr, lo:lo + nk, :]
            if merged:
                vv = jnp.concatenate([vwin, jnp.ones((nk, LANES), BF16)], axis=1)
                ones_row = kw
            else:
                vv = jnp.where(v_head == h, vwin.astype(F32), 1.0).astype(BF16)
                ones_row = (1 - h) * HEAD_DIM
            o_t = _dot_tn(vv, p)
            d = o_t[ones_row:ones_row + 1, :]
            if has_sink:
                d = d + jnp.exp2(sink - m)
            d = jnp.where(d > 0, d, 1.0)
            inv = 1.0 / d
            if want_lse:
                lse_t = jnp.broadcast_to(m * LN2 + jnp.log(d), (HEAD_DIM, rows))
            for r in range(stacked):
                cols = slice(r * sb, (r + 1) * sb)
                vrow = (r if merged else h) * HEAD_DIM
                o = (o_t[vrow:vrow + HEAD_DIM, cols] * inv[:, cols]).T
                if gate_ref is not None:
                    gc = r * 3 + gate_branch
                    o = o * gate_ref[h, 0, qrows, gc:gc + 1]
                outs.append(o)
                if want_lse:
                    lses.append(lse_t[:, cols].T)
            if merged or h == n_kv - 1:
                o_ref[0, rr, qrows, :] = jnp.concatenate(outs, axis=-1).astype(o_ref.dtype)
                if want_lse:
                    lse_ref[0, rr, qrows, :] = jnp.concatenate(lses, axis=-1)
                outs, lses = [], []

    first = pl.program_id(2) == 0
    pl.when(first)(functools.partial(compute, True))
    pl.when(jnp.logical_not(first))(functools.partial(compute, False))


def _banded(q, k, v, *, n_kv, group, max_dist, tl, pk, span, sink=None, gate=None, gate_branch=None,
            want_lse=False):
    bsz, n_res, length, kw = k.shape
    qw = n_kv * group * HEAD_DIM
    s_shape = (span + BAND_SB, (n_kv if group == 1 else group) * BAND_SB)
    ratio = tl // pk
    n_seq = min(n_res, max(1, BAND_TL // tl))
    prev = lambda b, i, l: (b, i, jnp.maximum(l * ratio - 1, 0), 0)
    cur = lambda b, i, l: (b, i, l, 0)
    if group == 1:
        q_spec = pl.BlockSpec((1, n_seq, tl, qw), cur)
    else:
        q_spec = pl.BlockSpec((n_kv * group, 1, tl, HEAD_DIM), lambda b, i, l: (0, b, l, 0))
    o_shape = (bsz, n_res, length, qw)
    in_specs = [
        q_spec,
        pl.BlockSpec((1, n_seq, pk, kw), prev),
        pl.BlockSpec((1, n_seq, tl, kw), cur),
        pl.BlockSpec((1, n_seq, pk, kw), prev),
        pl.BlockSpec((1, n_seq, tl, kw), cur),
    ]
    args = [q, k, k, v, v]
    if sink is not None:
        in_specs.append(pl.BlockSpec(memory_space=pltpu.SMEM))
        args.append(sink)
    if gate is not None:
        in_specs.append(pl.BlockSpec((n_kv, 1, tl, LANES), lambda b, i, l: (0, b, l, 0)))
        args.append(gate)
    out_spec = pl.BlockSpec((1, n_seq, tl, qw), cur)
    out_shape = [jax.ShapeDtypeStruct(o_shape, BF16)]
    out_specs = [out_spec]
    if want_lse:
        out_shape.append(jax.ShapeDtypeStruct(o_shape, F32))
        out_specs.append(out_spec)
    res = pl.pallas_call(
        functools.partial(_banded_kernel, n_kv=n_kv, group=group, tl=tl, pk=pk, span=span,
                          max_dist=max_dist, has_sink=sink is not None, gate_branch=gate_branch,
                          want_lse=want_lse),
        grid=(bsz, n_res // n_seq, length // tl),
        in_specs=in_specs,
        out_specs=out_specs,
        out_shape=out_shape,
        scratch_shapes=[pltpu.VMEM((n_seq, pk + tl, kw), BF16), pltpu.VMEM((n_seq, pk + tl, kw), BF16),
                        pltpu.VMEM(s_shape, F32), pltpu.VMEM(s_shape, F32),
                        pltpu.VMEM((1 + span // BAND_SB,) + s_shape, F32)],
        compiler_params=_params("parallel", "parallel", "parallel"),
    )(*args)
    return res if want_lse else res[0]


def _compress_one(z_ref, pe_ref, w1_ref, w2_ref, o_ref):
    half = NSA_CMP_STRIDE * HEAD_DIM
    z = z_ref[0, 0].astype(F32)
    za = (z + pe_ref[0, :, 0:half]).astype(BF16)
    zb = (z + pe_ref[0, :, half:2 * half]).astype(BF16)
    ya = _dot(za, w1_ref[0, 0:half, :])
    yb = _dot(zb, w1_ref[0, half:2 * half, :])
    nrow = ya.shape[0]
    hid = jax.nn.gelu(ya + pltpu.roll(yb, nrow - 1, 0))
    o_ref[0, 0] = _dot(hid.astype(BF16), w2_ref[0]).astype(o_ref.dtype)


def _compress_kernel(zk_ref, zv_ref, pek_ref, pev_ref, w1k_ref, w1v_ref, w2k_ref, w2v_ref, ok_ref, ov_ref):
    _compress_one(zk_ref, pek_ref, w1k_ref, w2k_ref, ok_ref)
    _compress_one(zv_ref, pev_ref, w1v_ref, w2v_ref, ov_ref)


def _compress(zk, zv, pe, w1, w2):
    hk, bsz, nch, width = zk.shape
    zspec = pl.BlockSpec((1, 1, nch, width), lambda h, b: (h, b, 0, 0))
    ospec = pl.BlockSpec((1, 1, nch, HEAD_DIM), lambda h, b: (h, b, 0, 0))
    oshape = jax.ShapeDtypeStruct((hk, bsz, nch, HEAD_DIM), BF16)

    def wspec(arr, which):
        return pl.BlockSpec((1,) + arr.shape[1:], lambda h, b: (which, 0, 0))

    return pl.pallas_call(
        _compress_kernel,
        grid=(hk, bsz),
        in_specs=[zspec, zspec, wspec(pe, 0), wspec(pe, 1), wspec(w1, 0), wspec(w1, 1), wspec(w2, 0), wspec(w2, 1)],
        out_specs=[ospec, ospec],
        out_shape=[oshape, oshape],
        compiler_params=_params("parallel", "parallel"),
    )(zk, zv, pe, pe, w1, w1, w2, w2)


def _topk_select(score, k, idxf):
    ncand = score.shape[0]
    rem = score
    sel = jnp.zeros_like(score)
    for _ in range(k):
        mx = jnp.max(rem, axis=0, keepdims=True)
        first = jnp.min(jnp.where(rem == mx, idxf, float(ncand)), axis=0, keepdims=True)
        pick = idxf == first
        sel = jnp.where(pick, jnp.where(mx > 0.5 * NEG_INF, 1.0, 0.0), sel)
        rem = jnp.where(pick, REMOVED, rem)
    return sel


def _cmp_kernel(q_ref, kc_ref, vc_ref, gate_ref, o_ref, bias_ref, *, tq, group):
    qi = pl.program_id(2)
    kc = kc_ref[0, 0]
    vc = vc_ref[0, 0]
    ncp = kc.shape[0]
    nsb = bias_ref.shape[-1]
    rows = group * tq
    last = lax.broadcasted_iota(jnp.int32, (ncp, 1), 0) * NSA_CMP_STRIDE + (NSA_CMP_LEN - 1)
    t_in_tile = lax.broadcasted_iota(jnp.int32, (1, rows), 1) & (tq - 1)
    visible = (last - t_in_tile) <= qi * tq
    qs = jnp.concatenate([q_ref[0, :, g * HEAD_DIM:(g + 1) * HEAD_DIM] for g in range(group)], axis=0)
    s = jnp.where(visible, _dot_t(kc, qs), NEG_INF)
    m = jnp.max(s, axis=0, keepdims=True)
    m = jnp.where(m > 0.5 * NEG_INF, m, 0.0)
    p = jnp.exp2(s - m)
    vc_ones = jnp.concatenate([vc, jnp.ones_like(vc)], axis=1)
    o_t = _dot_tn(vc_ones, p.astype(BF16))
    d = o_t[HEAD_DIM:HEAD_DIM + 1, :]
    d = jnp.where(d > 0, d, 1.0)
    inv = 1.0 / d
    pc = p * inv
    o_t = o_t[0:HEAD_DIM, :] * inv
    outs = []
    psum = pc[:, 0:tq]
    for g in range(group):
        if g:
            psum = psum + pc[:, g * tq:(g + 1) * tq]
        outs.append(o_t[:, g * tq:(g + 1) * tq].T * gate_ref[0, 0, :, g * 3:g * 3 + 1])
    o_ref[0] = jnp.concatenate(outs, axis=-1).astype(o_ref.dtype)

    jj = lax.broadcasted_iota(jnp.int32, (nsb, ncp), 0) * NSA_SEL_LEN
    nn = lax.broadcasted_iota(jnp.int32, (nsb, ncp), 1) * NSA_CMP_STRIDE
    ov = jnp.maximum(jnp.minimum(nn + NSA_CMP_LEN, jj + NSA_SEL_LEN) - jnp.maximum(nn, jj), 0)
    ov = (ov.astype(F32) / NSA_CMP_LEN).astype(BF16)
    p_hi = psum.astype(BF16)
    p_lo = (psum - p_hi.astype(F32)).astype(BF16)
    imp = _dot(ov, p_hi) + _dot(ov, p_lo)

    j = lax.broadcasted_iota(jnp.int32, (nsb, tq), 0)
    tt = qi * tq + lax.broadcasted_iota(jnp.int32, (nsb, tq), 1)
    cb = tt >> _log2(NSA_SEL_LEN)
    forced = (j == 0) | (j == cb) | (j == cb - 1)
    score = jnp.where(j <= cb, jnp.where(forced, FORCE_SCORE, imp), NEG_INF)
    sel = _topk_select(score, min(NSA_SEL_COUNT, nsb), j.astype(F32))
    bias_ref[0, 0] = jnp.where(sel > 0.5, 0.0, MASK_BIAS).T.astype(bias_ref.dtype)


def _cmp_attention(qu, kc, vc, gate, t_len):
    bsz = qu.shape[0]
    hk, _, ncp, _ = kc.shape
    group = B_Q_HEADS // B_KV_HEADS
    tq = CMP_TQ
    nsb = t_len // NSA_SEL_LEN
    gw = group * HEAD_DIM
    cspec = pl.BlockSpec((1, 1, ncp, HEAD_DIM), lambda b, h, i: (h, b, 0, 0))
    return pl.pallas_call(
        functools.partial(_cmp_kernel, tq=tq, group=group),
        grid=(bsz, hk, t_len // tq),
        in_specs=[
            pl.BlockSpec((1, tq, gw), lambda b, h, i: (b, i, h)),
            cspec, cspec,
            pl.BlockSpec((1, 1, tq, LANES), lambda b, h, i: (h, b, i, 0)),
        ],
        out_specs=[
            pl.BlockSpec((1, tq, gw), lambda b, h, i: (b, i, h)),
            pl.BlockSpec((1, 1, tq, nsb), lambda b, h, i: (h, b, i, 0)),
        ],
        out_shape=[
            jax.ShapeDtypeStruct((bsz, t_len, hk * gw), BF16),
            jax.ShapeDtypeStruct((hk, bsz, t_len, nsb), BF16),
        ],
        compiler_params=_params("parallel", "parallel", "parallel"),
    )(qu, kc, vc, gate)


def _moba_gate_kernel(q_ref, k_ref, bias_ref, km_hi, km_lo, *, tq):
    qi = pl.program_id(2)
    t_len = k_ref.shape[2]
    ncol = bias_ref.shape[-1]
    nb = t_len // MOBA_BLOCK

    @pl.when(qi == 0)
    def _():
        blk = lax.broadcasted_iota(jnp.int32, (ncol, t_len), 0)
        pos = lax.broadcasted_iota(jnp.int32, (ncol, t_len), 1)
        member = jnp.where((pos >> _log2(MOBA_BLOCK)) == blk, 1.0, 0.0).astype(BF16)
        kmean = _dot(member, k_ref[0, 0]) * (1.0 / MOBA_BLOCK)
        hi = kmean.astype(BF16)
        km_hi[...] = hi
        km_lo[...] = (kmean - hi.astype(F32)).astype(BF16)

    q = q_ref[0, 0]
    gate = (_dot_t(km_hi[...], q) + _dot_t(km_lo[...], q))[0:nb]
    j = lax.broadcasted_iota(jnp.int32, (nb, tq), 0)
    t = qi * tq + lax.broadcasted_iota(jnp.int32, (nb, tq), 1)
    cb = t >> _log2(MOBA_BLOCK)
    score = jnp.where(j < cb, gate, NEG_INF)
    sel = _topk_select(score, min(MOBA_TOPK, nb - 1), j.astype(F32))
    keep = (sel > 0.5) | (j == cb)
    bias = jnp.where(keep, 0.0, MASK_BIAS)
    bias = jnp.concatenate([bias, jnp.zeros((ncol - nb, tq), F32)], axis=0)
    bias_ref[0, 0] = bias.T.astype(bias_ref.dtype)


def _moba_gate(q, k):
    nh, bsz, t_len, _ = q.shape
    tq = min(GATE_TQ, t_len)
    return pl.pallas_call(
        functools.partial(_moba_gate_kernel, tq=tq),
        grid=(nh, bsz, t_len // tq),
        in_specs=[
            pl.BlockSpec((1, 1, tq, HEAD_DIM), lambda h, b, i: (h, b, i, 0)),
            pl.BlockSpec((1, 1, t_len, HEAD_DIM), lambda h, b, i: (h, b, 0, 0)),
        ],
        out_specs=pl.BlockSpec((1, 1, tq, HEAD_DIM), lambda h, b, i: (h, b, i, 0)),
        out_shape=jax.ShapeDtypeStruct((nh, bsz, t_len, HEAD_DIM), BF16),
        scratch_shapes=[pltpu.VMEM((HEAD_DIM, HEAD_DIM), BF16), pltpu.VMEM((HEAD_DIM, HEAD_DIM), BF16)],
        compiler_params=_params("parallel", "parallel", "arbitrary"),
    )(q, k)


def _flash_kernel(q_ref, bias_ref, k_ref, vt_ref, *rest, group, chain_shape, lanes_out, tq, tk, blk_len, q_scale,
                  gate_branch):
    rest = list(rest)
    gate_ref = rest.pop(0) if gate_branch is not None else None
    o_ref, qa, ka, m_s, acc = rest[:5]
    s_bufs = rest[5:]
    chain_ids = [(ch, cb) for cb in range(chain_shape[1]) for ch in range(chain_shape[0])]
    chains = len(chain_ids)
    qi = pl.program_id(2)
    rows = group * tq
    t_len = k_ref.shape[2]

    @pl.when(qi == 0)
    def _():
        kpos = lax.broadcasted_iota(jnp.int32, (t_len, HEAD_DIM), 0)
        kblk = lax.broadcasted_iota(jnp.int32, (t_len, HEAD_DIM), 1)
        onehot = jnp.where((kpos >> _log2(blk_len)) == kblk, 1.0, 0.0).astype(BF16)
        for c, (ch, cb) in enumerate(chain_ids):
            ka[c, :, 0:HEAD_DIM] = k_ref[ch, cb]
            ka[c, :, HEAD_DIM:2 * HEAD_DIM] = onehot

    for c, (ch, cb) in enumerate(chain_ids):
        bias = bias_ref[ch, cb]
        for g in range(group):
            qg = q_ref[ch * group + g, cb]
            if q_scale != 1.0:
                qg = (qg.astype(F32) * q_scale).astype(BF16)
            qa[c, g * tq:(g + 1) * tq, 0:HEAD_DIM] = qg
            qa[c, g * tq:(g + 1) * tq, HEAD_DIM:2 * HEAD_DIM] = bias
    m_s[...] = jnp.full_like(m_s, NEG_INF)
    acc[...] = jnp.zeros_like(acc)

    def scores(ki, slot):
        start = pl.multiple_of(ki * tk, tk)
        for c in range(chains):
            s_bufs[2 * c + slot][...] = _dot_t(ka[c, pl.ds(start, tk), :], qa[c])

    def softmax_pv(ki, slot, masked):
        for c, (ch, cb) in enumerate(chain_ids):
            s = s_bufs[2 * c + slot][...]
            if masked:
                kpos = ki * tk + lax.broadcasted_iota(jnp.int32, (tk, rows), 0)
                qpos = qi * tq + (lax.broadcasted_iota(jnp.int32, (tk, rows), 1) & (tq - 1))
                s = jnp.where(kpos <= qpos, s, NEG_INF)
            m_old = m_s[c]
            m_new = jnp.maximum(m_old, jnp.max(s, axis=0, keepdims=True))
            alpha = jnp.exp2(m_old - m_new)
            p = jnp.exp2(s - m_new)
            acc[c] = alpha * acc[c] + _dot(vt_ref[ch, cb, ki], p.astype(BF16))
            m_s[c] = m_new

    n_full = (qi * tq) >> _log2(tk)
    scores(0, 0)

    def pair(j, carry):
        k0 = 2 * j
        scores(k0 + 1, 1)
        softmax_pv(k0, 0, False)
        scores(k0 + 2, 0)
        softmax_pv(k0 + 1, 1, False)
        return carry

    lax.fori_loop(0, n_full >> 1, pair, 0)
    cur = (n_full >> 1) << 1

    @pl.when((n_full & 1) == 1)
    def _():
        scores(cur + 1, 1)
        softmax_pv(cur, 0, False)
        softmax_pv(cur + 1, 1, True)

    @pl.when((n_full & 1) == 0)
    def _():
        softmax_pv(cur, 0, True)

    outs = {}
    for c, (ch, cb) in enumerate(chain_ids):
        d = acc[c, HEAD_DIM:HEAD_DIM + 1, :]
        d = jnp.where(d > 0, d, 1.0)
        o_t = acc[c, 0:HEAD_DIM, :] * (1.0 / d)
        for g in range(group):
            og = o_t[:, g * tq:(g + 1) * tq].T
            if gate_ref is not None:
                gc = g * 3 + gate_branch
                og = og * gate_ref[ch, cb, :, gc:gc + 1]
            outs[(cb, ch * group + g)] = og
    for cb in range(chain_shape[1]):
        planes = [outs[(cb, hq)] for hq in range(chain_shape[0] * group)]
        if lanes_out:
            o_ref[0, cb] = jnp.concatenate(planes, axis=-1).astype(o_ref.dtype)
        else:
            for hq, og in enumerate(planes):
                o_ref[hq, cb] = og.astype(o_ref.dtype)


def _flash(q, bias, k, vt, *, group, tq, blk_len, q_scale, lanes_out, gate=None, gate_branch=None):
    nh, bsz, t_len, _ = k.shape
    ch = min(nh, FLASH_CHAINS)
    cb = max(1, FLASH_CHAINS // ch)
    cb = cb if bsz % cb == 0 else 1
    chains = ch * cb
    nt, tk = vt.shape[2], vt.shape[4]
    per_step = lambda h, b, i: (h, b, i, 0)
    whole_seq = lambda h, b, i: (h, b, 0, 0)
    in_specs = [
        pl.BlockSpec((ch * group, cb, tq, HEAD_DIM), per_step),
        pl.BlockSpec((ch, cb, tq, HEAD_DIM), per_step),
        pl.BlockSpec((ch, cb, t_len, HEAD_DIM), whole_seq),
        pl.BlockSpec((ch, cb, nt, V_ROWS, tk), lambda h, b, i: (h, b, 0, 0, 0)),
    ]
    args = [q, bias, k, vt]
    if gate is not None:
        in_specs.append(pl.BlockSpec((ch, cb, tq, LANES), per_step))
        args.append(gate)
    if lanes_out:
        out_shape = (1, bsz, t_len, nh * group * HEAD_DIM)
        out_spec = pl.BlockSpec((1, cb, tq, ch * group * HEAD_DIM), lambda h, b, i: (0, b, i, h))
    else:
        out_shape = (nh * group, bsz, t_len, HEAD_DIM)
        out_spec = pl.BlockSpec((ch * group, cb, tq, HEAD_DIM), per_step)
    rows = group * tq
    return pl.pallas_call(
        functools.partial(_flash_kernel, group=group, chain_shape=(ch, cb), lanes_out=lanes_out, tq=tq, tk=tk,
                          blk_len=blk_len, q_scale=q_scale, gate_branch=gate_branch),
        grid=(nh // ch, bsz // cb, t_len // tq),
        in_specs=in_specs,
        out_specs=out_spec,
        out_shape=jax.ShapeDtypeStruct(out_shape, BF16),
        scratch_shapes=[
            pltpu.VMEM((chains, rows, 2 * HEAD_DIM), BF16),
            pltpu.VMEM((chains, t_len, 2 * HEAD_DIM), BF16),
            pltpu.VMEM((chains, 1, rows), F32),
            pltpu.VMEM((chains, V_ROWS, rows), F32),
        ] + [pltpu.VMEM((tk, rows), F32)] * (2 * chains),
        compiler_params=_params("parallel", "parallel", "arbitrary"),
    )(*args)


def _even_delta(refs, scratch):
    oa_ref, oc_ref, os_ref, ow_ref, w_ref = refs
    na = oa_ref.shape[-1]
    ob = (oc_ref[...].astype(F32) + os_ref[...].astype(F32) + ow_ref[...].astype(F32)).astype(BF16)
    return _dot(oa_ref[...], w_ref[0:na, :]) + _dot(ob, w_ref[na:, :])


def _even_mix(oa, ocmp, osel, owin, w):
    aspec = pl.BlockSpec((FFN_TM, oa.shape[-1]), lambda i: (i, 0))
    return dict(args=[oa, ocmp, osel, owin, w], specs=[aspec] * 4 + [_resident(w.shape)], scratch=[],
                fn=_even_delta)


def _odd_delta(refs, scratch, *, dilations):
    ng = len(dilations)
    o_refs, l_refs = refs[:ng], refs[ng:2 * ng]
    od_ref, w_ref = refs[2 * ng:]
    scratch = list(scratch)
    tm = od_ref.shape[1]

    def tokens(ref, r):
        if r == 1:
            return ref[0, 0].astype(F32)
        pieces = []
        for c in range(ref.shape[-1] // LANES):
            scr = scratch.pop(0)
            for i in range(r):
                scr[pl.ds(i, tm // r, stride=r), :] = ref[0, i, :, c * LANES:(c + 1) * LANES].astype(F32)
            pieces.append(scr[...])
        return jnp.concatenate(pieces, axis=-1)

    outs = [tokens(ref, r) for ref, r in zip(o_refs, dilations)]
    lses = [tokens(ref, r) for ref, r in zip(l_refs, dilations)]
    mx = functools.reduce(jnp.maximum, lses)
    es = [jnp.exp(l - mx) for l in lses]
    tot = functools.reduce(lambda a, b: a + b, es)
    oc = functools.reduce(lambda a, b: a + b, [(e / tot) * o for e, o in zip(es, outs)])
    nc = oc.shape[-1]
    delta = _dot(oc.astype(BF16), w_ref[0:nc, :])
    for hh in range(od_ref.shape[0]):
        delta = delta + _dot(od_ref[hh], w_ref[nc + hh * HEAD_DIM: nc + (hh + 1) * HEAD_DIM, :])
    return delta


def _odd_mix(outs, lses, od, w, t_len):
    tm = FFN_TM
    nt = t_len // tm
    gw = outs[0].shape[-1]
    dilations = tuple(o.shape[1] for o in outs)
    gspecs = [pl.BlockSpec((1, r, tm // r, gw), lambda i: (i // nt, 0, i % nt, 0)) for r in dilations]
    n_scr = 2 * (gw // LANES) * sum(1 for r in dilations if r > 1)
    specs = gspecs + gspecs + [pl.BlockSpec((od.shape[0], tm, HEAD_DIM), lambda i: (0, i, 0)), _resident(w.shape)]
    return dict(args=[*outs, *lses, od, w], specs=specs, scratch=[pltpu.VMEM((tm, LANES), F32)] * n_scr,
                fn=functools.partial(_odd_delta, dilations=dilations))


def _col_ranges(sizes):
    offs, acc = [], 0
    for s in sizes:
        offs.append((acc, acc + s))
        acc += s
    return offs


def _even_mixer(h, gain, w_in, w_out, sinks, cmp_pe, cmp_w1, cmp_w2, cos, sin, bsz, t_len):
    m = bsz * t_len
    qa_w, kva_w = A_Q_HEADS * HEAD_DIM, A_KV_HEADS * HEAD_DIM
    qb_w, kvb_w = B_Q_HEADS * HEAD_DIM, B_KV_HEADS * HEAD_DIM
    sizes = [qa_w, kva_w, kva_w, qb_w] + [kvb_w] * 6 + [3 * B_Q_HEADS]
    (aq, ak, av, bq, bkc, bvc, bks, bvs, bkw, bvw, bg) = [w_in[:, a:b] for a, b in _col_ranges(sizes)]
    group = B_Q_HEADS // B_KV_HEADS
    gpad = jnp.zeros((D_MODEL, LANES - 3 * group), w_in.dtype)
    gates = [x for hk in range(B_KV_HEADS) for x in (bg[:, hk * 3 * group:(hk + 1) * 3 * group], gpad)]
    w = jnp.concatenate([aq, bq, ak, bkw, bks, av, bvw, bvs, bkc, bvc] + gates, axis=1).astype(BF16)
    c = [0]

    def take(width):
        c[0] += width
        return c[0] - width

    def out(idx, off, width, kind, **kw):
        return dict(idx=idx, off=off, width=width, kind=kind, **kw)

    plan = [
        (take(qa_w), qa_w, [out(0, 0, qa_w, "split", hw=HEAD_DIM, rope=True, scale=True)]),
        (take(qb_w), qb_w, [out(1, 0, qb_w, "split", hw=HEAD_DIM, rope=True, scale=True),
                            out(5, 0, qb_w, "flat", scale=True)]),
        (take(3 * kva_w), 3 * kva_w, [out(2, 0, kva_w, "flat", rope=True),
                                      out(3, kva_w, kvb_w, "flat", rope=True),
                                      out(4, kva_w + kvb_w, kvb_w, "split", hw=HEAD_DIM, rope=True)]),
        (take(5 * kvb_w), 5 * kvb_w, [out(6, 0, kva_w, "flat"), out(7, kva_w, kvb_w, "flat"),
                                      out(8, 2 * kvb_w, kvb_w, "vt", tile=FLASH_TK),
                                      out(9, 3 * kvb_w, kvb_w, "chunk"),
                                      out(10, 4 * kvb_w, kvb_w, "chunk")]),
        (take(B_KV_HEADS * LANES), B_KV_HEADS * LANES,
         [out(11, 0, B_KV_HEADS * LANES, "split", hw=LANES, sigmoid=True)]),
    ]
    out_defs = [("split", A_Q_HEADS, HEAD_DIM, BF16), ("split", B_Q_HEADS, HEAD_DIM, BF16), ("flat", kva_w, BF16),
                ("flat", kvb_w, BF16),
                ("split", B_KV_HEADS, HEAD_DIM, BF16), ("flat", qb_w, BF16), ("flat", kva_w, BF16),
                ("flat", kvb_w, BF16), ("vt", B_KV_HEADS, FLASH_TK, BF16), ("chunk", B_KV_HEADS, BF16),
                ("chunk", B_KV_HEADS, BF16), ("split", B_KV_HEADS, LANES, F32)]
    (aq_r, bq_r, ak_r, bkw_r, bks_r, bq_u, av_, bvw_, bvs_t, zk, zv, gate) = _proj(
        h, gain, w, cos, sin, plan, out_defs, bsz, t_len)

    def seq(x):
        return x.reshape(bsz, 1, t_len, x.shape[-1])

    def heads(x):
        return x.reshape(x.shape[0], bsz, t_len, x.shape[-1])

    gate4 = heads(gate)
    oa = _banded(heads(aq_r), seq(ak_r), seq(av_), n_kv=A_KV_HEADS, group=A_Q_HEADS // A_KV_HEADS,
                 max_dist=A_WINDOW - 1, tl=BAND_TL, pk=128, span=128, sink=sinks)
    nch = t_len // NSA_CMP_STRIDE
    zshape = (B_KV_HEADS, bsz, nch, NSA_CMP_STRIDE * HEAD_DIM)
    kc, vc = _compress(zk.reshape(zshape), zv.reshape(zshape), cmp_pe.reshape(2, 1, NSA_CMP_LEN * HEAD_DIM),
                       cmp_w1.astype(BF16), cmp_w2.astype(BF16))
    ocmp, bias = _cmp_attention(bq_u.reshape(bsz, t_len, qb_w), kc, vc, gate4, t_len)
    osel = _flash(heads(bq_r), bias, heads(bks_r), bvs_t, group=group, tq=FLASH_T, blk_len=NSA_SEL_LEN,
                  q_scale=1.0, lanes_out=True, gate=gate4, gate_branch=1)
    owin = _banded(heads(bq_r), seq(bkw_r), seq(bvw_), n_kv=B_KV_HEADS, group=group,
                   max_dist=NSA_WINDOW - 1, tl=BAND_TL, pk=512, span=512, gate=gate4, gate_branch=2)
    return _even_mix(oa.reshape(m, qa_w), ocmp.reshape(m, qb_w), osel.reshape(m, qb_w),
                     owin.reshape(m, qb_w), w_out.astype(BF16))


def _odd_mixer(h, gain, w_in, w_out, cos, sin, bsz, t_len):
    m = bsz * t_len
    cw = C_HEADS * HEAD_DIM
    dw = D_HEADS * HEAD_DIM
    gw = C_HEADS_PER_GROUP * HEAD_DIM
    n_groups = len(C_GROUPS)

    def group_outs(base, **kw):
        return [dict(idx=base + gi, off=gi * gw, width=gw, kind="dilate", r=r, **kw)
                for gi, (_, r) in enumerate(C_GROUPS)]

    plan = [
        (0, cw, group_outs(0, rope=True, scale=True)),
        (cw, cw, group_outs(n_groups, rope=True)),
        (2 * cw, cw, group_outs(2 * n_groups)),
        (3 * cw, 3 * dw, [dict(idx=3 * n_groups, off=0, width=dw, kind="split", hw=HEAD_DIM, rope=True),
                          dict(idx=3 * n_groups + 1, off=dw, width=dw, kind="split", hw=HEAD_DIM, rope=True),
                          dict(idx=3 * n_groups + 2, off=2 * dw, width=dw, kind="vt", tile=MOBA_TK)]),
    ]
    out_defs = [("dilate", r, gw, BF16) for _ in range(3) for _, r in C_GROUPS]
    out_defs += [("split", D_HEADS, HEAD_DIM, BF16), ("split", D_HEADS, HEAD_DIM, BF16),
                 ("vt", D_HEADS, MOBA_TK, BF16)]
    res = _proj(h, gain, w_in.astype(BF16), cos, sin, plan, out_defs, bsz, t_len)
    cq, ck, cv = res[0:n_groups], res[n_groups:2 * n_groups], res[2 * n_groups:3 * n_groups]
    dq, dk, dv_t = res[3 * n_groups:]
    outs, lses = [], []
    for gi, (wlen, r) in enumerate(C_GROUPS):
        o, lse = _banded(cq[gi], ck[gi], cv[gi], n_kv=C_HEADS_PER_GROUP, group=1, max_dist=wlen // r,
                         tl=min(BAND_TL, t_len // r), pk=128, span=128, want_lse=True)
        outs.append(o)
        lses.append(lse)

    def heads(x):
        return x.reshape(x.shape[0], bsz, t_len, x.shape[-1])

    bias = _moba_gate(heads(dq), heads(dk))
    od = _flash(heads(dq), bias, heads(dk), dv_t, group=1, tq=MOBA_T, blk_len=MOBA_BLOCK, q_scale=QK_SCALE,
                lanes_out=False)
    return _odd_mix(outs, lses, od.reshape(D_HEADS, m, HEAD_DIM), w_out.astype(BF16), t_len)


def kernel(x, ffn_norm_pre, mix_norm, ffn_norm_post, ffn_wi, ffn_wo, even_w_in, even_w_out, even_sinks,
           nsa_cmp_pe, nsa_cmp_w1, nsa_cmp_w2, odd_w_in, odd_w_out, final_norm):
    bsz, t_len, _ = x.shape
    depth = ffn_wi.shape[0]
    cos, sin = _rope_tables(t_len)
    wi = ffn_wi.astype(BF16)
    wo = ffn_wo.astype(BF16)
    h = x.reshape(bsz * t_len, D_MODEL)
    for layer in range(depth):
        i = layer // 2
        h = _ffn(h, ffn_norm_pre[layer], wi, wo, (layer, 0))
        if layer % 2 == 0:
            mix = _even_mixer(h, mix_norm[layer], even_w_in[i], even_w_out[i], even_sinks[i], nsa_cmp_pe[i],
                              nsa_cmp_w1[i], nsa_cmp_w2[i], cos, sin, bsz, t_len)
        else:
            mix = _odd_mixer(h, mix_norm[layer], odd_w_in[i], odd_w_out[i], cos, sin, bsz, t_len)
        last = layer == depth - 1
        h = _ffn(h, ffn_norm_post[layer], wi, wo, (layer, 1), final_norm if last else None, mix=mix)
    return h.reshape(bsz, t_len, D_MODEL)
```

```python
import functools

import jax
import jax.numpy as jnp
from jax import lax
from jax.experimental import pallas as pl
from jax.experimental.pallas import tpu as pltpu

D_MODEL = 1024
HEAD_DIM = 64
ROPE_THETA = 10000.0
NORM_EPS = 1e-6
D_FF = 2816
NEG_INF = -1e30
FORCE_SCORE = 1e4

A_Q_HEADS = 8
A_KV_HEADS = 2
A_WINDOW = 128
B_Q_HEADS = 8
B_KV_HEADS = 2
NSA_CMP_LEN = 32
NSA_CMP_STRIDE = 16
NSA_CMP_HIDDEN = 256
NSA_SEL_LEN = 64
NSA_SEL_COUNT = 8
NSA_WINDOW = 512
C_GROUPS = ((128, 1), (512, 4), (2048, 16))
C_HEADS_PER_GROUP = 4
C_HEADS = len(C_GROUPS) * C_HEADS_PER_GROUP
D_HEADS = 4
MOBA_BLOCK = 256
MOBA_TOPK = 3

LANES = 128
LOG2E = 1.4426950408889634
LN2 = 0.6931471805599453
QK_SCALE = HEAD_DIM ** -0.5 * LOG2E
MASK_BIAS = -32768.0
V_ROWS = HEAD_DIM + 16
REMOVED = -3e38
VMEM_LIMIT = 52 * 1024 * 1024

FFN_TM = 512
FFN_TF = 256
PROJ_TM = 1024
BAND_TL = 512
BAND_SB = 128
FLASH_T = 256
FLASH_TK = 256
FLASH_CHAINS = 4
MOBA_T = 512
MOBA_TK = 512
GATE_TQ = 4096
CMP_TQ = 1024

BF16 = jnp.bfloat16
F32 = jnp.float32


def _params(*sem):
    return pltpu.CompilerParams(dimension_semantics=sem, vmem_limit_bytes=VMEM_LIMIT)


def _rms(x, g):
    return x * lax.rsqrt(jnp.mean(x * x, axis=-1, keepdims=True) + NORM_EPS) * g


def _log2(n):
    assert n & (n - 1) == 0
    return n.bit_length() - 1


def _dot(a, b):
    return jnp.dot(a, b, preferred_element_type=F32)


def _dot_t(a, b):
    return lax.dot_general(a, b, (((1,), (1,)), ((), ())), preferred_element_type=F32)


def _dot_tn(a, b):
    return lax.dot_general(a, b, (((0,), (0,)), ((), ())), preferred_element_type=F32)


def _ffn_kernel(x_ref, g_ref, wi_ref, wo_ref, *rest, final, mix_fn, n_mix):
    rest = list(rest)
    fg_ref = rest.pop(0) if final else None
    mix_refs = [rest.pop(0) for _ in range(n_mix)]
    o_ref, act_scr = rest.pop(0), rest.pop(0)
    x = x_ref[...]
    if mix_fn is not None:
        x = x + mix_fn(mix_refs, rest)
    o_ref[...] = x
    n = _rms(x, g_ref[...]).astype(BF16)
    for c in range(D_FF // FFN_TF):
        cols = slice(c * FFN_TF, (c + 1) * FFN_TF)
        gate = _dot(n, wi_ref[:, cols])
        up = _dot(n, wi_ref[:, D_FF + c * FFN_TF:D_FF + (c + 1) * FFN_TF])
        act_scr[:, cols] = (gate * jax.nn.sigmoid(gate) * up).astype(BF16)
    h = o_ref[...] + 0.5 * _dot(act_scr[...], wo_ref[...])
    if final:
        h = _rms(h, fg_ref[...])
    o_ref[...] = h


def _resident(shape, lead=()):
    block = (None,) * len(lead) + tuple(shape[len(lead):])
    index = tuple(lead) + (0,) * (len(shape) - len(lead))
    return pl.BlockSpec(block, lambda i: index, pipeline_mode=pl.Buffered(1))


def _ffn(h, gain, wi, wo, which, final_gain=None, mix=None):
    m = h.shape[0]
    tm = FFN_TM
    final = final_gain is not None
    in_specs = [
        pl.BlockSpec((tm, D_MODEL), lambda i: (i, 0)),
        pl.BlockSpec((1, D_MODEL), lambda i: (0, 0)),
        _resident(wi.shape, which),
        _resident(wo.shape, which),
    ]
    args = [h, gain.reshape(1, D_MODEL), wi, wo]
    if final:
        in_specs.append(pl.BlockSpec((1, D_MODEL), lambda i: (0, 0)))
        args.append(final_gain.reshape(1, D_MODEL))
    scratch = [pltpu.VMEM((tm, D_FF), BF16)]
    if mix is not None:
        in_specs += mix["specs"]
        args += mix["args"]
        scratch += mix["scratch"]
    return pl.pallas_call(
        functools.partial(_ffn_kernel, final=final, mix_fn=mix and mix["fn"], n_mix=len(mix["args"]) if mix else 0),
        grid=(m // tm,),
        in_specs=in_specs,
        out_specs=pl.BlockSpec((tm, D_MODEL), lambda i: (i, 0)),
        out_shape=jax.ShapeDtypeStruct((m, D_MODEL), F32),
        scratch_shapes=scratch,
        compiler_params=_params("parallel"),
    )(*args)


def _rope_tables(t):
    inv = 1.0 / (ROPE_THETA ** (jnp.arange(0, HEAD_DIM, 2, dtype=F32) / HEAD_DIM))
    ang = jnp.arange(t, dtype=F32)[:, None] * inv[None, :]
    cos = jnp.cos(ang)
    sin = jnp.sin(ang)
    return (jnp.concatenate([cos, cos, cos, cos], axis=-1),
            jnp.concatenate([-sin, sin, -sin, sin], axis=-1))


def _proj_kernel(x_ref, g_ref, w_ref, cos_ref, sin_ref, *rest, plan, n_out):
    out_refs, (ys,) = rest[:n_out], rest[n_out:]
    n = _rms(x_ref[...], g_ref[...]).astype(BF16)
    tm = n.shape[0]
    cos = cos_ref[...]
    sin = sin_ref[...]
    lane = lax.broadcasted_iota(jnp.int32, cos.shape, 1)
    first_half = (lane & (HEAD_DIM - 1)) < HEAD_DIM // 2
    heads_per_tile = LANES // HEAD_DIM
    for off, width, outs in plan:
        y = _dot(n, w_ref[:, off:off + width])
        for o in outs:
            o_ref = out_refs[o["idx"]]
            dt = o_ref.dtype
            kind = o["kind"]
            for c in range(o["width"] // LANES):
                piece = y[:, o["off"] + c * LANES: o["off"] + (c + 1) * LANES]
                if o.get("rope"):
                    rot = jnp.where(first_half, pltpu.roll(piece, LANES - HEAD_DIM // 2, 1),
                                    pltpu.roll(piece, HEAD_DIM // 2, 1))
                    piece = piece * cos + rot * sin
                if o.get("scale"):
                    piece = piece * QK_SCALE
                if o.get("sigmoid"):
                    piece = jax.nn.sigmoid(piece)
                lanes = slice(c * LANES, (c + 1) * LANES)
                if kind == "flat":
                    o_ref[:, lanes] = piece.astype(dt)
                elif kind == "split":
                    hw = o["hw"]
                    per = LANES // hw
                    for k in range(per):
                        o_ref[c * per + k] = piece[:, k * hw:(k + 1) * hw].astype(dt)
                elif kind == "dilate":
                    r = o["r"]
                    if r == 1:
                        o_ref[0, 0, :, lanes] = piece.astype(dt)
                    else:
                        ys[...] = piece
                        for i in range(r):
                            o_ref[0, i, :, lanes] = ys[pl.ds(i, tm // r, stride=r), :].astype(dt)
                elif kind == "chunk":
                    ys[...] = piece
                    for l in range(NSA_CMP_STRIDE):
                        rows = ys[pl.ds(l, tm // NSA_CMP_STRIDE, stride=NSA_CMP_STRIDE), :].astype(dt)
                        for k in range(heads_per_tile):
                            o_ref[c * heads_per_tile + k, :, l * HEAD_DIM:(l + 1) * HEAD_DIM] = rows[
                                :, k * HEAD_DIM:(k + 1) * HEAD_DIM]
                else:
                    tile = o["tile"]
                    pt = piece.T
                    for k in range(heads_per_tile):
                        for kt in range(tm // tile):
                            o_ref[c * heads_per_tile + k, 0, kt, 0:HEAD_DIM, :] = pt[
                                k * HEAD_DIM:(k + 1) * HEAD_DIM, kt * tile:(kt + 1) * tile].astype(dt)
                            o_ref[c * heads_per_tile + k, 0, kt, HEAD_DIM:V_ROWS, :] = jnp.ones(
                                (V_ROWS - HEAD_DIM, tile), dt)


def _proj(h, gain, w, cos, sin, plan, out_defs, bsz, t_len):
    m = h.shape[0]
    tm = PROJ_TM
    nt = t_len // tm
    out_shapes, out_specs = [], []
    for d in out_defs:
        kind, dt = d[0], d[-1]
        if kind == "flat":
            shape, block, index = (m, d[1]), (tm, d[1]), (lambda i: (i, 0))
        elif kind == "split":
            shape, block, index = (d[1], m, d[2]), (d[1], tm, d[2]), (lambda i: (0, i, 0))
        elif kind == "dilate":
            r = d[1]
            shape, block = (bsz, r, t_len // r, d[2]), (1, r, tm // r, d[2])
            index = lambda i: (i // nt, 0, i % nt, 0)
        elif kind == "chunk":
            width = NSA_CMP_STRIDE * HEAD_DIM
            shape, block = (d[1], m // NSA_CMP_STRIDE, width), (d[1], tm // NSA_CMP_STRIDE, width)
            index = lambda i: (0, i, 0)
        else:
            tile = d[2]
            shape, block = (d[1], bsz, t_len // tile, V_ROWS, tile), (d[1], 1, tm // tile, V_ROWS, tile)
            index = lambda i: (0, i // nt, i % nt, 0, 0)
        out_shapes.append(jax.ShapeDtypeStruct(shape, dt))
        out_specs.append(pl.BlockSpec(block, index))
    return pl.pallas_call(
        functools.partial(_proj_kernel, plan=plan, n_out=len(out_defs)),
        grid=(m // tm,),
        in_specs=[
            pl.BlockSpec((tm, D_MODEL), lambda i: (i, 0)),
            pl.BlockSpec((1, D_MODEL), lambda i: (0, 0)),
            pl.BlockSpec(w.shape, lambda i: (0, 0)),
            pl.BlockSpec((tm, LANES), lambda i: (i % nt, 0)),
            pl.BlockSpec((tm, LANES), lambda i: (i % nt, 0)),
        ],
        out_specs=out_specs,
        out_shape=out_shapes,
        scratch_shapes=[pltpu.VMEM((tm, LANES), F32)],
        compiler_params=_params("parallel"),
    )(h, gain.reshape(1, D_MODEL), w, cos, sin)


def _banded_kernel(q_ref, kp_ref, kc_ref, vp_ref, vc_ref, *rest, n_kv, group, tl, pk, span, max_dist,
                   has_sink, gate_branch, want_lse):
    rest = list(rest)
    sink_ref = rest.pop(0) if has_sink else None
    gate_ref = rest.pop(0) if gate_branch is not None else None
    o_ref = rest.pop(0)
    lse_ref = rest.pop(0) if want_lse else None
    kcat, vcat, s_a, s_b, band = rest
    sb = BAND_SB
    n_seq = kc_ref.shape[1]
    for rr in range(n_seq):
        kcat[rr, 0:pk, :] = kp_ref[0, rr]
        kcat[rr, pk:, :] = kc_ref[0, rr]
        vcat[rr, 0:pk, :] = vp_ref[0, rr]
        vcat[rr, pk:, :] = vc_ref[0, rr]
    nk = span + sb
    kw = n_kv * HEAD_DIM
    merged = group == 1
    stacked = n_kv if merged else group
    rows = stacked * sb
    key = lax.broadcasted_iota(jnp.int32, (nk, rows), 0)
    qry = lax.broadcasted_iota(jnp.int32, (nk, rows), 1) & (sb - 1)
    dist = qry + span - key
    in_band = (dist >= 0) & (dist <= max_dist)
    lane_head = lax.broadcasted_iota(jnp.int32, (1, rows), 1) >> _log2(sb)
    q_head = lax.broadcasted_iota(jnp.int32, (sb, kw), 1) >> _log2(HEAD_DIM)
    v_head = lax.broadcasted_iota(jnp.int32, (nk, kw), 1) >> _log2(HEAD_DIM)
    assert merged or kw == LANES
    items = [(rr, j, h) for rr in range(n_seq) for j in range(tl // sb)
             for h in ([0] if merged else range(n_kv))]
    s_bufs = (s_a, s_b)

    def compute(first_tile):
        def scores(n):
            rr, j, h = items[n]
            lo = pk + j * sb - span
            qrows = slice(j * sb, (j + 1) * sb)
            if merged:
                qj = q_ref[0, rr, qrows, :].astype(F32)
                qs = jnp.concatenate([jnp.where(q_head == r, qj, 0.0).astype(BF16) for r in range(n_kv)], axis=0)
                kk = kcat[rr, lo:lo + nk, :]
            else:
                qs = jnp.concatenate([q_ref[h * group + g, 0, qrows, :] for g in range(group)], axis=0)
                kk = kcat[rr, lo:lo + nk, h * HEAD_DIM:(h + 1) * HEAD_DIM]
            variant = 1 + j if first_tile and j * sb < span else 0
            s_bufs[n % 2][...] = _dot_t(kk, qs) + band[variant]

        band[0] = jnp.where(in_band, 0.0, NEG_INF)
        if first_tile:
            for j in range(span // sb):
                band[1 + j] = jnp.where(in_band & (key >= span - j * sb), 0.0, NEG_INF)

        outs, lses = [], []
        scores(0)
        for n, (rr, j, h) in enumerate(items):
            if n + 1 < len(items):
                scores(n + 1)
            s = s_bufs[n % 2][...]
            lo = pk + j * sb - span
            qrows = slice(j * sb, (j + 1) * sb)
            m = jnp.max(s, axis=0, keepdims=True)
            if has_sink:
                sink = jnp.full((1, rows), sink_ref[h * group], F32)
                for g in range(1, group):
                    sink = jnp.where(lane_head == g, sink_ref[h * group + g], sink)
                sink = sink * LOG2E
                m = jnp.maximum(m, sink)
            p = jnp.exp2(s - m).astype(BF16)
            vwin = vcat[rr, lo:lo + nk, :]
            if merged:
                vv = jnp.concatenate([vwin, jnp.ones((nk, LANES), BF16)], axis=1)
                ones_row = kw
            else:
                vv = jnp.where(v_head == h, vwin.astype(F32), 1.0).astype(BF16)
                ones_row = (1 - h) * HEAD_DIM
            o_t = _dot_tn(vv, p)
            d = o_t[ones_row:ones_row + 1, :]
            if has_sink:
                d = d + jnp.exp2(sink - m)
            d = jnp.where(d > 0, d, 1.0)
            inv = 1.0 / d
            if want_lse:
                lse_t = jnp.broadcast_to(m * LN2 + jnp.log(d), (HEAD_DIM, rows))
            for r in range(stacked):
                cols = slice(r * sb, (r + 1) * sb)
                vrow = (r if merged else h) * HEAD_DIM
                o = (o_t[vrow:vrow + HEAD_DIM, cols] * inv[:, cols]).T
                if gate_ref is not None:
                    gc = r * 3 + gate_branch
                    o = o * gate_ref[h, 0, qrows, gc:gc + 1]
                outs.append(o)
                if want_lse:
                    lses.append(lse_t[:, cols].T)
            if merged or h == n_kv - 1:
                o_ref[0, rr, qrows, :] = jnp.concatenate(outs, axis=-1).astype(o_ref.dtype)
                if want_lse:
                    lse_ref[0, rr, qrows, :] = jnp.concatenate(lses, axis=-1)
                outs, lses = [], []

    first = pl.program_id(2) == 0
    pl.when(first)(functools.partial(compute, True))
    pl.when(jnp.logical_not(first))(functools.partial(compute, False))


def _banded(q, k, v, *, n_kv, group, max_dist, tl, pk, span, sink=None, gate=None, gate_branch=None,
            want_lse=False):
    bsz, n_res, length, kw = k.shape
    qw = n_kv * group * HEAD_DIM
    s_shape = (span + BAND_SB, (n_kv if group == 1 else group) * BAND_SB)
    ratio = tl // pk
    n_seq = min(n_res, max(1, BAND_TL // tl))
    prev = lambda b, i, l: (b, i, jnp.maximum(l * ratio - 1, 0), 0)
    cur = lambda b, i, l: (b, i, l, 0)
    if group == 1:
        q_spec = pl.BlockSpec((1, n_seq, tl, qw), cur)
    else:
        q_spec = pl.BlockSpec((n_kv * group, 1, tl, HEAD_DIM), lambda b, i, l: (0, b, l, 0))
    o_shape = (bsz, n_res, length, qw)
    in_specs = [
        q_spec,
        pl.BlockSpec((1, n_seq, pk, kw), prev),
        pl.BlockSpec((1, n_seq, tl, kw), cur),
        pl.BlockSpec((1, n_seq, pk, kw), prev),
        pl.BlockSpec((1, n_seq, tl, kw), cur),
    ]
    args = [q, k, k, v, v]
    if sink is not None:
        in_specs.append(pl.BlockSpec(memory_space=pltpu.SMEM))
        args.append(sink)
    if gate is not None:
        in_specs.append(pl.BlockSpec((n_kv, 1, tl, LANES), lambda b, i, l: (0, b, l, 0)))
        args.append(gate)
    out_spec = pl.BlockSpec((1, n_seq, tl, qw), cur)
    out_shape = [jax.ShapeDtypeStruct(o_shape, BF16)]
    out_specs = [out_spec]
    if want_lse:
        out_shape.append(jax.ShapeDtypeStruct(o_shape, F32))
        out_specs.append(out_spec)
    res = pl.pallas_call(
        functools.partial(_banded_kernel, n_kv=n_kv, group=group, tl=tl, pk=pk, span=span,
                          max_dist=max_dist, has_sink=sink is not None, gate_branch=gate_branch,
                          want_lse=want_lse),
        grid=(bsz, n_res // n_seq, length // tl),
        in_specs=in_specs,
        out_specs=out_specs,
        out_shape=out_shape,
        scratch_shapes=[pltpu.VMEM((n_seq, pk + tl, kw), BF16), pltpu.VMEM((n_seq, pk + tl, kw), BF16),
                        pltpu.VMEM(s_shape, F32), pltpu.VMEM(s_shape, F32),
                        pltpu.VMEM((1 + span // BAND_SB,) + s_shape, F32)],
        compiler_params=_params("parallel", "parallel", "parallel"),
    )(*args)
    return res if want_lse else res[0]


def _compress_one(z_ref, pe_ref, w1_ref, w2_ref, o_ref):
    half = NSA_CMP_STRIDE * HEAD_DIM
    z = z_ref[0, 0].astype(F32)
    za = (z + pe_ref[0, :, 0:half]).astype(BF16)
    zb = (z + pe_ref[0, :, half:2 * half]).astype(BF16)
    ya = _dot(za, w1_ref[0, 0:half, :])
    yb = _dot(zb, w1_ref[0, half:2 * half, :])
    nrow = ya.shape[0]
    hid = jax.nn.gelu(ya + pltpu.roll(yb, nrow - 1, 0))
    o_ref[0, 0] = _dot(hid.astype(BF16), w2_ref[0]).astype(o_ref.dtype)


def _compress_kernel(zk_ref, zv_ref, pek_ref, pev_ref, w1k_ref, w1v_ref, w2k_ref, w2v_ref, ok_ref, ov_ref):
    _compress_one(zk_ref, pek_ref, w1k_ref, w2k_ref, ok_ref)
    _compress_one(zv_ref, pev_ref, w1v_ref, w2v_ref, ov_ref)


def _compress(zk, zv, pe, w1, w2):
    hk, bsz, nch, width = zk.shape
    zspec = pl.BlockSpec((1, 1, nch, width), lambda h, b: (h, b, 0, 0))
    ospec = pl.BlockSpec((1, 1, nch, HEAD_DIM), lambda h, b: (h, b, 0, 0))
    oshape = jax.ShapeDtypeStruct((hk, bsz, nch, HEAD_DIM), BF16)

    def wspec(arr, which):
        return pl.BlockSpec((1,) + arr.shape[1:], lambda h, b: (which, 0, 0))

    return pl.pallas_call(
        _compress_kernel,
        grid=(hk, bsz),
        in_specs=[zspec, zspec, wspec(pe, 0), wspec(pe, 1), wspec(w1, 0), wspec(w1, 1), wspec(w2, 0), wspec(w2, 1)],
        out_specs=[ospec, ospec],
        out_shape=[oshape, oshape],
        compiler_params=_params("parallel", "parallel"),
    )(zk, zv, pe, pe, w1, w1, w2, w2)


def _topk_select(score, k, idxf):
    ncand = score.shape[0]
    rem = score
    sel = jnp.zeros_like(score)
    for _ in range(k):
        mx = jnp.max(rem, axis=0, keepdims=True)
        first = jnp.min(jnp.where(rem == mx, idxf, float(ncand)), axis=0, keepdims=True)
        pick = idxf == first
        sel = jnp.where(pick, jnp.where(mx > 0.5 * NEG_INF, 1.0, 0.0), sel)
        rem = jnp.where(pick, REMOVED, rem)
    return sel


def _cmp_kernel(q_ref, kc_ref, vc_ref, gate_ref, o_ref, bias_ref, *, tq, group, n_tiles):
    qi = pl.program_id(2)
    for v in range(n_tiles):
        ncp = min(kc_ref.shape[2], (v + 1) * tq // NSA_CMP_STRIDE)
        pl.when(qi == v)(functools.partial(_cmp_tile, q_ref, kc_ref, vc_ref, gate_ref, o_ref, bias_ref, qi,
                                           tq=tq, group=group, ncp=ncp))


def _cmp_tile(q_ref, kc_ref, vc_ref, gate_ref, o_ref, bias_ref, qi, *, tq, group, ncp):
    kc = kc_ref[0, 0, 0:ncp, :]
    vc = vc_ref[0, 0, 0:ncp, :]
    nsb = bias_ref.shape[-1]
    rows = group * tq
    last = lax.broadcasted_iota(jnp.int32, (ncp, 1), 0) * NSA_CMP_STRIDE + (NSA_CMP_LEN - 1)
    t_in_tile = lax.broadcasted_iota(jnp.int32, (1, rows), 1) & (tq - 1)
    visible = (last - t_in_tile) <= qi * tq
    qs = jnp.concatenate([q_ref[0, :, g * HEAD_DIM:(g + 1) * HEAD_DIM] for g in range(group)], axis=0)
    s = jnp.where(visible, _dot_t(kc, qs), NEG_INF)
    m = jnp.max(s, axis=0, keepdims=True)
    m = jnp.where(m > 0.5 * NEG_INF, m, 0.0)
    p = jnp.exp2(s - m)
    vc_ones = jnp.concatenate([vc, jnp.ones_like(vc)], axis=1)
    o_t = _dot_tn(vc_ones, p.astype(BF16))
    d = o_t[HEAD_DIM:HEAD_DIM + 1, :]
    d = jnp.where(d > 0, d, 1.0)
    inv = 1.0 / d
    pc = p * inv
    o_t = o_t[0:HEAD_DIM, :] * inv
    outs = []
    psum = pc[:, 0:tq]
    for g in range(group):
        if g:
            psum = psum + pc[:, g * tq:(g + 1) * tq]
        outs.append(o_t[:, g * tq:(g + 1) * tq].T * gate_ref[0, 0, :, g * 3:g * 3 + 1])
    o_ref[0] = jnp.concatenate(outs, axis=-1).astype(o_ref.dtype)

    jj = lax.broadcasted_iota(jnp.int32, (nsb, ncp), 0) * NSA_SEL_LEN
    nn = lax.broadcasted_iota(jnp.int32, (nsb, ncp), 1) * NSA_CMP_STRIDE
    ov = jnp.maximum(jnp.minimum(nn + NSA_CMP_LEN, jj + NSA_SEL_LEN) - jnp.maximum(nn, jj), 0)
    ov = (ov.astype(F32) / NSA_CMP_LEN).astype(BF16)
    p_hi = psum.astype(BF16)
    p_lo = (psum - p_hi.astype(F32)).astype(BF16)
    imp = _dot(ov, p_hi) + _dot(ov, p_lo)

    j = lax.broadcasted_iota(jnp.int32, (nsb, tq), 0)
    tt = qi * tq + lax.broadcasted_iota(jnp.int32, (nsb, tq), 1)
    cb = tt >> _log2(NSA_SEL_LEN)
    forced = (j == 0) | (j == cb) | (j == cb - 1)
    score = jnp.where(j <= cb, jnp.where(forced, FORCE_SCORE, imp), NEG_INF)
    sel = _topk_select(score, min(NSA_SEL_COUNT, nsb), j.astype(F32))
    bias_ref[0, 0] = jnp.where(sel > 0.5, 0.0, MASK_BIAS).T.astype(bias_ref.dtype)


def _cmp_attention(qu, kc, vc, gate, t_len):
    bsz = qu.shape[0]
    hk, _, ncp, _ = kc.shape
    group = B_Q_HEADS // B_KV_HEADS
    tq = CMP_TQ
    nsb = t_len // NSA_SEL_LEN
    gw = group * HEAD_DIM
    cspec = pl.BlockSpec((1, 1, ncp, HEAD_DIM), lambda b, h, i: (h, b, 0, 0))
    return pl.pallas_call(
        functools.partial(_cmp_kernel, tq=tq, group=group, n_tiles=t_len // tq),
        grid=(bsz, hk, t_len // tq),
        in_specs=[
            pl.BlockSpec((1, tq, gw), lambda b, h, i: (b, i, h)),
            cspec, cspec,
            pl.BlockSpec((1, 1, tq, LANES), lambda b, h, i: (h, b, i, 0)),
        ],
        out_specs=[
            pl.BlockSpec((1, tq, gw), lambda b, h, i: (b, i, h)),
            pl.BlockSpec((1, 1, tq, nsb), lambda b, h, i: (h, b, i, 0)),
        ],
        out_shape=[
            jax.ShapeDtypeStruct((bsz, t_len, hk * gw), BF16),
            jax.ShapeDtypeStruct((hk, bsz, t_len, nsb), BF16),
        ],
        compiler_params=_params("parallel", "parallel", "parallel"),
    )(qu, kc, vc, gate)


def _moba_gate_kernel(q_ref, k_ref, bias_ref, km_hi, km_lo, *, tq):
    qi = pl.program_id(2)
    t_len = k_ref.shape[2]
    ncol = bias_ref.shape[-1]
    nb = t_len // MOBA_BLOCK

    @pl.when(qi == 0)
    def _():
        blk = lax.broadcasted_iota(jnp.int32, (ncol, t_len), 0)
        pos = lax.broadcasted_iota(jnp.int32, (ncol, t_len), 1)
        member = jnp.where((pos >> _log2(MOBA_BLOCK)) == blk, 1.0, 0.0).astype(BF16)
        kmean = _dot(member, k_ref[0, 0]) * (1.0 / MOBA_BLOCK)
        hi = kmean.astype(BF16)
        km_hi[...] = hi
        km_lo[...] = (kmean - hi.astype(F32)).astype(BF16)

    q = q_ref[0, 0]
    gate = (_dot_t(km_hi[...], q) + _dot_t(km_lo[...], q))[0:nb]
    j = lax.broadcasted_iota(jnp.int32, (nb, tq), 0)
    t = qi * tq + lax.broadcasted_iota(jnp.int32, (nb, tq), 1)
    cb = t >> _log2(MOBA_BLOCK)
    score = jnp.where(j < cb, gate, NEG_INF)
    sel = _topk_select(score, min(MOBA_TOPK, nb - 1), j.astype(F32))
    keep = (sel > 0.5) | (j == cb)
    bias = jnp.where(keep, 0.0, MASK_BIAS)
    bias = jnp.concatenate([bias, jnp.zeros((ncol - nb, tq), F32)], axis=0)
    bias_ref[0, 0] = bias.T.astype(bias_ref.dtype)


def _moba_gate(q, k):
    nh, bsz, t_len, _ = q.shape
    tq = min(GATE_TQ, t_len)
    return pl.pallas_call(
        functools.partial(_moba_gate_kernel, tq=tq),
        grid=(nh, bsz, t_len // tq),
        in_specs=[
            pl.BlockSpec((1, 1, tq, HEAD_DIM), lambda h, b, i: (h, b, i, 0)),
            pl.BlockSpec((1, 1, t_len, HEAD_DIM), lambda h, b, i: (h, b, 0, 0)),
        ],
        out_specs=pl.BlockSpec((1, 1, tq, HEAD_DIM), lambda h, b, i: (h, b, i, 0)),
        out_shape=jax.ShapeDtypeStruct((nh, bsz, t_len, HEAD_DIM), BF16),
        scratch_shapes=[pltpu.VMEM((HEAD_DIM, HEAD_DIM), BF16), pltpu.VMEM((HEAD_DIM, HEAD_DIM), BF16)],
        compiler_params=_params("parallel", "parallel", "arbitrary"),
    )(q, k)


def _flash_kernel(q_ref, bias_ref, k_ref, vt_ref, *rest, group, chain_shape, lanes_out, tq, tk, blk_len, q_scale,
                  gate_branch):
    rest = list(rest)
    gate_ref = rest.pop(0) if gate_branch is not None else None
    o_ref, qa, ka, m_s, acc = rest[:5]
    s_bufs = rest[5:]
    chain_ids = [(ch, cb) for cb in range(chain_shape[1]) for ch in range(chain_shape[0])]
    chains = len(chain_ids)
    qi = pl.program_id(2)
    rows = group * tq
    t_len = k_ref.shape[2]

    @pl.when(qi == 0)
    def _():
        kpos = lax.broadcasted_iota(jnp.int32, (t_len, HEAD_DIM), 0)
        kblk = lax.broadcasted_iota(jnp.int32, (t_len, HEAD_DIM), 1)
        onehot = jnp.where((kpos >> _log2(blk_len)) == kblk, 1.0, 0.0).astype(BF16)
        for c, (ch, cb) in enumerate(chain_ids):
            ka[c, :, 0:HEAD_DIM] = k_ref[ch, cb]
            ka[c, :, HEAD_DIM:2 * HEAD_DIM] = onehot

    for c, (ch, cb) in enumerate(chain_ids):
        bias = bias_ref[ch, cb]
        for g in range(group):
            qg = q_ref[ch * group + g, cb]
            if q_scale != 1.0:
                qg = (qg.astype(F32) * q_scale).astype(BF16)
            qa[c, g * tq:(g + 1) * tq, 0:HEAD_DIM] = qg
            qa[c, g * tq:(g + 1) * tq, HEAD_DIM:2 * HEAD_DIM] = bias
    m_s[...] = jnp.full_like(m_s, NEG_INF)
    acc[...] = jnp.zeros_like(acc)

    def scores(ki, slot):
        start = pl.multiple_of(ki * tk, tk)
        for c in range(chains):
            s_bufs[2 * c + slot][...] = _dot_t(ka[c, pl.ds(start, tk), :], qa[c])

    def softmax_pv(ki, slot, masked):
        for c, (ch, cb) in enumerate(chain_ids):
            s = s_bufs[2 * c + slot][...]
            if masked:
                kpos = ki * tk + lax.broadcasted_iota(jnp.int32, (tk, rows), 0)
                qpos = qi * tq + (lax.broadcasted_iota(jnp.int32, (tk, rows), 1) & (tq - 1))
                s = jnp.where(kpos <= qpos, s, NEG_INF)
            m_old = m_s[c]
            m_new = jnp.maximum(m_old, jnp.max(s, axis=0, keepdims=True))
            alpha = jnp.exp2(m_old - m_new)
            p = jnp.exp2(s - m_new)
            acc[c] = alpha * acc[c] + _dot(vt_ref[ch, cb, ki], p.astype(BF16))
            m_s[c] = m_new

    n_full = (qi * tq) >> _log2(tk)
    scores(0, 0)

    def pair(j, carry):
        k0 = 2 * j
        scores(k0 + 1, 1)
        softmax_pv(k0, 0, False)
        scores(k0 + 2, 0)
        softmax_pv(k0 + 1, 1, False)
        return carry

    lax.fori_loop(0, n_full >> 1, pair, 0)
    cur = (n_full >> 1) << 1

    @pl.when((n_full & 1) == 1)
    def _():
        scores(cur + 1, 1)
        softmax_pv(cur, 0, False)
        softmax_pv(cur + 1, 1, True)

    @pl.when((n_full & 1) == 0)
    def _():
        softmax_pv(cur, 0, True)

    outs = {}
    for c, (ch, cb) in enumerate(chain_ids):
        d = acc[c, HEAD_DIM:HEAD_DIM + 1, :]
        d = jnp.where(d > 0, d, 1.0)
        o_t = acc[c, 0:HEAD_DIM, :] * (1.0 / d)
        for g in range(group):
            og = o_t[:, g * tq:(g + 1) * tq].T
            if gate_ref is not None:
                gc = g * 3 + gate_branch
                og = og * gate_ref[ch, cb, :, gc:gc + 1]
            outs[(cb, ch * group + g)] = og
    for cb in range(chain_shape[1]):
        planes = [outs[(cb, hq)] for hq in range(chain_shape[0] * group)]
        if lanes_out:
            o_ref[0, cb] = jnp.concatenate(planes, axis=-1).astype(o_ref.dtype)
        else:
            for hq, og in enumerate(planes):
                o_ref[hq, cb] = og.astype(o_ref.dtype)


def _flash(q, bias, k, vt, *, group, tq, blk_len, q_scale, lanes_out, gate=None, gate_branch=None):
    nh, bsz, t_len, _ = k.shape
    ch = min(nh, FLASH_CHAINS)
    cb = max(1, FLASH_CHAINS // ch)
    cb = cb if bsz % cb == 0 else 1
    chains = ch * cb
    nt, tk = vt.shape[2], vt.shape[4]
    per_step = lambda h, b, i: (h, b, i, 0)
    whole_seq = lambda h, b, i: (h, b, 0, 0)
    in_specs = [
        pl.BlockSpec((ch * group, cb, tq, HEAD_DIM), per_step),
        pl.BlockSpec((ch, cb, tq, HEAD_DIM), per_step),
        pl.BlockSpec((ch, cb, t_len, HEAD_DIM), whole_seq),
        pl.BlockSpec((ch, cb, nt, V_ROWS, tk), lambda h, b, i: (h, b, 0, 0, 0)),
    ]
    args = [q, bias, k, vt]
    if gate is not None:
        in_specs.append(pl.BlockSpec((ch, cb, tq, LANES), per_step))
        args.append(gate)
    if lanes_out:
        out_shape = (1, bsz, t_len, nh * group * HEAD_DIM)
        out_spec = pl.BlockSpec((1, cb, tq, ch * group * HEAD_DIM), lambda h, b, i: (0, b, i, h))
    else:
        out_shape = (nh * group, bsz, t_len, HEAD_DIM)
        out_spec = pl.BlockSpec((ch * group, cb, tq, HEAD_DIM), per_step)
    rows = group * tq
    return pl.pallas_call(
        functools.partial(_flash_kernel, group=group, chain_shape=(ch, cb), lanes_out=lanes_out, tq=tq, tk=tk,
                          blk_len=blk_len, q_scale=q_scale, gate_branch=gate_branch),
        grid=(nh // ch, bsz // cb, t_len // tq),
        in_specs=in_specs,
        out_specs=out_spec,
        out_shape=jax.ShapeDtypeStruct(out_shape, BF16),
        scratch_shapes=[
            pltpu.VMEM((chains, rows, 2 * HEAD_DIM), BF16),
            pltpu.VMEM((chains, t_len, 2 * HEAD_DIM), BF16),
            pltpu.VMEM((chains, 1, rows), F32),
            pltpu.VMEM((chains, V_ROWS, rows), F32),
        ] + [pltpu.VMEM((tk, rows), F32)] * (2 * chains),
        compiler_params=_params("parallel", "parallel", "arbitrary"),
    )(*args)


def _even_delta(refs, scratch):
    oa_ref, oc_ref, os_ref, ow_ref, w_ref = refs
    na = oa_ref.shape[-1]
    ob = (oc_ref[...].astype(F32) + os_ref[...].astype(F32) + ow_ref[...].astype(F32)).astype(BF16)
    return _dot(oa_ref[...], w_ref[0:na, :]) + _dot(ob, w_ref[na:, :])


def _even_mix(oa, ocmp, osel, owin, w):
    aspec = pl.BlockSpec((FFN_TM, oa.shape[-1]), lambda i: (i, 0))
    return dict(args=[oa, ocmp, osel, owin, w], specs=[aspec] * 4 + [_resident(w.shape)], scratch=[],
                fn=_even_delta)


def _odd_delta(refs, scratch, *, dilations):
    ng = len(dilations)
    o_refs, l_refs = refs[:ng], refs[ng:2 * ng]
    od_ref, w_ref = refs[2 * ng:]
    scratch = list(scratch)
    tm = od_ref.shape[0]

    def tokens(ref, r):
        if r == 1:
            return ref[0, 0].astype(F32)
        pieces = []
        for c in range(ref.shape[-1] // LANES):
            scr = scratch.pop(0)
            for i in range(r):
                scr[pl.ds(i, tm // r, stride=r), :] = ref[0, i, :, c * LANES:(c + 1) * LANES].astype(F32)
            pieces.append(scr[...])
        return jnp.concatenate(pieces, axis=-1)

    outs = [tokens(ref, r) for ref, r in zip(o_refs, dilations)]
    lses = [tokens(ref, r) for ref, r in zip(l_refs, dilations)]
    mx = functools.reduce(jnp.maximum, lses)
    es = [jnp.exp(l - mx) for l in lses]
    tot = functools.reduce(lambda a, b: a + b, es)
    oc = functools.reduce(lambda a, b: a + b, [(e / tot) * o for e, o in zip(es, outs)])
    nc = oc.shape[-1]
    return _dot(oc.astype(BF16), w_ref[0:nc, :]) + _dot(od_ref[...], w_ref[nc:, :])


def _odd_mix(outs, lses, od, w, t_len):
    tm = FFN_TM
    nt = t_len // tm
    gw = outs[0].shape[-1]
    dilations = tuple(o.shape[1] for o in outs)
    gspecs = [pl.BlockSpec((1, r, tm // r, gw), lambda i: (i // nt, 0, i % nt, 0)) for r in dilations]
    n_scr = 2 * (gw // LANES) * sum(1 for r in dilations if r > 1)
    specs = gspecs + gspecs + [pl.BlockSpec((tm, od.shape[1]), lambda i: (i, 0)), _resident(w.shape)]
    return dict(args=[*outs, *lses, od, w], specs=specs, scratch=[pltpu.VMEM((tm, LANES), F32)] * n_scr,
                fn=functools.partial(_odd_delta, dilations=dilations))


def _col_ranges(sizes):
    offs, acc = [], 0
    for s in sizes:
        offs.append((acc, acc + s))
        acc += s
    return offs


def _even_mixer(h, gain, w_in, w_out, sinks, cmp_pe, cmp_w1, cmp_w2, cos, sin, bsz, t_len):
    m = bsz * t_len
    qa_w, kva_w = A_Q_HEADS * HEAD_DIM, A_KV_HEADS * HEAD_DIM
    qb_w, kvb_w = B_Q_HEADS * HEAD_DIM, B_KV_HEADS * HEAD_DIM
    sizes = [qa_w, kva_w, kva_w, qb_w] + [kvb_w] * 6 + [3 * B_Q_HEADS]
    (aq, ak, av, bq, bkc, bvc, bks, bvs, bkw, bvw, bg) = [w_in[:, a:b] for a, b in _col_ranges(sizes)]
    group = B_Q_HEADS // B_KV_HEADS
    gpad = jnp.zeros((D_MODEL, LANES - 3 * group), w_in.dtype)
    gates = [x for hk in range(B_KV_HEADS) for x in (bg[:, hk * 3 * group:(hk + 1) * 3 * group], gpad)]
    w = jnp.concatenate([aq, bq, ak, bkw, bks, av, bvw, bvs, bkc, bvc] + gates, axis=1).astype(BF16)
    c = [0]

    def take(width):
        c[0] += width
        return c[0] - width

    def out(idx, off, width, kind, **kw):
        return dict(idx=idx, off=off, width=width, kind=kind, **kw)

    plan = [
        (take(qa_w), qa_w, [out(0, 0, qa_w, "split", hw=HEAD_DIM, rope=True, scale=True)]),
        (take(qb_w), qb_w, [out(1, 0, qb_w, "split", hw=HEAD_DIM, rope=True, scale=True),
                            out(5, 0, qb_w, "flat", scale=True)]),
        (take(3 * kva_w), 3 * kva_w, [out(2, 0, kva_w, "flat", rope=True),
                                      out(3, kva_w, kvb_w, "flat", rope=True),
                                      out(4, kva_w + kvb_w, kvb_w, "split", hw=HEAD_DIM, rope=True)]),
        (take(5 * kvb_w), 5 * kvb_w, [out(6, 0, kva_w, "flat"), out(7, kva_w, kvb_w, "flat"),
                                      out(8, 2 * kvb_w, kvb_w, "vt", tile=FLASH_TK),
                                      out(9, 3 * kvb_w, kvb_w, "chunk"),
                                      out(10, 4 * kvb_w, kvb_w, "chunk")]),
        (take(B_KV_HEADS * LANES), B_KV_HEADS * LANES,
         [out(11, 0, B_KV_HEADS * LANES, "split", hw=LANES, sigmoid=True)]),
    ]
    out_defs = [("split", A_Q_HEADS, HEAD_DIM, BF16), ("split", B_Q_HEADS, HEAD_DIM, BF16), ("flat", kva_w, BF16),
                ("flat", kvb_w, BF16),
                ("split", B_KV_HEADS, HEAD_DIM, BF16), ("flat", qb_w, BF16), ("flat", kva_w, BF16),
                ("flat", kvb_w, BF16), ("vt", B_KV_HEADS, FLASH_TK, BF16), ("chunk", B_KV_HEADS, BF16),
                ("chunk", B_KV_HEADS, BF16), ("split", B_KV_HEADS, LANES, F32)]
    (aq_r, bq_r, ak_r, bkw_r, bks_r, bq_u, av_, bvw_, bvs_t, zk, zv, gate) = _proj(
        h, gain, w, cos, sin, plan, out_defs, bsz, t_len)

    def seq(x):
        return x.reshape(bsz, 1, t_len, x.shape[-1])

    def heads(x):
        return x.reshape(x.shape[0], bsz, t_len, x.shape[-1])

    gate4 = heads(gate)
    oa = _banded(heads(aq_r), seq(ak_r), seq(av_), n_kv=A_KV_HEADS, group=A_Q_HEADS // A_KV_HEADS,
                 max_dist=A_WINDOW - 1, tl=BAND_TL, pk=128, span=128, sink=sinks)
    nch = t_len // NSA_CMP_STRIDE
    zshape = (B_KV_HEADS, bsz, nch, NSA_CMP_STRIDE * HEAD_DIM)
    kc, vc = _compress(zk.reshape(zshape), zv.reshape(zshape), cmp_pe.reshape(2, 1, NSA_CMP_LEN * HEAD_DIM),
                       cmp_w1.astype(BF16), cmp_w2.astype(BF16))
    ocmp, bias = _cmp_attention(bq_u.reshape(bsz, t_len, qb_w), kc, vc, gate4, t_len)
    osel = _flash(heads(bq_r), bias, heads(bks_r), bvs_t, group=group, tq=FLASH_T, blk_len=NSA_SEL_LEN,
                  q_scale=1.0, lanes_out=True, gate=gate4, gate_branch=1)
    owin = _banded(heads(bq_r), seq(bkw_r), seq(bvw_), n_kv=B_KV_HEADS, group=group,
                   max_dist=NSA_WINDOW - 1, tl=BAND_TL, pk=512, span=512, gate=gate4, gate_branch=2)
    return _even_mix(oa.reshape(m, qa_w), ocmp.reshape(m, qb_w), osel.reshape(m, qb_w),
                     owin.reshape(m, qb_w), w_out.astype(BF16))


def _odd_mixer(h, gain, w_in, w_out, cos, sin, bsz, t_len):
    m = bsz * t_len
    cw = C_HEADS * HEAD_DIM
    dw = D_HEADS * HEAD_DIM
    gw = C_HEADS_PER_GROUP * HEAD_DIM
    n_groups = len(C_GROUPS)

    def group_outs(base, **kw):
        return [dict(idx=base + gi, off=gi * gw, width=gw, kind="dilate", r=r, **kw)
                for gi, (_, r) in enumerate(C_GROUPS)]

    plan = [
        (0, cw, group_outs(0, rope=True, scale=True)),
        (cw, cw, group_outs(n_groups, rope=True)),
        (2 * cw, cw, group_outs(2 * n_groups)),
        (3 * cw, 3 * dw, [dict(idx=3 * n_groups, off=0, width=dw, kind="split", hw=HEAD_DIM, rope=True),
                          dict(idx=3 * n_groups + 1, off=dw, width=dw, kind="split", hw=HEAD_DIM, rope=True),
                          dict(idx=3 * n_groups + 2, off=2 * dw, width=dw, kind="vt", tile=MOBA_TK)]),
    ]
    out_defs = [("dilate", r, gw, BF16) for _ in range(3) for _, r in C_GROUPS]
    out_defs += [("split", D_HEADS, HEAD_DIM, BF16), ("split", D_HEADS, HEAD_DIM, BF16),
                 ("vt", D_HEADS, MOBA_TK, BF16)]
    res = _proj(h, gain, w_in.astype(BF16), cos, sin, plan, out_defs, bsz, t_len)
    cq, ck, cv = res[0:n_groups], res[n_groups:2 * n_groups], res[2 * n_groups:3 * n_groups]
    dq, dk, dv_t = res[3 * n_groups:]
    outs, lses = [], []
    for gi, (wlen, r) in enumerate(C_GROUPS):
        o, lse = _banded(cq[gi], ck[gi], cv[gi], n_kv=C_HEADS_PER_GROUP, group=1, max_dist=wlen // r,
                         tl=min(BAND_TL, t_len // r), pk=128, span=128, want_lse=True)
        outs.append(o)
        lses.append(lse)

    def heads(x):
        return x.reshape(x.shape[0], bsz, t_len, x.shape[-1])

    bias = _moba_gate(heads(dq), heads(dk))
    od = _flash(heads(dq), bias, heads(dk), dv_t, group=1, tq=MOBA_T, blk_len=MOBA_BLOCK, q_scale=QK_SCALE,
                lanes_out=True)
    return _odd_mix(outs, lses, od.reshape(m, D_HEADS * HEAD_DIM), w_out.astype(BF16), t_len)


def kernel(x, ffn_norm_pre, mix_norm, ffn_norm_post, ffn_wi, ffn_wo, even_w_in, even_w_out, even_sinks,
           nsa_cmp_pe, nsa_cmp_w1, nsa_cmp_w2, odd_w_in, odd_w_out, final_norm):
    bsz, t_len, _ = x.shape
    depth = ffn_wi.shape[0]
    cos, sin = _rope_tables(t_len)
    wi = ffn_wi.astype(BF16)
    wo = ffn_wo.astype(BF16)
    h = x.reshape(bsz * t_len, D_MODEL)
    for layer in range(depth):
        i = layer // 2
        h = _ffn(h, ffn_norm_pre[layer], wi, wo, (layer, 0))
        if layer % 2 == 0:
            mix = _even_mixer(h, mix_norm[layer], even_w_in[i], even_w_out[i], even_sinks[i], nsa_cmp_pe[i],
                              nsa_cmp_w1[i], nsa_cmp_w2[i], cos, sin, bsz, t_len)
        else:
            mix = _odd_mixer(h, mix_norm[layer], odd_w_in[i], odd_w_out[i], cos, sin, bsz, t_len)
        last = layer == depth - 1
        h = _ffn(h, ffn_norm_post[layer], wi, wo, (layer, 1), final_norm if last else None, mix=mix)
    return h.reshape(bsz, t_len, D_MODEL)
```

```python
import functools

import jax
import jax.numpy as jnp
from jax import lax
from jax.experimental import pallas as pl
from jax.experimental.pallas import tpu as pltpu

D_MODEL = 1024
HEAD_DIM = 64
ROPE_THETA = 10000.0
NORM_EPS = 1e-6
D_FF = 2816
NEG_INF = -1e30
FORCE_SCORE = 1e4

A_Q_HEADS = 8
A_KV_HEADS = 2
A_WINDOW = 128
B_Q_HEADS = 8
B_KV_HEADS = 2
NSA_CMP_LEN = 32
NSA_CMP_STRIDE = 16
NSA_CMP_HIDDEN = 256
NSA_SEL_LEN = 64
NSA_SEL_COUNT = 8
NSA_WINDOW = 512
C_GROUPS = ((128, 1), (512, 4), (2048, 16))
C_HEADS_PER_GROUP = 4
C_HEADS = len(C_GROUPS) * C_HEADS_PER_GROUP
D_HEADS = 4
MOBA_BLOCK = 256
MOBA_TOPK = 3

LANES = 128
LOG2E = 1.4426950408889634
LN2 = 0.6931471805599453
QK_SCALE = HEAD_DIM ** -0.5 * LOG2E
MASK_BIAS = -32768.0
V_ROWS = HEAD_DIM + 16
REMOVED = -3e38
VMEM_LIMIT = 52 * 1024 * 1024

FFN_TM = 512
FFN_TF = 256
PROJ_TM = 1024
BAND_TL = 1024
BAND_SB = 128
FLASH_T = 256
FLASH_TK = 256
FLASH_CHAINS = 4
MOBA_T = 512
MOBA_TK = 512
GATE_TQ = 4096
CMP_TQ = 1024

BF16 = jnp.bfloat16
F32 = jnp.float32


def _params(*sem):
    return pltpu.CompilerParams(dimension_semantics=sem, vmem_limit_bytes=VMEM_LIMIT)


def _rms(x, g):
    return x * lax.rsqrt(jnp.mean(x * x, axis=-1, keepdims=True) + NORM_EPS) * g


def _log2(n):
    assert n & (n - 1) == 0
    return n.bit_length() - 1


def _dot(a, b):
    return jnp.dot(a, b, preferred_element_type=F32)


def _dot_t(a, b):
    return lax.dot_general(a, b, (((1,), (1,)), ((), ())), preferred_element_type=F32)


def _dot_tn(a, b):
    return lax.dot_general(a, b, (((0,), (0,)), ((), ())), preferred_element_type=F32)


def _ffn_kernel(x_ref, g_ref, wi_ref, wo_ref, *rest, final, mix_fn, n_mix):
    rest = list(rest)
    fg_ref = rest.pop(0) if final else None
    mix_refs = [rest.pop(0) for _ in range(n_mix)]
    o_ref, act_scr = rest.pop(0), rest.pop(0)
    x = x_ref[...]
    if mix_fn is not None:
        x = x + mix_fn(mix_refs, rest)
    o_ref[...] = x
    n = _rms(x, g_ref[...]).astype(BF16)
    for c in range(D_FF // FFN_TF):
        cols = slice(c * FFN_TF, (c + 1) * FFN_TF)
        gate = _dot(n, wi_ref[:, cols])
        up = _dot(n, wi_ref[:, D_FF + c * FFN_TF:D_FF + (c + 1) * FFN_TF])
        act_scr[:, cols] = (gate * jax.nn.sigmoid(gate) * up).astype(BF16)
    h = o_ref[...] + 0.5 * _dot(act_scr[...], wo_ref[...])
    if final:
        h = _rms(h, fg_ref[...])
    o_ref[...] = h


def _resident(shape, lead=()):
    block = (None,) * len(lead) + tuple(shape[len(lead):])
    index = tuple(lead) + (0,) * (len(shape) - len(lead))
    return pl.BlockSpec(block, lambda i: index, pipeline_mode=pl.Buffered(1))


def _ffn(h, gain, wi, wo, which, final_gain=None, mix=None):
    m = h.shape[0]
    tm = FFN_TM
    final = final_gain is not None
    in_specs = [
        pl.BlockSpec((tm, D_MODEL), lambda i: (i, 0)),
        pl.BlockSpec((1, D_MODEL), lambda i: (0, 0)),
        _resident(wi.shape, which),
        _resident(wo.shape, which),
    ]
    args = [h, gain.reshape(1, D_MODEL), wi, wo]
    if final:
        in_specs.append(pl.BlockSpec((1, D_MODEL), lambda i: (0, 0)))
        args.append(final_gain.reshape(1, D_MODEL))
    scratch = [pltpu.VMEM((tm, D_FF), BF16)]
    if mix is not None:
        in_specs += mix["specs"]
        args += mix["args"]
        scratch += mix["scratch"]
    return pl.pallas_call(
        functools.partial(_ffn_kernel, final=final, mix_fn=mix and mix["fn"], n_mix=len(mix["args"]) if mix else 0),
        grid=(m // tm,),
        in_specs=in_specs,
        out_specs=pl.BlockSpec((tm, D_MODEL), lambda i: (i, 0)),
        out_shape=jax.ShapeDtypeStruct((m, D_MODEL), F32),
        scratch_shapes=scratch,
        compiler_params=_params("parallel"),
    )(*args)


def _rope_tables(t):
    inv = 1.0 / (ROPE_THETA ** (jnp.arange(0, HEAD_DIM, 2, dtype=F32) / HEAD_DIM))
    ang = jnp.arange(t, dtype=F32)[:, None] * inv[None, :]
    cos = jnp.cos(ang)
    sin = jnp.sin(ang)
    return (jnp.concatenate([cos, cos, cos, cos], axis=-1),
            jnp.concatenate([-sin, sin, -sin, sin], axis=-1))


def _proj_kernel(x_ref, g_ref, w_ref, cos_ref, sin_ref, *rest, plan, n_out):
    out_refs, (ys,) = rest[:n_out], rest[n_out:]
    n = _rms(x_ref[...], g_ref[...]).astype(BF16)
    tm = n.shape[0]
    cos = cos_ref[...]
    sin = sin_ref[...]
    lane = lax.broadcasted_iota(jnp.int32, cos.shape, 1)
    first_half = (lane & (HEAD_DIM - 1)) < HEAD_DIM // 2
    heads_per_tile = LANES // HEAD_DIM
    for off, width, outs in plan:
        y = _dot(n, w_ref[:, off:off + width])
        for o in outs:
            o_ref = out_refs[o["idx"]]
            dt = o_ref.dtype
            kind = o["kind"]
            for c in range(o["width"] // LANES):
                piece = y[:, o["off"] + c * LANES: o["off"] + (c + 1) * LANES]
                if o.get("rope"):
                    rot = jnp.where(first_half, pltpu.roll(piece, LANES - HEAD_DIM // 2, 1),
                                    pltpu.roll(piece, HEAD_DIM // 2, 1))
                    piece = piece * cos + rot * sin
                if o.get("scale"):
                    piece = piece * QK_SCALE
                if o.get("sigmoid"):
                    piece = jax.nn.sigmoid(piece)
                lanes = slice(c * LANES, (c + 1) * LANES)
                if kind == "flat":
                    o_ref[:, lanes] = piece.astype(dt)
                elif kind == "split":
                    hw = o["hw"]
                    per = LANES // hw
                    for k in range(per):
                        o_ref[c * per + k] = piece[:, k * hw:(k + 1) * hw].astype(dt)
                elif kind == "dilate":
                    r = o["r"]
                    if r == 1:
                        o_ref[0, 0, :, lanes] = piece.astype(dt)
                    else:
                        ys[...] = piece
                        for i in range(r):
                            o_ref[0, i, :, lanes] = ys[pl.ds(i, tm // r, stride=r), :].astype(dt)
                elif kind == "chunk":
                    ys[...] = piece
                    for l in range(NSA_CMP_STRIDE):
                        rows = ys[pl.ds(l, tm // NSA_CMP_STRIDE, stride=NSA_CMP_STRIDE), :].astype(dt)
                        for k in range(heads_per_tile):
                            o_ref[c * heads_per_tile + k, :, l * HEAD_DIM:(l + 1) * HEAD_DIM] = rows[
                                :, k * HEAD_DIM:(k + 1) * HEAD_DIM]
                else:
                    tile = o["tile"]
                    pt = piece.T
                    for k in range(heads_per_tile):
                        for kt in range(tm // tile):
                            o_ref[c * heads_per_tile + k, 0, kt, 0:HEAD_DIM, :] = pt[
                                k * HEAD_DIM:(k + 1) * HEAD_DIM, kt * tile:(kt + 1) * tile].astype(dt)
                            o_ref[c * heads_per_tile + k, 0, kt, HEAD_DIM:V_ROWS, :] = jnp.ones(
                                (V_ROWS - HEAD_DIM, tile), dt)


def _proj(h, gain, w, cos, sin, plan, out_defs, bsz, t_len):
    m = h.shape[0]
    tm = PROJ_TM
    nt = t_len // tm
    out_shapes, out_specs = [], []
    for d in out_defs:
        kind, dt = d[0], d[-1]
        if kind == "flat":
            shape, block, index = (m, d[1]), (tm, d[1]), (lambda i: (i, 0))
        elif kind == "split":
            shape, block, index = (d[1], m, d[2]), (d[1], tm, d[2]), (lambda i: (0, i, 0))
        elif kind == "dilate":
            r = d[1]
            shape, block = (bsz, r, t_len // r, d[2]), (1, r, tm // r, d[2])
            index = lambda i: (i // nt, 0, i % nt, 0)
        elif kind == "chunk":
            width = NSA_CMP_STRIDE * HEAD_DIM
            shape, block = (d[1], m // NSA_CMP_STRIDE, width), (d[1], tm // NSA_CMP_STRIDE, width)
            index = lambda i: (0, i, 0)
        else:
            tile = d[2]
            shape, block = (d[1], bsz, t_len // tile, V_ROWS, tile), (d[1], 1, tm // tile, V_ROWS, tile)
            index = lambda i: (0, i // nt, i % nt, 0, 0)
        out_shapes.append(jax.ShapeDtypeStruct(shape, dt))
        out_specs.append(pl.BlockSpec(block, index))
    return pl.pallas_call(
        functools.partial(_proj_kernel, plan=plan, n_out=len(out_defs)),
        grid=(m // tm,),
        in_specs=[
            pl.BlockSpec((tm, D_MODEL), lambda i: (i, 0)),
            pl.BlockSpec((1, D_MODEL), lambda i: (0, 0)),
            pl.BlockSpec(w.shape, lambda i: (0, 0)),
            pl.BlockSpec((tm, LANES), lambda i: (i % nt, 0)),
            pl.BlockSpec((tm, LANES), lambda i: (i % nt, 0)),
        ],
        out_specs=out_specs,
        out_shape=out_shapes,
        scratch_shapes=[pltpu.VMEM((tm, LANES), F32)],
        compiler_params=_params("parallel"),
    )(h, gain.reshape(1, D_MODEL), w, cos, sin)


def _banded_kernel(q_ref, kp_ref, kc_ref, vp_ref, vc_ref, *rest, n_kv, group, tl, pk, span, max_dist,
                   has_sink, gate_branch, want_lse):
    rest = list(rest)
    sink_ref = rest.pop(0) if has_sink else None
    gate_ref = rest.pop(0) if gate_branch is not None else None
    o_ref = rest.pop(0)
    lse_ref = rest.pop(0) if want_lse else None
    kcat, vcat, s_a, s_b, band = rest
    sb = BAND_SB
    n_seq = kc_ref.shape[1]
    for rr in range(n_seq):
        kcat[rr, 0:pk, :] = kp_ref[0, rr]
        kcat[rr, pk:, :] = kc_ref[0, rr]
        vcat[rr, 0:pk, :] = vp_ref[0, rr]
        vcat[rr, pk:, :] = vc_ref[0, rr]
    nk = span + sb
    kw = n_kv * HEAD_DIM
    merged = group == 1
    stacked = n_kv if merged else group
    rows = stacked * sb
    key = lax.broadcasted_iota(jnp.int32, (nk, rows), 0)
    qry = lax.broadcasted_iota(jnp.int32, (nk, rows), 1) & (sb - 1)
    dist = qry + span - key
    in_band = (dist >= 0) & (dist <= max_dist)
    lane_head = lax.broadcasted_iota(jnp.int32, (1, rows), 1) >> _log2(sb)
    q_head = lax.broadcasted_iota(jnp.int32, (sb, kw), 1) >> _log2(HEAD_DIM)
    v_head = lax.broadcasted_iota(jnp.int32, (nk, kw), 1) >> _log2(HEAD_DIM)
    assert merged or kw == LANES
    items = [(rr, j, h) for rr in range(n_seq) for j in range(tl // sb)
             for h in ([0] if merged else range(n_kv))]
    s_bufs = (s_a, s_b)

    def compute(first_tile):
        def scores(n):
            rr, j, h = items[n]
            lo = pk + j * sb - span
            qrows = slice(j * sb, (j + 1) * sb)
            if merged:
                qj = q_ref[0, rr, qrows, :].astype(F32)
                qs = jnp.concatenate([jnp.where(q_head == r, qj, 0.0).astype(BF16) for r in range(n_kv)], axis=0)
                kk = kcat[rr, lo:lo + nk, :]
            else:
                qs = jnp.concatenate([q_ref[h * group + g, 0, qrows, :] for g in range(group)], axis=0)
                kk = kcat[rr, lo:lo + nk, h * HEAD_DIM:(h + 1) * HEAD_DIM]
            variant = 1 + j if first_tile and j * sb < span else 0
            s_bufs[n % 2][...] = _dot_t(kk, qs) + band[variant]

        band[0] = jnp.where(in_band, 0.0, NEG_INF)
        if first_tile:
            for j in range(span // sb):
                band[1 + j] = jnp.where(in_band & (key >= span - j * sb), 0.0, NEG_INF)

        outs, lses = [], []
        scores(0)
        for n, (rr, j, h) in enumerate(items):
            if n + 1 < len(items):
                scores(n + 1)
            s = s_bufs[n % 2][...]
            lo = pk + j * sb - span
            qrows = slice(j * sb, (j + 1) * sb)
            m = jnp.max(s, axis=0, keepdims=True)
            if has_sink:
                sink = jnp.full((1, rows), sink_ref[h * group], F32)
                for g in range(1, group):
                    sink = jnp.where(lane_head == g, sink_ref[h * group + g], sink)
                sink = sink * LOG2E
                m = jnp.maximum(m, sink)
            p = jnp.exp2(s - m).astype(BF16)
            vwin = vcat[rr, lo:lo + nk, :]
            if merged:
                vv = jnp.concatenate([vwin, jnp.ones((nk, LANES), BF16)], axis=1)
                ones_row = kw
            else:
                vv = jnp.where(v_head == h, vwin.astype(F32), 1.0).astype(BF16)
                ones_row = (1 - h) * HEAD_DIM
            o_t = _dot_tn(vv, p)
            d = o_t[ones_row:ones_row + 1, :]
            if has_sink:
                d = d + jnp.exp2(sink - m)
            d = jnp.where(d > 0, d, 1.0)
            inv = 1.0 / d
            if want_lse:
                lse_t = jnp.broadcast_to(m * LN2 + jnp.log(d), (HEAD_DIM, rows))
            for r in range(stacked):
                cols = slice(r * sb, (r + 1) * sb)
                vrow = (r if merged else h) * HEAD_DIM
                o = (o_t[vrow:vrow + HEAD_DIM, cols] * inv[:, cols]).T
                if gate_ref is not None:
                    gc = r * 3 + gate_branch
                    o = o * gate_ref[h, 0, qrows, gc:gc + 1]
                outs.append(o)
                if want_lse:
                    lses.append(lse_t[:, cols].T)
            if merged or h == n_kv - 1:
                o_ref[0, rr, qrows, :] = jnp.concatenate(outs, axis=-1).astype(o_ref.dtype)
                if want_lse:
                    lse_ref[0, rr, qrows, :] = jnp.concatenate(lses, axis=-1)
                outs, lses = [], []

    first = pl.program_id(2) == 0
    pl.when(first)(functools.partial(compute, True))
    pl.when(jnp.logical_not(first))(functools.partial(compute, False))


def _banded(q, k, v, *, n_kv, group, max_dist, tl, pk, span, sink=None, gate=None, gate_branch=None,
            want_lse=False):
    bsz, n_res, length, kw = k.shape
    qw = n_kv * group * HEAD_DIM
    s_shape = (span + BAND_SB, (n_kv if group == 1 else group) * BAND_SB)
    ratio = tl // pk
    n_seq = min(n_res, max(1, BAND_TL // tl))
    prev = lambda b, i, l: (b, i, jnp.maximum(l * ratio - 1, 0), 0)
    cur = lambda b, i, l: (b, i, l, 0)
    if group == 1:
        q_spec = pl.BlockSpec((1, n_seq, tl, qw), cur)
    else:
        q_spec = pl.BlockSpec((n_kv * group, 1, tl, HEAD_DIM), lambda b, i, l: (0, b, l, 0))
    o_shape = (bsz, n_res, length, qw)
    in_specs = [
        q_spec,
        pl.BlockSpec((1, n_seq, pk, kw), prev),
        pl.BlockSpec((1, n_seq, tl, kw), cur),
        pl.BlockSpec((1, n_seq, pk, kw), prev),
        pl.BlockSpec((1, n_seq, tl, kw), cur),
    ]
    args = [q, k, k, v, v]
    if sink is not None:
        in_specs.append(pl.BlockSpec(memory_space=pltpu.SMEM))
        args.append(sink)
    if gate is not None:
        in_specs.append(pl.BlockSpec((n_kv, 1, tl, LANES), lambda b, i, l: (0, b, l, 0)))
        args.append(gate)
    out_spec = pl.BlockSpec((1, n_seq, tl, qw), cur)
    out_shape = [jax.ShapeDtypeStruct(o_shape, BF16)]
    out_specs = [out_spec]
    if want_lse:
        out_shape.append(jax.ShapeDtypeStruct(o_shape, F32))
        out_specs.append(out_spec)
    res = pl.pallas_call(
        functools.partial(_banded_kernel, n_kv=n_kv, group=group, tl=tl, pk=pk, span=span,
                          max_dist=max_dist, has_sink=sink is not None, gate_branch=gate_branch,
                          want_lse=want_lse),
        grid=(bsz, n_res // n_seq, length // tl),
        in_specs=in_specs,
        out_specs=out_specs,
        out_shape=out_shape,
        scratch_shapes=[pltpu.VMEM((n_seq, pk + tl, kw), BF16), pltpu.VMEM((n_seq, pk + tl, kw), BF16),
                        pltpu.VMEM(s_shape, F32), pltpu.VMEM(s_shape, F32),
                        pltpu.VMEM((1 + span // BAND_SB,) + s_shape, F32)],
        compiler_params=_params("parallel", "parallel", "parallel"),
    )(*args)
    return res if want_lse else res[0]


def _compress_one(z_ref, pe_ref, w1_ref, w2_ref, o_ref):
    half = NSA_CMP_STRIDE * HEAD_DIM
    z = z_ref[0, 0].astype(F32)
    za = (z + pe_ref[0, :, 0:half]).astype(BF16)
    zb = (z + pe_ref[0, :, half:2 * half]).astype(BF16)
    ya = _dot(za, w1_ref[0, 0:half, :])
    yb = _dot(zb, w1_ref[0, half:2 * half, :])
    nrow = ya.shape[0]
    hid = jax.nn.gelu(ya + pltpu.roll(yb, nrow - 1, 0))
    o_ref[0, 0] = _dot(hid.astype(BF16), w2_ref[0]).astype(o_ref.dtype)


def _compress_kernel(zk_ref, zv_ref, pek_ref, pev_ref, w1k_ref, w1v_ref, w2k_ref, w2v_ref, ok_ref, ov_ref):
    _compress_one(zk_ref, pek_ref, w1k_ref, w2k_ref, ok_ref)
    _compress_one(zv_ref, pev_ref, w1v_ref, w2v_ref, ov_ref)


def _compress(zk, zv, pe, w1, w2):
    hk, bsz, nch, width = zk.shape
    zspec = pl.BlockSpec((1, 1, nch, width), lambda h, b: (h, b, 0, 0))
    ospec = pl.BlockSpec((1, 1, nch, HEAD_DIM), lambda h, b: (h, b, 0, 0))
    oshape = jax.ShapeDtypeStruct((hk, bsz, nch, HEAD_DIM), BF16)

    def wspec(arr, which):
        return pl.BlockSpec((1,) + arr.shape[1:], lambda h, b: (which, 0, 0))

    return pl.pallas_call(
        _compress_kernel,
        grid=(hk, bsz),
        in_specs=[zspec, zspec, wspec(pe, 0), wspec(pe, 1), wspec(w1, 0), wspec(w1, 1), wspec(w2, 0), wspec(w2, 1)],
        out_specs=[ospec, ospec],
        out_shape=[oshape, oshape],
        compiler_params=_params("parallel", "parallel"),
    )(zk, zv, pe, pe, w1, w1, w2, w2)


def _topk_select(score, k, idxf):
    ncand = score.shape[0]
    rem = score
    sel = jnp.zeros_like(score)
    for _ in range(k):
        mx = jnp.max(rem, axis=0, keepdims=True)
        first = jnp.min(jnp.where(rem == mx, idxf, float(ncand)), axis=0, keepdims=True)
        pick = idxf == first
        sel = jnp.where(pick, jnp.where(mx > 0.5 * NEG_INF, 1.0, 0.0), sel)
        rem = jnp.where(pick, REMOVED, rem)
    return sel


def _cmp_kernel(q_ref, kc_ref, vc_ref, gate_ref, o_ref, bias_ref, *, tq, group, n_tiles):
    qi = pl.program_id(2)
    for v in range(n_tiles):
        ncp = min(kc_ref.shape[2], (v + 1) * tq // NSA_CMP_STRIDE)
        pl.when(qi == v)(functools.partial(_cmp_tile, q_ref, kc_ref, vc_ref, gate_ref, o_ref, bias_ref, qi,
                                           tq=tq, group=group, ncp=ncp))


def _cmp_tile(q_ref, kc_ref, vc_ref, gate_ref, o_ref, bias_ref, qi, *, tq, group, ncp):
    kc = kc_ref[0, 0, 0:ncp, :]
    vc = vc_ref[0, 0, 0:ncp, :]
    nsb = bias_ref.shape[-1]
    rows = group * tq
    last = lax.broadcasted_iota(jnp.int32, (ncp, 1), 0) * NSA_CMP_STRIDE + (NSA_CMP_LEN - 1)
    t_in_tile = lax.broadcasted_iota(jnp.int32, (1, rows), 1) & (tq - 1)
    visible = (last - t_in_tile) <= qi * tq
    qs = jnp.concatenate([q_ref[0, :, g * HEAD_DIM:(g + 1) * HEAD_DIM] for g in range(group)], axis=0)
    s = jnp.where(visible, _dot_t(kc, qs), NEG_INF)
    m = jnp.max(s, axis=0, keepdims=True)
    m = jnp.where(m > 0.5 * NEG_INF, m, 0.0)
    p = jnp.exp2(s - m)
    vc_ones = jnp.concatenate([vc, jnp.ones_like(vc)], axis=1)
    o_t = _dot_tn(vc_ones, p.astype(BF16))
    d = o_t[HEAD_DIM:HEAD_DIM + 1, :]
    d = jnp.where(d > 0, d, 1.0)
    inv = 1.0 / d
    pc = p * inv
    o_t = o_t[0:HEAD_DIM, :] * inv
    outs = []
    psum = pc[:, 0:tq]
    for g in range(group):
        if g:
            psum = psum + pc[:, g * tq:(g + 1) * tq]
        outs.append(o_t[:, g * tq:(g + 1) * tq].T * gate_ref[0, 0, :, g * 3:g * 3 + 1])
    o_ref[0] = jnp.concatenate(outs, axis=-1).astype(o_ref.dtype)

    jj = lax.broadcasted_iota(jnp.int32, (nsb, ncp), 0) * NSA_SEL_LEN
    nn = lax.broadcasted_iota(jnp.int32, (nsb, ncp), 1) * NSA_CMP_STRIDE
    ov = jnp.maximum(jnp.minimum(nn + NSA_CMP_LEN, jj + NSA_SEL_LEN) - jnp.maximum(nn, jj), 0)
    ov = (ov.astype(F32) / NSA_CMP_LEN).astype(BF16)
    p_hi = psum.astype(BF16)
    p_lo = (psum - p_hi.astype(F32)).astype(BF16)
    imp = _dot(ov, p_hi) + _dot(ov, p_lo)

    j = lax.broadcasted_iota(jnp.int32, (nsb, tq), 0)
    tt = qi * tq + lax.broadcasted_iota(jnp.int32, (nsb, tq), 1)
    cb = tt >> _log2(NSA_SEL_LEN)
    forced = (j == 0) | (j == cb) | (j == cb - 1)
    score = jnp.where(j <= cb, jnp.where(forced, FORCE_SCORE, imp), NEG_INF)
    sel = _topk_select(score, min(NSA_SEL_COUNT, nsb), j.astype(F32))
    bias_ref[0, 0] = jnp.where(sel > 0.5, 0.0, MASK_BIAS).T.astype(bias_ref.dtype)


def _cmp_attention(qu, kc, vc, gate, t_len):
    bsz = qu.shape[0]
    hk, _, ncp, _ = kc.shape
    group = B_Q_HEADS // B_KV_HEADS
    tq = CMP_TQ
    nsb = t_len // NSA_SEL_LEN
    gw = group * HEAD_DIM
    cspec = pl.BlockSpec((1, 1, ncp, HEAD_DIM), lambda b, h, i: (h, b, 0, 0))
    return pl.pallas_call(
        functools.partial(_cmp_kernel, tq=tq, group=group, n_tiles=t_len // tq),
        grid=(bsz, hk, t_len // tq),
        in_specs=[
            pl.BlockSpec((1, tq, gw), lambda b, h, i: (b, i, h)),
            cspec, cspec,
            pl.BlockSpec((1, 1, tq, LANES), lambda b, h, i: (h, b, i, 0)),
        ],
        out_specs=[
            pl.BlockSpec((1, tq, gw), lambda b, h, i: (b, i, h)),
            pl.BlockSpec((1, 1, tq, nsb), lambda b, h, i: (h, b, i, 0)),
        ],
        out_shape=[
            jax.ShapeDtypeStruct((bsz, t_len, hk * gw), BF16),
            jax.ShapeDtypeStruct((hk, bsz, t_len, nsb), BF16),
        ],
        compiler_params=_params("parallel", "parallel", "parallel"),
    )(qu, kc, vc, gate)


def _moba_gate_kernel(q_ref, k_ref, bias_ref, km_hi, km_lo, *, tq):
    qi = pl.program_id(2)
    t_len = k_ref.shape[2]
    ncol = bias_ref.shape[-1]
    nb = t_len // MOBA_BLOCK

    @pl.when(qi == 0)
    def _():
        blk = lax.broadcasted_iota(jnp.int32, (ncol, t_len), 0)
        pos = lax.broadcasted_iota(jnp.int32, (ncol, t_len), 1)
        member = jnp.where((pos >> _log2(MOBA_BLOCK)) == blk, 1.0, 0.0).astype(BF16)
        kmean = _dot(member, k_ref[0, 0]) * (1.0 / MOBA_BLOCK)
        hi = kmean.astype(BF16)
        km_hi[...] = hi
        km_lo[...] = (kmean - hi.astype(F32)).astype(BF16)

    q = q_ref[0, 0]
    gate = (_dot_t(km_hi[...], q) + _dot_t(km_lo[...], q))[0:nb]
    j = lax.broadcasted_iota(jnp.int32, (nb, tq), 0)
    t = qi * tq + lax.broadcasted_iota(jnp.int32, (nb, tq), 1)
    cb = t >> _log2(MOBA_BLOCK)
    score = jnp.where(j < cb, gate, NEG_INF)
    sel = _topk_select(score, min(MOBA_TOPK, nb - 1), j.astype(F32))
    keep = (sel > 0.5) | (j == cb)
    bias = jnp.where(keep, 0.0, MASK_BIAS)
    bias = jnp.concatenate([bias, jnp.zeros((ncol - nb, tq), F32)], axis=0)
    bias_ref[0, 0] = bias.T.astype(bias_ref.dtype)


def _moba_gate(q, k):
    nh, bsz, t_len, _ = q.shape
    tq = min(GATE_TQ, t_len)
    return pl.pallas_call(
        functools.partial(_moba_gate_kernel, tq=tq),
        grid=(nh, bsz, t_len // tq),
        in_specs=[
            pl.BlockSpec((1, 1, tq, HEAD_DIM), lambda h, b, i: (h, b, i, 0)),
            pl.BlockSpec((1, 1, t_len, HEAD_DIM), lambda h, b, i: (h, b, 0, 0)),
        ],
        out_specs=pl.BlockSpec((1, 1, tq, HEAD_DIM), lambda h, b, i: (h, b, i, 0)),
        out_shape=jax.ShapeDtypeStruct((nh, bsz, t_len, HEAD_DIM), BF16),
        scratch_shapes=[pltpu.VMEM((HEAD_DIM, HEAD_DIM), BF16), pltpu.VMEM((HEAD_DIM, HEAD_DIM), BF16)],
        compiler_params=_params("parallel", "parallel", "arbitrary"),
    )(q, k)


def _flash_kernel(q_ref, bias_ref, k_ref, vt_ref, *rest, group, chain_shape, lanes_out, tq, tk, blk_len, q_scale,
                  gate_branch):
    rest = list(rest)
    gate_ref = rest.pop(0) if gate_branch is not None else None
    o_ref, qa, ka, m_s, acc = rest[:5]
    s_bufs = rest[5:]
    chain_ids = [(ch, cb) for cb in range(chain_shape[1]) for ch in range(chain_shape[0])]
    chains = len(chain_ids)
    qi = pl.program_id(2)
    rows = group * tq
    t_len = k_ref.shape[2]

    @pl.when(qi == 0)
    def _():
        kpos = lax.broadcasted_iota(jnp.int32, (t_len, HEAD_DIM), 0)
        kblk = lax.broadcasted_iota(jnp.int32, (t_len, HEAD_DIM), 1)
        onehot = jnp.where((kpos >> _log2(blk_len)) == kblk, 1.0, 0.0).astype(BF16)
        for c, (ch, cb) in enumerate(chain_ids):
            ka[c, :, 0:HEAD_DIM] = k_ref[ch, cb]
            ka[c, :, HEAD_DIM:2 * HEAD_DIM] = onehot

    for c, (ch, cb) in enumerate(chain_ids):
        bias = bias_ref[ch, cb]
        for g in range(group):
            qg = q_ref[ch * group + g, cb]
            if q_scale != 1.0:
                qg = (qg.astype(F32) * q_scale).astype(BF16)
            qa[c, g * tq:(g + 1) * tq, 0:HEAD_DIM] = qg
            qa[c, g * tq:(g + 1) * tq, HEAD_DIM:2 * HEAD_DIM] = bias
    m_s[...] = jnp.full_like(m_s, NEG_INF)
    acc[...] = jnp.zeros_like(acc)

    def scores(ki, slot):
        start = pl.multiple_of(ki * tk, tk)
        for c in range(chains):
            s_bufs[2 * c + slot][...] = _dot_t(ka[c, pl.ds(start, tk), :], qa[c])

    def softmax_pv(ki, slot, masked):
        for c, (ch, cb) in enumerate(chain_ids):
            s = s_bufs[2 * c + slot][...]
            if masked:
                kpos = ki * tk + lax.broadcasted_iota(jnp.int32, (tk, rows), 0)
                qpos = qi * tq + (lax.broadcasted_iota(jnp.int32, (tk, rows), 1) & (tq - 1))
                s = jnp.where(kpos <= qpos, s, NEG_INF)
            m_old = m_s[c]
            m_new = jnp.maximum(m_old, jnp.max(s, axis=0, keepdims=True))
            alpha = jnp.exp2(m_old - m_new)
            p = jnp.exp2(s - m_new)
            acc[c] = alpha * acc[c] + _dot(vt_ref[ch, cb, ki], p.astype(BF16))
            m_s[c] = m_new

    n_full = (qi * tq) >> _log2(tk)
    scores(0, 0)

    def pair(j, carry):
        k0 = 2 * j
        scores(k0 + 1, 1)
        softmax_pv(k0, 0, False)
        scores(k0 + 2, 0)
        softmax_pv(k0 + 1, 1, False)
        return carry

    lax.fori_loop(0, n_full >> 1, pair, 0)
    cur = (n_full >> 1) << 1

    @pl.when((n_full & 1) == 1)
    def _():
        scores(cur + 1, 1)
        softmax_pv(cur, 0, False)
        softmax_pv(cur + 1, 1, True)

    @pl.when((n_full & 1) == 0)
    def _():
        softmax_pv(cur, 0, True)

    outs = {}
    for c, (ch, cb) in enumerate(chain_ids):
        d = acc[c, HEAD_DIM:HEAD_DIM + 1, :]
        d = jnp.where(d > 0, d, 1.0)
        o_t = acc[c, 0:HEAD_DIM, :] * (1.0 / d)
        for g in range(group):
            og = o_t[:, g * tq:(g + 1) * tq].T
            if gate_ref is not None:
                gc = g * 3 + gate_branch
                og = og * gate_ref[ch, cb, :, gc:gc + 1]
            outs[(cb, ch * group + g)] = og
    for cb in range(chain_shape[1]):
        planes = [outs[(cb, hq)] for hq in range(chain_shape[0] * group)]
        if lanes_out:
            o_ref[0, cb] = jnp.concatenate(planes, axis=-1).astype(o_ref.dtype)
        else:
            for hq, og in enumerate(planes):
                o_ref[hq, cb] = og.astype(o_ref.dtype)


def _flash(q, bias, k, vt, *, group, tq, blk_len, q_scale, lanes_out, gate=None, gate_branch=None):
    nh, bsz, t_len, _ = k.shape
    ch = min(nh, FLASH_CHAINS)
    cb = max(1, FLASH_CHAINS // ch)
    cb = cb if bsz % cb == 0 else 1
    chains = ch * cb
    nt, tk = vt.shape[2], vt.shape[4]
    per_step = lambda h, b, i: (h, b, i, 0)
    whole_seq = lambda h, b, i: (h, b, 0, 0)
    in_specs = [
        pl.BlockSpec((ch * group, cb, tq, HEAD_DIM), per_step),
        pl.BlockSpec((ch, cb, tq, HEAD_DIM), per_step),
        pl.BlockSpec((ch, cb, t_len, HEAD_DIM), whole_seq),
        pl.BlockSpec((ch, cb, nt, V_ROWS, tk), lambda h, b, i: (h, b, 0, 0, 0)),
    ]
    args = [q, bias, k, vt]
    if gate is not None:
        in_specs.append(pl.BlockSpec((ch, cb, tq, LANES), per_step))
        args.append(gate)
    if lanes_out:
        out_shape = (1, bsz, t_len, nh * group * HEAD_DIM)
        out_spec = pl.BlockSpec((1, cb, tq, ch * group * HEAD_DIM), lambda h, b, i: (0, b, i, h))
    else:
        out_shape = (nh * group, bsz, t_len, HEAD_DIM)
        out_spec = pl.BlockSpec((ch * group, cb, tq, HEAD_DIM), per_step)
    rows = group * tq
    return pl.pallas_call(
        functools.partial(_flash_kernel, group=group, chain_shape=(ch, cb), lanes_out=lanes_out, tq=tq, tk=tk,
                          blk_len=blk_len, q_scale=q_scale, gate_branch=gate_branch),
        grid=(nh // ch, bsz // cb, t_len // tq),
        in_specs=in_specs,
        out_specs=out_spec,
        out_shape=jax.ShapeDtypeStruct(out_shape, BF16),
        scratch_shapes=[
            pltpu.VMEM((chains, rows, 2 * HEAD_DIM), BF16),
            pltpu.VMEM((chains, t_len, 2 * HEAD_DIM), BF16),
            pltpu.VMEM((chains, 1, rows), F32),
            pltpu.VMEM((chains, V_ROWS, rows), F32),
        ] + [pltpu.VMEM((tk, rows), F32)] * (2 * chains),
        compiler_params=_params("parallel", "parallel", "arbitrary"),
    )(*args)


def _even_delta(refs, scratch):
    oa_ref, oc_ref, os_ref, ow_ref, w_ref = refs
    na = oa_ref.shape[-1]
    ob = (oc_ref[...].astype(F32) + os_ref[...].astype(F32) + ow_ref[...].astype(F32)).astype(BF16)
    return _dot(oa_ref[...], w_ref[0:na, :]) + _dot(ob, w_ref[na:, :])


def _even_mix(oa, ocmp, osel, owin, w):
    aspec = pl.BlockSpec((FFN_TM, oa.shape[-1]), lambda i: (i, 0))
    return dict(args=[oa, ocmp, osel, owin, w], specs=[aspec] * 4 + [_resident(w.shape)], scratch=[],
                fn=_even_delta)


def _odd_delta(refs, scratch, *, dilations):
    ng = len(dilations)
    o_refs, l_refs = refs[:ng], refs[ng:2 * ng]
    od_ref, w_ref = refs[2 * ng:]
    scratch = list(scratch)
    tm = od_ref.shape[0]

    def tokens(ref, r):
        if r == 1:
            return ref[0, 0].astype(F32)
        pieces = []
        for c in range(ref.shape[-1] // LANES):
            scr = scratch.pop(0)
            for i in range(r):
                scr[pl.ds(i, tm // r, stride=r), :] = ref[0, i, :, c * LANES:(c + 1) * LANES].astype(F32)
            pieces.append(scr[...])
        return jnp.concatenate(pieces, axis=-1)

    outs = [tokens(ref, r) for ref, r in zip(o_refs, dilations)]
    lses = [tokens(ref, r) for ref, r in zip(l_refs, dilations)]
    mx = functools.reduce(jnp.maximum, lses)
    es = [jnp.exp(l - mx) for l in lses]
    tot = functools.reduce(lambda a, b: a + b, es)
    oc = functools.reduce(lambda a, b: a + b, [(e / tot) * o for e, o in zip(es, outs)])
    nc = oc.shape[-1]
    return _dot(oc.astype(BF16), w_ref[0:nc, :]) + _dot(od_ref[...], w_ref[nc:, :])


def _odd_mix(outs, lses, od, w, t_len):
    tm = FFN_TM
    nt = t_len // tm
    gw = outs[0].shape[-1]
    dilations = tuple(o.shape[1] for o in outs)
    gspecs = [pl.BlockSpec((1, r, tm // r, gw), lambda i: (i // nt, 0, i % nt, 0)) for r in dilations]
    n_scr = 2 * (gw // LANES) * sum(1 for r in dilations if r > 1)
    specs = gspecs + gspecs + [pl.BlockSpec((tm, od.shape[1]), lambda i: (i, 0)), _resident(w.shape)]
    return dict(args=[*outs, *lses, od, w], specs=specs, scratch=[pltpu.VMEM((tm, LANES), F32)] * n_scr,
                fn=functools.partial(_odd_delta, dilations=dilations))


def _col_ranges(sizes):
    offs, acc = [], 0
    for s in sizes:
        offs.append((acc, acc + s))
        acc += s
    return offs


def _even_mixer(h, gain, w_in, w_out, sinks, cmp_pe, cmp_w1, cmp_w2, cos, sin, bsz, t_len):
    m = bsz * t_len
    qa_w, kva_w = A_Q_HEADS * HEAD_DIM, A_KV_HEADS * HEAD_DIM
    qb_w, kvb_w = B_Q_HEADS * HEAD_DIM, B_KV_HEADS * HEAD_DIM
    sizes = [qa_w, kva_w, kva_w, qb_w] + [kvb_w] * 6 + [3 * B_Q_HEADS]
    (aq, ak, av, bq, bkc, bvc, bks, bvs, bkw, bvw, bg) = [w_in[:, a:b] for a, b in _col_ranges(sizes)]
    group = B_Q_HEADS // B_KV_HEADS
    gpad = jnp.zeros((D_MODEL, LANES - 3 * group), w_in.dtype)
    gates = [x for hk in range(B_KV_HEADS) for x in (bg[:, hk * 3 * group:(hk + 1) * 3 * group], gpad)]
    w = jnp.concatenate([aq, bq, ak, bkw, bks, av, bvw, bvs, bkc, bvc] + gates, axis=1).astype(BF16)
    c = [0]

    def take(width):
        c[0] += width
        return c[0] - width

    def out(idx, off, width, kind, **kw):
        return dict(idx=idx, off=off, width=width, kind=kind, **kw)

    plan = [
        (take(qa_w), qa_w, [out(0, 0, qa_w, "split", hw=HEAD_DIM, rope=True, scale=True)]),
        (take(qb_w), qb_w, [out(1, 0, qb_w, "split", hw=HEAD_DIM, rope=True, scale=True),
                            out(5, 0, qb_w, "flat", scale=True)]),
        (take(3 * kva_w), 3 * kva_w, [out(2, 0, kva_w, "flat", rope=True),
                                      out(3, kva_w, kvb_w, "flat", rope=True),
                                      out(4, kva_w + kvb_w, kvb_w, "split", hw=HEAD_DIM, rope=True)]),
        (take(5 * kvb_w), 5 * kvb_w, [out(6, 0, kva_w, "flat"), out(7, kva_w, kvb_w, "flat"),
                                      out(8, 2 * kvb_w, kvb_w, "vt", tile=FLASH_TK),
                                      out(9, 3 * kvb_w, kvb_w, "chunk"),
                                      out(10, 4 * kvb_w, kvb_w, "chunk")]),
        (take(B_KV_HEADS * LANES), B_KV_HEADS * LANES,
         [out(11, 0, B_KV_HEADS * LANES, "split", hw=LANES, sigmoid=True)]),
    ]
    out_defs = [("split", A_Q_HEADS, HEAD_DIM, BF16), ("split", B_Q_HEADS, HEAD_DIM, BF16), ("flat", kva_w, BF16),
                ("flat", kvb_w, BF16),
                ("split", B_KV_HEADS, HEAD_DIM, BF16), ("flat", qb_w, BF16), ("flat", kva_w, BF16),
                ("flat", kvb_w, BF16), ("vt", B_KV_HEADS, FLASH_TK, BF16), ("chunk", B_KV_HEADS, BF16),
                ("chunk", B_KV_HEADS, BF16), ("split", B_KV_HEADS, LANES, F32)]
    (aq_r, bq_r, ak_r, bkw_r, bks_r, bq_u, av_, bvw_, bvs_t, zk, zv, gate) = _proj(
        h, gain, w, cos, sin, plan, out_defs, bsz, t_len)

    def seq(x):
        return x.reshape(bsz, 1, t_len, x.shape[-1])

    def heads(x):
        return x.reshape(x.shape[0], bsz, t_len, x.shape[-1])

    gate4 = heads(gate)
    oa = _banded(heads(aq_r), seq(ak_r), seq(av_), n_kv=A_KV_HEADS, group=A_Q_HEADS // A_KV_HEADS,
                 max_dist=A_WINDOW - 1, tl=BAND_TL, pk=128, span=128, sink=sinks)
    nch = t_len // NSA_CMP_STRIDE
    zshape = (B_KV_HEADS, bsz, nch, NSA_CMP_STRIDE * HEAD_DIM)
    kc, vc = _compress(zk.reshape(zshape), zv.reshape(zshape), cmp_pe.reshape(2, 1, NSA_CMP_LEN * HEAD_DIM),
                       cmp_w1.astype(BF16), cmp_w2.astype(BF16))
    ocmp, bias = _cmp_attention(bq_u.reshape(bsz, t_len, qb_w), kc, vc, gate4, t_len)
    osel = _flash(heads(bq_r), bias, heads(bks_r), bvs_t, group=group, tq=FLASH_T, blk_len=NSA_SEL_LEN,
                  q_scale=1.0, lanes_out=True, gate=gate4, gate_branch=1)
    owin = _banded(heads(bq_r), seq(bkw_r), seq(bvw_), n_kv=B_KV_HEADS, group=group,
                   max_dist=NSA_WINDOW - 1, tl=BAND_TL, pk=512, span=512, gate=gate4, gate_branch=2)
    return _even_mix(oa.reshape(m, qa_w), ocmp.reshape(m, qb_w), osel.reshape(m, qb_w),
                     owin.reshape(m, qb_w), w_out.astype(BF16))


def _odd_mixer(h, gain, w_in, w_out, cos, sin, bsz, t_len):
    m = bsz * t_len
    cw = C_HEADS * HEAD_DIM
    dw = D_HEADS * HEAD_DIM
    gw = C_HEADS_PER_GROUP * HEAD_DIM
    n_groups = len(C_GROUPS)

    def group_outs(base, **kw):
        return [dict(idx=base + gi, off=gi * gw, width=gw, kind="dilate", r=r, **kw)
                for gi, (_, r) in enumerate(C_GROUPS)]

    plan = [
        (0, cw, group_outs(0, rope=True, scale=True)),
        (cw, cw, group_outs(n_groups, rope=True)),
        (2 * cw, cw, group_outs(2 * n_groups)),
        (3 * cw, 3 * dw, [dict(idx=3 * n_groups, off=0, width=dw, kind="split", hw=HEAD_DIM, rope=True),
                          dict(idx=3 * n_groups + 1, off=dw, width=dw, kind="split", hw=HEAD_DIM, rope=True),
                          dict(idx=3 * n_groups + 2, off=2 * dw, width=dw, kind="vt", tile=MOBA_TK)]),
    ]
    out_defs = [("dilate", r, gw, BF16) for _ in range(3) for _, r in C_GROUPS]
    out_defs += [("split", D_HEADS, HEAD_DIM, BF16), ("split", D_HEADS, HEAD_DIM, BF16),
                 ("vt", D_HEADS, MOBA_TK, BF16)]
    res = _proj(h, gain, w_in.astype(BF16), cos, sin, plan, out_defs, bsz, t_len)
    cq, ck, cv = res[0:n_groups], res[n_groups:2 * n_groups], res[2 * n_groups:3 * n_groups]
    dq, dk, dv_t = res[3 * n_groups:]
    outs, lses = [], []
    for gi, (wlen, r) in enumerate(C_GROUPS):
        o, lse = _banded(cq[gi], ck[gi], cv[gi], n_kv=C_HEADS_PER_GROUP, group=1, max_dist=wlen // r,
                         tl=min(BAND_TL, t_len // r), pk=128, span=128, want_lse=True)
        outs.append(o)
        lses.append(lse)

    def heads(x):
        return x.reshape(x.shape[0], bsz, t_len, x.shape[-1])

    bias = _moba_gate(heads(dq), heads(dk))
    od = _flash(heads(dq), bias, heads(dk), dv_t, group=1, tq=MOBA_T, blk_len=MOBA_BLOCK, q_scale=QK_SCALE,
                lanes_out=True)
    return _odd_mix(outs, lses, od.reshape(m, D_HEADS * HEAD_DIM), w_out.astype(BF16), t_len)


def kernel(x, ffn_norm_pre, mix_norm, ffn_norm_post, ffn_wi, ffn_wo, even_w_in, even_w_out, even_sinks,
           nsa_cmp_pe, nsa_cmp_w1, nsa_cmp_w2, odd_w_in, odd_w_out, final_norm):
    bsz, t_len, _ = x.shape
    depth = ffn_wi.shape[0]
    cos, sin = _rope_tables(t_len)
    wi = ffn_wi.astype(BF16)
    wo = ffn_wo.astype(BF16)
    h = x.reshape(bsz * t_len, D_MODEL)
    for layer in range(depth):
        i = layer // 2
        h = _ffn(h, ffn_norm_pre[layer], wi, wo, (layer, 0))
        if layer % 2 == 0:
            mix = _even_mixer(h, mix_norm[layer], even_w_in[i], even_w_out[i], even_sinks[i], nsa_cmp_pe[i],
                              nsa_cmp_w1[i], nsa_cmp_w2[i], cos, sin, bsz, t_len)
        else:
            mix = _odd_mixer(h, mix_norm[layer], odd_w_in[i], odd_w_out[i], cos, sin, bsz, t_len)
        last = layer == depth - 1
        h = _ffn(h, ffn_norm_post[layer], wi, wo, (layer, 1), final_norm if last else None, mix=mix)
    return h.reshape(bsz, t_len, D_MODEL)
```

```python
import functools

import jax
import jax.numpy as jnp
from jax import lax
from jax.experimental import pallas as pl
from jax.experimental.pallas import tpu as pltpu

D_MODEL = 1024
HEAD_DIM = 64
ROPE_THETA = 10000.0
NORM_EPS = 1e-6
D_FF = 2816
NEG_INF = -1e30
FORCE_SCORE = 1e4

A_Q_HEADS = 8
A_KV_HEADS = 2
A_WINDOW = 128
B_Q_HEADS = 8
B_KV_HEADS = 2
NSA_CMP_LEN = 32
NSA_CMP_STRIDE = 16
NSA_SEL_LEN = 64
NSA_SEL_COUNT = 8
NSA_WINDOW = 512
C_GROUPS = ((128, 1), (512, 4), (2048, 16))
C_HEADS_PER_GROUP = 4
C_HEADS = len(C_GROUPS) * C_HEADS_PER_GROUP
D_HEADS = 4
MOBA_BLOCK = 256
MOBA_TOPK = 3

LANES = 128
LOG2E = 1.4426950408889634
LN2 = 0.6931471805599453
QK_SCALE = HEAD_DIM ** -0.5 * LOG2E
MASK_BIAS = -32768.0
V_ROWS = HEAD_DIM + 16
REMOVED = -3e38
VMEM_LIMIT = 52 * 1024 * 1024

FFN_TM = 512
FFN_TF = 256
PROJ_TM = 1024
BAND_TL = 1024
WIN_TL = 512
BAND_SB = 128
FLASH_T = 256
FLASH_TK = 256
FLASH_CHAINS = 4
MOBA_T = 512
MOBA_TK = 512
GATE_TQ = 4096
CMP_TQ = 1024

BF16 = jnp.bfloat16
F32 = jnp.float32


def _params(*sem):
    return pltpu.CompilerParams(dimension_semantics=sem, vmem_limit_bytes=VMEM_LIMIT)


def _rms(x, g):
    return x * lax.rsqrt(jnp.mean(x * x, axis=-1, keepdims=True) + NORM_EPS) * g


def _log2(n):
    assert n & (n - 1) == 0
    return n.bit_length() - 1


def _dot(a, b):
    return jnp.dot(a, b, preferred_element_type=F32)


def _dot_t(a, b):
    return lax.dot_general(a, b, (((1,), (1,)), ((), ())), preferred_element_type=F32)


def _dot_tn(a, b):
    return lax.dot_general(a, b, (((0,), (0,)), ((), ())), preferred_element_type=F32)


def _ffn_kernel(x_ref, g_ref, wi_ref, wo_ref, *rest, final, mix_fn, n_mix):
    rest = list(rest)
    fg_ref = rest.pop(0) if final else None
    mix_refs = [rest.pop(0) for _ in range(n_mix)]
    o_ref, act_scr = rest.pop(0), rest.pop(0)
    x = x_ref[...]
    if mix_fn is not None:
        x = x + mix_fn(mix_refs, rest)
    o_ref[...] = x
    n = _rms(x, g_ref[...]).astype(BF16)
    for c in range(D_FF // FFN_TF):
        cols = slice(c * FFN_TF, (c + 1) * FFN_TF)
        gate = _dot(n, wi_ref[:, cols])
        up = _dot(n, wi_ref[:, D_FF + c * FFN_TF:D_FF + (c + 1) * FFN_TF])
        act_scr[:, cols] = (gate * jax.nn.sigmoid(gate) * up).astype(BF16)
    h = o_ref[...] + 0.5 * _dot(act_scr[...], wo_ref[...])
    if final:
        h = _rms(h, fg_ref[...])
    o_ref[...] = h


def _resident(shape, lead=()):
    block = (None,) * len(lead) + tuple(shape[len(lead):])
    index = tuple(lead) + (0,) * (len(shape) - len(lead))
    return pl.BlockSpec(block, lambda i: index, pipeline_mode=pl.Buffered(1))


def _ffn(h, gain, wi, wo, which, final_gain=None, mix=None):
    m = h.shape[0]
    tm = FFN_TM
    final = final_gain is not None
    in_specs = [
        pl.BlockSpec((tm, D_MODEL), lambda i: (i, 0)),
        pl.BlockSpec((1, D_MODEL), lambda i: (0, 0)),
        _resident(wi.shape, which),
        _resident(wo.shape, which),
    ]
    args = [h, gain.reshape(1, D_MODEL), wi, wo]
    if final:
        in_specs.append(pl.BlockSpec((1, D_MODEL), lambda i: (0, 0)))
        args.append(final_gain.reshape(1, D_MODEL))
    scratch = [pltpu.VMEM((tm, D_FF), BF16)]
    if mix is not None:
        in_specs += mix["specs"]
        args += mix["args"]
        scratch += mix["scratch"]
    return pl.pallas_call(
        functools.partial(_ffn_kernel, final=final, mix_fn=mix and mix["fn"], n_mix=len(mix["args"]) if mix else 0),
        grid=(m // tm,),
        in_specs=in_specs,
        out_specs=pl.BlockSpec((tm, D_MODEL), lambda i: (i, 0)),
        out_shape=jax.ShapeDtypeStruct((m, D_MODEL), F32),
        scratch_shapes=scratch,
        compiler_params=_params("parallel"),
    )(*args)


def _rope_tables(t):
    inv = 1.0 / (ROPE_THETA ** (jnp.arange(0, HEAD_DIM, 2, dtype=F32) / HEAD_DIM))
    ang = jnp.arange(t, dtype=F32)[:, None] * inv[None, :]
    cos = jnp.cos(ang)
    sin = jnp.sin(ang)
    return (jnp.concatenate([cos, cos, cos, cos], axis=-1),
            jnp.concatenate([-sin, sin, -sin, sin], axis=-1))


def _proj_kernel(x_ref, g_ref, w_ref, cos_ref, sin_ref, *rest, plan, n_out):
    out_refs, (ys,) = rest[:n_out], rest[n_out:]
    n = _rms(x_ref[...], g_ref[...]).astype(BF16)
    tm = n.shape[0]
    cos = cos_ref[...]
    sin = sin_ref[...]
    lane = lax.broadcasted_iota(jnp.int32, cos.shape, 1)
    first_half = (lane & (HEAD_DIM - 1)) < HEAD_DIM // 2
    heads_per_tile = LANES // HEAD_DIM
    for off, width, outs in plan:
        y = _dot(n, w_ref[:, off:off + width])
        for o in outs:
            o_ref = out_refs[o["idx"]]
            dt = o_ref.dtype
            kind = o["kind"]
            for c in range(o["width"] // LANES):
                piece = y[:, o["off"] + c * LANES: o["off"] + (c + 1) * LANES]
                if o.get("rope"):
                    rot = jnp.where(first_half, pltpu.roll(piece, LANES - HEAD_DIM // 2, 1),
                                    pltpu.roll(piece, HEAD_DIM // 2, 1))
                    piece = piece * cos + rot * sin
                if o.get("scale"):
                    piece = piece * QK_SCALE
                if o.get("sigmoid"):
                    piece = jax.nn.sigmoid(piece)
                lanes = slice(c * LANES, (c + 1) * LANES)
                if kind == "flat":
                    o_ref[:, lanes] = piece.astype(dt)
                elif kind == "split":
                    hw = o["hw"]
                    per = LANES // hw
                    for k in range(per):
                        o_ref[c * per + k] = piece[:, k * hw:(k + 1) * hw].astype(dt)
                elif kind == "dilate":
                    r = o["r"]
                    if r == 1:
                        o_ref[0, 0, :, lanes] = piece.astype(dt)
                    else:
                        ys[...] = piece
                        for i in range(r):
                            o_ref[0, i, :, lanes] = ys[pl.ds(i, tm // r, stride=r), :].astype(dt)
                elif kind == "chunk":
                    ys[...] = piece
                    for l in range(NSA_CMP_STRIDE):
                        rows = ys[pl.ds(l, tm // NSA_CMP_STRIDE, stride=NSA_CMP_STRIDE), :].astype(dt)
                        for k in range(heads_per_tile):
                            o_ref[c * heads_per_tile + k, :, l * HEAD_DIM:(l + 1) * HEAD_DIM] = rows[
                                :, k * HEAD_DIM:(k + 1) * HEAD_DIM]
                else:
                    tile = o["tile"]
                    pt = piece.T
                    for k in range(heads_per_tile):
                        for kt in range(tm // tile):
                            o_ref[c * heads_per_tile + k, 0, kt, 0:HEAD_DIM, :] = pt[
                                k * HEAD_DIM:(k + 1) * HEAD_DIM, kt * tile:(kt + 1) * tile].astype(dt)
                            o_ref[c * heads_per_tile + k, 0, kt, HEAD_DIM:V_ROWS, :] = jnp.ones(
                                (V_ROWS - HEAD_DIM, tile), dt)


def _proj(h, gain, w, cos, sin, plan, out_defs, bsz, t_len):
    m = h.shape[0]
    tm = PROJ_TM
    nt = t_len // tm
    out_shapes, out_specs = [], []
    for d in out_defs:
        kind, dt = d[0], d[-1]
        if kind == "flat":
            shape, block, index = (m, d[1]), (tm, d[1]), (lambda i: (i, 0))
        elif kind == "split":
            shape, block, index = (d[1], m, d[2]), (d[1], tm, d[2]), (lambda i: (0, i, 0))
        elif kind == "dilate":
            r = d[1]
            shape, block = (bsz, r, t_len // r, d[2]), (1, r, tm // r, d[2])
            index = lambda i: (i // nt, 0, i % nt, 0)
        elif kind == "chunk":
            width = NSA_CMP_STRIDE * HEAD_DIM
            shape, block = (d[1], m // NSA_CMP_STRIDE, width), (d[1], tm // NSA_CMP_STRIDE, width)
            index = lambda i: (0, i, 0)
        else:
            tile = d[2]
            shape, block = (d[1], bsz, t_len // tile, V_ROWS, tile), (d[1], 1, tm // tile, V_ROWS, tile)
            index = lambda i: (0, i // nt, i % nt, 0, 0)
        out_shapes.append(jax.ShapeDtypeStruct(shape, dt))
        out_specs.append(pl.BlockSpec(block, index))
    return pl.pallas_call(
        functools.partial(_proj_kernel, plan=plan, n_out=len(out_defs)),
        grid=(m // tm,),
        in_specs=[
            pl.BlockSpec((tm, D_MODEL), lambda i: (i, 0)),
            pl.BlockSpec((1, D_MODEL), lambda i: (0, 0)),
            pl.BlockSpec(w.shape, lambda i: (0, 0)),
            pl.BlockSpec((tm, LANES), lambda i: (i % nt, 0)),
            pl.BlockSpec((tm, LANES), lambda i: (i % nt, 0)),
        ],
        out_specs=out_specs,
        out_shape=out_shapes,
        scratch_shapes=[pltpu.VMEM((tm, LANES), F32)],
        compiler_params=_params("parallel"),
    )(h, gain.reshape(1, D_MODEL), w, cos, sin)


def _banded_kernel(q_ref, kp_ref, kc_ref, vp_ref, vc_ref, *rest, n_kv, group, tl, pk, span, max_dist,
                   has_sink, gate_branch, want_lse):
    rest = list(rest)
    sink_ref = rest.pop(0) if has_sink else None
    gate_ref = rest.pop(0) if gate_branch is not None else None
    o_ref = rest.pop(0)
    lse_ref = rest.pop(0) if want_lse else None
    kcat, vcat, s_a, s_b, band = rest
    sb = BAND_SB
    n_seq = kc_ref.shape[1]
    for rr in range(n_seq):
        kcat[rr, 0:pk, :] = kp_ref[0, rr]
        kcat[rr, pk:, :] = kc_ref[0, rr]
        vcat[rr, 0:pk, :] = vp_ref[0, rr]
        vcat[rr, pk:, :] = vc_ref[0, rr]
    nk = span + sb
    kw = n_kv * HEAD_DIM
    merged = group == 1
    stacked = n_kv if merged else group
    rows = stacked * sb
    key = lax.broadcasted_iota(jnp.int32, (nk, rows), 0)
    qry = lax.broadcasted_iota(jnp.int32, (nk, rows), 1) & (sb - 1)
    dist = qry + span - key
    in_band = (dist >= 0) & (dist <= max_dist)
    lane_head = lax.broadcasted_iota(jnp.int32, (1, rows), 1) >> _log2(sb)
    q_head = lax.broadcasted_iota(jnp.int32, (sb, kw), 1) >> _log2(HEAD_DIM)
    v_head = lax.broadcasted_iota(jnp.int32, (nk, kw), 1) >> _log2(HEAD_DIM)
    assert merged or kw == LANES
    items = [(rr, j, h) for rr in range(n_seq) for j in range(tl // sb)
             for h in ([0] if merged else range(n_kv))]
    s_bufs = (s_a, s_b)

    def compute(first_tile):
        def scores(n):
            rr, j, h = items[n]
            lo = pk + j * sb - span
            qrows = slice(j * sb, (j + 1) * sb)
            if merged:
                qj = q_ref[0, rr, qrows, :].astype(F32)
                qs = jnp.concatenate([jnp.where(q_head == r, qj, 0.0).astype(BF16) for r in range(n_kv)], axis=0)
                kk = kcat[rr, lo:lo + nk, :]
            else:
                qs = jnp.concatenate([q_ref[h * group + g, 0, qrows, :] for g in range(group)], axis=0)
                kk = kcat[rr, lo:lo + nk, h * HEAD_DIM:(h + 1) * HEAD_DIM]
            variant = 1 + j if first_tile and j * sb < span else 0
            s_bufs[n % 2][...] = _dot_t(kk, qs) + band[variant]

        band[0] = jnp.where(in_band, 0.0, NEG_INF)
        if first_tile:
            for j in range(span // sb):
                band[1 + j] = jnp.where(in_band & (key >= span - j * sb), 0.0, NEG_INF)

        outs, lses = [], []
        scores(0)
        for n, (rr, j, h) in enumerate(items):
            if n + 1 < len(items):
                scores(n + 1)
            s = s_bufs[n % 2][...]
            lo = pk + j * sb - span
            qrows = slice(j * sb, (j + 1) * sb)
            m = jnp.max(s, axis=0, keepdims=True)
            if has_sink:
                sink = jnp.full((1, rows), sink_ref[h * group], F32)
                for g in range(1, group):
                    sink = jnp.where(lane_head == g, sink_ref[h * group + g], sink)
                sink = sink * LOG2E
                m = jnp.maximum(m, sink)
            p = jnp.exp2(s - m).astype(BF16)
            vwin = vcat[rr, lo:lo + nk, :]
            if merged:
                vv = jnp.concatenate([vwin, jnp.ones((nk, LANES), BF16)], axis=1)
                ones_row = kw
            else:
                vv = jnp.where(v_head == h, vwin.astype(F32), 1.0).astype(BF16)
                ones_row = (1 - h) * HEAD_DIM
            o_t = _dot_tn(vv, p)
            d = o_t[ones_row:ones_row + 1, :]
            if has_sink:
                d = d + jnp.exp2(sink - m)
            d = jnp.where(d > 0, d, 1.0)
            inv = 1.0 / d
            if want_lse:
                lse_t = jnp.broadcast_to(m * LN2 + jnp.log(d), (HEAD_DIM, rows))
            for r in range(stacked):
                cols = slice(r * sb, (r + 1) * sb)
                vrow = (r if merged else h) * HEAD_DIM
                o = (o_t[vrow:vrow + HEAD_DIM, cols] * inv[:, cols]).T
                if gate_ref is not None:
                    gc = r * 3 + gate_branch
                    o = o * gate_ref[h, 0, qrows, gc:gc + 1]
                outs.append(o)
                if want_lse:
                    lses.append(lse_t[:, cols].T)
            if merged or h == n_kv - 1:
                o_ref[0, rr, qrows, :] = jnp.concatenate(outs, axis=-1).astype(o_ref.dtype)
                if want_lse:
                    lse_ref[0, rr, qrows, :] = jnp.concatenate(lses, axis=-1)
                outs, lses = [], []

    first = pl.program_id(2) == 0
    pl.when(first)(functools.partial(compute, True))
    pl.when(jnp.logical_not(first))(functools.partial(compute, False))


def _banded(q, k, v, *, n_kv, group, max_dist, tl, sink=None, gate=None, gate_branch=None, want_lse=False):
    bsz, n_res, length, kw = k.shape
    qw = n_kv * group * HEAD_DIM
    span = -(-max_dist // BAND_SB) * BAND_SB
    pk = span
    assert tl % pk == 0
    s_shape = (span + BAND_SB, (n_kv if group == 1 else group) * BAND_SB)
    ratio = tl // pk
    n_seq = min(n_res, max(1, BAND_TL // tl))
    prev = lambda b, i, l: (b, i, jnp.maximum(l * ratio - 1, 0), 0)
    cur = lambda b, i, l: (b, i, l, 0)
    if group == 1:
        q_spec = pl.BlockSpec((1, n_seq, tl, qw), cur)
    else:
        q_spec = pl.BlockSpec((n_kv * group, 1, tl, HEAD_DIM), lambda b, i, l: (0, b, l, 0))
    o_shape = (bsz, n_res, length, qw)
    in_specs = [
        q_spec,
        pl.BlockSpec((1, n_seq, pk, kw), prev),
        pl.BlockSpec((1, n_seq, tl, kw), cur),
        pl.BlockSpec((1, n_seq, pk, kw), prev),
        pl.BlockSpec((1, n_seq, tl, kw), cur),
    ]
    args = [q, k, k, v, v]
    if sink is not None:
        in_specs.append(pl.BlockSpec(memory_space=pltpu.SMEM))
        args.append(sink)
    if gate is not None:
        in_specs.append(pl.BlockSpec((n_kv, 1, tl, LANES), lambda b, i, l: (0, b, l, 0)))
        args.append(gate)
    out_spec = pl.BlockSpec((1, n_seq, tl, qw), cur)
    out_shape = [jax.ShapeDtypeStruct(o_shape, BF16)]
    out_specs = [out_spec]
    if want_lse:
        out_shape.append(jax.ShapeDtypeStruct(o_shape, F32))
        out_specs.append(out_spec)
    res = pl.pallas_call(
        functools.partial(_banded_kernel, n_kv=n_kv, group=group, tl=tl, pk=pk, span=span,
                          max_dist=max_dist, has_sink=sink is not None, gate_branch=gate_branch,
                          want_lse=want_lse),
        grid=(bsz, n_res // n_seq, length // tl),
        in_specs=in_specs,
        out_specs=out_specs,
        out_shape=out_shape,
        scratch_shapes=[pltpu.VMEM((n_seq, pk + tl, kw), BF16), pltpu.VMEM((n_seq, pk + tl, kw), BF16),
                        pltpu.VMEM(s_shape, F32), pltpu.VMEM(s_shape, F32),
                        pltpu.VMEM((1 + span // BAND_SB,) + s_shape, F32)],
        compiler_params=_params("parallel", "parallel", "parallel"),
    )(*args)
    return res if want_lse else res[0]


def _compress_one(z_ref, pe_ref, w1_ref, w2_ref, o_ref):
    half = NSA_CMP_STRIDE * HEAD_DIM
    z = z_ref[0, 0].astype(F32)
    za = (z + pe_ref[0, :, 0:half]).astype(BF16)
    zb = (z + pe_ref[0, :, half:2 * half]).astype(BF16)
    ya = _dot(za, w1_ref[0, 0:half, :])
    yb = _dot(zb, w1_ref[0, half:2 * half, :])
    nrow = ya.shape[0]
    hid = jax.nn.gelu(ya + pltpu.roll(yb, nrow - 1, 0))
    o_ref[0, 0] = _dot(hid.astype(BF16), w2_ref[0]).astype(o_ref.dtype)


def _compress_kernel(zk_ref, zv_ref, pek_ref, pev_ref, w1k_ref, w1v_ref, w2k_ref, w2v_ref, ok_ref, ov_ref):
    _compress_one(zk_ref, pek_ref, w1k_ref, w2k_ref, ok_ref)
    _compress_one(zv_ref, pev_ref, w1v_ref, w2v_ref, ov_ref)


def _compress(zk, zv, pe, w1, w2):
    hk, bsz, nch, width = zk.shape
    zspec = pl.BlockSpec((1, 1, nch, width), lambda h, b: (h, b, 0, 0))
    ospec = pl.BlockSpec((1, 1, nch, HEAD_DIM), lambda h, b: (h, b, 0, 0))
    oshape = jax.ShapeDtypeStruct((hk, bsz, nch, HEAD_DIM), BF16)

    def wspec(arr, which):
        return pl.BlockSpec((1,) + arr.shape[1:], lambda h, b: (which, 0, 0))

    return pl.pallas_call(
        _compress_kernel,
        grid=(hk, bsz),
        in_specs=[zspec, zspec, wspec(pe, 0), wspec(pe, 1), wspec(w1, 0), wspec(w1, 1), wspec(w2, 0), wspec(w2, 1)],
        out_specs=[ospec, ospec],
        out_shape=[oshape, oshape],
        compiler_params=_params("parallel", "parallel"),
    )(zk, zv, pe, pe, w1, w1, w2, w2)


def _topk_select(score, k, idxf):
    ncand = score.shape[0]
    rem = score
    sel = jnp.zeros_like(score)
    for _ in range(k):
        mx = jnp.max(rem, axis=0, keepdims=True)
        first = jnp.min(jnp.where(rem == mx, idxf, float(ncand)), axis=0, keepdims=True)
        pick = idxf == first
        sel = jnp.where(pick, jnp.where(mx > 0.5 * NEG_INF, 1.0, 0.0), sel)
        rem = jnp.where(pick, REMOVED, rem)
    return sel


def _cmp_kernel(q_ref, kc_ref, vc_ref, gate_ref, o_ref, bias_ref, *, tq, group, n_tiles):
    qi = pl.program_id(1)

    def tile(ncp):
        for hk in range(kc_ref.shape[0]):
            _cmp_tile(q_ref, kc_ref, vc_ref, gate_ref, o_ref, bias_ref, qi, hk, tq=tq, group=group, ncp=ncp)

    for v in range(n_tiles):
        pl.when(qi == v)(functools.partial(tile, min(kc_ref.shape[2], (v + 1) * tq // NSA_CMP_STRIDE)))


def _cmp_tile(q_ref, kc_ref, vc_ref, gate_ref, o_ref, bias_ref, qi, hk, *, tq, group, ncp):
    kc = kc_ref[hk, 0, 0:ncp, :]
    vc = vc_ref[hk, 0, 0:ncp, :]
    gw = group * HEAD_DIM
    nsb = bias_ref.shape[-1]
    rows = group * tq
    last = lax.broadcasted_iota(jnp.int32, (ncp, 1), 0) * NSA_CMP_STRIDE + (NSA_CMP_LEN - 1)
    t_in_tile = lax.broadcasted_iota(jnp.int32, (1, rows), 1) & (tq - 1)
    visible = (last - t_in_tile) <= qi * tq
    qs = jnp.concatenate([q_ref[0, :, hk * gw + g * HEAD_DIM:hk * gw + (g + 1) * HEAD_DIM]
                          for g in range(group)], axis=0)
    s = jnp.where(visible, _dot_t(kc, qs), NEG_INF)
    m = jnp.max(s, axis=0, keepdims=True)
    m = jnp.where(m > 0.5 * NEG_INF, m, 0.0)
    p = jnp.exp2(s - m)
    vc_ones = jnp.concatenate([vc, jnp.ones_like(vc)], axis=1)
    o_t = _dot_tn(vc_ones, p.astype(BF16))
    d = o_t[HEAD_DIM:HEAD_DIM + 1, :]
    d = jnp.where(d > 0, d, 1.0)
    inv = 1.0 / d
    pc = p * inv
    o_t = o_t[0:HEAD_DIM, :] * inv
    outs = []
    psum = pc[:, 0:tq]
    for g in range(group):
        if g:
            psum = psum + pc[:, g * tq:(g + 1) * tq]
        outs.append(o_t[:, g * tq:(g + 1) * tq].T * gate_ref[hk, 0, :, g * 3:g * 3 + 1])
    o_ref[0, :, hk * gw:(hk + 1) * gw] = jnp.concatenate(outs, axis=-1).astype(o_ref.dtype)

    jj = lax.broadcasted_iota(jnp.int32, (nsb, ncp), 0) * NSA_SEL_LEN
    nn = lax.broadcasted_iota(jnp.int32, (nsb, ncp), 1) * NSA_CMP_STRIDE
    ov = jnp.maximum(jnp.minimum(nn + NSA_CMP_LEN, jj + NSA_SEL_LEN) - jnp.maximum(nn, jj), 0)
    ov = (ov.astype(F32) / NSA_CMP_LEN).astype(BF16)
    p_hi = psum.astype(BF16)
    p_lo = (psum - p_hi.astype(F32)).astype(BF16)
    imp = _dot(ov, p_hi) + _dot(ov, p_lo)

    j = lax.broadcasted_iota(jnp.int32, (nsb, tq), 0)
    tt = qi * tq + lax.broadcasted_iota(jnp.int32, (nsb, tq), 1)
    cb = tt >> _log2(NSA_SEL_LEN)
    forced = (j == 0) | (j == cb) | (j == cb - 1)
    assert FORCE_SCORE > group
    n_forced = 3
    free = jnp.where((j <= cb) & jnp.logical_not(forced), imp, NEG_INF)
    sel = _topk_select(free, min(NSA_SEL_COUNT, nsb) - n_forced, j.astype(F32))
    keep = (sel > 0.5) | (forced & (j <= cb))
    bias_ref[hk, 0] = jnp.where(keep, 0.0, MASK_BIAS).T.astype(bias_ref.dtype)


def _cmp_attention(qu, kc, vc, gate, t_len):
    bsz = qu.shape[0]
    hk, _, ncp, _ = kc.shape
    group = B_Q_HEADS // B_KV_HEADS
    tq = CMP_TQ
    nsb = t_len // NSA_SEL_LEN
    gw = group * HEAD_DIM
    cspec = pl.BlockSpec((hk, 1, ncp, HEAD_DIM), lambda b, i: (0, b, 0, 0))
    return pl.pallas_call(
        functools.partial(_cmp_kernel, tq=tq, group=group, n_tiles=t_len // tq),
        grid=(bsz, t_len // tq),
        in_specs=[
            pl.BlockSpec((1, tq, hk * gw), lambda b, i: (b, i, 0)),
            cspec, cspec,
            pl.BlockSpec((hk, 1, tq, LANES), lambda b, i: (0, b, i, 0)),
        ],
        out_specs=[
            pl.BlockSpec((1, tq, hk * gw), lambda b, i: (b, i, 0)),
            pl.BlockSpec((hk, 1, tq, nsb), lambda b, i: (0, b, i, 0)),
        ],
        out_shape=[
            jax.ShapeDtypeStruct((bsz, t_len, hk * gw), BF16),
            jax.ShapeDtypeStruct((hk, bsz, t_len, nsb), BF16),
        ],
        compiler_params=_params("parallel", "parallel"),
    )(qu, kc, vc, gate)


def _moba_gate_kernel(q_ref, k_ref, bias_ref, km_hi, km_lo, *, tq):
    qi = pl.program_id(1)
    nh, _, t_len, _ = k_ref.shape
    ncol = bias_ref.shape[-1]
    nb = t_len // MOBA_BLOCK

    @pl.when(qi == 0)
    def _():
        blk = lax.broadcasted_iota(jnp.int32, (ncol, t_len), 0)
        pos = lax.broadcasted_iota(jnp.int32, (ncol, t_len), 1)
        member = jnp.where((pos >> _log2(MOBA_BLOCK)) == blk, 1.0, 0.0).astype(BF16)
        for h in range(nh):
            kmean = _dot(member, k_ref[h, 0]) * (1.0 / MOBA_BLOCK)
            hi = kmean.astype(BF16)
            km_hi[h] = hi
            km_lo[h] = (kmean - hi.astype(F32)).astype(BF16)

    j = lax.broadcasted_iota(jnp.int32, (nb, tq), 0)
    t = qi * tq + lax.broadcasted_iota(jnp.int32, (nb, tq), 1)
    cb = t >> _log2(MOBA_BLOCK)
    for h in range(nh):
        q = q_ref[h, 0]
        gate = (_dot_t(km_hi[h], q) + _dot_t(km_lo[h], q))[0:nb]
        score = jnp.where(j < cb, gate, NEG_INF)
        sel = _topk_select(score, min(MOBA_TOPK, nb - 1), j.astype(F32))
        keep = (sel > 0.5) | (j == cb)
        bias = jnp.where(keep, 0.0, MASK_BIAS)
        bias = jnp.concatenate([bias, jnp.zeros((ncol - nb, tq), F32)], axis=0)
        bias_ref[h, 0] = bias.T.astype(bias_ref.dtype)


def _moba_gate(q, k):
    nh, bsz, t_len, _ = q.shape
    tq = min(GATE_TQ, t_len)
    return pl.pallas_call(
        functools.partial(_moba_gate_kernel, tq=tq),
        grid=(bsz, t_len // tq),
        in_specs=[
            pl.BlockSpec((nh, 1, tq, HEAD_DIM), lambda b, i: (0, b, i, 0)),
            pl.BlockSpec((nh, 1, t_len, HEAD_DIM), lambda b, i: (0, b, 0, 0)),
        ],
        out_specs=pl.BlockSpec((nh, 1, tq, HEAD_DIM), lambda b, i: (0, b, i, 0)),
        out_shape=jax.ShapeDtypeStruct((nh, bsz, t_len, HEAD_DIM), BF16),
        scratch_shapes=[pltpu.VMEM((nh, HEAD_DIM, HEAD_DIM), BF16), pltpu.VMEM((nh, HEAD_DIM, HEAD_DIM), BF16)],
        compiler_params=_params("parallel", "arbitrary"),
    )(q, k)


def _flash_kernel(q_ref, bias_ref, k_ref, vt_ref, *rest, group, chain_shape, lanes_out, tq, tk, blk_len, q_scale,
                  gate_branch):
    rest = list(rest)
    gate_ref = rest.pop(0) if gate_branch is not None else None
    o_ref, qa, ka, m_s, acc = rest[:5]
    s_bufs = rest[5:]
    chain_ids = [(ch, cb) for cb in range(chain_shape[1]) for ch in range(chain_shape[0])]
    chains = len(chain_ids)
    qi = pl.program_id(2)
    rows = group * tq
    t_len = k_ref.shape[2]

    @pl.when(qi == 0)
    def _():
        kpos = lax.broadcasted_iota(jnp.int32, (t_len, HEAD_DIM), 0)
        kblk = lax.broadcasted_iota(jnp.int32, (t_len, HEAD_DIM), 1)
        onehot = jnp.where((kpos >> _log2(blk_len)) == kblk, 1.0, 0.0).astype(BF16)
        for c, (ch, cb) in enumerate(chain_ids):
            ka[c, :, 0:HEAD_DIM] = k_ref[ch, cb]
            ka[c, :, HEAD_DIM:2 * HEAD_DIM] = onehot

    for c, (ch, cb) in enumerate(chain_ids):
        bias = bias_ref[ch, cb]
        for g in range(group):
            qg = q_ref[ch * group + g, cb]
            if q_scale != 1.0:
                qg = (qg.astype(F32) * q_scale).astype(BF16)
            qa[c, g * tq:(g + 1) * tq, 0:HEAD_DIM] = qg
            qa[c, g * tq:(g + 1) * tq, HEAD_DIM:2 * HEAD_DIM] = bias
    m_s[...] = jnp.full_like(m_s, NEG_INF)
    acc[...] = jnp.zeros_like(acc)

    def scores(ki, slot):
        start = pl.multiple_of(ki * tk, tk)
        for c in range(chains):
            s_bufs[2 * c + slot][...] = _dot_t(ka[c, pl.ds(start, tk), :], qa[c])

    def softmax_pv(ki, slot, masked):
        for c, (ch, cb) in enumerate(chain_ids):
            s = s_bufs[2 * c + slot][...]
            if masked:
                kpos = ki * tk + lax.broadcasted_iota(jnp.int32, (tk, rows), 0)
                qpos = qi * tq + (lax.broadcasted_iota(jnp.int32, (tk, rows), 1) & (tq - 1))
                s = jnp.where(kpos <= qpos, s, NEG_INF)
            m_old = m_s[c]
            m_new = jnp.maximum(m_old, jnp.max(s, axis=0, keepdims=True))
            alpha = jnp.exp2(m_old - m_new)
            p = jnp.exp2(s - m_new)
            acc[c] = alpha * acc[c] + _dot(vt_ref[ch, cb, ki], p.astype(BF16))
            m_s[c] = m_new

    n_full = (qi * tq) >> _log2(tk)
    scores(0, 0)

    def pair(j, carry):
        k0 = 2 * j
        scores(k0 + 1, 1)
        softmax_pv(k0, 0, False)
        scores(k0 + 2, 0)
        softmax_pv(k0 + 1, 1, False)
        return carry

    lax.fori_loop(0, n_full >> 1, pair, 0)
    cur = (n_full >> 1) << 1

    @pl.when((n_full & 1) == 1)
    def _():
        scores(cur + 1, 1)
        softmax_pv(cur, 0, False)
        softmax_pv(cur + 1, 1, True)

    @pl.when((n_full & 1) == 0)
    def _():
        softmax_pv(cur, 0, True)

    outs = {}
    for c, (ch, cb) in enumerate(chain_ids):
        d = acc[c, HEAD_DIM:HEAD_DIM + 1, :]
        d = jnp.where(d > 0, d, 1.0)
        o_t = acc[c, 0:HEAD_DIM, :] * (1.0 / d)
        for g in range(group):
            og = o_t[:, g * tq:(g + 1) * tq].T
            if gate_ref is not None:
                gc = g * 3 + gate_branch
                og = og * gate_ref[ch, cb, :, gc:gc + 1]
            outs[(cb, ch * group + g)] = og
    for cb in range(chain_shape[1]):
        planes = [outs[(cb, hq)] for hq in range(chain_shape[0] * group)]
        if lanes_out:
            o_ref[0, cb] = jnp.concatenate(planes, axis=-1).astype(o_ref.dtype)
        else:
            for hq, og in enumerate(planes):
                o_ref[hq, cb] = og.astype(o_ref.dtype)


def _flash(q, bias, k, vt, *, group, tq, blk_len, q_scale, lanes_out, gate=None, gate_branch=None):
    nh, bsz, t_len, _ = k.shape
    ch = min(nh, FLASH_CHAINS)
    cb = max(1, FLASH_CHAINS // ch)
    cb = cb if bsz % cb == 0 else 1
    chains = ch * cb
    nt, tk = vt.shape[2], vt.shape[4]
    per_step = lambda h, b, i: (h, b, i, 0)
    whole_seq = lambda h, b, i: (h, b, 0, 0)
    in_specs = [
        pl.BlockSpec((ch * group, cb, tq, HEAD_DIM), per_step),
        pl.BlockSpec((ch, cb, tq, HEAD_DIM), per_step),
        pl.BlockSpec((ch, cb, t_len, HEAD_DIM), whole_seq),
        pl.BlockSpec((ch, cb, nt, V_ROWS, tk), lambda h, b, i: (h, b, 0, 0, 0)),
    ]
    args = [q, bias, k, vt]
    if gate is not None:
        in_specs.append(pl.BlockSpec((ch, cb, tq, LANES), per_step))
        args.append(gate)
    if lanes_out:
        out_shape = (1, bsz, t_len, nh * group * HEAD_DIM)
        out_spec = pl.BlockSpec((1, cb, tq, ch * group * HEAD_DIM), lambda h, b, i: (0, b, i, h))
    else:
        out_shape = (nh * group, bsz, t_len, HEAD_DIM)
        out_spec = pl.BlockSpec((ch * group, cb, tq, HEAD_DIM), per_step)
    rows = group * tq
    return pl.pallas_call(
        functools.partial(_flash_kernel, group=group, chain_shape=(ch, cb), lanes_out=lanes_out, tq=tq, tk=tk,
                          blk_len=blk_len, q_scale=q_scale, gate_branch=gate_branch),
        grid=(nh // ch, bsz // cb, t_len // tq),
        in_specs=in_specs,
        out_specs=out_spec,
        out_shape=jax.ShapeDtypeStruct(out_shape, BF16),
        scratch_shapes=[
            pltpu.VMEM((chains, rows, 2 * HEAD_DIM), BF16),
            pltpu.VMEM((chains, t_len, 2 * HEAD_DIM), BF16),
            pltpu.VMEM((chains, 1, rows), F32),
            pltpu.VMEM((chains, V_ROWS, rows), F32),
        ] + [pltpu.VMEM((tk, rows), F32)] * (2 * chains),
        compiler_params=_params("parallel", "parallel", "arbitrary"),
    )(*args)


def _even_delta(refs, scratch):
    oa_ref, oc_ref, os_ref, ow_ref, w_ref = refs
    na = oa_ref.shape[-1]
    ob = (oc_ref[...].astype(F32) + os_ref[...].astype(F32) + ow_ref[...].astype(F32)).astype(BF16)
    return _dot(oa_ref[...], w_ref[0:na, :]) + _dot(ob, w_ref[na:, :])


def _even_mix(oa, ocmp, osel, owin, w):
    aspec = pl.BlockSpec((FFN_TM, oa.shape[-1]), lambda i: (i, 0))
    return dict(args=[oa, ocmp, osel, owin, w], specs=[aspec] * 4 + [_resident(w.shape)], scratch=[],
                fn=_even_delta)


def _odd_delta(refs, scratch, *, dilations):
    ng = len(dilations)
    o_refs, l_refs = refs[:ng], refs[ng:2 * ng]
    od_ref, w_ref = refs[2 * ng:]
    scratch = list(scratch)
    tm = od_ref.shape[0]

    def tokens(ref, r):
        if r == 1:
            return ref[0, 0].astype(F32)
        pieces = []
        for c in range(ref.shape[-1] // LANES):
            scr = scratch.pop(0)
            for i in range(r):
                scr[pl.ds(i, tm // r, stride=r), :] = ref[0, i, :, c * LANES:(c + 1) * LANES].astype(F32)
            pieces.append(scr[...])
        return jnp.concatenate(pieces, axis=-1)

    outs = [tokens(ref, r) for ref, r in zip(o_refs, dilations)]
    lses = [tokens(ref, r) for ref, r in zip(l_refs, dilations)]
    mx = functools.reduce(jnp.maximum, lses)
    es = [jnp.exp(l - mx) for l in lses]
    tot = functools.reduce(lambda a, b: a + b, es)
    oc = functools.reduce(lambda a, b: a + b, [(e / tot) * o for e, o in zip(es, outs)])
    nc = oc.shape[-1]
    return _dot(oc.astype(BF16), w_ref[0:nc, :]) + _dot(od_ref[...], w_ref[nc:, :])


def _odd_mix(outs, lses, od, w, t_len):
    tm = FFN_TM
    nt = t_len // tm
    gw = outs[0].shape[-1]
    dilations = tuple(o.shape[1] for o in outs)
    gspecs = [pl.BlockSpec((1, r, tm // r, gw), lambda i: (i // nt, 0, i % nt, 0)) for r in dilations]
    n_scr = 2 * (gw // LANES) * sum(1 for r in dilations if r > 1)
    specs = gspecs + gspecs + [pl.BlockSpec((tm, od.shape[1]), lambda i: (i, 0)), _resident(w.shape)]
    return dict(args=[*outs, *lses, od, w], specs=specs, scratch=[pltpu.VMEM((tm, LANES), F32)] * n_scr,
                fn=functools.partial(_odd_delta, dilations=dilations))


def _col_ranges(sizes):
    offs, acc = [], 0
    for s in sizes:
        offs.append((acc, acc + s))
        acc += s
    return offs


def _even_mixer(h, gain, w_in, w_out, sinks, cmp_pe, cmp_w1, cmp_w2, cos, sin, bsz, t_len):
    m = bsz * t_len
    qa_w, kva_w = A_Q_HEADS * HEAD_DIM, A_KV_HEADS * HEAD_DIM
    qb_w, kvb_w = B_Q_HEADS * HEAD_DIM, B_KV_HEADS * HEAD_DIM
    sizes = [qa_w, kva_w, kva_w, qb_w] + [kvb_w] * 6 + [3 * B_Q_HEADS]
    (aq, ak, av, bq, bkc, bvc, bks, bvs, bkw, bvw, bg) = [w_in[:, a:b] for a, b in _col_ranges(sizes)]
    group = B_Q_HEADS // B_KV_HEADS
    gpad = jnp.zeros((D_MODEL, LANES - 3 * group), w_in.dtype)
    gates = [x for hk in range(B_KV_HEADS) for x in (bg[:, hk * 3 * group:(hk + 1) * 3 * group], gpad)]
    w = jnp.concatenate([aq, bq, ak, bkw, bks, av, bvw, bvs, bkc, bvc] + gates, axis=1).astype(BF16)
    c = [0]

    def take(width):
        c[0] += width
        return c[0] - width

    def out(idx, off, width, kind, **kw):
        return dict(idx=idx, off=off, width=width, kind=kind, **kw)

    plan = [
        (take(qa_w), qa_w, [out(0, 0, qa_w, "split", hw=HEAD_DIM, rope=True, scale=True)]),
        (take(qb_w), qb_w, [out(1, 0, qb_w, "split", hw=HEAD_DIM, rope=True, scale=True),
                            out(5, 0, qb_w, "flat", scale=True)]),
        (take(3 * kva_w), 3 * kva_w, [out(2, 0, kva_w, "flat", rope=True),
                                      out(3, kva_w, kvb_w, "flat", rope=True),
                                      out(4, kva_w + kvb_w, kvb_w, "split", hw=HEAD_DIM, rope=True)]),
        (take(5 * kvb_w), 5 * kvb_w, [out(6, 0, kva_w, "flat"), out(7, kva_w, kvb_w, "flat"),
                                      out(8, 2 * kvb_w, kvb_w, "vt", tile=FLASH_TK),
                                      out(9, 3 * kvb_w, kvb_w, "chunk"),
                                      out(10, 4 * kvb_w, kvb_w, "chunk")]),
        (take(B_KV_HEADS * LANES), B_KV_HEADS * LANES,
         [out(11, 0, B_KV_HEADS * LANES, "split", hw=LANES, sigmoid=True)]),
    ]
    out_defs = [("split", A_Q_HEADS, HEAD_DIM, BF16), ("split", B_Q_HEADS, HEAD_DIM, BF16), ("flat", kva_w, BF16),
                ("flat", kvb_w, BF16),
                ("split", B_KV_HEADS, HEAD_DIM, BF16), ("flat", qb_w, BF16), ("flat", kva_w, BF16),
                ("flat", kvb_w, BF16), ("vt", B_KV_HEADS, FLASH_TK, BF16), ("chunk", B_KV_HEADS, BF16),
                ("chunk", B_KV_HEADS, BF16), ("split", B_KV_HEADS, LANES, F32)]
    (aq_r, bq_r, ak_r, bkw_r, bks_r, bq_u, av_, bvw_, bvs_t, zk, zv, gate) = _proj(
        h, gain, w, cos, sin, plan, out_defs, bsz, t_len)

    def seq(x):
        return x.reshape(bsz, 1, t_len, x.shape[-1])

    def heads(x):
        return x.reshape(x.shape[0], bsz, t_len, x.shape[-1])

    gate4 = heads(gate)
    oa = _banded(heads(aq_r), seq(ak_r), seq(av_), n_kv=A_KV_HEADS, group=A_Q_HEADS // A_KV_HEADS,
                 max_dist=A_WINDOW - 1, tl=BAND_TL, sink=sinks)
    nch = t_len // NSA_CMP_STRIDE
    zshape = (B_KV_HEADS, bsz, nch, NSA_CMP_STRIDE * HEAD_DIM)
    kc, vc = _compress(zk.reshape(zshape), zv.reshape(zshape), cmp_pe.reshape(2, 1, NSA_CMP_LEN * HEAD_DIM),
                       cmp_w1.astype(BF16), cmp_w2.astype(BF16))
    ocmp, bias = _cmp_attention(bq_u.reshape(bsz, t_len, qb_w), kc, vc, gate4, t_len)
    osel = _flash(heads(bq_r), bias, heads(bks_r), bvs_t, group=group, tq=FLASH_T, blk_len=NSA_SEL_LEN,
                  q_scale=1.0, lanes_out=True, gate=gate4, gate_branch=1)
    owin = _banded(heads(bq_r), seq(bkw_r), seq(bvw_), n_kv=B_KV_HEADS, group=group,
                   max_dist=NSA_WINDOW - 1, tl=WIN_TL, gate=gate4, gate_branch=2)
    return _even_mix(oa.reshape(m, qa_w), ocmp.reshape(m, qb_w), osel.reshape(m, qb_w),
                     owin.reshape(m, qb_w), w_out.astype(BF16))


def _odd_mixer(h, gain, w_in, w_out, cos, sin, bsz, t_len):
    m = bsz * t_len
    cw = C_HEADS * HEAD_DIM
    dw = D_HEADS * HEAD_DIM
    gw = C_HEADS_PER_GROUP * HEAD_DIM
    n_groups = len(C_GROUPS)

    def group_outs(base, **kw):
        return [dict(idx=base + gi, off=gi * gw, width=gw, kind="dilate", r=r, **kw)
                for gi, (_, r) in enumerate(C_GROUPS)]

    plan = [
        (0, cw, group_outs(0, rope=True, scale=True)),
        (cw, cw, group_outs(n_groups, rope=True)),
        (2 * cw, cw, group_outs(2 * n_groups)),
        (3 * cw, 3 * dw, [dict(idx=3 * n_groups, off=0, width=dw, kind="split", hw=HEAD_DIM, rope=True),
                          dict(idx=3 * n_groups + 1, off=dw, width=dw, kind="split", hw=HEAD_DIM, rope=True),
                          dict(idx=3 * n_groups + 2, off=2 * dw, width=dw, kind="vt", tile=MOBA_TK)]),
    ]
    out_defs = [("dilate", r, gw, BF16) for _ in range(3) for _, r in C_GROUPS]
    out_defs += [("split", D_HEADS, HEAD_DIM, BF16), ("split", D_HEADS, HEAD_DIM, BF16),
                 ("vt", D_HEADS, MOBA_TK, BF16)]
    res = _proj(h, gain, w_in.astype(BF16), cos, sin, plan, out_defs, bsz, t_len)
    cq, ck, cv = res[0:n_groups], res[n_groups:2 * n_groups], res[2 * n_groups:3 * n_groups]
    dq, dk, dv_t = res[3 * n_groups:]
    outs, lses = [], []
    for gi, (wlen, r) in enumerate(C_GROUPS):
        o, lse = _banded(cq[gi], ck[gi], cv[gi], n_kv=C_HEADS_PER_GROUP, group=1, max_dist=wlen // r,
                         tl=min(BAND_TL, t_len // r), want_lse=True)
        outs.append(o)
        lses.append(lse)

    def heads(x):
        return x.reshape(x.shape[0], bsz, t_len, x.shape[-1])

    bias = _moba_gate(heads(dq), heads(dk))
    od = _flash(heads(dq), bias, heads(dk), dv_t, group=1, tq=MOBA_T, blk_len=MOBA_BLOCK, q_scale=QK_SCALE,
                lanes_out=True)
    return _odd_mix(outs, lses, od.reshape(m, D_HEADS * HEAD_DIM), w_out.astype(BF16), t_len)


def kernel(x, ffn_norm_pre, mix_norm, ffn_norm_post, ffn_wi, ffn_wo, even_w_in, even_w_out, even_sinks,
           nsa_cmp_pe, nsa_cmp_w1, nsa_cmp_w2, odd_w_in, odd_w_out, final_norm):
    bsz, t_len, _ = x.shape
    depth = ffn_wi.shape[0]
    cos, sin = _rope_tables(t_len)
    wi = ffn_wi.astype(BF16)
    wo = ffn_wo.astype(BF16)
    h = x.reshape(bsz * t_len, D_MODEL)
    for layer in range(depth):
        i = layer // 2
        h = _ffn(h, ffn_norm_pre[layer], wi, wo, (layer, 0))
        if layer % 2 == 0:
            mix = _even_mixer(h, mix_norm[layer], even_w_in[i], even_w_out[i], even_sinks[i], nsa_cmp_pe[i],
                              nsa_cmp_w1[i], nsa_cmp_w2[i], cos, sin, bsz, t_len)
        else:
            mix = _odd_mixer(h, mix_norm[layer], odd_w_in[i], odd_w_out[i], cos, sin, bsz, t_len)
        last = layer == depth - 1
        h = _ffn(h, ffn_norm_post[layer], wi, wo, (layer, 1), final_norm if last else None, mix=mix)
    return h.reshape(bsz, t_len, D_MODEL)
```

```python
import functools

import jax
import jax.numpy as jnp
from jax import lax
from jax.experimental import pallas as pl
from jax.experimental.pallas import tpu as pltpu

D_MODEL = 1024
HEAD_DIM = 64
ROPE_THETA = 10000.0
NORM_EPS = 1e-6
D_FF = 2816
NEG_INF = -1e30
FORCE_SCORE = 1e4

A_Q_HEADS = 8
A_KV_HEADS = 2
A_WINDOW = 128
B_Q_HEADS = 8
B_KV_HEADS = 2
NSA_CMP_LEN = 32
NSA_CMP_STRIDE = 16
NSA_SEL_LEN = 64
NSA_SEL_COUNT = 8
NSA_WINDOW = 512
C_GROUPS = ((128, 1), (512, 4), (2048, 16))
C_HEADS_PER_GROUP = 4
C_HEADS = len(C_GROUPS) * C_HEADS_PER_GROUP
D_HEADS = 4
MOBA_BLOCK = 256
MOBA_TOPK = 3

LANES = 128
LOG2E = 1.4426950408889634
LN2 = 0.6931471805599453
QK_SCALE = HEAD_DIM ** -0.5 * LOG2E
MASK_BIAS = NEG_INF
V_ROWS = HEAD_DIM + 16
REMOVED = -3e38
VMEM_LIMIT = 52 * 1024 * 1024

FFN_TM = 512
FFN_TM_PLAIN = 1024
FFN_TF = 256
PROJ_TM = 1024
BAND_TL = 1024
WIN_TL = 512
BAND_SB = 128
FLASH_T = 256
FLASH_TK = 512
FLASH_CHAINS = 4
MOBA_T = 512
MOBA_TK = 512
GATE_TQ = 4096
CMP_TQ = 1024

BF16 = jnp.bfloat16
F32 = jnp.float32


def _params(*sem):
    return pltpu.CompilerParams(dimension_semantics=sem, vmem_limit_bytes=VMEM_LIMIT)


def _rms(x, g):
    return x * lax.rsqrt(jnp.mean(x * x, axis=-1, keepdims=True) + NORM_EPS) * g


def _log2(n):
    assert n & (n - 1) == 0
    return n.bit_length() - 1


def _dot(a, b):
    return jnp.dot(a, b, preferred_element_type=F32)


def _dot_t(a, b):
    return lax.dot_general(a, b, (((1,), (1,)), ((), ())), preferred_element_type=F32)


def _dot_tn(a, b):
    return lax.dot_general(a, b, (((0,), (0,)), ((), ())), preferred_element_type=F32)


def _ffn_kernel(x_ref, g_ref, wi_ref, wo_ref, *rest, final, mix_fn, n_mix):
    rest = list(rest)
    fg_ref = rest.pop(0) if final else None
    mix_refs = [rest.pop(0) for _ in range(n_mix)]
    o_ref, act_scr = rest.pop(0), rest.pop(0)
    n_sub = x_ref.shape[0] // FFN_TM
    assert mix_fn is None or n_sub == 1
    for sub in range(n_sub):
        rows = slice(sub * FFN_TM, (sub + 1) * FFN_TM)
        x = x_ref[rows, :]
        if mix_fn is not None:
            x = x + mix_fn(mix_refs, rest)
        o_ref[rows, :] = x
        n = _rms(x, g_ref[...]).astype(BF16)
        for c in range(D_FF // FFN_TF):
            cols = slice(c * FFN_TF, (c + 1) * FFN_TF)
            gate = _dot(n, wi_ref[:, cols])
            up = _dot(n, wi_ref[:, D_FF + c * FFN_TF:D_FF + (c + 1) * FFN_TF])
            act_scr[rows, cols] = (gate * jax.nn.sigmoid(gate) * up).astype(BF16)
        h = o_ref[rows, :] + 0.5 * _dot(act_scr[rows, :], wo_ref[...])
        if final:
            h = _rms(h, fg_ref[...])
        o_ref[rows, :] = h


def _resident(shape, lead=()):
    block = (None,) * len(lead) + tuple(shape[len(lead):])
    index = tuple(lead) + (0,) * (len(shape) - len(lead))
    return pl.BlockSpec(block, lambda i: index, pipeline_mode=pl.Buffered(1))


def _ffn(h, gain, wi, wo, which, final_gain=None, mix=None):
    m = h.shape[0]
    tm = FFN_TM if mix is not None else FFN_TM_PLAIN
    final = final_gain is not None
    in_specs = [
        pl.BlockSpec((tm, D_MODEL), lambda i: (i, 0)),
        pl.BlockSpec((1, D_MODEL), lambda i: (0, 0)),
        _resident(wi.shape, which),
        _resident(wo.shape, which),
    ]
    args = [h, gain.reshape(1, D_MODEL), wi, wo]
    if final:
        in_specs.append(pl.BlockSpec((1, D_MODEL), lambda i: (0, 0)))
        args.append(final_gain.reshape(1, D_MODEL))
    scratch = [pltpu.VMEM((tm, D_FF), BF16)]
    if mix is not None:
        in_specs += mix["specs"]
        args += mix["args"]
        scratch += mix["scratch"]
    return pl.pallas_call(
        functools.partial(_ffn_kernel, final=final, mix_fn=mix and mix["fn"], n_mix=len(mix["args"]) if mix else 0),
        grid=(m // tm,),
        in_specs=in_specs,
        out_specs=pl.BlockSpec((tm, D_MODEL), lambda i: (i, 0)),
        out_shape=jax.ShapeDtypeStruct((m, D_MODEL), F32),
        scratch_shapes=scratch,
        compiler_params=_params("parallel"),
    )(*args)


def _rope_tables(t):
    inv = 1.0 / (ROPE_THETA ** (jnp.arange(0, HEAD_DIM, 2, dtype=F32) / HEAD_DIM))
    ang = jnp.arange(t, dtype=F32)[:, None] * inv[None, :]
    cos = jnp.cos(ang)
    sin = jnp.sin(ang)
    return (jnp.concatenate([cos, cos, cos, cos], axis=-1),
            jnp.concatenate([-sin, sin, -sin, sin], axis=-1))


def _proj_kernel(x_ref, g_ref, w_ref, cos_ref, sin_ref, *rest, plan, n_out):
    out_refs, (ys,) = rest[:n_out], rest[n_out:]
    n = _rms(x_ref[...], g_ref[...]).astype(BF16)
    tm = n.shape[0]
    cos = cos_ref[...]
    sin = sin_ref[...]
    lane = lax.broadcasted_iota(jnp.int32, cos.shape, 1)
    first_half = (lane & (HEAD_DIM - 1)) < HEAD_DIM // 2
    heads_per_tile = LANES // HEAD_DIM
    for off, width, outs in plan:
        y = _dot(n, w_ref[:, off:off + width])
        for o in outs:
            o_ref = out_refs[o["idx"]]
            dt = o_ref.dtype
            kind = o["kind"]
            for c in range(o["width"] // LANES):
                piece = y[:, o["off"] + c * LANES: o["off"] + (c + 1) * LANES]
                if o.get("rope"):
                    rot = jnp.where(first_half, pltpu.roll(piece, LANES - HEAD_DIM // 2, 1),
                                    pltpu.roll(piece, HEAD_DIM // 2, 1))
                    piece = piece * cos + rot * sin
                if o.get("scale"):
                    piece = piece * QK_SCALE
                if o.get("sigmoid"):
                    piece = jax.nn.sigmoid(piece)
                lanes = slice(c * LANES, (c + 1) * LANES)
                if kind == "flat":
                    o_ref[:, lanes] = piece.astype(dt)
                elif kind == "split":
                    hw = o["hw"]
                    per = LANES // hw
                    for k in range(per):
                        o_ref[c * per + k] = piece[:, k * hw:(k + 1) * hw].astype(dt)
                elif kind == "dilate":
                    r = o["r"]
                    if r == 1:
                        o_ref[0, 0, :, lanes] = piece.astype(dt)
                    else:
                        ys[...] = piece
                        for i in range(r):
                            o_ref[0, i, :, lanes] = ys[pl.ds(i, tm // r, stride=r), :].astype(dt)
                elif kind == "chunk":
                    ys[...] = piece
                    for l in range(NSA_CMP_STRIDE):
                        rows = ys[pl.ds(l, tm // NSA_CMP_STRIDE, stride=NSA_CMP_STRIDE), :].astype(dt)
                        for k in range(heads_per_tile):
                            o_ref[c * heads_per_tile + k, :, l * HEAD_DIM:(l + 1) * HEAD_DIM] = rows[
                                :, k * HEAD_DIM:(k + 1) * HEAD_DIM]
                else:
                    tile = o["tile"]
                    pt = piece.T
                    for k in range(heads_per_tile):
                        for kt in range(tm // tile):
                            o_ref[c * heads_per_tile + k, 0, kt, 0:HEAD_DIM, :] = pt[
                                k * HEAD_DIM:(k + 1) * HEAD_DIM, kt * tile:(kt + 1) * tile].astype(dt)
                            o_ref[c * heads_per_tile + k, 0, kt, HEAD_DIM:V_ROWS, :] = jnp.ones(
                                (V_ROWS - HEAD_DIM, tile), dt)


def _proj(h, gain, w, cos, sin, plan, out_defs, bsz, t_len):
    m = h.shape[0]
    tm = PROJ_TM
    nt = t_len // tm
    out_shapes, out_specs = [], []
    for d in out_defs:
        kind, dt = d[0], d[-1]
        if kind == "flat":
            shape, block, index = (m, d[1]), (tm, d[1]), (lambda i: (i, 0))
        elif kind == "split":
            shape, block, index = (d[1], m, d[2]), (d[1], tm, d[2]), (lambda i: (0, i, 0))
        elif kind == "dilate":
            r = d[1]
            shape, block = (bsz, r, t_len // r, d[2]), (1, r, tm // r, d[2])
            index = lambda i: (i // nt, 0, i % nt, 0)
        elif kind == "chunk":
            width = NSA_CMP_STRIDE * HEAD_DIM
            shape, block = (d[1], m // NSA_CMP_STRIDE, width), (d[1], tm // NSA_CMP_STRIDE, width)
            index = lambda i: (0, i, 0)
        else:
            tile = d[2]
            shape, block = (d[1], bsz, t_len // tile, V_ROWS, tile), (d[1], 1, tm // tile, V_ROWS, tile)
            index = lambda i: (0, i // nt, i % nt, 0, 0)
        out_shapes.append(jax.ShapeDtypeStruct(shape, dt))
        out_specs.append(pl.BlockSpec(block, index))
    return pl.pallas_call(
        functools.partial(_proj_kernel, plan=plan, n_out=len(out_defs)),
        grid=(m // tm,),
        in_specs=[
            pl.BlockSpec((tm, D_MODEL), lambda i: (i, 0)),
            pl.BlockSpec((1, D_MODEL), lambda i: (0, 0)),
            pl.BlockSpec(w.shape, lambda i: (0, 0)),
            pl.BlockSpec((tm, LANES), lambda i: (i % nt, 0)),
            pl.BlockSpec((tm, LANES), lambda i: (i % nt, 0)),
        ],
        out_specs=out_specs,
        out_shape=out_shapes,
        scratch_shapes=[pltpu.VMEM((tm, LANES), F32)],
        compiler_params=_params("parallel"),
    )(h, gain.reshape(1, D_MODEL), w, cos, sin)


def _banded_kernel(q_ref, kp_ref, kc_ref, vp_ref, vc_ref, *rest, n_kv, group, tl, pk, span, max_dist,
                   has_sink, gate_branch, want_lse):
    rest = list(rest)
    sink_ref = rest.pop(0) if has_sink else None
    gate_ref = rest.pop(0) if gate_branch is not None else None
    o_ref = rest.pop(0)
    lse_ref = rest.pop(0) if want_lse else None
    kcat, vcat, s_a, s_b, band = rest
    sb = BAND_SB
    n_seq = kc_ref.shape[1]
    for rr in range(n_seq):
        kcat[rr, 0:pk, :] = kp_ref[0, rr]
        kcat[rr, pk:, :] = kc_ref[0, rr]
        vcat[rr, 0:pk, :] = vp_ref[0, rr]
        vcat[rr, pk:, :] = vc_ref[0, rr]
    nk = span + sb
    kw = n_kv * HEAD_DIM
    merged = group == 1
    stacked = n_kv if merged else group
    rows = stacked * sb
    key = lax.broadcasted_iota(jnp.int32, (nk, rows), 0)
    qry = lax.broadcasted_iota(jnp.int32, (nk, rows), 1) & (sb - 1)
    dist = qry + span - key
    in_band = (dist >= 0) & (dist <= max_dist)
    lane_head = lax.broadcasted_iota(jnp.int32, (1, rows), 1) >> _log2(sb)
    q_head = lax.broadcasted_iota(jnp.int32, (sb, kw), 1) >> _log2(HEAD_DIM)
    v_head = lax.broadcasted_iota(jnp.int32, (nk, kw), 1) >> _log2(HEAD_DIM)
    assert merged or kw == LANES
    items = [(rr, j, h) for rr in range(n_seq) for j in range(tl // sb)
             for h in ([0] if merged else range(n_kv))]
    s_bufs = (s_a, s_b)

    def compute(first_tile):
        def scores(n):
            rr, j, h = items[n]
            lo = pk + j * sb - span
            qrows = slice(j * sb, (j + 1) * sb)
            if merged:
                qj = q_ref[0, rr, qrows, :].astype(F32)
                qs = jnp.concatenate([jnp.where(q_head == r, qj, 0.0).astype(BF16) for r in range(n_kv)], axis=0)
                kk = kcat[rr, lo:lo + nk, :]
            else:
                qs = jnp.concatenate([q_ref[h * group + g, 0, qrows, :] for g in range(group)], axis=0)
                kk = kcat[rr, lo:lo + nk, h * HEAD_DIM:(h + 1) * HEAD_DIM]
            variant = 1 + j if first_tile and j * sb < span else 0
            s_bufs[n % 2][...] = _dot_t(kk, qs) + band[variant]

        band[0] = jnp.where(in_band, 0.0, NEG_INF)
        if first_tile:
            for j in range(span // sb):
                band[1 + j] = jnp.where(in_band & (key >= span - j * sb), 0.0, NEG_INF)

        outs, lses = [], []
        scores(0)
        for n, (rr, j, h) in enumerate(items):
            if n + 1 < len(items):
                scores(n + 1)
            s = s_bufs[n % 2][...]
            lo = pk + j * sb - span
            qrows = slice(j * sb, (j + 1) * sb)
            m = jnp.max(s, axis=0, keepdims=True)
            if has_sink:
                sink = jnp.full((1, rows), sink_ref[h * group], F32)
                for g in range(1, group):
                    sink = jnp.where(lane_head == g, sink_ref[h * group + g], sink)
                sink = sink * LOG2E
                m = jnp.maximum(m, sink)
            p = jnp.exp2(s - m).astype(BF16)
            vwin = vcat[rr, lo:lo + nk, :]
            if merged:
                vv = jnp.concatenate([vwin, jnp.ones((nk, LANES), BF16)], axis=1)
                ones_row = kw
            else:
                vv = jnp.where(v_head == h, vwin.astype(F32), 1.0).astype(BF16)
                ones_row = (1 - h) * HEAD_DIM
            o_t = _dot_tn(vv, p)
            d = o_t[ones_row:ones_row + 1, :]
            if has_sink:
                d = d + jnp.exp2(sink - m)
            d = jnp.where(d > 0, d, 1.0)
            inv = 1.0 / d
            if want_lse:
                lse_t = jnp.broadcast_to(m * LN2 + jnp.log(d), (HEAD_DIM, rows))
            for r in range(stacked):
                cols = slice(r * sb, (r + 1) * sb)
                vrow = (r if merged else h) * HEAD_DIM
                o = (o_t[vrow:vrow + HEAD_DIM, cols] * inv[:, cols]).T
                if gate_ref is not None:
                    gc = r * 3 + gate_branch
                    o = o * gate_ref[h, 0, qrows, gc:gc + 1]
                outs.append(o)
                if want_lse:
                    lses.append(lse_t[:, cols].T)
            if merged or h == n_kv - 1:
                o_ref[0, rr, qrows, :] = jnp.concatenate(outs, axis=-1).astype(o_ref.dtype)
                if want_lse:
                    lse_ref[0, rr, qrows, :] = jnp.concatenate(lses, axis=-1)
                outs, lses = [], []

    first = pl.program_id(2) == 0
    pl.when(first)(functools.partial(compute, True))
    pl.when(jnp.logical_not(first))(functools.partial(compute, False))


def _banded(q, k, v, *, n_kv, group, max_dist, tl, sink=None, gate=None, gate_branch=None, want_lse=False):
    bsz, n_res, length, kw = k.shape
    qw = n_kv * group * HEAD_DIM
    span = -(-max_dist // BAND_SB) * BAND_SB
    pk = span
    assert tl % pk == 0
    s_shape = (span + BAND_SB, (n_kv if group == 1 else group) * BAND_SB)
    ratio = tl // pk
    n_seq = min(n_res, max(1, BAND_TL // tl))
    prev = lambda b, i, l: (b, i, jnp.maximum(l * ratio - 1, 0), 0)
    cur = lambda b, i, l: (b, i, l, 0)
    if group == 1:
        q_spec = pl.BlockSpec((1, n_seq, tl, qw), cur)
    else:
        q_spec = pl.BlockSpec((n_kv * group, 1, tl, HEAD_DIM), lambda b, i, l: (0, b, l, 0))
    o_shape = (bsz, n_res, length, qw)
    in_specs = [
        q_spec,
        pl.BlockSpec((1, n_seq, pk, kw), prev),
        pl.BlockSpec((1, n_seq, tl, kw), cur),
        pl.BlockSpec((1, n_seq, pk, kw), prev),
        pl.BlockSpec((1, n_seq, tl, kw), cur),
    ]
    args = [q, k, k, v, v]
    if sink is not None:
        in_specs.append(pl.BlockSpec(memory_space=pltpu.SMEM))
        args.append(sink)
    if gate is not None:
        in_specs.append(pl.BlockSpec((n_kv, 1, tl, LANES), lambda b, i, l: (0, b, l, 0)))
        args.append(gate)
    out_spec = pl.BlockSpec((1, n_seq, tl, qw), cur)
    out_shape = [jax.ShapeDtypeStruct(o_shape, BF16)]
    out_specs = [out_spec]
    if want_lse:
        out_shape.append(jax.ShapeDtypeStruct(o_shape, F32))
        out_specs.append(out_spec)
    res = pl.pallas_call(
        functools.partial(_banded_kernel, n_kv=n_kv, group=group, tl=tl, pk=pk, span=span,
                          max_dist=max_dist, has_sink=sink is not None, gate_branch=gate_branch,
                          want_lse=want_lse),
        grid=(bsz, n_res // n_seq, length // tl),
        in_specs=in_specs,
        out_specs=out_specs,
        out_shape=out_shape,
        scratch_shapes=[pltpu.VMEM((n_seq, pk + tl, kw), BF16), pltpu.VMEM((n_seq, pk + tl, kw), BF16),
                        pltpu.VMEM(s_shape, F32), pltpu.VMEM(s_shape, F32),
                        pltpu.VMEM((1 + span // BAND_SB,) + s_shape, F32)],
        compiler_params=_params("parallel", "parallel", "parallel"),
    )(*args)
    return res if want_lse else res[0]


def _compress_one(z_ref, pe_ref, w1_ref, w2_ref, o_ref):
    half = NSA_CMP_STRIDE * HEAD_DIM
    z = z_ref[0, 0].astype(F32)
    za = (z + pe_ref[0, :, 0:half]).astype(BF16)
    zb = (z + pe_ref[0, :, half:2 * half]).astype(BF16)
    ya = _dot(za, w1_ref[0, 0:half, :])
    yb = _dot(zb, w1_ref[0, half:2 * half, :])
    nrow = ya.shape[0]
    hid = jax.nn.gelu(ya + pltpu.roll(yb, nrow - 1, 0))
    o_ref[0, 0] = _dot(hid.astype(BF16), w2_ref[0]).astype(o_ref.dtype)


def _compress_kernel(zk_ref, zv_ref, pek_ref, pev_ref, w1k_ref, w1v_ref, w2k_ref, w2v_ref, ok_ref, ov_ref):
    _compress_one(zk_ref, pek_ref, w1k_ref, w2k_ref, ok_ref)
    _compress_one(zv_ref, pev_ref, w1v_ref, w2v_ref, ov_ref)


def _compress(zk, zv, pe, w1, w2):
    hk, bsz, nch, width = zk.shape
    zspec = pl.BlockSpec((1, 1, nch, width), lambda h, b: (h, b, 0, 0))
    ospec = pl.BlockSpec((1, 1, nch, HEAD_DIM), lambda h, b: (h, b, 0, 0))
    oshape = jax.ShapeDtypeStruct((hk, bsz, nch, HEAD_DIM), BF16)

    def wspec(arr, which):
        return pl.BlockSpec((1,) + arr.shape[1:], lambda h, b: (which, 0, 0))

    return pl.pallas_call(
        _compress_kernel,
        grid=(hk, bsz),
        in_specs=[zspec, zspec, wspec(pe, 0), wspec(pe, 1), wspec(w1, 0), wspec(w1, 1), wspec(w2, 0), wspec(w2, 1)],
        out_specs=[ospec, ospec],
        out_shape=[oshape, oshape],
        compiler_params=_params("parallel", "parallel"),
    )(zk, zv, pe, pe, w1, w1, w2, w2)


def _topk_select(score, k, idxf):
    ncand = score.shape[0]
    rem = score
    sel = jnp.zeros_like(score)
    for _ in range(k):
        mx = jnp.max(rem, axis=0, keepdims=True)
        first = jnp.min(jnp.where(rem == mx, idxf, float(ncand)), axis=0, keepdims=True)
        pick = idxf == first
        sel = jnp.where(pick, jnp.where(mx > 0.5 * NEG_INF, 1.0, 0.0), sel)
        rem = jnp.where(pick, REMOVED, rem)
    return sel


def _cmp_kernel(q_ref, kc_ref, vc_ref, gate_ref, o_ref, bias_ref, *, tq, group, n_tiles):
    qi = pl.program_id(1)

    def tile(ncp):
        for hk in range(kc_ref.shape[0]):
            _cmp_tile(q_ref, kc_ref, vc_ref, gate_ref, o_ref, bias_ref, qi, hk, tq=tq, group=group, ncp=ncp)

    for v in range(n_tiles):
        pl.when(qi == v)(functools.partial(tile, min(kc_ref.shape[2], (v + 1) * tq // NSA_CMP_STRIDE)))


def _cmp_tile(q_ref, kc_ref, vc_ref, gate_ref, o_ref, bias_ref, qi, hk, *, tq, group, ncp):
    kc = kc_ref[hk, 0, 0:ncp, :]
    vc = vc_ref[hk, 0, 0:ncp, :]
    gw = group * HEAD_DIM
    nsb = bias_ref.shape[-1]
    rows = group * tq
    last = lax.broadcasted_iota(jnp.int32, (ncp, 1), 0) * NSA_CMP_STRIDE + (NSA_CMP_LEN - 1)
    t_in_tile = lax.broadcasted_iota(jnp.int32, (1, rows), 1) & (tq - 1)
    visible = (last - t_in_tile) <= qi * tq
    qs = jnp.concatenate([q_ref[0, :, hk * gw + g * HEAD_DIM:hk * gw + (g + 1) * HEAD_DIM]
                          for g in range(group)], axis=0)
    s = jnp.where(visible, _dot_t(kc, qs), NEG_INF)
    m = jnp.max(s, axis=0, keepdims=True)
    m = jnp.where(m > 0.5 * NEG_INF, m, 0.0)
    p = jnp.exp2(s - m)
    vc_ones = jnp.concatenate([vc, jnp.ones_like(vc)], axis=1)
    o_t = _dot_tn(vc_ones, p.astype(BF16))
    d = o_t[HEAD_DIM:HEAD_DIM + 1, :]
    d = jnp.where(d > 0, d, 1.0)
    inv = 1.0 / d
    pc = p * inv
    o_t = o_t[0:HEAD_DIM, :] * inv
    outs = []
    psum = pc[:, 0:tq]
    for g in range(group):
        if g:
            psum = psum + pc[:, g * tq:(g + 1) * tq]
        outs.append(o_t[:, g * tq:(g + 1) * tq].T * gate_ref[hk, 0, :, g * 3:g * 3 + 1])
    o_ref[0, :, hk * gw:(hk + 1) * gw] = jnp.concatenate(outs, axis=-1).astype(o_ref.dtype)

    jj = lax.broadcasted_iota(jnp.int32, (nsb, ncp), 0) * NSA_SEL_LEN
    nn = lax.broadcasted_iota(jnp.int32, (nsb, ncp), 1) * NSA_CMP_STRIDE
    ov = jnp.maximum(jnp.minimum(nn + NSA_CMP_LEN, jj + NSA_SEL_LEN) - jnp.maximum(nn, jj), 0)
    ov = (ov.astype(F32) / NSA_CMP_LEN).astype(BF16)
    p_hi = psum.astype(BF16)
    p_lo = (psum - p_hi.astype(F32)).astype(BF16)
    imp = _dot(ov, p_hi) + _dot(ov, p_lo)

    j = lax.broadcasted_iota(jnp.int32, (nsb, tq), 0)
    tt = qi * tq + lax.broadcasted_iota(jnp.int32, (nsb, tq), 1)
    cb = tt >> _log2(NSA_SEL_LEN)
    forced = (j == 0) | (j == cb) | (j == cb - 1)
    assert FORCE_SCORE > group
    n_forced = 3
    free = jnp.where((j <= cb) & jnp.logical_not(forced), imp, NEG_INF)
    sel = _topk_select(free, min(NSA_SEL_COUNT, nsb) - n_forced, j.astype(F32))
    keep = (sel > 0.5) | (forced & (j <= cb))
    bias_ref[hk, 0] = jnp.where(keep, 0.0, MASK_BIAS).T.astype(bias_ref.dtype)


def _cmp_attention(qu, kc, vc, gate, t_len):
    bsz = qu.shape[0]
    hk, _, ncp, _ = kc.shape
    group = B_Q_HEADS // B_KV_HEADS
    tq = CMP_TQ
    nsb = t_len // NSA_SEL_LEN
    gw = group * HEAD_DIM
    cspec = pl.BlockSpec((hk, 1, ncp, HEAD_DIM), lambda b, i: (0, b, 0, 0))
    return pl.pallas_call(
        functools.partial(_cmp_kernel, tq=tq, group=group, n_tiles=t_len // tq),
        grid=(bsz, t_len // tq),
        in_specs=[
            pl.BlockSpec((1, tq, hk * gw), lambda b, i: (b, i, 0)),
            cspec, cspec,
            pl.BlockSpec((hk, 1, tq, LANES), lambda b, i: (0, b, i, 0)),
        ],
        out_specs=[
            pl.BlockSpec((1, tq, hk * gw), lambda b, i: (b, i, 0)),
            pl.BlockSpec((hk, 1, tq, nsb), lambda b, i: (0, b, i, 0)),
        ],
        out_shape=[
            jax.ShapeDtypeStruct((bsz, t_len, hk * gw), BF16),
            jax.ShapeDtypeStruct((hk, bsz, t_len, nsb), BF16),
        ],
        compiler_params=_params("parallel", "parallel"),
    )(qu, kc, vc, gate)


def _moba_gate_kernel(q_ref, k_ref, bias_ref, km_hi, km_lo, *, tq):
    qi = pl.program_id(1)
    nh, _, t_len, _ = k_ref.shape
    ncol = bias_ref.shape[-1]
    nb = t_len // MOBA_BLOCK

    @pl.when(qi == 0)
    def _():
        blk = lax.broadcasted_iota(jnp.int32, (ncol, t_len), 0)
        pos = lax.broadcasted_iota(jnp.int32, (ncol, t_len), 1)
        member = jnp.where((pos >> _log2(MOBA_BLOCK)) == blk, 1.0, 0.0).astype(BF16)
        for h in range(nh):
            kmean = _dot(member, k_ref[h, 0]) * (1.0 / MOBA_BLOCK)
            hi = kmean.astype(BF16)
            km_hi[h] = hi
            km_lo[h] = (kmean - hi.astype(F32)).astype(BF16)

    j = lax.broadcasted_iota(jnp.int32, (nb, tq), 0)
    t = qi * tq + lax.broadcasted_iota(jnp.int32, (nb, tq), 1)
    cb = t >> _log2(MOBA_BLOCK)
    for h in range(nh):
        q = q_ref[h, 0]
        gate = (_dot_t(km_hi[h], q) + _dot_t(km_lo[h], q))[0:nb]
        score = jnp.where(j < cb, gate, NEG_INF)
        sel = _topk_select(score, min(MOBA_TOPK, nb - 1), j.astype(F32))
        keep = (sel > 0.5) | (j == cb)
        bias = jnp.where(keep, 0.0, MASK_BIAS)
        bias = jnp.concatenate([bias, jnp.zeros((ncol - nb, tq), F32)], axis=0)
        bias_ref[h, 0] = bias.T.astype(bias_ref.dtype)


def _moba_gate(q, k):
    nh, bsz, t_len, _ = q.shape
    tq = min(GATE_TQ, t_len)
    return pl.pallas_call(
        functools.partial(_moba_gate_kernel, tq=tq),
        grid=(bsz, t_len // tq),
        in_specs=[
            pl.BlockSpec((nh, 1, tq, HEAD_DIM), lambda b, i: (0, b, i, 0)),
            pl.BlockSpec((nh, 1, t_len, HEAD_DIM), lambda b, i: (0, b, 0, 0)),
        ],
        out_specs=pl.BlockSpec((nh, 1, tq, HEAD_DIM), lambda b, i: (0, b, i, 0)),
        out_shape=jax.ShapeDtypeStruct((nh, bsz, t_len, HEAD_DIM), BF16),
        scratch_shapes=[pltpu.VMEM((nh, HEAD_DIM, HEAD_DIM), BF16), pltpu.VMEM((nh, HEAD_DIM, HEAD_DIM), BF16)],
        compiler_params=_params("parallel", "arbitrary"),
    )(q, k)


def _flash_kernel(q_ref, bias_ref, k_ref, vt_ref, *rest, group, chain_shape, lanes_out, tq, tk, blk_len, q_scale,
                  gate_branch):
    rest = list(rest)
    gate_ref = rest.pop(0) if gate_branch is not None else None
    o_ref, qa, ka, m_s, acc = rest[:5]
    s_bufs = rest[5:]
    chain_ids = [(ch, cb) for cb in range(chain_shape[1]) for ch in range(chain_shape[0])]
    chains = len(chain_ids)
    qi = pl.program_id(2)
    rows = group * tq
    t_len = k_ref.shape[2]

    @pl.when(qi == 0)
    def _():
        kpos = lax.broadcasted_iota(jnp.int32, (t_len, HEAD_DIM), 0)
        kblk = lax.broadcasted_iota(jnp.int32, (t_len, HEAD_DIM), 1)
        onehot = jnp.where((kpos >> _log2(blk_len)) == kblk, 1.0, 0.0).astype(BF16)
        for c, (ch, cb) in enumerate(chain_ids):
            ka[c, :, 0:HEAD_DIM] = k_ref[ch, cb]
            ka[c, :, HEAD_DIM:2 * HEAD_DIM] = onehot

    for c, (ch, cb) in enumerate(chain_ids):
        bias = bias_ref[ch, cb]
        for g in range(group):
            qg = q_ref[ch * group + g, cb]
            if q_scale != 1.0:
                qg = (qg.astype(F32) * q_scale).astype(BF16)
            qa[c, g * tq:(g + 1) * tq, 0:HEAD_DIM] = qg
            qa[c, g * tq:(g + 1) * tq, HEAD_DIM:2 * HEAD_DIM] = bias
    m_s[...] = jnp.full_like(m_s, NEG_INF)
    acc[...] = jnp.zeros_like(acc)

    def scores(ki, slot):
        start = pl.multiple_of(ki * tk, tk)
        for c in range(chains):
            s_bufs[2 * c + slot][...] = _dot_t(ka[c, pl.ds(start, tk), :], qa[c])

    def softmax_pv(ki, slot, masked):
        for c, (ch, cb) in enumerate(chain_ids):
            s = s_bufs[2 * c + slot][...]
            if masked:
                kpos = ki * tk + lax.broadcasted_iota(jnp.int32, (tk, rows), 0)
                qpos = qi * tq + (lax.broadcasted_iota(jnp.int32, (tk, rows), 1) & (tq - 1))
                s = jnp.where(kpos <= qpos, s, NEG_INF)
            m_old = m_s[c]
            m_new = jnp.maximum(m_old, jnp.max(s, axis=0, keepdims=True))
            alpha = jnp.exp2(m_old - m_new)
            p = jnp.exp2(s - m_new)
            acc[c] = alpha * acc[c] + _dot(vt_ref[ch, cb, ki], p.astype(BF16))
            m_s[c] = m_new

    n_full = (qi * tq) >> _log2(tk)
    scores(0, 0)

    def pair(j, carry):
        k0 = 2 * j
        scores(k0 + 1, 1)
        softmax_pv(k0, 0, False)
        scores(k0 + 2, 0)
        softmax_pv(k0 + 1, 1, False)
        return carry

    lax.fori_loop(0, n_full >> 1, pair, 0)
    cur = (n_full >> 1) << 1

    @pl.when((n_full & 1) == 1)
    def _():
        scores(cur + 1, 1)
        softmax_pv(cur, 0, False)
        softmax_pv(cur + 1, 1, True)

    @pl.when((n_full & 1) == 0)
    def _():
        softmax_pv(cur, 0, True)

    outs = {}
    for c, (ch, cb) in enumerate(chain_ids):
        d = acc[c, HEAD_DIM:HEAD_DIM + 1, :]
        d = jnp.where(d > 0, d, 1.0)
        o_t = acc[c, 0:HEAD_DIM, :] * (1.0 / d)
        for g in range(group):
            og = o_t[:, g * tq:(g + 1) * tq].T
            if gate_ref is not None:
                gc = g * 3 + gate_branch
                og = og * gate_ref[ch, cb, :, gc:gc + 1]
            outs[(cb, ch * group + g)] = og
    for cb in range(chain_shape[1]):
        planes = [outs[(cb, hq)] for hq in range(chain_shape[0] * group)]
        if lanes_out:
            o_ref[0, cb] = jnp.concatenate(planes, axis=-1).astype(o_ref.dtype)
        else:
            for hq, og in enumerate(planes):
                o_ref[hq, cb] = og.astype(o_ref.dtype)


def _flash(q, bias, k, vt, *, group, tq, blk_len, q_scale, lanes_out, gate=None, gate_branch=None):
    nh, bsz, t_len, _ = k.shape
    ch = min(nh, FLASH_CHAINS)
    cb = max(1, FLASH_CHAINS // ch)
    cb = cb if bsz % cb == 0 else 1
    chains = ch * cb
    nt, tk = vt.shape[2], vt.shape[4]
    per_step = lambda h, b, i: (h, b, i, 0)
    whole_seq = lambda h, b, i: (h, b, 0, 0)
    in_specs = [
        pl.BlockSpec((ch * group, cb, tq, HEAD_DIM), per_step),
        pl.BlockSpec((ch, cb, tq, HEAD_DIM), per_step),
        pl.BlockSpec((ch, cb, t_len, HEAD_DIM), whole_seq),
        pl.BlockSpec((ch, cb, nt, V_ROWS, tk), lambda h, b, i: (h, b, 0, 0, 0)),
    ]
    args = [q, bias, k, vt]
    if gate is not None:
        in_specs.append(pl.BlockSpec((ch, cb, tq, LANES), per_step))
        args.append(gate)
    if lanes_out:
        out_shape = (1, bsz, t_len, nh * group * HEAD_DIM)
        out_spec = pl.BlockSpec((1, cb, tq, ch * group * HEAD_DIM), lambda h, b, i: (0, b, i, h))
    else:
        out_shape = (nh * group, bsz, t_len, HEAD_DIM)
        out_spec = pl.BlockSpec((ch * group, cb, tq, HEAD_DIM), per_step)
    rows = group * tq
    return pl.pallas_call(
        functools.partial(_flash_kernel, group=group, chain_shape=(ch, cb), lanes_out=lanes_out, tq=tq, tk=tk,
                          blk_len=blk_len, q_scale=q_scale, gate_branch=gate_branch),
        grid=(nh // ch, bsz // cb, t_len // tq),
        in_specs=in_specs,
        out_specs=out_spec,
        out_shape=jax.ShapeDtypeStruct(out_shape, BF16),
        scratch_shapes=[
            pltpu.VMEM((chains, rows, 2 * HEAD_DIM), BF16),
            pltpu.VMEM((chains, t_len, 2 * HEAD_DIM), BF16),
            pltpu.VMEM((chains, 1, rows), F32),
            pltpu.VMEM((chains, V_ROWS, rows), F32),
        ] + [pltpu.VMEM((tk, rows), F32)] * (2 * chains),
        compiler_params=_params("parallel", "parallel", "arbitrary"),
    )(*args)


def _even_delta(refs, scratch):
    oa_ref, oc_ref, os_ref, ow_ref, w_ref = refs
    na = oa_ref.shape[-1]
    ob = (oc_ref[...].astype(F32) + os_ref[...].astype(F32) + ow_ref[...].astype(F32)).astype(BF16)
    return _dot(oa_ref[...], w_ref[0:na, :]) + _dot(ob, w_ref[na:, :])


def _even_mix(oa, ocmp, osel, owin, w):
    aspec = pl.BlockSpec((FFN_TM, oa.shape[-1]), lambda i: (i, 0))
    return dict(args=[oa, ocmp, osel, owin, w], specs=[aspec] * 4 + [_resident(w.shape)], scratch=[],
                fn=_even_delta)


def _odd_delta(refs, scratch, *, dilations):
    ng = len(dilations)
    o_refs, l_refs = refs[:ng], refs[ng:2 * ng]
    od_ref, w_ref = refs[2 * ng:]
    scratch = list(scratch)
    tm = od_ref.shape[0]

    def tokens(ref, r):
        if r == 1:
            return ref[0, 0].astype(F32)
        pieces = []
        for c in range(ref.shape[-1] // LANES):
            scr = scratch.pop(0)
            for i in range(r):
                scr[pl.ds(i, tm // r, stride=r), :] = ref[0, i, :, c * LANES:(c + 1) * LANES].astype(F32)
            pieces.append(scr[...])
        return jnp.concatenate(pieces, axis=-1)

    outs = [tokens(ref, r) for ref, r in zip(o_refs, dilations)]
    lses = [tokens(ref, r) for ref, r in zip(l_refs, dilations)]
    mx = functools.reduce(jnp.maximum, lses)
    es = [jnp.exp(l - mx) for l in lses]
    tot = functools.reduce(lambda a, b: a + b, es)
    oc = functools.reduce(lambda a, b: a + b, [(e / tot) * o for e, o in zip(es, outs)])
    nc = oc.shape[-1]
    return _dot(oc.astype(BF16), w_ref[0:nc, :]) + _dot(od_ref[...], w_ref[nc:, :])


def _odd_mix(outs, lses, od, w, t_len):
    tm = FFN_TM
    nt = t_len // tm
    gw = outs[0].shape[-1]
    dilations = tuple(o.shape[1] for o in outs)
    gspecs = [pl.BlockSpec((1, r, tm // r, gw), lambda i: (i // nt, 0, i % nt, 0)) for r in dilations]
    n_scr = 2 * (gw // LANES) * sum(1 for r in dilations if r > 1)
    specs = gspecs + gspecs + [pl.BlockSpec((tm, od.shape[1]), lambda i: (i, 0)), _resident(w.shape)]
    return dict(args=[*outs, *lses, od, w], specs=specs, scratch=[pltpu.VMEM((tm, LANES), F32)] * n_scr,
                fn=functools.partial(_odd_delta, dilations=dilations))


def _col_ranges(sizes):
    offs, acc = [], 0
    for s in sizes:
        offs.append((acc, acc + s))
        acc += s
    return offs


def _even_mixer(h, gain, w_in, w_out, sinks, cmp_pe, cmp_w1, cmp_w2, cos, sin, bsz, t_len):
    m = bsz * t_len
    qa_w, kva_w = A_Q_HEADS * HEAD_DIM, A_KV_HEADS * HEAD_DIM
    qb_w, kvb_w = B_Q_HEADS * HEAD_DIM, B_KV_HEADS * HEAD_DIM
    sizes = [qa_w, kva_w, kva_w, qb_w] + [kvb_w] * 6 + [3 * B_Q_HEADS]
    (aq, ak, av, bq, bkc, bvc, bks, bvs, bkw, bvw, bg) = [w_in[:, a:b] for a, b in _col_ranges(sizes)]
    group = B_Q_HEADS // B_KV_HEADS
    gpad = jnp.zeros((D_MODEL, LANES - 3 * group), w_in.dtype)
    gates = [x for hk in range(B_KV_HEADS) for x in (bg[:, hk * 3 * group:(hk + 1) * 3 * group], gpad)]
    w = jnp.concatenate([aq, bq, ak, bkw, bks, av, bvw, bvs, bkc, bvc] + gates, axis=1).astype(BF16)
    c = [0]

    def take(width):
        c[0] += width
        return c[0] - width

    def out(idx, off, width, kind, **kw):
        return dict(idx=idx, off=off, width=width, kind=kind, **kw)

    plan = [
        (take(qa_w), qa_w, [out(0, 0, qa_w, "split", hw=HEAD_DIM, rope=True, scale=True)]),
        (take(qb_w), qb_w, [out(1, 0, qb_w, "split", hw=HEAD_DIM, rope=True, scale=True),
                            out(5, 0, qb_w, "flat", scale=True)]),
        (take(3 * kva_w), 3 * kva_w, [out(2, 0, kva_w, "flat", rope=True),
                                      out(3, kva_w, kvb_w, "flat", rope=True),
                                      out(4, kva_w + kvb_w, kvb_w, "split", hw=HEAD_DIM, rope=True)]),
        (take(5 * kvb_w), 5 * kvb_w, [out(6, 0, kva_w, "flat"), out(7, kva_w, kvb_w, "flat"),
                                      out(8, 2 * kvb_w, kvb_w, "vt", tile=FLASH_TK),
                                      out(9, 3 * kvb_w, kvb_w, "chunk"),
                                      out(10, 4 * kvb_w, kvb_w, "chunk")]),
        (take(B_KV_HEADS * LANES), B_KV_HEADS * LANES,
         [out(11, 0, B_KV_HEADS * LANES, "split", hw=LANES, sigmoid=True)]),
    ]
    out_defs = [("split", A_Q_HEADS, HEAD_DIM, BF16), ("split", B_Q_HEADS, HEAD_DIM, BF16), ("flat", kva_w, BF16),
                ("flat", kvb_w, BF16),
                ("split", B_KV_HEADS, HEAD_DIM, BF16), ("flat", qb_w, BF16), ("flat", kva_w, BF16),
                ("flat", kvb_w, BF16), ("vt", B_KV_HEADS, FLASH_TK, BF16), ("chunk", B_KV_HEADS, BF16),
                ("chunk", B_KV_HEADS, BF16), ("split", B_KV_HEADS, LANES, F32)]
    (aq_r, bq_r, ak_r, bkw_r, bks_r, bq_u, av_, bvw_, bvs_t, zk, zv, gate) = _proj(
        h, gain, w, cos, sin, plan, out_defs, bsz, t_len)

    def seq(x):
        return x.reshape(bsz, 1, t_len, x.shape[-1])

    def heads(x):
        return x.reshape(x.shape[0], bsz, t_len, x.shape[-1])

    gate4 = heads(gate)
    oa = _banded(heads(aq_r), seq(ak_r), seq(av_), n_kv=A_KV_HEADS, group=A_Q_HEADS // A_KV_HEADS,
                 max_dist=A_WINDOW - 1, tl=BAND_TL, sink=sinks)
    nch = t_len // NSA_CMP_STRIDE
    zshape = (B_KV_HEADS, bsz, nch, NSA_CMP_STRIDE * HEAD_DIM)
    kc, vc = _compress(zk.reshape(zshape), zv.reshape(zshape), cmp_pe.reshape(2, 1, NSA_CMP_LEN * HEAD_DIM),
                       cmp_w1.astype(BF16), cmp_w2.astype(BF16))
    ocmp, bias = _cmp_attention(bq_u.reshape(bsz, t_len, qb_w), kc, vc, gate4, t_len)
    osel = _flash(heads(bq_r), bias, heads(bks_r), bvs_t, group=group, tq=FLASH_T, blk_len=NSA_SEL_LEN,
                  q_scale=1.0, lanes_out=True, gate=gate4, gate_branch=1)
    owin = _banded(heads(bq_r), seq(bkw_r), seq(bvw_), n_kv=B_KV_HEADS, group=group,
                   max_dist=NSA_WINDOW - 1, tl=WIN_TL, gate=gate4, gate_branch=2)
    return _even_mix(oa.reshape(m, qa_w), ocmp.reshape(m, qb_w), osel.reshape(m, qb_w),
                     owin.reshape(m, qb_w), w_out.astype(BF16))


def _odd_mixer(h, gain, w_in, w_out, cos, sin, bsz, t_len):
    m = bsz * t_len
    cw = C_HEADS * HEAD_DIM
    dw = D_HEADS * HEAD_DIM
    gw = C_HEADS_PER_GROUP * HEAD_DIM
    n_groups = len(C_GROUPS)

    def group_outs(base, **kw):
        return [dict(idx=base + gi, off=gi * gw, width=gw, kind="dilate", r=r, **kw)
                for gi, (_, r) in enumerate(C_GROUPS)]

    plan = [
        (0, cw, group_outs(0, rope=True, scale=True)),
        (cw, cw, group_outs(n_groups, rope=True)),
        (2 * cw, cw, group_outs(2 * n_groups)),
        (3 * cw, 3 * dw, [dict(idx=3 * n_groups, off=0, width=dw, kind="split", hw=HEAD_DIM, rope=True),
                          dict(idx=3 * n_groups + 1, off=dw, width=dw, kind="split", hw=HEAD_DIM, rope=True),
                          dict(idx=3 * n_groups + 2, off=2 * dw, width=dw, kind="vt", tile=MOBA_TK)]),
    ]
    out_defs = [("dilate", r, gw, BF16) for _ in range(3) for _, r in C_GROUPS]
    out_defs += [("split", D_HEADS, HEAD_DIM, BF16), ("split", D_HEADS, HEAD_DIM, BF16),
                 ("vt", D_HEADS, MOBA_TK, BF16)]
    res = _proj(h, gain, w_in.astype(BF16), cos, sin, plan, out_defs, bsz, t_len)
    cq, ck, cv = res[0:n_groups], res[n_groups:2 * n_groups], res[2 * n_groups:3 * n_groups]
    dq, dk, dv_t = res[3 * n_groups:]
    outs, lses = [], []
    for gi, (wlen, r) in enumerate(C_GROUPS):
        o, lse = _banded(cq[gi], ck[gi], cv[gi], n_kv=C_HEADS_PER_GROUP, group=1, max_dist=wlen // r,
                         tl=min(BAND_TL, t_len // r), want_lse=True)
        outs.append(o)
        lses.append(lse)

    def heads(x):
        return x.reshape(x.shape[0], bsz, t_len, x.shape[-1])

    bias = _moba_gate(heads(dq), heads(dk))
    od = _flash(heads(dq), bias, heads(dk), dv_t, group=1, tq=MOBA_T, blk_len=MOBA_BLOCK, q_scale=QK_SCALE,
                lanes_out=True)
    return _odd_mix(outs, lses, od.reshape(m, D_HEADS * HEAD_DIM), w_out.astype(BF16), t_len)


def kernel(x, ffn_norm_pre, mix_norm, ffn_norm_post, ffn_wi, ffn_wo, even_w_in, even_w_out, even_sinks,
           nsa_cmp_pe, nsa_cmp_w1, nsa_cmp_w2, odd_w_in, odd_w_out, final_norm):
    bsz, t_len, d_model = x.shape
    depth = ffn_wi.shape[0]
    assert d_model == D_MODEL and ffn_wi.shape[-1] == 2 * D_FF
    assert t_len % max(PROJ_TM, FFN_TM_PLAIN, BAND_TL, CMP_TQ, MOBA_T) == 0
    assert t_len % (BAND_SB * max(r for _, r in C_GROUPS)) == 0
    cos, sin = _rope_tables(t_len)
    wi = ffn_wi.astype(BF16)
    wo = ffn_wo.astype(BF16)
    h = x.reshape(bsz * t_len, D_MODEL)
    for layer in range(depth):
        i = layer // 2
        h = _ffn(h, ffn_norm_pre[layer], wi, wo, (layer, 0))
        if layer % 2 == 0:
            mix = _even_mixer(h, mix_norm[layer], even_w_in[i], even_w_out[i], even_sinks[i], nsa_cmp_pe[i],
                              nsa_cmp_w1[i], nsa_cmp_w2[i], cos, sin, bsz, t_len)
        else:
            mix = _odd_mixer(h, mix_norm[layer], odd_w_in[i], odd_w_out[i], cos, sin, bsz, t_len)
        last = layer == depth - 1
        h = _ffn(h, ffn_norm_post[layer], wi, wo, (layer, 1), final_norm if last else None, mix=mix)
    return h.reshape(bsz, t_len, D_MODEL)
```

```python
import functools

import jax
import jax.numpy as jnp
from jax import lax
from jax.experimental import pallas as pl
from jax.experimental.pallas import tpu as pltpu

D_MODEL = 1024
HEAD_DIM = 64
ROPE_THETA = 10000.0
NORM_EPS = 1e-6
D_FF = 2816
NEG_INF = -1e30
FORCE_SCORE = 1e4

A_Q_HEADS = 8
A_KV_HEADS = 2
A_WINDOW = 128
B_Q_HEADS = 8
B_KV_HEADS = 2
NSA_CMP_LEN = 32
NSA_CMP_STRIDE = 16
NSA_SEL_LEN = 64
NSA_SEL_COUNT = 8
NSA_WINDOW = 512
C_GROUPS = ((128, 1), (512, 4), (2048, 16))
C_HEADS_PER_GROUP = 4
C_HEADS = len(C_GROUPS) * C_HEADS_PER_GROUP
D_HEADS = 4
MOBA_BLOCK = 256
MOBA_TOPK = 3

LANES = 128
LOG2E = 1.4426950408889634
LN2 = 0.6931471805599453
QK_SCALE = HEAD_DIM ** -0.5 * LOG2E
MASK_BIAS = NEG_INF
V_ROWS = HEAD_DIM + 16
REMOVED = -3e38
VMEM_LIMIT = 52 * 1024 * 1024

FFN_TM = 512
FFN_TM_PLAIN = 1024
FFN_TF = 256
PROJ_TM = 1024
BAND_TL = 1024
WIN_TL = 512
BAND_SB = 128
FLASH_T = 256
FLASH_TK = 512
FLASH_CHAINS = 4
MOBA_T = 512
MOBA_TK = 512
GATE_TQ = 4096
CMP_TQ = 1024

BF16 = jnp.bfloat16
F32 = jnp.float32


def _params(*sem):
    return pltpu.CompilerParams(dimension_semantics=sem, vmem_limit_bytes=VMEM_LIMIT)


def _rms(x, g):
    return x * lax.rsqrt(jnp.mean(x * x, axis=-1, keepdims=True) + NORM_EPS) * g


def _log2(n):
    assert n & (n - 1) == 0
    return n.bit_length() - 1


def _dot(a, b):
    return jnp.dot(a, b, preferred_element_type=F32)


def _dot_t(a, b):
    return lax.dot_general(a, b, (((1,), (1,)), ((), ())), preferred_element_type=F32)


def _dot_tn(a, b):
    return lax.dot_general(a, b, (((0,), (0,)), ((), ())), preferred_element_type=F32)


def _ffn_kernel(x_ref, g_ref, wi_ref, wo_ref, *rest, final, mix_fn, n_mix):
    rest = list(rest)
    fg_ref = rest.pop(0) if final else None
    mix_refs = [rest.pop(0) for _ in range(n_mix)]
    o_ref, act_scr = rest.pop(0), rest.pop(0)
    n_sub = x_ref.shape[0] // FFN_TM
    assert mix_fn is None or n_sub == 1
    for sub in range(n_sub):
        rows = slice(sub * FFN_TM, (sub + 1) * FFN_TM)
        x = x_ref[rows, :]
        if mix_fn is not None:
            x = x + mix_fn(mix_refs, rest)
        o_ref[rows, :] = x
        n = _rms(x, g_ref[...]).astype(BF16)
        for c in range(D_FF // FFN_TF):
            cols = slice(c * FFN_TF, (c + 1) * FFN_TF)
            gate = _dot(n, wi_ref[:, cols])
            up = _dot(n, wi_ref[:, D_FF + c * FFN_TF:D_FF + (c + 1) * FFN_TF])
            act_scr[rows, cols] = (gate * jax.nn.sigmoid(gate) * up).astype(BF16)
        h = o_ref[rows, :] + 0.5 * _dot(act_scr[rows, :], wo_ref[...])
        if final:
            h = _rms(h, fg_ref[...])
        o_ref[rows, :] = h


def _resident(shape, lead=()):
    block = (None,) * len(lead) + tuple(shape[len(lead):])
    index = tuple(lead) + (0,) * (len(shape) - len(lead))
    return pl.BlockSpec(block, lambda i: index, pipeline_mode=pl.Buffered(1))


def _ffn(h, gain, wi, wo, which, final_gain=None, mix=None):
    m = h.shape[0]
    tm = FFN_TM if mix is not None else FFN_TM_PLAIN
    final = final_gain is not None
    in_specs = [
        pl.BlockSpec((tm, D_MODEL), lambda i: (i, 0)),
        pl.BlockSpec((1, D_MODEL), lambda i: (0, 0)),
        _resident(wi.shape, which),
        _resident(wo.shape, which),
    ]
    args = [h, gain.reshape(1, D_MODEL), wi, wo]
    if final:
        in_specs.append(pl.BlockSpec((1, D_MODEL), lambda i: (0, 0)))
        args.append(final_gain.reshape(1, D_MODEL))
    scratch = [pltpu.VMEM((tm, D_FF), BF16)]
    if mix is not None:
        in_specs += mix["specs"]
        args += mix["args"]
        scratch += mix["scratch"]
    return pl.pallas_call(
        functools.partial(_ffn_kernel, final=final, mix_fn=mix and mix["fn"], n_mix=len(mix["args"]) if mix else 0),
        grid=(m // tm,),
        in_specs=in_specs,
        out_specs=pl.BlockSpec((tm, D_MODEL), lambda i: (i, 0)),
        out_shape=jax.ShapeDtypeStruct((m, D_MODEL), F32),
        scratch_shapes=scratch,
        compiler_params=_params("parallel"),
    )(*args)


def _rope_tables(t):
    inv = 1.0 / (ROPE_THETA ** (jnp.arange(0, HEAD_DIM, 2, dtype=F32) / HEAD_DIM))
    ang = jnp.arange(t, dtype=F32)[:, None] * inv[None, :]
    cos = jnp.cos(ang)
    sin = jnp.sin(ang)
    return (jnp.concatenate([cos, cos, cos, cos], axis=-1),
            jnp.concatenate([-sin, sin, -sin, sin], axis=-1))


def _proj_kernel(x_ref, g_ref, w_ref, cos_ref, sin_ref, *rest, plan, n_out):
    out_refs, (ys,) = rest[:n_out], rest[n_out:]
    n = _rms(x_ref[...], g_ref[...]).astype(BF16)
    tm = n.shape[0]
    cos = cos_ref[...]
    sin = sin_ref[...]
    lane = lax.broadcasted_iota(jnp.int32, cos.shape, 1)
    first_half = (lane & (HEAD_DIM - 1)) < HEAD_DIM // 2
    heads_per_tile = LANES // HEAD_DIM
    for off, width, outs in plan:
        y = _dot(n, w_ref[:, off:off + width])
        for o in outs:
            o_ref = out_refs[o["idx"]]
            dt = o_ref.dtype
            kind = o["kind"]
            for c in range(o["width"] // LANES):
                piece = y[:, o["off"] + c * LANES: o["off"] + (c + 1) * LANES]
                if o.get("rope"):
                    rot = jnp.where(first_half, pltpu.roll(piece, LANES - HEAD_DIM // 2, 1),
                                    pltpu.roll(piece, HEAD_DIM // 2, 1))
                    piece = piece * cos + rot * sin
                if o.get("scale"):
                    piece = piece * QK_SCALE
                if o.get("sigmoid"):
                    piece = jax.nn.sigmoid(piece)
                lanes = slice(c * LANES, (c + 1) * LANES)
                if kind == "flat":
                    o_ref[:, lanes] = piece.astype(dt)
                elif kind == "split":
                    hw = o["hw"]
                    per = LANES // hw
                    for k in range(per):
                        o_ref[c * per + k] = piece[:, k * hw:(k + 1) * hw].astype(dt)
                elif kind == "dilate":
                    r = o["r"]
                    if r == 1:
                        o_ref[0, 0, :, lanes] = piece.astype(dt)
                    else:
                        ys[...] = piece
                        for i in range(r):
                            o_ref[0, i, :, lanes] = ys[pl.ds(i, tm // r, stride=r), :].astype(dt)
                elif kind == "chunk":
                    ys[...] = piece
                    for l in range(NSA_CMP_STRIDE):
                        rows = ys[pl.ds(l, tm // NSA_CMP_STRIDE, stride=NSA_CMP_STRIDE), :].astype(dt)
                        for k in range(heads_per_tile):
                            o_ref[c * heads_per_tile + k, :, l * HEAD_DIM:(l + 1) * HEAD_DIM] = rows[
                                :, k * HEAD_DIM:(k + 1) * HEAD_DIM]
                else:
                    tile = o["tile"]
                    pt = piece.T
                    for k in range(heads_per_tile):
                        for kt in range(tm // tile):
                            o_ref[c * heads_per_tile + k, 0, kt, 0:HEAD_DIM, :] = pt[
                                k * HEAD_DIM:(k + 1) * HEAD_DIM, kt * tile:(kt + 1) * tile].astype(dt)
                            o_ref[c * heads_per_tile + k, 0, kt, HEAD_DIM:V_ROWS, :] = jnp.ones(
                                (V_ROWS - HEAD_DIM, tile), dt)


def _proj(h, gain, w, cos, sin, plan, out_defs, bsz, t_len):
    m = h.shape[0]
    tm = PROJ_TM
    nt = t_len // tm
    out_shapes, out_specs = [], []
    for d in out_defs:
        kind, dt = d[0], d[-1]
        if kind == "flat":
            shape, block, index = (m, d[1]), (tm, d[1]), (lambda i: (i, 0))
        elif kind == "split":
            shape, block, index = (d[1], m, d[2]), (d[1], tm, d[2]), (lambda i: (0, i, 0))
        elif kind == "dilate":
            r = d[1]
            shape, block = (bsz, r, t_len // r, d[2]), (1, r, tm // r, d[2])
            index = lambda i: (i // nt, 0, i % nt, 0)
        elif kind == "chunk":
            width = NSA_CMP_STRIDE * HEAD_DIM
            shape, block = (d[1], m // NSA_CMP_STRIDE, width), (d[1], tm // NSA_CMP_STRIDE, width)
            index = lambda i: (0, i, 0)
        else:
            tile = d[2]
            shape, block = (d[1], bsz, t_len // tile, V_ROWS, tile), (d[1], 1, tm // tile, V_ROWS, tile)
            index = lambda i: (0, i // nt, i % nt, 0, 0)
        out_shapes.append(jax.ShapeDtypeStruct(shape, dt))
        out_specs.append(pl.BlockSpec(block, index))
    return pl.pallas_call(
        functools.partial(_proj_kernel, plan=plan, n_out=len(out_defs)),
        grid=(m // tm,),
        in_specs=[
            pl.BlockSpec((tm, D_MODEL), lambda i: (i, 0)),
            pl.BlockSpec((1, D_MODEL), lambda i: (0, 0)),
            pl.BlockSpec(w.shape, lambda i: (0, 0)),
            pl.BlockSpec((tm, LANES), lambda i: (i % nt, 0)),
            pl.BlockSpec((tm, LANES), lambda i: (i % nt, 0)),
        ],
        out_specs=out_specs,
        out_shape=out_shapes,
        scratch_shapes=[pltpu.VMEM((tm, LANES), F32)],
        compiler_params=_params("parallel"),
    )(h, gain.reshape(1, D_MODEL), w, cos, sin)


def _banded_kernel(q_ref, kp_ref, kc_ref, vp_ref, vc_ref, *rest, n_kv, group, tl, pk, span, max_dist,
                   has_sink, gate_branch, want_lse, v_transposed):
    rest = list(rest)
    sink_ref = rest.pop(0) if has_sink else None
    gate_ref = rest.pop(0) if gate_branch is not None else None
    o_ref = rest.pop(0)
    lse_ref = rest.pop(0) if want_lse else None
    kcat, vcat, s_a, s_b, band = rest
    sb = BAND_SB
    n_seq = kc_ref.shape[1]
    for rr in range(n_seq):
        kcat[rr, 0:pk, :] = kp_ref[0, rr]
        kcat[rr, pk:, :] = kc_ref[0, rr]
        if v_transposed:
            vcat[rr, :, 0:pk] = vp_ref[0, rr].astype(F32).T.astype(BF16)
            vcat[rr, :, pk:] = vc_ref[0, rr].astype(F32).T.astype(BF16)
        else:
            vcat[rr, 0:pk, :] = vp_ref[0, rr]
            vcat[rr, pk:, :] = vc_ref[0, rr]
    nk = span + sb
    kw = n_kv * HEAD_DIM
    merged = group == 1
    stacked = n_kv if merged else group
    rows = stacked * sb
    key = lax.broadcasted_iota(jnp.int32, (nk, rows), 0)
    qry = lax.broadcasted_iota(jnp.int32, (nk, rows), 1) & (sb - 1)
    dist = qry + span - key
    in_band = (dist >= 0) & (dist <= max_dist)
    lane_head = lax.broadcasted_iota(jnp.int32, (1, rows), 1) >> _log2(sb)
    q_head = lax.broadcasted_iota(jnp.int32, (sb, kw), 1) >> _log2(HEAD_DIM)
    v_head = (lax.broadcasted_iota(jnp.int32, (kw, nk), 0) if v_transposed
              else lax.broadcasted_iota(jnp.int32, (nk, kw), 1)) >> _log2(HEAD_DIM)
    assert merged or kw == LANES
    items = [(rr, j, h) for rr in range(n_seq) for j in range(tl // sb)
             for h in ([0] if merged else range(n_kv))]
    s_bufs = (s_a, s_b)

    def compute(first_tile):
        def scores(n):
            rr, j, h = items[n]
            lo = pk + j * sb - span
            qrows = slice(j * sb, (j + 1) * sb)
            if merged:
                qj = q_ref[0, rr, qrows, :].astype(F32)
                qs = jnp.concatenate([jnp.where(q_head == r, qj, 0.0).astype(BF16) for r in range(n_kv)], axis=0)
                kk = kcat[rr, lo:lo + nk, :]
            else:
                qs = jnp.concatenate([q_ref[h * group + g, 0, qrows, :] for g in range(group)], axis=0)
                kk = kcat[rr, lo:lo + nk, h * HEAD_DIM:(h + 1) * HEAD_DIM]
            variant = 1 + j if first_tile and j * sb < span else 0
            s_bufs[n % 2][...] = _dot_t(kk, qs) + band[variant]

        band[0] = jnp.where(in_band, 0.0, NEG_INF)
        if first_tile:
            for j in range(span // sb):
                band[1 + j] = jnp.where(in_band & (key >= span - j * sb), 0.0, NEG_INF)

        outs, lses = [], []
        scores(0)
        for n, (rr, j, h) in enumerate(items):
            if n + 1 < len(items):
                scores(n + 1)
            s = s_bufs[n % 2][...]
            lo = pk + j * sb - span
            qrows = slice(j * sb, (j + 1) * sb)
            m = jnp.max(s, axis=0, keepdims=True)
            if has_sink:
                sink = jnp.full((1, rows), sink_ref[h * group], F32)
                for g in range(1, group):
                    sink = jnp.where(lane_head == g, sink_ref[h * group + g], sink)
                sink = sink * LOG2E
                m = jnp.maximum(m, sink)
            p = jnp.exp2(s - m).astype(BF16)
            vwin = vcat[rr, :, lo:lo + nk] if v_transposed else vcat[rr, lo:lo + nk, :]
            if merged:
                ones = jnp.ones((V_ROWS - HEAD_DIM, nk) if v_transposed else (nk, LANES), BF16)
                vv = jnp.concatenate([vwin, ones], axis=0 if v_transposed else 1)
                ones_row = kw
            else:
                vv = jnp.where(v_head == h, vwin.astype(F32), 1.0).astype(BF16)
                ones_row = (1 - h) * HEAD_DIM
            o_t = _dot(vv, p) if v_transposed else _dot_tn(vv, p)
            d = o_t[ones_row:ones_row + 1, :]
            if has_sink:
                d = d + jnp.exp2(sink - m)
            d = jnp.where(d > 0, d, 1.0)
            inv = 1.0 / d
            if want_lse:
                lse_t = jnp.broadcast_to(m * LN2 + jnp.log(d), (HEAD_DIM, rows))
            for r in range(stacked):
                cols = slice(r * sb, (r + 1) * sb)
                vrow = (r if merged else h) * HEAD_DIM
                o = (o_t[vrow:vrow + HEAD_DIM, cols] * inv[:, cols]).T
                if gate_ref is not None:
                    gc = r * 3 + gate_branch
                    o = o * gate_ref[h, 0, qrows, gc:gc + 1]
                outs.append(o)
                if want_lse:
                    lses.append(lse_t[:, cols].T)
            if merged or h == n_kv - 1:
                o_ref[0, rr, qrows, :] = jnp.concatenate(outs, axis=-1).astype(o_ref.dtype)
                if want_lse:
                    lse_ref[0, rr, qrows, :] = jnp.concatenate(lses, axis=-1)
                outs, lses = [], []

    first = pl.program_id(2) == 0
    pl.when(first)(functools.partial(compute, True))
    pl.when(jnp.logical_not(first))(functools.partial(compute, False))


def _banded(q, k, v, *, n_kv, group, max_dist, tl, sink=None, gate=None, gate_branch=None, want_lse=False):
    bsz, n_res, length, kw = k.shape
    qw = n_kv * group * HEAD_DIM
    span = -(-max_dist // BAND_SB) * BAND_SB
    pk = span
    assert tl % pk == 0
    v_transposed = group == 1 or span > BAND_SB
    s_shape = (span + BAND_SB, (n_kv if group == 1 else group) * BAND_SB)
    ratio = tl // pk
    n_seq = min(n_res, max(1, BAND_TL // tl))
    prev = lambda b, i, l: (b, i, jnp.maximum(l * ratio - 1, 0), 0)
    cur = lambda b, i, l: (b, i, l, 0)
    if group == 1:
        q_spec = pl.BlockSpec((1, n_seq, tl, qw), cur)
    else:
        q_spec = pl.BlockSpec((n_kv * group, 1, tl, HEAD_DIM), lambda b, i, l: (0, b, l, 0))
    o_shape = (bsz, n_res, length, qw)
    in_specs = [
        q_spec,
        pl.BlockSpec((1, n_seq, pk, kw), prev),
        pl.BlockSpec((1, n_seq, tl, kw), cur),
        pl.BlockSpec((1, n_seq, pk, kw), prev),
        pl.BlockSpec((1, n_seq, tl, kw), cur),
    ]
    args = [q, k, k, v, v]
    if sink is not None:
        in_specs.append(pl.BlockSpec(memory_space=pltpu.SMEM))
        args.append(sink)
    if gate is not None:
        in_specs.append(pl.BlockSpec((n_kv, 1, tl, LANES), lambda b, i, l: (0, b, l, 0)))
        args.append(gate)
    out_spec = pl.BlockSpec((1, n_seq, tl, qw), cur)
    out_shape = [jax.ShapeDtypeStruct(o_shape, BF16)]
    out_specs = [out_spec]
    if want_lse:
        out_shape.append(jax.ShapeDtypeStruct(o_shape, F32))
        out_specs.append(out_spec)
    res = pl.pallas_call(
        functools.partial(_banded_kernel, n_kv=n_kv, group=group, tl=tl, pk=pk, span=span,
                          max_dist=max_dist, has_sink=sink is not None, gate_branch=gate_branch,
                          want_lse=want_lse, v_transposed=v_transposed),
        grid=(bsz, n_res // n_seq, length // tl),
        in_specs=in_specs,
        out_specs=out_specs,
        out_shape=out_shape,
        scratch_shapes=[pltpu.VMEM((n_seq, pk + tl, kw), BF16),
                        pltpu.VMEM((n_seq, kw, pk + tl) if v_transposed else (n_seq, pk + tl, kw), BF16),
                        pltpu.VMEM(s_shape, F32), pltpu.VMEM(s_shape, F32),
                        pltpu.VMEM((1 + span // BAND_SB,) + s_shape, F32)],
        compiler_params=_params("parallel", "parallel", "parallel"),
    )(*args)
    return res if want_lse else res[0]


def _compress_one(z_ref, pe_ref, w1_ref, w2_ref, o_ref):
    half = NSA_CMP_STRIDE * HEAD_DIM
    z = z_ref[0, 0].astype(F32)
    za = (z + pe_ref[0, :, 0:half]).astype(BF16)
    zb = (z + pe_ref[0, :, half:2 * half]).astype(BF16)
    ya = _dot(za, w1_ref[0, 0:half, :])
    yb = _dot(zb, w1_ref[0, half:2 * half, :])
    nrow = ya.shape[0]
    hid = jax.nn.gelu(ya + pltpu.roll(yb, nrow - 1, 0))
    o_ref[0, 0] = _dot(hid.astype(BF16), w2_ref[0]).astype(o_ref.dtype)


def _compress_kernel(zk_ref, zv_ref, pek_ref, pev_ref, w1k_ref, w1v_ref, w2k_ref, w2v_ref, ok_ref, ov_ref):
    _compress_one(zk_ref, pek_ref, w1k_ref, w2k_ref, ok_ref)
    _compress_one(zv_ref, pev_ref, w1v_ref, w2v_ref, ov_ref)


def _compress(zk, zv, pe, w1, w2):
    hk, bsz, nch, width = zk.shape
    zspec = pl.BlockSpec((1, 1, nch, width), lambda h, b: (h, b, 0, 0))
    ospec = pl.BlockSpec((1, 1, nch, HEAD_DIM), lambda h, b: (h, b, 0, 0))
    oshape = jax.ShapeDtypeStruct((hk, bsz, nch, HEAD_DIM), BF16)

    def wspec(arr, which):
        return pl.BlockSpec((1,) + arr.shape[1:], lambda h, b: (which, 0, 0))

    return pl.pallas_call(
        _compress_kernel,
        grid=(hk, bsz),
        in_specs=[zspec, zspec, wspec(pe, 0), wspec(pe, 1), wspec(w1, 0), wspec(w1, 1), wspec(w2, 0), wspec(w2, 1)],
        out_specs=[ospec, ospec],
        out_shape=[oshape, oshape],
        compiler_params=_params("parallel", "parallel"),
    )(zk, zv, pe, pe, w1, w1, w2, w2)


def _topk_select(score, k, idxf):
    ncand = score.shape[0]
    rem = score
    sel = jnp.zeros_like(score)
    for _ in range(k):
        mx = jnp.max(rem, axis=0, keepdims=True)
        first = jnp.min(jnp.where(rem == mx, idxf, float(ncand)), axis=0, keepdims=True)
        pick = idxf == first
        sel = jnp.where(pick, jnp.where(mx > 0.5 * NEG_INF, 1.0, 0.0), sel)
        rem = jnp.where(pick, REMOVED, rem)
    return sel


def _cmp_kernel(q_ref, kc_ref, vc_ref, gate_ref, o_ref, bias_ref, *, tq, group, n_tiles):
    qi = pl.program_id(1)

    def tile(ncp):
        for hk in range(kc_ref.shape[0]):
            _cmp_tile(q_ref, kc_ref, vc_ref, gate_ref, o_ref, bias_ref, qi, hk, tq=tq, group=group, ncp=ncp)

    for v in range(n_tiles):
        pl.when(qi == v)(functools.partial(tile, min(kc_ref.shape[2], (v + 1) * tq // NSA_CMP_STRIDE)))


def _cmp_tile(q_ref, kc_ref, vc_ref, gate_ref, o_ref, bias_ref, qi, hk, *, tq, group, ncp):
    kc = kc_ref[hk, 0, 0:ncp, :]
    vc = vc_ref[hk, 0, 0:ncp, :]
    gw = group * HEAD_DIM
    nsb = bias_ref.shape[-1]
    rows = group * tq
    last = lax.broadcasted_iota(jnp.int32, (ncp, 1), 0) * NSA_CMP_STRIDE + (NSA_CMP_LEN - 1)
    t_in_tile = lax.broadcasted_iota(jnp.int32, (1, rows), 1) & (tq - 1)
    visible = (last - t_in_tile) <= qi * tq
    qs = jnp.concatenate([q_ref[0, :, hk * gw + g * HEAD_DIM:hk * gw + (g + 1) * HEAD_DIM]
                          for g in range(group)], axis=0)
    s = jnp.where(visible, _dot_t(kc, qs), NEG_INF)
    m = jnp.max(s, axis=0, keepdims=True)
    m = jnp.where(m > 0.5 * NEG_INF, m, 0.0)
    p = jnp.exp2(s - m)
    vc_ones = jnp.concatenate([vc, jnp.ones_like(vc)], axis=1)
    o_t = _dot_tn(vc_ones, p.astype(BF16))
    d = o_t[HEAD_DIM:HEAD_DIM + 1, :]
    d = jnp.where(d > 0, d, 1.0)
    inv = 1.0 / d
    pc = p * inv
    o_t = o_t[0:HEAD_DIM, :] * inv
    outs = []
    psum = pc[:, 0:tq]
    for g in range(group):
        if g:
            psum = psum + pc[:, g * tq:(g + 1) * tq]
        outs.append(o_t[:, g * tq:(g + 1) * tq].T * gate_ref[hk, 0, :, g * 3:g * 3 + 1])
    o_ref[0, :, hk * gw:(hk + 1) * gw] = jnp.concatenate(outs, axis=-1).astype(o_ref.dtype)

    jj = lax.broadcasted_iota(jnp.int32, (nsb, ncp), 0) * NSA_SEL_LEN
    nn = lax.broadcasted_iota(jnp.int32, (nsb, ncp), 1) * NSA_CMP_STRIDE
    ov = jnp.maximum(jnp.minimum(nn + NSA_CMP_LEN, jj + NSA_SEL_LEN) - jnp.maximum(nn, jj), 0)
    ov = (ov.astype(F32) / NSA_CMP_LEN).astype(BF16)
    p_hi = psum.astype(BF16)
    p_lo = (psum - p_hi.astype(F32)).astype(BF16)
    imp = _dot(ov, p_hi) + _dot(ov, p_lo)

    j = lax.broadcasted_iota(jnp.int32, (nsb, tq), 0)
    tt = qi * tq + lax.broadcasted_iota(jnp.int32, (nsb, tq), 1)
    cb = tt >> _log2(NSA_SEL_LEN)
    forced = (j == 0) | (j == cb) | (j == cb - 1)
    assert FORCE_SCORE > group
    n_forced = 3
    free = jnp.where((j <= cb) & jnp.logical_not(forced), imp, NEG_INF)
    sel = _topk_select(free, min(NSA_SEL_COUNT, nsb) - n_forced, j.astype(F32))
    keep = (sel > 0.5) | (forced & (j <= cb))
    bias_ref[hk, 0] = jnp.where(keep, 0.0, MASK_BIAS).T.astype(bias_ref.dtype)


def _cmp_attention(qu, kc, vc, gate, t_len):
    bsz = qu.shape[0]
    hk, _, ncp, _ = kc.shape
    group = B_Q_HEADS // B_KV_HEADS
    tq = CMP_TQ
    nsb = t_len // NSA_SEL_LEN
    gw = group * HEAD_DIM
    cspec = pl.BlockSpec((hk, 1, ncp, HEAD_DIM), lambda b, i: (0, b, 0, 0))
    return pl.pallas_call(
        functools.partial(_cmp_kernel, tq=tq, group=group, n_tiles=t_len // tq),
        grid=(bsz, t_len // tq),
        in_specs=[
            pl.BlockSpec((1, tq, hk * gw), lambda b, i: (b, i, 0)),
            cspec, cspec,
            pl.BlockSpec((hk, 1, tq, LANES), lambda b, i: (0, b, i, 0)),
        ],
        out_specs=[
            pl.BlockSpec((1, tq, hk * gw), lambda b, i: (b, i, 0)),
            pl.BlockSpec((hk, 1, tq, nsb), lambda b, i: (0, b, i, 0)),
        ],
        out_shape=[
            jax.ShapeDtypeStruct((bsz, t_len, hk * gw), BF16),
            jax.ShapeDtypeStruct((hk, bsz, t_len, nsb), BF16),
        ],
        compiler_params=_params("parallel", "parallel"),
    )(qu, kc, vc, gate)


def _moba_gate_kernel(q_ref, k_ref, bias_ref, km_hi, km_lo, *, tq):
    qi = pl.program_id(1)
    nh, _, t_len, _ = k_ref.shape
    ncol = bias_ref.shape[-1]
    nb = t_len // MOBA_BLOCK

    @pl.when(qi == 0)
    def _():
        blk = lax.broadcasted_iota(jnp.int32, (ncol, t_len), 0)
        pos = lax.broadcasted_iota(jnp.int32, (ncol, t_len), 1)
        member = jnp.where((pos >> _log2(MOBA_BLOCK)) == blk, 1.0, 0.0).astype(BF16)
        for h in range(nh):
            kmean = _dot(member, k_ref[h, 0]) * (1.0 / MOBA_BLOCK)
            hi = kmean.astype(BF16)
            km_hi[h] = hi
            km_lo[h] = (kmean - hi.astype(F32)).astype(BF16)

    j = lax.broadcasted_iota(jnp.int32, (nb, tq), 0)
    t = qi * tq + lax.broadcasted_iota(jnp.int32, (nb, tq), 1)
    cb = t >> _log2(MOBA_BLOCK)
    for h in range(nh):
        q = q_ref[h, 0]
        gate = (_dot_t(km_hi[h], q) + _dot_t(km_lo[h], q))[0:nb]
        score = jnp.where(j < cb, gate, NEG_INF)
        sel = _topk_select(score, min(MOBA_TOPK, nb - 1), j.astype(F32))
        keep = (sel > 0.5) | (j == cb)
        bias = jnp.where(keep, 0.0, MASK_BIAS)
        bias = jnp.concatenate([bias, jnp.zeros((ncol - nb, tq), F32)], axis=0)
        bias_ref[h, 0] = bias.T.astype(bias_ref.dtype)


def _moba_gate(q, k):
    nh, bsz, t_len, _ = q.shape
    tq = min(GATE_TQ, t_len)
    return pl.pallas_call(
        functools.partial(_moba_gate_kernel, tq=tq),
        grid=(bsz, t_len // tq),
        in_specs=[
            pl.BlockSpec((nh, 1, tq, HEAD_DIM), lambda b, i: (0, b, i, 0)),
            pl.BlockSpec((nh, 1, t_len, HEAD_DIM), lambda b, i: (0, b, 0, 0)),
        ],
        out_specs=pl.BlockSpec((nh, 1, tq, HEAD_DIM), lambda b, i: (0, b, i, 0)),
        out_shape=jax.ShapeDtypeStruct((nh, bsz, t_len, HEAD_DIM), BF16),
        scratch_shapes=[pltpu.VMEM((nh, HEAD_DIM, HEAD_DIM), BF16), pltpu.VMEM((nh, HEAD_DIM, HEAD_DIM), BF16)],
        compiler_params=_params("parallel", "arbitrary"),
    )(q, k)


def _flash_kernel(q_ref, bias_ref, k_ref, vt_ref, *rest, group, chain_shape, lanes_out, tq, tk, blk_len, q_scale,
                  gate_branch):
    rest = list(rest)
    gate_ref = rest.pop(0) if gate_branch is not None else None
    o_ref, qa, ka, m_s, acc = rest[:5]
    s_bufs = rest[5:]
    chain_ids = [(ch, cb) for cb in range(chain_shape[1]) for ch in range(chain_shape[0])]
    chains = len(chain_ids)
    qi = pl.program_id(2)
    rows = group * tq
    t_len = k_ref.shape[2]

    @pl.when(qi == 0)
    def _():
        kpos = lax.broadcasted_iota(jnp.int32, (t_len, HEAD_DIM), 0)
        kblk = lax.broadcasted_iota(jnp.int32, (t_len, HEAD_DIM), 1)
        onehot = jnp.where((kpos >> _log2(blk_len)) == kblk, 1.0, 0.0).astype(BF16)
        for c, (ch, cb) in enumerate(chain_ids):
            ka[c, :, 0:HEAD_DIM] = k_ref[ch, cb]
            ka[c, :, HEAD_DIM:2 * HEAD_DIM] = onehot

    for c, (ch, cb) in enumerate(chain_ids):
        bias = bias_ref[ch, cb]
        for g in range(group):
            qg = q_ref[ch * group + g, cb]
            if q_scale != 1.0:
                qg = (qg.astype(F32) * q_scale).astype(BF16)
            qa[c, g * tq:(g + 1) * tq, 0:HEAD_DIM] = qg
            qa[c, g * tq:(g + 1) * tq, HEAD_DIM:2 * HEAD_DIM] = bias
    m_s[...] = jnp.full_like(m_s, NEG_INF)
    acc[...] = jnp.zeros_like(acc)

    def scores(ki, slot):
        start = pl.multiple_of(ki * tk, tk)
        for c in range(chains):
            s_bufs[2 * c + slot][...] = _dot_t(ka[c, pl.ds(start, tk), :], qa[c])

    def softmax_pv(ki, slot, masked):
        for c, (ch, cb) in enumerate(chain_ids):
            s = s_bufs[2 * c + slot][...]
            if masked:
                kpos = ki * tk + lax.broadcasted_iota(jnp.int32, (tk, rows), 0)
                qpos = qi * tq + (lax.broadcasted_iota(jnp.int32, (tk, rows), 1) & (tq - 1))
                s = jnp.where(kpos <= qpos, s, NEG_INF)
            m_old = m_s[c]
            m_new = jnp.maximum(m_old, jnp.max(s, axis=0, keepdims=True))
            alpha = jnp.exp2(m_old - m_new)
            p = jnp.exp2(s - m_new)
            acc[c] = alpha * acc[c] + _dot(vt_ref[ch, cb, ki], p.astype(BF16))
            m_s[c] = m_new

    n_full = (qi * tq) >> _log2(tk)
    scores(0, 0)

    def pair(j, carry):
        k0 = 2 * j
        scores(k0 + 1, 1)
        softmax_pv(k0, 0, False)
        scores(k0 + 2, 0)
        softmax_pv(k0 + 1, 1, False)
        return carry

    lax.fori_loop(0, n_full >> 1, pair, 0)
    cur = (n_full >> 1) << 1

    @pl.when((n_full & 1) == 1)
    def _():
        scores(cur + 1, 1)
        softmax_pv(cur, 0, False)
        softmax_pv(cur + 1, 1, True)

    @pl.when((n_full & 1) == 0)
    def _():
        softmax_pv(cur, 0, True)

    outs = {}
    for c, (ch, cb) in enumerate(chain_ids):
        d = acc[c, HEAD_DIM:HEAD_DIM + 1, :]
        d = jnp.where(d > 0, d, 1.0)
        o_t = acc[c, 0:HEAD_DIM, :] * (1.0 / d)
        for g in range(group):
            og = o_t[:, g * tq:(g + 1) * tq].T
            if gate_ref is not None:
                gc = g * 3 + gate_branch
                og = og * gate_ref[ch, cb, :, gc:gc + 1]
            outs[(cb, ch * group + g)] = og
    for cb in range(chain_shape[1]):
        planes = [outs[(cb, hq)] for hq in range(chain_shape[0] * group)]
        if lanes_out:
            o_ref[0, cb] = jnp.concatenate(planes, axis=-1).astype(o_ref.dtype)
        else:
            for hq, og in enumerate(planes):
                o_ref[hq, cb] = og.astype(o_ref.dtype)


def _flash(q, bias, k, vt, *, group, tq, blk_len, q_scale, lanes_out, gate=None, gate_branch=None):
    nh, bsz, t_len, _ = k.shape
    ch = min(nh, FLASH_CHAINS)
    cb = max(1, FLASH_CHAINS // ch)
    cb = cb if bsz % cb == 0 else 1
    chains = ch * cb
    nt, tk = vt.shape[2], vt.shape[4]
    per_step = lambda h, b, i: (h, b, i, 0)
    whole_seq = lambda h, b, i: (h, b, 0, 0)
    in_specs = [
        pl.BlockSpec((ch * group, cb, tq, HEAD_DIM), per_step),
        pl.BlockSpec((ch, cb, tq, HEAD_DIM), per_step),
        pl.BlockSpec((ch, cb, t_len, HEAD_DIM), whole_seq),
        pl.BlockSpec((ch, cb, nt, V_ROWS, tk), lambda h, b, i: (h, b, 0, 0, 0)),
    ]
    args = [q, bias, k, vt]
    if gate is not None:
        in_specs.append(pl.BlockSpec((ch, cb, tq, LANES), per_step))
        args.append(gate)
    if lanes_out:
        out_shape = (1, bsz, t_len, nh * group * HEAD_DIM)
        out_spec = pl.BlockSpec((1, cb, tq, ch * group * HEAD_DIM), lambda h, b, i: (0, b, i, h))
    else:
        out_shape = (nh * group, bsz, t_len, HEAD_DIM)
        out_spec = pl.BlockSpec((ch * group, cb, tq, HEAD_DIM), per_step)
    rows = group * tq
    return pl.pallas_call(
        functools.partial(_flash_kernel, group=group, chain_shape=(ch, cb), lanes_out=lanes_out, tq=tq, tk=tk,
                          blk_len=blk_len, q_scale=q_scale, gate_branch=gate_branch),
        grid=(nh // ch, bsz // cb, t_len // tq),
        in_specs=in_specs,
        out_specs=out_spec,
        out_shape=jax.ShapeDtypeStruct(out_shape, BF16),
        scratch_shapes=[
            pltpu.VMEM((chains, rows, 2 * HEAD_DIM), BF16),
            pltpu.VMEM((chains, t_len, 2 * HEAD_DIM), BF16),
            pltpu.VMEM((chains, 1, rows), F32),
            pltpu.VMEM((chains, V_ROWS, rows), F32),
        ] + [pltpu.VMEM((tk, rows), F32)] * (2 * chains),
        compiler_params=_params("parallel", "parallel", "arbitrary"),
    )(*args)


def _even_delta(refs, scratch):
    oa_ref, oc_ref, os_ref, ow_ref, w_ref = refs
    na = oa_ref.shape[-1]
    ob = (oc_ref[...].astype(F32) + os_ref[...].astype(F32) + ow_ref[...].astype(F32)).astype(BF16)
    return _dot(oa_ref[...], w_ref[0:na, :]) + _dot(ob, w_ref[na:, :])


def _even_mix(oa, ocmp, osel, owin, w):
    aspec = pl.BlockSpec((FFN_TM, oa.shape[-1]), lambda i: (i, 0))
    return dict(args=[oa, ocmp, osel, owin, w], specs=[aspec] * 4 + [_resident(w.shape)], scratch=[],
                fn=_even_delta)


def _odd_delta(refs, scratch, *, dilations):
    ng = len(dilations)
    o_refs, l_refs = refs[:ng], refs[ng:2 * ng]
    od_ref, w_ref = refs[2 * ng:]
    scratch = list(scratch)
    tm = od_ref.shape[0]

    def tokens(ref, r):
        if r == 1:
            return ref[0, 0].astype(F32)
        pieces = []
        for c in range(ref.shape[-1] // LANES):
            scr = scratch.pop(0)
            for i in range(r):
                scr[pl.ds(i, tm // r, stride=r), :] = ref[0, i, :, c * LANES:(c + 1) * LANES].astype(F32)
            pieces.append(scr[...])
        return jnp.concatenate(pieces, axis=-1)

    outs = [tokens(ref, r) for ref, r in zip(o_refs, dilations)]
    lses = [tokens(ref, r) for ref, r in zip(l_refs, dilations)]
    mx = functools.reduce(jnp.maximum, lses)
    es = [jnp.exp(l - mx) for l in lses]
    tot = functools.reduce(lambda a, b: a + b, es)
    oc = functools.reduce(lambda a, b: a + b, [(e / tot) * o for e, o in zip(es, outs)])
    nc = oc.shape[-1]
    return _dot(oc.astype(BF16), w_ref[0:nc, :]) + _dot(od_ref[...], w_ref[nc:, :])


def _odd_mix(outs, lses, od, w, t_len):
    tm = FFN_TM
    nt = t_len // tm
    gw = outs[0].shape[-1]
    dilations = tuple(o.shape[1] for o in outs)
    gspecs = [pl.BlockSpec((1, r, tm // r, gw), lambda i: (i // nt, 0, i % nt, 0)) for r in dilations]
    n_scr = 2 * (gw // LANES) * sum(1 for r in dilations if r > 1)
    specs = gspecs + gspecs + [pl.BlockSpec((tm, od.shape[1]), lambda i: (i, 0)), _resident(w.shape)]
    return dict(args=[*outs, *lses, od, w], specs=specs, scratch=[pltpu.VMEM((tm, LANES), F32)] * n_scr,
                fn=functools.partial(_odd_delta, dilations=dilations))


def _col_ranges(sizes):
    offs, acc = [], 0
    for s in sizes:
        offs.append((acc, acc + s))
        acc += s
    return offs


def _even_mixer(h, gain, w_in, w_out, sinks, cmp_pe, cmp_w1, cmp_w2, cos, sin, bsz, t_len):
    m = bsz * t_len
    qa_w, kva_w = A_Q_HEADS * HEAD_DIM, A_KV_HEADS * HEAD_DIM
    qb_w, kvb_w = B_Q_HEADS * HEAD_DIM, B_KV_HEADS * HEAD_DIM
    sizes = [qa_w, kva_w, kva_w, qb_w] + [kvb_w] * 6 + [3 * B_Q_HEADS]
    (aq, ak, av, bq, bkc, bvc, bks, bvs, bkw, bvw, bg) = [w_in[:, a:b] for a, b in _col_ranges(sizes)]
    group = B_Q_HEADS // B_KV_HEADS
    gpad = jnp.zeros((D_MODEL, LANES - 3 * group), w_in.dtype)
    gates = [x for hk in range(B_KV_HEADS) for x in (bg[:, hk * 3 * group:(hk + 1) * 3 * group], gpad)]
    w = jnp.concatenate([aq, bq, ak, bkw, bks, av, bvw, bvs, bkc, bvc] + gates, axis=1).astype(BF16)
    c = [0]

    def take(width):
        c[0] += width
        return c[0] - width

    def out(idx, off, width, kind, **kw):
        return dict(idx=idx, off=off, width=width, kind=kind, **kw)

    plan = [
        (take(qa_w), qa_w, [out(0, 0, qa_w, "split", hw=HEAD_DIM, rope=True, scale=True)]),
        (take(qb_w), qb_w, [out(1, 0, qb_w, "split", hw=HEAD_DIM, rope=True, scale=True),
                            out(5, 0, qb_w, "flat", scale=True)]),
        (take(3 * kva_w), 3 * kva_w, [out(2, 0, kva_w, "flat", rope=True),
                                      out(3, kva_w, kvb_w, "flat", rope=True),
                                      out(4, kva_w + kvb_w, kvb_w, "split", hw=HEAD_DIM, rope=True)]),
        (take(5 * kvb_w), 5 * kvb_w, [out(6, 0, kva_w, "flat"), out(7, kva_w, kvb_w, "flat"),
                                      out(8, 2 * kvb_w, kvb_w, "vt", tile=FLASH_TK),
                                      out(9, 3 * kvb_w, kvb_w, "chunk"),
                                      out(10, 4 * kvb_w, kvb_w, "chunk")]),
        (take(B_KV_HEADS * LANES), B_KV_HEADS * LANES,
         [out(11, 0, B_KV_HEADS * LANES, "split", hw=LANES, sigmoid=True)]),
    ]
    out_defs = [("split", A_Q_HEADS, HEAD_DIM, BF16), ("split", B_Q_HEADS, HEAD_DIM, BF16), ("flat", kva_w, BF16),
                ("flat", kvb_w, BF16),
                ("split", B_KV_HEADS, HEAD_DIM, BF16), ("flat", qb_w, BF16), ("flat", kva_w, BF16),
                ("flat", kvb_w, BF16), ("vt", B_KV_HEADS, FLASH_TK, BF16), ("chunk", B_KV_HEADS, BF16),
                ("chunk", B_KV_HEADS, BF16), ("split", B_KV_HEADS, LANES, F32)]
    (aq_r, bq_r, ak_r, bkw_r, bks_r, bq_u, av_, bvw_, bvs_t, zk, zv, gate) = _proj(
        h, gain, w, cos, sin, plan, out_defs, bsz, t_len)

    def seq(x):
        return x.reshape(bsz, 1, t_len, x.shape[-1])

    def heads(x):
        return x.reshape(x.shape[0], bsz, t_len, x.shape[-1])

    gate4 = heads(gate)
    oa = _banded(heads(aq_r), seq(ak_r), seq(av_), n_kv=A_KV_HEADS, group=A_Q_HEADS // A_KV_HEADS,
                 max_dist=A_WINDOW - 1, tl=BAND_TL, sink=sinks)
    nch = t_len // NSA_CMP_STRIDE
    zshape = (B_KV_HEADS, bsz, nch, NSA_CMP_STRIDE * HEAD_DIM)
    kc, vc = _compress(zk.reshape(zshape), zv.reshape(zshape), cmp_pe.reshape(2, 1, NSA_CMP_LEN * HEAD_DIM),
                       cmp_w1.astype(BF16), cmp_w2.astype(BF16))
    ocmp, bias = _cmp_attention(bq_u.reshape(bsz, t_len, qb_w), kc, vc, gate4, t_len)
    osel = _flash(heads(bq_r), bias, heads(bks_r), bvs_t, group=group, tq=FLASH_T, blk_len=NSA_SEL_LEN,
                  q_scale=1.0, lanes_out=True, gate=gate4, gate_branch=1)
    owin = _banded(heads(bq_r), seq(bkw_r), seq(bvw_), n_kv=B_KV_HEADS, group=group,
                   max_dist=NSA_WINDOW - 1, tl=WIN_TL, gate=gate4, gate_branch=2)
    return _even_mix(oa.reshape(m, qa_w), ocmp.reshape(m, qb_w), osel.reshape(m, qb_w),
                     owin.reshape(m, qb_w), w_out.astype(BF16))


def _odd_mixer(h, gain, w_in, w_out, cos, sin, bsz, t_len):
    m = bsz * t_len
    cw = C_HEADS * HEAD_DIM
    dw = D_HEADS * HEAD_DIM
    gw = C_HEADS_PER_GROUP * HEAD_DIM
    n_groups = len(C_GROUPS)

    def group_outs(base, **kw):
        return [dict(idx=base + gi, off=gi * gw, width=gw, kind="dilate", r=r, **kw)
                for gi, (_, r) in enumerate(C_GROUPS)]

    plan = [
        (0, cw, group_outs(0, rope=True, scale=True)),
        (cw, cw, group_outs(n_groups, rope=True)),
        (2 * cw, cw, group_outs(2 * n_groups)),
        (3 * cw, 3 * dw, [dict(idx=3 * n_groups, off=0, width=dw, kind="split", hw=HEAD_DIM, rope=True),
                          dict(idx=3 * n_groups + 1, off=dw, width=dw, kind="split", hw=HEAD_DIM, rope=True),
                          dict(idx=3 * n_groups + 2, off=2 * dw, width=dw, kind="vt", tile=MOBA_TK)]),
    ]
    out_defs = [("dilate", r, gw, BF16) for _ in range(3) for _, r in C_GROUPS]
    out_defs += [("split", D_HEADS, HEAD_DIM, BF16), ("split", D_HEADS, HEAD_DIM, BF16),
                 ("vt", D_HEADS, MOBA_TK, BF16)]
    res = _proj(h, gain, w_in.astype(BF16), cos, sin, plan, out_defs, bsz, t_len)
    cq, ck, cv = res[0:n_groups], res[n_groups:2 * n_groups], res[2 * n_groups:3 * n_groups]
    dq, dk, dv_t = res[3 * n_groups:]
    outs, lses = [], []
    for gi, (wlen, r) in enumerate(C_GROUPS):
        o, lse = _banded(cq[gi], ck[gi], cv[gi], n_kv=C_HEADS_PER_GROUP, group=1, max_dist=wlen // r,
                         tl=min(BAND_TL, t_len // r), want_lse=True)
        outs.append(o)
        lses.append(lse)

    def heads(x):
        return x.reshape(x.shape[0], bsz, t_len, x.shape[-1])

    bias = _moba_gate(heads(dq), heads(dk))
    od = _flash(heads(dq), bias, heads(dk), dv_t, group=1, tq=MOBA_T, blk_len=MOBA_BLOCK, q_scale=QK_SCALE,
                lanes_out=True)
    return _odd_mix(outs, lses, od.reshape(m, D_HEADS * HEAD_DIM), w_out.astype(BF16), t_len)


def kernel(x, ffn_norm_pre, mix_norm, ffn_norm_post, ffn_wi, ffn_wo, even_w_in, even_w_out, even_sinks,
           nsa_cmp_pe, nsa_cmp_w1, nsa_cmp_w2, odd_w_in, odd_w_out, final_norm):
    bsz, t_len, d_model = x.shape
    depth = ffn_wi.shape[0]
    assert d_model == D_MODEL and ffn_wi.shape[-1] == 2 * D_FF
    assert t_len % max(PROJ_TM, FFN_TM_PLAIN, BAND_TL, CMP_TQ, MOBA_T) == 0
    assert t_len % (BAND_SB * max(r for _, r in C_GROUPS)) == 0
    cos, sin = _rope_tables(t_len)
    wi = ffn_wi.astype(BF16)
    wo = ffn_wo.astype(BF16)
    h = x.reshape(bsz * t_len, D_MODEL)
    for layer in range(depth):
        i = layer // 2
        h = _ffn(h, ffn_norm_pre[layer], wi, wo, (layer, 0))
        if layer % 2 == 0:
            mix = _even_mixer(h, mix_norm[layer], even_w_in[i], even_w_out[i], even_sinks[i], nsa_cmp_pe[i],
                              nsa_cmp_w1[i], nsa_cmp_w2[i], cos, sin, bsz, t_len)
        else:
            mix = _odd_mixer(h, mix_norm[layer], odd_w_in[i], odd_w_out[i], cos, sin, bsz, t_len)
        last = layer == depth - 1
        h = _ffn(h, ffn_norm_post[layer], wi, wo, (layer, 1), final_norm if last else None, mix=mix)
    return h.reshape(bsz, t_len, D_MODEL)
```

```python
import functools

import jax
import jax.numpy as jnp
from jax import lax
from jax.experimental import pallas as pl
from jax.experimental.pallas import tpu as pltpu

D_MODEL = 1024
HEAD_DIM = 64
ROPE_THETA = 10000.0
NORM_EPS = 1e-6
D_FF = 2816
NEG_INF = -1e30
FORCE_SCORE = 1e4

A_Q_HEADS = 8
A_KV_HEADS = 2
A_WINDOW = 128
B_Q_HEADS = 8
B_KV_HEADS = 2
NSA_CMP_LEN = 32
NSA_CMP_STRIDE = 16
NSA_SEL_LEN = 64
NSA_SEL_COUNT = 8
NSA_WINDOW = 512
C_GROUPS = ((128, 1), (512, 4), (2048, 16))
C_HEADS_PER_GROUP = 4
C_HEADS = len(C_GROUPS) * C_HEADS_PER_GROUP
D_HEADS = 4
MOBA_BLOCK = 256
MOBA_TOPK = 3

LANES = 128
LOG2E = 1.4426950408889634
LN2 = 0.6931471805599453
QK_SCALE = HEAD_DIM ** -0.5 * LOG2E
MASK_BIAS = NEG_INF
V_ROWS = HEAD_DIM + 16
REMOVED = -3e38
VMEM_LIMIT = 52 * 1024 * 1024

FFN_TM = 512
FFN_TM_PLAIN = 1024
FFN_TF = 256
PROJ_TM = 1024
BAND_TL = 1024
WIN_TL = 512
BAND_SB = 128
FLASH_T = 256
FLASH_TK = 512
FLASH_CHAINS = 4
MOBA_T = 512
MOBA_TK = 512
GATE_TQ = 4096
CMP_TQ = 1024

BF16 = jnp.bfloat16
F32 = jnp.float32


def _params(*sem):
    return pltpu.CompilerParams(dimension_semantics=sem, vmem_limit_bytes=VMEM_LIMIT)


def _rms(x, g):
    return x * lax.rsqrt(jnp.mean(x * x, axis=-1, keepdims=True) + NORM_EPS) * g


def _log2(n):
    assert n & (n - 1) == 0
    return n.bit_length() - 1


def _dot(a, b):
    return jnp.dot(a, b, preferred_element_type=F32)


def _dot_t(a, b):
    return lax.dot_general(a, b, (((1,), (1,)), ((), ())), preferred_element_type=F32)


def _dot_tn(a, b):
    return lax.dot_general(a, b, (((0,), (0,)), ((), ())), preferred_element_type=F32)


def _ffn_kernel(x_ref, g_ref, wi_ref, wo_ref, *rest, final, mix_fn, n_mix):
    rest = list(rest)
    fg_ref = rest.pop(0) if final else None
    mix_refs = [rest.pop(0) for _ in range(n_mix)]
    o_ref, act_scr = rest.pop(0), rest.pop(0)
    n_sub = x_ref.shape[0] // FFN_TM
    assert mix_fn is None or n_sub == 1
    for sub in range(n_sub):
        rows = slice(sub * FFN_TM, (sub + 1) * FFN_TM)
        x = x_ref[rows, :]
        if mix_fn is not None:
            x = x + mix_fn(mix_refs, rest)
        o_ref[rows, :] = x
        n = _rms(x, g_ref[...]).astype(BF16)
        for c in range(D_FF // FFN_TF):
            cols = slice(c * FFN_TF, (c + 1) * FFN_TF)
            gate = _dot(n, wi_ref[:, cols])
            up = _dot(n, wi_ref[:, D_FF + c * FFN_TF:D_FF + (c + 1) * FFN_TF])
            act_scr[rows, cols] = (gate * jax.nn.sigmoid(gate) * up).astype(BF16)
        h = o_ref[rows, :] + 0.5 * _dot(act_scr[rows, :], wo_ref[...])
        if final:
            h = _rms(h, fg_ref[...])
        o_ref[rows, :] = h


def _resident(shape, lead=()):
    block = (None,) * len(lead) + tuple(shape[len(lead):])
    index = tuple(lead) + (0,) * (len(shape) - len(lead))
    return pl.BlockSpec(block, lambda i: index, pipeline_mode=pl.Buffered(1))


def _ffn(h, gain, wi, wo, which, final_gain=None, mix=None):
    m = h.shape[0]
    tm = FFN_TM if mix is not None else FFN_TM_PLAIN
    final = final_gain is not None
    in_specs = [
        pl.BlockSpec((tm, D_MODEL), lambda i: (i, 0)),
        pl.BlockSpec((1, D_MODEL), lambda i: (0, 0)),
        _resident(wi.shape, which),
        _resident(wo.shape, which),
    ]
    args = [h, gain.reshape(1, D_MODEL), wi, wo]
    if final:
        in_specs.append(pl.BlockSpec((1, D_MODEL), lambda i: (0, 0)))
        args.append(final_gain.reshape(1, D_MODEL))
    scratch = [pltpu.VMEM((tm, D_FF), BF16)]
    if mix is not None:
        in_specs += mix["specs"]
        args += mix["args"]
        scratch += mix["scratch"]
    return pl.pallas_call(
        functools.partial(_ffn_kernel, final=final, mix_fn=mix and mix["fn"], n_mix=len(mix["args"]) if mix else 0),
        grid=(m // tm,),
        in_specs=in_specs,
        out_specs=pl.BlockSpec((tm, D_MODEL), lambda i: (i, 0)),
        out_shape=jax.ShapeDtypeStruct((m, D_MODEL), F32),
        scratch_shapes=scratch,
        compiler_params=_params("parallel"),
    )(*args)


def _rope_tables(t):
    inv = 1.0 / (ROPE_THETA ** (jnp.arange(0, HEAD_DIM, 2, dtype=F32) / HEAD_DIM))
    ang = jnp.arange(t, dtype=F32)[:, None] * inv[None, :]
    cos = jnp.cos(ang)
    sin = jnp.sin(ang)
    return (jnp.concatenate([cos, cos, cos, cos], axis=-1),
            jnp.concatenate([-sin, sin, -sin, sin], axis=-1))


def _proj_kernel(x_ref, g_ref, w_ref, cos_ref, sin_ref, *rest, plan, n_out):
    out_refs, (ys,) = rest[:n_out], rest[n_out:]
    n = _rms(x_ref[...], g_ref[...]).astype(BF16)
    tm = n.shape[0]
    cos = cos_ref[...]
    sin = sin_ref[...]
    lane = lax.broadcasted_iota(jnp.int32, cos.shape, 1)
    first_half = (lane & (HEAD_DIM - 1)) < HEAD_DIM // 2
    heads_per_tile = LANES // HEAD_DIM
    for off, width, outs in plan:
        y = _dot(n, w_ref[:, off:off + width])
        for o in outs:
            o_ref = out_refs[o["idx"]]
            dt = o_ref.dtype
            kind = o["kind"]
            for c in range(o["width"] // LANES):
                piece = y[:, o["off"] + c * LANES: o["off"] + (c + 1) * LANES]
                if o.get("rope"):
                    rot = jnp.where(first_half, pltpu.roll(piece, LANES - HEAD_DIM // 2, 1),
                                    pltpu.roll(piece, HEAD_DIM // 2, 1))
                    piece = piece * cos + rot * sin
                if o.get("scale"):
                    piece = piece * QK_SCALE
                if o.get("sigmoid"):
                    piece = jax.nn.sigmoid(piece)
                lanes = slice(c * LANES, (c + 1) * LANES)
                if kind == "flat":
                    o_ref[:, lanes] = piece.astype(dt)
                elif kind == "split":
                    hw = o["hw"]
                    per = LANES // hw
                    for k in range(per):
                        o_ref[c * per + k] = piece[:, k * hw:(k + 1) * hw].astype(dt)
                elif kind == "dilate":
                    r = o["r"]
                    if r == 1:
                        o_ref[0, 0, :, lanes] = piece.astype(dt)
                    else:
                        ys[...] = piece
                        for i in range(r):
                            o_ref[0, i, :, lanes] = ys[pl.ds(i, tm // r, stride=r), :].astype(dt)
                elif kind == "chunk":
                    ys[...] = piece
                    for l in range(NSA_CMP_STRIDE):
                        rows = ys[pl.ds(l, tm // NSA_CMP_STRIDE, stride=NSA_CMP_STRIDE), :].astype(dt)
                        for k in range(heads_per_tile):
                            o_ref[c * heads_per_tile + k, :, l * HEAD_DIM:(l + 1) * HEAD_DIM] = rows[
                                :, k * HEAD_DIM:(k + 1) * HEAD_DIM]
                else:
                    tile = o["tile"]
                    pt = piece.T
                    for k in range(heads_per_tile):
                        for kt in range(tm // tile):
                            o_ref[c * heads_per_tile + k, 0, kt, 0:HEAD_DIM, :] = pt[
                                k * HEAD_DIM:(k + 1) * HEAD_DIM, kt * tile:(kt + 1) * tile].astype(dt)
                            o_ref[c * heads_per_tile + k, 0, kt, HEAD_DIM:V_ROWS, :] = jnp.ones(
                                (V_ROWS - HEAD_DIM, tile), dt)


def _proj(h, gain, w, cos, sin, plan, out_defs, bsz, t_len):
    m = h.shape[0]
    tm = PROJ_TM
    nt = t_len // tm
    out_shapes, out_specs = [], []
    for d in out_defs:
        kind, dt = d[0], d[-1]
        if kind == "flat":
            shape, block, index = (m, d[1]), (tm, d[1]), (lambda i: (i, 0))
        elif kind == "split":
            shape, block, index = (d[1], m, d[2]), (d[1], tm, d[2]), (lambda i: (0, i, 0))
        elif kind == "dilate":
            r = d[1]
            shape, block = (bsz, r, t_len // r, d[2]), (1, r, tm // r, d[2])
            index = lambda i: (i // nt, 0, i % nt, 0)
        elif kind == "chunk":
            width = NSA_CMP_STRIDE * HEAD_DIM
            shape, block = (d[1], m // NSA_CMP_STRIDE, width), (d[1], tm // NSA_CMP_STRIDE, width)
            index = lambda i: (0, i, 0)
        else:
            tile = d[2]
            shape, block = (d[1], bsz, t_len // tile, V_ROWS, tile), (d[1], 1, tm // tile, V_ROWS, tile)
            index = lambda i: (0, i // nt, i % nt, 0, 0)
        out_shapes.append(jax.ShapeDtypeStruct(shape, dt))
        out_specs.append(pl.BlockSpec(block, index))
    return pl.pallas_call(
        functools.partial(_proj_kernel, plan=plan, n_out=len(out_defs)),
        grid=(m // tm,),
        in_specs=[
            pl.BlockSpec((tm, D_MODEL), lambda i: (i, 0)),
            pl.BlockSpec((1, D_MODEL), lambda i: (0, 0)),
            pl.BlockSpec(w.shape, lambda i: (0, 0)),
            pl.BlockSpec((tm, LANES), lambda i: (i % nt, 0)),
            pl.BlockSpec((tm, LANES), lambda i: (i % nt, 0)),
        ],
        out_specs=out_specs,
        out_shape=out_shapes,
        scratch_shapes=[pltpu.VMEM((tm, LANES), F32)],
        compiler_params=_params("parallel"),
    )(h, gain.reshape(1, D_MODEL), w, cos, sin)


def _banded_kernel(q_ref, kp_ref, kc_ref, vp_ref, vc_ref, *rest, n_kv, group, tl, pk, span, max_dist,
                   has_sink, gate_branch, want_lse):
    rest = list(rest)
    sink_ref = rest.pop(0) if has_sink else None
    gate_ref = rest.pop(0) if gate_branch is not None else None
    o_ref = rest.pop(0)
    lse_ref = rest.pop(0) if want_lse else None
    kcat, vcat, s_a, s_b, band = rest
    sb = BAND_SB
    n_seq = kc_ref.shape[1]
    for rr in range(n_seq):
        kcat[rr, 0:pk, :] = kp_ref[0, rr]
        kcat[rr, pk:, :] = kc_ref[0, rr]
        vcat[rr, 0:pk, :] = vp_ref[0, rr]
        vcat[rr, pk:, :] = vc_ref[0, rr]
    nk = span + sb
    kw = n_kv * HEAD_DIM
    merged = group == 1
    stacked = n_kv if merged else group
    rows = stacked * sb
    key = lax.broadcasted_iota(jnp.int32, (nk, rows), 0)
    qry = lax.broadcasted_iota(jnp.int32, (nk, rows), 1) & (sb - 1)
    dist = qry + span - key
    in_band = (dist >= 0) & (dist <= max_dist)
    lane_head = lax.broadcasted_iota(jnp.int32, (1, rows), 1) >> _log2(sb)
    q_head = lax.broadcasted_iota(jnp.int32, (sb, kw), 1) >> _log2(HEAD_DIM)
    v_head = lax.broadcasted_iota(jnp.int32, (nk, kw), 1) >> _log2(HEAD_DIM)
    assert merged or kw == LANES
    items = [(rr, j, h) for rr in range(n_seq) for j in range(tl // sb)
             for h in ([0] if merged else range(n_kv))]
    s_bufs = (s_a, s_b)

    def compute(first_tile):
        def scores(n):
            rr, j, h = items[n]
            lo = pk + j * sb - span
            qrows = slice(j * sb, (j + 1) * sb)
            if merged:
                qj = q_ref[0, rr, qrows, :].astype(F32)
                qs = jnp.concatenate([jnp.where(q_head == r, qj, 0.0).astype(BF16) for r in range(n_kv)], axis=0)
                kk = kcat[rr, lo:lo + nk, :]
            else:
                qs = jnp.concatenate([q_ref[h * group + g, 0, qrows, :] for g in range(group)], axis=0)
                kk = kcat[rr, lo:lo + nk, h * HEAD_DIM:(h + 1) * HEAD_DIM]
            variant = 1 + j if first_tile and j * sb < span else 0
            s_bufs[n % 2][...] = _dot_t(kk, qs) + band[variant]

        band[0] = jnp.where(in_band, 0.0, NEG_INF)
        if first_tile:
            for j in range(span // sb):
                band[1 + j] = jnp.where(in_band & (key >= span - j * sb), 0.0, NEG_INF)

        outs, lses = [], []
        scores(0)
        for n, (rr, j, h) in enumerate(items):
            if n + 1 < len(items):
                scores(n + 1)
            s = s_bufs[n % 2][...]
            lo = pk + j * sb - span
            qrows = slice(j * sb, (j + 1) * sb)
            m = jnp.max(s, axis=0, keepdims=True)
            if has_sink:
                sink = jnp.full((1, rows), sink_ref[h * group], F32)
                for g in range(1, group):
                    sink = jnp.where(lane_head == g, sink_ref[h * group + g], sink)
                sink = sink * LOG2E
                m = jnp.maximum(m, sink)
            p = jnp.exp2(s - m).astype(BF16)
            vwin = vcat[rr, lo:lo + nk, :]
            if merged:
                vv = jnp.concatenate([vwin, jnp.ones((nk, LANES), BF16)], axis=1)
                ones_row = kw
            else:
                vv = jnp.where(v_head == h, vwin.astype(F32), 1.0).astype(BF16)
                ones_row = (1 - h) * HEAD_DIM
            o_t = _dot_tn(vv, p)
            d = o_t[ones_row:ones_row + 1, :]
            if has_sink:
                d = d + jnp.exp2(sink - m)
            d = jnp.where(d > 0, d, 1.0)
            inv = 1.0 / d
            if want_lse:
                lse_t = jnp.broadcast_to(m * LN2 + jnp.log(d), (HEAD_DIM, rows))
            for r in range(stacked):
                cols = slice(r * sb, (r + 1) * sb)
                vrow = (r if merged else h) * HEAD_DIM
                o = (o_t[vrow:vrow + HEAD_DIM, cols] * inv[:, cols]).T
                if gate_ref is not None:
                    gc = r * 3 + gate_branch
                    o = o * gate_ref[h, 0, qrows, gc:gc + 1]
                outs.append(o)
                if want_lse:
                    lses.append(lse_t[:, cols].T)
            if merged or h == n_kv - 1:
                o_ref[0, rr, qrows, :] = jnp.concatenate(outs, axis=-1).astype(o_ref.dtype)
                if want_lse:
                    lse_ref[0, rr, qrows, :] = jnp.concatenate(lses, axis=-1)
                outs, lses = [], []

    first = pl.program_id(2) == 0
    pl.when(first)(functools.partial(compute, True))
    pl.when(jnp.logical_not(first))(functools.partial(compute, False))


def _banded(q, k, v, *, n_kv, group, max_dist, tl, sink=None, gate=None, gate_branch=None, want_lse=False):
    bsz, n_res, length, kw = k.shape
    qw = n_kv * group * HEAD_DIM
    span = -(-max_dist // BAND_SB) * BAND_SB
    pk = span
    assert tl % pk == 0
    s_shape = (span + BAND_SB, (n_kv if group == 1 else group) * BAND_SB)
    ratio = tl // pk
    n_seq = min(n_res, max(1, BAND_TL // tl))
    prev = lambda b, i, l: (b, i, jnp.maximum(l * ratio - 1, 0), 0)
    cur = lambda b, i, l: (b, i, l, 0)
    if group == 1:
        q_spec = pl.BlockSpec((1, n_seq, tl, qw), cur)
    else:
        q_spec = pl.BlockSpec((n_kv * group, 1, tl, HEAD_DIM), lambda b, i, l: (0, b, l, 0))
    o_shape = (bsz, n_res, length, qw)
    in_specs = [
        q_spec,
        pl.BlockSpec((1, n_seq, pk, kw), prev),
        pl.BlockSpec((1, n_seq, tl, kw), cur),
        pl.BlockSpec((1, n_seq, pk, kw), prev),
        pl.BlockSpec((1, n_seq, tl, kw), cur),
    ]
    args = [q, k, k, v, v]
    if sink is not None:
        in_specs.append(pl.BlockSpec(memory_space=pltpu.SMEM))
        args.append(sink)
    if gate is not None:
        in_specs.append(pl.BlockSpec((n_kv, 1, tl, LANES), lambda b, i, l: (0, b, l, 0)))
        args.append(gate)
    out_spec = pl.BlockSpec((1, n_seq, tl, qw), cur)
    out_shape = [jax.ShapeDtypeStruct(o_shape, BF16)]
    out_specs = [out_spec]
    if want_lse:
        out_shape.append(jax.ShapeDtypeStruct(o_shape, F32))
        out_specs.append(out_spec)
    res = pl.pallas_call(
        functools.partial(_banded_kernel, n_kv=n_kv, group=group, tl=tl, pk=pk, span=span,
                          max_dist=max_dist, has_sink=sink is not None, gate_branch=gate_branch,
                          want_lse=want_lse),
        grid=(bsz, n_res // n_seq, length // tl),
        in_specs=in_specs,
        out_specs=out_specs,
        out_shape=out_shape,
        scratch_shapes=[pltpu.VMEM((n_seq, pk + tl, kw), BF16), pltpu.VMEM((n_seq, pk + tl, kw), BF16),
                        pltpu.VMEM(s_shape, F32), pltpu.VMEM(s_shape, F32),
                        pltpu.VMEM((1 + span // BAND_SB,) + s_shape, F32)],
        compiler_params=_params("parallel", "parallel", "parallel"),
    )(*args)
    return res if want_lse else res[0]


def _compress_one(z_ref, pe_ref, w1_ref, w2_ref, o_ref):
    half = NSA_CMP_STRIDE * HEAD_DIM
    z = z_ref[0, 0].astype(F32)
    za = (z + pe_ref[0, :, 0:half]).astype(BF16)
    zb = (z + pe_ref[0, :, half:2 * half]).astype(BF16)
    ya = _dot(za, w1_ref[0, 0:half, :])
    yb = _dot(zb, w1_ref[0, half:2 * half, :])
    nrow = ya.shape[0]
    hid = jax.nn.gelu(ya + pltpu.roll(yb, nrow - 1, 0))
    o_ref[0, 0] = _dot(hid.astype(BF16), w2_ref[0]).astype(o_ref.dtype)


def _compress_kernel(zk_ref, zv_ref, pek_ref, pev_ref, w1k_ref, w1v_ref, w2k_ref, w2v_ref, ok_ref, ov_ref):
    _compress_one(zk_ref, pek_ref, w1k_ref, w2k_ref, ok_ref)
    _compress_one(zv_ref, pev_ref, w1v_ref, w2v_ref, ov_ref)


def _compress(zk, zv, pe, w1, w2):
    hk, bsz, nch, width = zk.shape
    zspec = pl.BlockSpec((1, 1, nch, width), lambda h, b: (h, b, 0, 0))
    ospec = pl.BlockSpec((1, 1, nch, HEAD_DIM), lambda h, b: (h, b, 0, 0))
    oshape = jax.ShapeDtypeStruct((hk, bsz, nch, HEAD_DIM), BF16)

    def wspec(arr, which):
        return pl.BlockSpec((1,) + arr.shape[1:], lambda h, b: (which, 0, 0))

    return pl.pallas_call(
        _compress_kernel,
        grid=(hk, bsz),
        in_specs=[zspec, zspec, wspec(pe, 0), wspec(pe, 1), wspec(w1, 0), wspec(w1, 1), wspec(w2, 0), wspec(w2, 1)],
        out_specs=[ospec, ospec],
        out_shape=[oshape, oshape],
        compiler_params=_params("parallel", "parallel"),
    )(zk, zv, pe, pe, w1, w1, w2, w2)


def _topk_select(score, k, idxf):
    ncand = score.shape[0]
    rem = score
    sel = jnp.zeros_like(score)
    for _ in range(k):
        mx = jnp.max(rem, axis=0, keepdims=True)
        first = jnp.min(jnp.where(rem == mx, idxf, float(ncand)), axis=0, keepdims=True)
        pick = idxf == first
        sel = jnp.where(pick, jnp.where(mx > 0.5 * NEG_INF, 1.0, 0.0), sel)
        rem = jnp.where(pick, REMOVED, rem)
    return sel


def _cmp_kernel(q_ref, kc_ref, vc_ref, gate_ref, o_ref, bias_ref, *, tq, group, n_tiles):
    qi = pl.program_id(1)

    def tile(ncp):
        for hk in range(kc_ref.shape[0]):
            _cmp_tile(q_ref, kc_ref, vc_ref, gate_ref, o_ref, bias_ref, qi, hk, tq=tq, group=group, ncp=ncp)

    for v in range(n_tiles):
        pl.when(qi == v)(functools.partial(tile, min(kc_ref.shape[2], (v + 1) * tq // NSA_CMP_STRIDE)))


def _cmp_tile(q_ref, kc_ref, vc_ref, gate_ref, o_ref, bias_ref, qi, hk, *, tq, group, ncp):
    kc = kc_ref[hk, 0, 0:ncp, :]
    vc = vc_ref[hk, 0, 0:ncp, :]
    gw = group * HEAD_DIM
    nsb = bias_ref.shape[-1]
    rows = group * tq
    last = lax.broadcasted_iota(jnp.int32, (ncp, 1), 0) * NSA_CMP_STRIDE + (NSA_CMP_LEN - 1)
    t_in_tile = lax.broadcasted_iota(jnp.int32, (1, rows), 1) & (tq - 1)
    visible = (last - t_in_tile) <= qi * tq
    qs = jnp.concatenate([q_ref[0, :, hk * gw + g * HEAD_DIM:hk * gw + (g + 1) * HEAD_DIM]
                          for g in range(group)], axis=0)
    s = jnp.where(visible, _dot_t(kc, qs), NEG_INF)
    m = jnp.max(s, axis=0, keepdims=True)
    m = jnp.where(m > 0.5 * NEG_INF, m, 0.0)
    p = jnp.exp2(s - m)
    vc_ones = jnp.concatenate([vc, jnp.ones_like(vc)], axis=1)
    o_t = _dot_tn(vc_ones, p.astype(BF16))
    d = o_t[HEAD_DIM:HEAD_DIM + 1, :]
    d = jnp.where(d > 0, d, 1.0)
    inv = 1.0 / d
    pc = p * inv
    o_t = o_t[0:HEAD_DIM, :] * inv
    outs = []
    psum = pc[:, 0:tq]
    for g in range(group):
        if g:
            psum = psum + pc[:, g * tq:(g + 1) * tq]
        outs.append(o_t[:, g * tq:(g + 1) * tq].T * gate_ref[hk, 0, :, g * 3:g * 3 + 1])
    o_ref[0, :, hk * gw:(hk + 1) * gw] = jnp.concatenate(outs, axis=-1).astype(o_ref.dtype)

    jj = lax.broadcasted_iota(jnp.int32, (nsb, ncp), 0) * NSA_SEL_LEN
    nn = lax.broadcasted_iota(jnp.int32, (nsb, ncp), 1) * NSA_CMP_STRIDE
    ov = jnp.maximum(jnp.minimum(nn + NSA_CMP_LEN, jj + NSA_SEL_LEN) - jnp.maximum(nn, jj), 0)
    ov = (ov.astype(F32) / NSA_CMP_LEN).astype(BF16)
    p_hi = psum.astype(BF16)
    p_lo = (psum - p_hi.astype(F32)).astype(BF16)
    imp = _dot(ov, p_hi) + _dot(ov, p_lo)

    j = lax.broadcasted_iota(jnp.int32, (nsb, tq), 0)
    tt = qi * tq + lax.broadcasted_iota(jnp.int32, (nsb, tq), 1)
    cb = tt >> _log2(NSA_SEL_LEN)
    forced = (j == 0) | (j == cb) | (j == cb - 1)
    assert FORCE_SCORE > group
    n_forced = 3
    free = jnp.where((j <= cb) & jnp.logical_not(forced), imp, NEG_INF)
    sel = _topk_select(free, min(NSA_SEL_COUNT, nsb) - n_forced, j.astype(F32))
    keep = (sel > 0.5) | (forced & (j <= cb))
    bias_ref[hk, 0] = jnp.where(keep, 0.0, MASK_BIAS).T.astype(bias_ref.dtype)


def _cmp_attention(qu, kc, vc, gate, t_len):
    bsz = qu.shape[0]
    hk, _, ncp, _ = kc.shape
    group = B_Q_HEADS // B_KV_HEADS
    tq = CMP_TQ
    nsb = t_len // NSA_SEL_LEN
    gw = group * HEAD_DIM
    cspec = pl.BlockSpec((hk, 1, ncp, HEAD_DIM), lambda b, i: (0, b, 0, 0))
    return pl.pallas_call(
        functools.partial(_cmp_kernel, tq=tq, group=group, n_tiles=t_len // tq),
        grid=(bsz, t_len // tq),
        in_specs=[
            pl.BlockSpec((1, tq, hk * gw), lambda b, i: (b, i, 0)),
            cspec, cspec,
            pl.BlockSpec((hk, 1, tq, LANES), lambda b, i: (0, b, i, 0)),
        ],
        out_specs=[
            pl.BlockSpec((1, tq, hk * gw), lambda b, i: (b, i, 0)),
            pl.BlockSpec((hk, 1, tq, nsb), lambda b, i: (0, b, i, 0)),
        ],
        out_shape=[
            jax.ShapeDtypeStruct((bsz, t_len, hk * gw), BF16),
            jax.ShapeDtypeStruct((hk, bsz, t_len, nsb), BF16),
        ],
        compiler_params=_params("parallel", "parallel"),
    )(qu, kc, vc, gate)


def _moba_gate_kernel(q_ref, k_ref, bias_ref, km_hi, km_lo, *, tq):
    qi = pl.program_id(1)
    nh, _, t_len, _ = k_ref.shape
    ncol = bias_ref.shape[-1]
    nb = t_len // MOBA_BLOCK

    @pl.when(qi == 0)
    def _():
        blk = lax.broadcasted_iota(jnp.int32, (ncol, t_len), 0)
        pos = lax.broadcasted_iota(jnp.int32, (ncol, t_len), 1)
        member = jnp.where((pos >> _log2(MOBA_BLOCK)) == blk, 1.0, 0.0).astype(BF16)
        for h in range(nh):
            kmean = _dot(member, k_ref[h, 0]) * (1.0 / MOBA_BLOCK)
            hi = kmean.astype(BF16)
            km_hi[h] = hi
            km_lo[h] = (kmean - hi.astype(F32)).astype(BF16)

    j = lax.broadcasted_iota(jnp.int32, (nb, tq), 0)
    t = qi * tq + lax.broadcasted_iota(jnp.int32, (nb, tq), 1)
    cb = t >> _log2(MOBA_BLOCK)
    for h in range(nh):
        q = q_ref[h, 0]
        gate = (_dot_t(km_hi[h], q) + _dot_t(km_lo[h], q))[0:nb]
        score = jnp.where(j < cb, gate, NEG_INF)
        sel = _topk_select(score, min(MOBA_TOPK, nb - 1), j.astype(F32))
        keep = (sel > 0.5) | (j == cb)
        bias = jnp.where(keep, 0.0, MASK_BIAS)
        bias = jnp.concatenate([bias, jnp.zeros((ncol - nb, tq), F32)], axis=0)
        bias_ref[h, 0] = bias.T.astype(bias_ref.dtype)


def _moba_gate(q, k):
    nh, bsz, t_len, _ = q.shape
    tq = min(GATE_TQ, t_len)
    return pl.pallas_call(
        functools.partial(_moba_gate_kernel, tq=tq),
        grid=(bsz, t_len // tq),
        in_specs=[
            pl.BlockSpec((nh, 1, tq, HEAD_DIM), lambda b, i: (0, b, i, 0)),
            pl.BlockSpec((nh, 1, t_len, HEAD_DIM), lambda b, i: (0, b, 0, 0)),
        ],
        out_specs=pl.BlockSpec((nh, 1, tq, HEAD_DIM), lambda b, i: (0, b, i, 0)),
        out_shape=jax.ShapeDtypeStruct((nh, bsz, t_len, HEAD_DIM), BF16),
        scratch_shapes=[pltpu.VMEM((nh, HEAD_DIM, HEAD_DIM), BF16), pltpu.VMEM((nh, HEAD_DIM, HEAD_DIM), BF16)],
        compiler_params=_params("parallel", "arbitrary"),
    )(q, k)


def _flash_kernel(q_ref, bias_ref, k_ref, vt_ref, *rest, group, chain_shape, lanes_out, tq, tk, blk_len, q_scale,
                  gate_branch):
    rest = list(rest)
    gate_ref = rest.pop(0) if gate_branch is not None else None
    o_ref, qa, ka, m_s, acc = rest[:5]
    s_bufs = rest[5:]
    chain_ids = [(ch, cb) for cb in range(chain_shape[1]) for ch in range(chain_shape[0])]
    chains = len(chain_ids)
    qi = pl.program_id(2)
    rows = group * tq
    t_len = k_ref.shape[2]

    @pl.when(qi == 0)
    def _():
        kpos = lax.broadcasted_iota(jnp.int32, (t_len, HEAD_DIM), 0)
        kblk = lax.broadcasted_iota(jnp.int32, (t_len, HEAD_DIM), 1)
        onehot = jnp.where((kpos >> _log2(blk_len)) == kblk, 1.0, 0.0).astype(BF16)
        for c, (ch, cb) in enumerate(chain_ids):
            ka[c, :, 0:HEAD_DIM] = k_ref[ch, cb]
            ka[c, :, HEAD_DIM:2 * HEAD_DIM] = onehot

    for c, (ch, cb) in enumerate(chain_ids):
        bias = bias_ref[ch, cb]
        for g in range(group):
            qg = q_ref[ch * group + g, cb]
            if q_scale != 1.0:
                qg = (qg.astype(F32) * q_scale).astype(BF16)
            qa[c, g * tq:(g + 1) * tq, 0:HEAD_DIM] = qg
            qa[c, g * tq:(g + 1) * tq, HEAD_DIM:2 * HEAD_DIM] = bias
    m_s[...] = jnp.full_like(m_s, NEG_INF)
    acc[...] = jnp.zeros_like(acc)

    def scores(ki, slot):
        start = pl.multiple_of(ki * tk, tk)
        for c in range(chains):
            s_bufs[2 * c + slot][...] = _dot_t(ka[c, pl.ds(start, tk), :], qa[c])

    def softmax_rows(ki, slot, masked, r0, nr):
        for c, (ch, cb) in enumerate(chain_ids):
            s = s_bufs[2 * c + slot][r0:r0 + nr, :]
            if masked:
                kpos = ki * tk + r0 + lax.broadcasted_iota(jnp.int32, (nr, rows), 0)
                qpos = qi * tq + (lax.broadcasted_iota(jnp.int32, (nr, rows), 1) & (tq - 1))
                s = jnp.where(kpos <= qpos, s, NEG_INF)
            m_old = m_s[c]
            m_new = jnp.maximum(m_old, jnp.max(s, axis=0, keepdims=True))
            alpha = jnp.exp2(m_old - m_new)
            p = jnp.exp2(s - m_new)
            acc[c] = alpha * acc[c] + _dot(vt_ref[ch, cb, ki, :, r0:r0 + nr], p.astype(BF16))
            m_s[c] = m_new

    def softmax_pv(ki, slot, masked):
        if masked and tk > tq:
            for sub in range(tk // tq):
                piece = functools.partial(softmax_rows, ki, slot, True, sub * tq, tq)
                if sub == 0:
                    piece()
                else:
                    pl.when(qi * tq - ki * tk >= sub * tq)(piece)
        else:
            softmax_rows(ki, slot, masked, 0, tk)

    n_full = (qi * tq) >> _log2(tk)
    scores(0, 0)

    def pair(j, carry):
        k0 = 2 * j
        scores(k0 + 1, 1)
        softmax_pv(k0, 0, False)
        scores(k0 + 2, 0)
        softmax_pv(k0 + 1, 1, False)
        return carry

    lax.fori_loop(0, n_full >> 1, pair, 0)
    cur = (n_full >> 1) << 1

    @pl.when((n_full & 1) == 1)
    def _():
        scores(cur + 1, 1)
        softmax_pv(cur, 0, False)
        softmax_pv(cur + 1, 1, True)

    @pl.when((n_full & 1) == 0)
    def _():
        softmax_pv(cur, 0, True)

    outs = {}
    for c, (ch, cb) in enumerate(chain_ids):
        d = acc[c, HEAD_DIM:HEAD_DIM + 1, :]
        d = jnp.where(d > 0, d, 1.0)
        o_t = acc[c, 0:HEAD_DIM, :] * (1.0 / d)
        for g in range(group):
            og = o_t[:, g * tq:(g + 1) * tq].T
            if gate_ref is not None:
                gc = g * 3 + gate_branch
                og = og * gate_ref[ch, cb, :, gc:gc + 1]
            outs[(cb, ch * group + g)] = og
    for cb in range(chain_shape[1]):
        planes = [outs[(cb, hq)] for hq in range(chain_shape[0] * group)]
        if lanes_out:
            o_ref[0, cb] = jnp.concatenate(planes, axis=-1).astype(o_ref.dtype)
        else:
            for hq, og in enumerate(planes):
                o_ref[hq, cb] = og.astype(o_ref.dtype)


def _flash(q, bias, k, vt, *, group, tq, blk_len, q_scale, lanes_out, gate=None, gate_branch=None):
    nh, bsz, t_len, _ = k.shape
    ch = min(nh, FLASH_CHAINS)
    cb = max(1, FLASH_CHAINS // ch)
    cb = cb if bsz % cb == 0 else 1
    chains = ch * cb
    nt, tk = vt.shape[2], vt.shape[4]
    per_step = lambda h, b, i: (h, b, i, 0)
    whole_seq = lambda h, b, i: (h, b, 0, 0)
    in_specs = [
        pl.BlockSpec((ch * group, cb, tq, HEAD_DIM), per_step),
        pl.BlockSpec((ch, cb, tq, HEAD_DIM), per_step),
        pl.BlockSpec((ch, cb, t_len, HEAD_DIM), whole_seq),
        pl.BlockSpec((ch, cb, nt, V_ROWS, tk), lambda h, b, i: (h, b, 0, 0, 0)),
    ]
    args = [q, bias, k, vt]
    if gate is not None:
        in_specs.append(pl.BlockSpec((ch, cb, tq, LANES), per_step))
        args.append(gate)
    if lanes_out:
        out_shape = (1, bsz, t_len, nh * group * HEAD_DIM)
        out_spec = pl.BlockSpec((1, cb, tq, ch * group * HEAD_DIM), lambda h, b, i: (0, b, i, h))
    else:
        out_shape = (nh * group, bsz, t_len, HEAD_DIM)
        out_spec = pl.BlockSpec((ch * group, cb, tq, HEAD_DIM), per_step)
    rows = group * tq
    return pl.pallas_call(
        functools.partial(_flash_kernel, group=group, chain_shape=(ch, cb), lanes_out=lanes_out, tq=tq, tk=tk,
                          blk_len=blk_len, q_scale=q_scale, gate_branch=gate_branch),
        grid=(nh // ch, bsz // cb, t_len // tq),
        in_specs=in_specs,
        out_specs=out_spec,
        out_shape=jax.ShapeDtypeStruct(out_shape, BF16),
        scratch_shapes=[
            pltpu.VMEM((chains, rows, 2 * HEAD_DIM), BF16),
            pltpu.VMEM((chains, t_len, 2 * HEAD_DIM), BF16),
            pltpu.VMEM((chains, 1, rows), F32),
            pltpu.VMEM((chains, V_ROWS, rows), F32),
        ] + [pltpu.VMEM((tk, rows), F32)] * (2 * chains),
        compiler_params=_params("parallel", "parallel", "arbitrary"),
    )(*args)


def _even_delta(refs, scratch):
    oa_ref, oc_ref, os_ref, ow_ref, w_ref = refs
    na = oa_ref.shape[-1]
    ob = (oc_ref[...].astype(F32) + os_ref[...].astype(F32) + ow_ref[...].astype(F32)).astype(BF16)
    return _dot(oa_ref[...], w_ref[0:na, :]) + _dot(ob, w_ref[na:, :])


def _even_mix(oa, ocmp, osel, owin, w):
    aspec = pl.BlockSpec((FFN_TM, oa.shape[-1]), lambda i: (i, 0))
    return dict(args=[oa, ocmp, osel, owin, w], specs=[aspec] * 4 + [_resident(w.shape)], scratch=[],
                fn=_even_delta)


def _odd_delta(refs, scratch, *, dilations):
    ng = len(dilations)
    o_refs, l_refs = refs[:ng], refs[ng:2 * ng]
    od_ref, w_ref = refs[2 * ng:]
    scratch = list(scratch)
    tm = od_ref.shape[0]

    def tokens(ref, r):
        if r == 1:
            return ref[0, 0].astype(F32)
        pieces = []
        for c in range(ref.shape[-1] // LANES):
            scr = scratch.pop(0)
            for i in range(r):
                scr[pl.ds(i, tm // r, stride=r), :] = ref[0, i, :, c * LANES:(c + 1) * LANES].astype(F32)
            pieces.append(scr[...])
        return jnp.concatenate(pieces, axis=-1)

    outs = [tokens(ref, r) for ref, r in zip(o_refs, dilations)]
    lses = [tokens(ref, r) for ref, r in zip(l_refs, dilations)]
    mx = functools.reduce(jnp.maximum, lses)
    es = [jnp.exp(l - mx) for l in lses]
    tot = functools.reduce(lambda a, b: a + b, es)
    oc = functools.reduce(lambda a, b: a + b, [(e / tot) * o for e, o in zip(es, outs)])
    nc = oc.shape[-1]
    return _dot(oc.astype(BF16), w_ref[0:nc, :]) + _dot(od_ref[...], w_ref[nc:, :])


def _odd_mix(outs, lses, od, w, t_len):
    tm = FFN_TM
    nt = t_len // tm
    gw = outs[0].shape[-1]
    dilations = tuple(o.shape[1] for o in outs)
    gspecs = [pl.BlockSpec((1, r, tm // r, gw), lambda i: (i // nt, 0, i % nt, 0)) for r in dilations]
    n_scr = 2 * (gw // LANES) * sum(1 for r in dilations if r > 1)
    specs = gspecs + gspecs + [pl.BlockSpec((tm, od.shape[1]), lambda i: (i, 0)), _resident(w.shape)]
    return dict(args=[*outs, *lses, od, w], specs=specs, scratch=[pltpu.VMEM((tm, LANES), F32)] * n_scr,
                fn=functools.partial(_odd_delta, dilations=dilations))


def _col_ranges(sizes):
    offs, acc = [], 0
    for s in sizes:
        offs.append((acc, acc + s))
        acc += s
    return offs


def _even_mixer(h, gain, w_in, w_out, sinks, cmp_pe, cmp_w1, cmp_w2, cos, sin, bsz, t_len):
    m = bsz * t_len
    qa_w, kva_w = A_Q_HEADS * HEAD_DIM, A_KV_HEADS * HEAD_DIM
    qb_w, kvb_w = B_Q_HEADS * HEAD_DIM, B_KV_HEADS * HEAD_DIM
    sizes = [qa_w, kva_w, kva_w, qb_w] + [kvb_w] * 6 + [3 * B_Q_HEADS]
    (aq, ak, av, bq, bkc, bvc, bks, bvs, bkw, bvw, bg) = [w_in[:, a:b] for a, b in _col_ranges(sizes)]
    group = B_Q_HEADS // B_KV_HEADS
    gpad = jnp.zeros((D_MODEL, LANES - 3 * group), w_in.dtype)
    gates = [x for hk in range(B_KV_HEADS) for x in (bg[:, hk * 3 * group:(hk + 1) * 3 * group], gpad)]
    w = jnp.concatenate([aq, bq, ak, bkw, bks, av, bvw, bvs, bkc, bvc] + gates, axis=1).astype(BF16)
    c = [0]

    def take(width):
        c[0] += width
        return c[0] - width

    def out(idx, off, width, kind, **kw):
        return dict(idx=idx, off=off, width=width, kind=kind, **kw)

    plan = [
        (take(qa_w), qa_w, [out(0, 0, qa_w, "split", hw=HEAD_DIM, rope=True, scale=True)]),
        (take(qb_w), qb_w, [out(1, 0, qb_w, "split", hw=HEAD_DIM, rope=True, scale=True),
                            out(5, 0, qb_w, "flat", scale=True)]),
        (take(3 * kva_w), 3 * kva_w, [out(2, 0, kva_w, "flat", rope=True),
                                      out(3, kva_w, kvb_w, "flat", rope=True),
                                      out(4, kva_w + kvb_w, kvb_w, "split", hw=HEAD_DIM, rope=True)]),
        (take(5 * kvb_w), 5 * kvb_w, [out(6, 0, kva_w, "flat"), out(7, kva_w, kvb_w, "flat"),
                                      out(8, 2 * kvb_w, kvb_w, "vt", tile=FLASH_TK),
                                      out(9, 3 * kvb_w, kvb_w, "chunk"),
                                      out(10, 4 * kvb_w, kvb_w, "chunk")]),
        (take(B_KV_HEADS * LANES), B_KV_HEADS * LANES,
         [out(11, 0, B_KV_HEADS * LANES, "split", hw=LANES, sigmoid=True)]),
    ]
    out_defs = [("split", A_Q_HEADS, HEAD_DIM, BF16), ("split", B_Q_HEADS, HEAD_DIM, BF16), ("flat", kva_w, BF16),
                ("flat", kvb_w, BF16),
                ("split", B_KV_HEADS, HEAD_DIM, BF16), ("flat", qb_w, BF16), ("flat", kva_w, BF16),
                ("flat", kvb_w, BF16), ("vt", B_KV_HEADS, FLASH_TK, BF16), ("chunk", B_KV_HEADS, BF16),
                ("chunk", B_KV_HEADS, BF16), ("split", B_KV_HEADS, LANES, F32)]
    (aq_r, bq_r, ak_r, bkw_r, bks_r, bq_u, av_, bvw_, bvs_t, zk, zv, gate) = _proj(
        h, gain, w, cos, sin, plan, out_defs, bsz, t_len)

    def seq(x):
        return x.reshape(bsz, 1, t_len, x.shape[-1])

    def heads(x):
        return x.reshape(x.shape[0], bsz, t_len, x.shape[-1])

    gate4 = heads(gate)
    oa = _banded(heads(aq_r), seq(ak_r), seq(av_), n_kv=A_KV_HEADS, group=A_Q_HEADS // A_KV_HEADS,
                 max_dist=A_WINDOW - 1, tl=BAND_TL, sink=sinks)
    nch = t_len // NSA_CMP_STRIDE
    zshape = (B_KV_HEADS, bsz, nch, NSA_CMP_STRIDE * HEAD_DIM)
    kc, vc = _compress(zk.reshape(zshape), zv.reshape(zshape), cmp_pe.reshape(2, 1, NSA_CMP_LEN * HEAD_DIM),
                       cmp_w1.astype(BF16), cmp_w2.astype(BF16))
    ocmp, bias = _cmp_attention(bq_u.reshape(bsz, t_len, qb_w), kc, vc, gate4, t_len)
    osel = _flash(heads(bq_r), bias, heads(bks_r), bvs_t, group=group, tq=FLASH_T, blk_len=NSA_SEL_LEN,
                  q_scale=1.0, lanes_out=True, gate=gate4, gate_branch=1)
    owin = _banded(heads(bq_r), seq(bkw_r), seq(bvw_), n_kv=B_KV_HEADS, group=group,
                   max_dist=NSA_WINDOW - 1, tl=WIN_TL, gate=gate4, gate_branch=2)
    return _even_mix(oa.reshape(m, qa_w), ocmp.reshape(m, qb_w), osel.reshape(m, qb_w),
                     owin.reshape(m, qb_w), w_out.astype(BF16))


def _odd_mixer(h, gain, w_in, w_out, cos, sin, bsz, t_len):
    m = bsz * t_len
    cw = C_HEADS * HEAD_DIM
    dw = D_HEADS * HEAD_DIM
    gw = C_HEADS_PER_GROUP * HEAD_DIM
    n_groups = len(C_GROUPS)

    def group_outs(base, **kw):
        return [dict(idx=base + gi, off=gi * gw, width=gw, kind="dilate", r=r, **kw)
                for gi, (_, r) in enumerate(C_GROUPS)]

    plan = [
        (0, cw, group_outs(0, rope=True, scale=True)),
        (cw, cw, group_outs(n_groups, rope=True)),
        (2 * cw, cw, group_outs(2 * n_groups)),
        (3 * cw, 3 * dw, [dict(idx=3 * n_groups, off=0, width=dw, kind="split", hw=HEAD_DIM, rope=True),
                          dict(idx=3 * n_groups + 1, off=dw, width=dw, kind="split", hw=HEAD_DIM, rope=True),
                          dict(idx=3 * n_groups + 2, off=2 * dw, width=dw, kind="vt", tile=MOBA_TK)]),
    ]
    out_defs = [("dilate", r, gw, BF16) for _ in range(3) for _, r in C_GROUPS]
    out_defs += [("split", D_HEADS, HEAD_DIM, BF16), ("split", D_HEADS, HEAD_DIM, BF16),
                 ("vt", D_HEADS, MOBA_TK, BF16)]
    res = _proj(h, gain, w_in.astype(BF16), cos, sin, plan, out_defs, bsz, t_len)
    cq, ck, cv = res[0:n_groups], res[n_groups:2 * n_groups], res[2 * n_groups:3 * n_groups]
    dq, dk, dv_t = res[3 * n_groups:]
    outs, lses = [], []
    for gi, (wlen, r) in enumerate(C_GROUPS):
        o, lse = _banded(cq[gi], ck[gi], cv[gi], n_kv=C_HEADS_PER_GROUP, group=1, max_dist=wlen // r,
                         tl=min(BAND_TL, t_len // r), want_lse=True)
        outs.append(o)
        lses.append(lse)

    def heads(x):
        return x.reshape(x.shape[0], bsz, t_len, x.shape[-1])

    bias = _moba_gate(heads(dq), heads(dk))
    od = _flash(heads(dq), bias, heads(dk), dv_t, group=1, tq=MOBA_T, blk_len=MOBA_BLOCK, q_scale=QK_SCALE,
                lanes_out=True)
    return _odd_mix(outs, lses, od.reshape(m, D_HEADS * HEAD_DIM), w_out.astype(BF16), t_len)


def kernel(x, ffn_norm_pre, mix_norm, ffn_norm_post, ffn_wi, ffn_wo, even_w_in, even_w_out, even_sinks,
           nsa_cmp_pe, nsa_cmp_w1, nsa_cmp_w2, odd_w_in, odd_w_out, final_norm):
    bsz, t_len, d_model = x.shape
    depth = ffn_wi.shape[0]
    assert d_model == D_MODEL and ffn_wi.shape[-1] == 2 * D_FF
    assert t_len % max(PROJ_TM, FFN_TM_PLAIN, BAND_TL, CMP_TQ, MOBA_T) == 0
    assert t_len % (BAND_SB * max(r for _, r in C_GROUPS)) == 0
    cos, sin = _rope_tables(t_len)
    wi = ffn_wi.astype(BF16)
    wo = ffn_wo.astype(BF16)
    h = x.reshape(bsz * t_len, D_MODEL)
    for layer in range(depth):
        i = layer // 2
        h = _ffn(h, ffn_norm_pre[layer], wi, wo, (layer, 0))
        if layer % 2 == 0:
            mix = _even_mixer(h, mix_norm[layer], even_w_in[i], even_w_out[i], even_sinks[i], nsa_cmp_pe[i],
                              nsa_cmp_w1[i], nsa_cmp_w2[i], cos, sin, bsz, t_len)
        else:
            mix = _odd_mixer(h, mix_norm[layer], odd_w_in[i], odd_w_out[i], cos, sin, bsz, t_len)
        last = layer == depth - 1
        h = _ffn(h, ffn_norm_post[layer], wi, wo, (layer, 1), final_norm if last else None, mix=mix)
    return h.reshape(bsz, t_len, D_MODEL)
```
